```python
import math
import jax, jax.numpy as jnp
from jax import lax
import numpy as np

D_MODEL = 2048
BATCH = 4
SEQ = 2048
DEPTH = 2

GRID_W = 64
CTX_LEN = 256
EPS = 1e-6
N_BRANCH = 4
ATTN_HEADS = 8
ATTN_KV_HEADS = 2
Q_PER_KV = ATTN_HEADS // ATTN_KV_HEADS
HEAD_DIM = 128
ROPE_THETA = 10000.0
Q_BLOCK = 128
RWKV_HEADS = 8
RWKV_HEAD = 64
RWKV_W = RWKV_HEADS * RWKV_HEAD
RWKV_DECAY_LORA = 64
RWKV_AAA_LORA = 64
RWKV_GATE_LORA = 128
RWKV_DECAY_SCALE = math.exp(-0.5)
RWKV_GN_EPS = 64e-5
MLSTM_HEADS = 4
MLSTM_HEAD = 128
MLSTM_W = MLSTM_HEADS * MLSTM_HEAD
MLSTM_CHUNK = 64
MLSTM_CONV = 3
MLSTM_NEG = -1e30
S5_W = 512
S5_GROUP = 16
S5_GROUPS = S5_W // S5_GROUP
S5_STATE = 64
FFN_HIDDEN = -(-8 * D_MODEL // (3 * 256)) * 256
ATTN_IN = (ATTN_HEADS + 2 * ATTN_KV_HEADS) * HEAD_DIM
RWKV_IN = 3 * RWKV_W + RWKV_DECAY_LORA + RWKV_AAA_LORA + RWKV_GATE_LORA
MLSTM_IN = 4 * MLSTM_W + 4 * MLSTM_HEADS
GATE_IN = N_BRANCH * D_MODEL
IN_GROUPS = (ATTN_IN, RWKV_IN, MLSTM_IN, S5_W, GATE_IN)
D_IN = ATTN_IN + RWKV_IN + MLSTM_IN + S5_W + GATE_IN

kernel_name = 'hybrid_flow_dit_block'


def rmsnorm(x, g):
    xf = x.astype(jnp.float32)
    y = xf * lax.rsqrt(jnp.mean(xf * xf, axis=-1, keepdims=True) + EPS)
    return (y * g.astype(jnp.float32)).astype(x.dtype)


def modulate(h, shift, scale):
    return h * (1 + scale) + shift


def split_cols(z, sizes):
    return jnp.split(z, np.cumsum(sizes)[:-1].tolist(), axis=-1)


def centred_dwconv(z, w):
    k = w.shape[0]
    pad = k // 2
    n = z.shape[1]
    zp = jnp.pad(z, ((0, 0), (pad, pad), (0, 0)))
    return sum(zp[:, j:j + n] * w[j] for j in range(k))


def bi_token_shift(z, mu):
    zp = jnp.pad(z, ((0, 0), (1, 1), (0, 0)))
    return z + mu * (0.5 * (zp[:, :-2] + zp[:, 2:]) - z)


def swiglu(h, w1, w2):
    a, b = jnp.split(h @ w1, 2, axis=-1)
    return (jax.nn.silu(a) * b) @ w2


def axial_rope_tables(n_tokens):
    rows = n_tokens // GRID_W
    row = jnp.repeat(jnp.arange(rows, dtype=jnp.float32), GRID_W)
    col = jnp.tile(jnp.arange(GRID_W, dtype=jnp.float32), rows)
    axis_dim = HEAD_DIM // 2
    inv_freq = ROPE_THETA ** (-jnp.arange(0, axis_dim, 2, dtype=jnp.float32) / axis_dim)
    ang_r = row[:, None] * inv_freq[None]
    ang_c = col[:, None] * inv_freq[None]
    return (jnp.cos(ang_r), jnp.sin(ang_r), jnp.cos(ang_c), jnp.sin(ang_c))


def _rotate(x, cos, sin):
    x1, x2 = jnp.split(x, 2, axis=-1)
    return jnp.concatenate([x1 * cos - x2 * sin, x1 * sin + x2 * cos], axis=-1)


def axial_rope(x, tables):
    extra = x.ndim - 3
    cr, sr, cc, sc = (t.reshape((t.shape[0],) + (1,) * extra + (t.shape[1],)) for t in tables)
    xr, xc = jnp.split(x.astype(jnp.float32), 2, axis=-1)
    return jnp.concatenate([_rotate(xr, cr, sr), _rotate(xc, cc, sc)], axis=-1).astype(x.dtype)


def _attend(q, k, v):
    s = jnp.einsum('bqkgd,bskd->bkgqs', q, k).astype(jnp.float32) * (HEAD_DIM ** -0.5)
    p = jax.nn.softmax(s, axis=-1).astype(v.dtype)
    return jnp.einsum('bkgqs,bskd->bqkgd', p, v)


def attention_mixer(zc, zl, q_g, k_g, rope):
    def heads(z):
        b, n = z.shape[:2]
        q, k, v = split_cols(z, (ATTN_HEADS * HEAD_DIM, ATTN_KV_HEADS * HEAD_DIM, ATTN_KV_HEADS * HEAD_DIM))
        q = rmsnorm(q.reshape(b, n, ATTN_KV_HEADS, Q_PER_KV, HEAD_DIM), q_g)
        k = rmsnorm(k.reshape(b, n, ATTN_KV_HEADS, HEAD_DIM), k_g)
        return q, k, v.reshape(b, n, ATTN_KV_HEADS, HEAD_DIM)
    qc, kc, vc = heads(zc)
    ql, kl, vl = heads(zl)
    ql = axial_rope(ql, rope)
    kl = axial_rope(kl, rope)
    b, n_ctx = zc.shape[:2]
    n_lat = zl.shape[1]
    yc = _attend(qc, kc, vc).reshape(b, n_ctx, -1)
    k_all = jnp.concatenate([kc, kl], axis=1)
    v_all = jnp.concatenate([vc, vl], axis=1)
    qb = jnp.moveaxis(ql.reshape(b, n_lat // Q_BLOCK, Q_BLOCK, ATTN_KV_HEADS, Q_PER_KV, HEAD_DIM), 1, 0)
    yl = lax.map(lambda q_blk: _attend(q_blk, k_all, v_all), qb)
    yl = jnp.moveaxis(yl, 0, 1).reshape(b, n_lat, -1)
    return yc, yl


def _rwkv_inputs(z, mu, w0, w_up, a0, a_up, g_up, k_k, k_a):
    b, n = z.shape[:2]
    hd = lambda t: t.reshape(b, n, RWKV_HEADS, RWKV_HEAD)
    z = bi_token_shift(z.astype(jnp.float32), mu)
    r, k, v, w_lo, a_lo, g_lo = split_cols(z, (RWKV_W, RWKV_W, RWKV_W, RWKV_DECAY_LORA, RWKV_AAA_LORA, RWKV_GATE_LORA))
    g = jax.nn.sigmoid(g_lo) @ g_up
    kk = hd(k * k_k)
    kk = kk * lax.rsqrt(jnp.sum(kk * kk, axis=-1, keepdims=True) + 1e-12)
    per_dir = []
    for d in range(2):
        w = jnp.exp(-RWKV_DECAY_SCALE * jax.nn.sigmoid(w0[d] + jnp.tanh(w_lo) @ w_up[d]))
        a = jax.nn.sigmoid(a0[d] + a_lo @ a_up[d])
        k_mod = k * (1 + (a - 1) * k_a)
        per_dir.append((hd(w), hd(a), hd(k_mod)))
    return hd(r), hd(v), kk, g, per_dir


def _rwkv_scan(r, w, kk, a, k, v, s0, reverse):
    xs = tuple(jnp.moveaxis(t, 1, 0) for t in (r, w, kk, a * kk, k, v))
    def step(s, inp):
        r_t, w_t, kk_t, b_t, k_t, v_t = inp
        s = (s * w_t[:, :, None, :]
             - jnp.einsum('bhvk,bhk->bhv', s, kk_t)[..., None] * b_t[:, :, None, :]
             + v_t[..., None] * k_t[:, :, None, :])
        return s, jnp.einsum('bhvk,bhk->bhv', s, r_t)
    s, o = lax.scan(step, s0, xs, reverse=reverse)
    return s, jnp.moveaxis(o, 0, 1)


def rwkv_mixer(zc, zl, mu, w0, w_up, a0, a_up, g_up, k_k, k_a, r_k, gn_g, gn_b):
    streams = [_rwkv_inputs(z, mu, w0, w_up, a0, a_up, g_up, k_k, k_a) for z in (zc, zl)]
    b = zl.shape[0]
    s0 = jnp.zeros((b, RWKV_HEADS, RWKV_HEAD, RWKV_HEAD), jnp.float32)
    outs = [0.0, 0.0]
    for d in range(2):
        s = s0
        for i, (r, v, kk, g, per_dir) in enumerate(streams):
            w, a, k_mod = per_dir[d]
            s, o = _rwkv_scan(r, w, kk, a, k_mod, v, s, d == 1)
            outs[i] = outs[i] + o + jnp.sum(r * k_mod * r_k, axis=-1, keepdims=True) * v
    results = []
    for (r, v, kk, g, _), o in zip(streams, outs):
        mean = jnp.mean(o, axis=-1, keepdims=True)
        var = jnp.mean(jnp.square(o - mean), axis=-1, keepdims=True)
        o = ((o - mean) * lax.rsqrt(var + RWKV_GN_EPS)).reshape(o.shape[0], o.shape[1], RWKV_W)
        results.append((o * gn_g + gn_b) * g)
    return results[0], results[1]


def _mlstm_chunked(q, k, v, ig, lf, state):
    b, h, n, dh = q.shape
    nc = n // MLSTM_CHUNK
    def chunks(t):
        return jnp.moveaxis(t.reshape((b, h, nc, MLSTM_CHUNK) + t.shape[3:]), 2, 0)
    lower = jnp.tril(jnp.ones((MLSTM_CHUNK, MLSTM_CHUNK), dtype=bool))
    def step(carry, inp):
        c_mat, n_vec, m = carry
        q_c, k_c, v_c, i_c, f_c = inp
        cum = jnp.cumsum(f_c, axis=-1)
        log_w = jnp.where(lower, cum[..., :, None] - cum[..., None, :] + i_c[..., None, :], MLSTM_NEG)
        m_inter = cum + m[..., None]
        m_t = jnp.maximum(jnp.max(log_w, axis=-1), m_inter)
        s = jnp.einsum('bhtd,bhsd->bhts', q_c, k_c) * jnp.exp(log_w - m_t[..., None])
        w_inter = jnp.exp(m_inter - m_t)
        num = jnp.einsum('bhts,bhsd->bhtd', s, v_c) + w_inter[..., None] * jnp.einsum('bhvk,bhtk->bhtv', c_mat, q_c)
        den = jnp.sum(s, axis=-1) + w_inter * jnp.einsum('bhk,bhtk->bht', n_vec, q_c)
        h_c = num / jnp.maximum(jnp.abs(den), jnp.exp(-m_t))[..., None]
        total = cum[..., -1]
        log_src = total[..., None] - cum + i_c
        m_new = jnp.maximum(total + m, jnp.max(log_src, axis=-1))
        src = jnp.exp(log_src - m_new[..., None])
        decay = jnp.exp(total + m - m_new)
        c_mat = decay[..., None, None] * c_mat + jnp.einsum('bhs,bhsv,bhsk->bhvk', src, v_c, k_c)
        n_vec = decay[..., None] * n_vec + jnp.einsum('bhs,bhsk->bhk', src, k_c)
        return (c_mat, n_vec, m_new), h_c
    state, hs = lax.scan(step, state, tuple(chunks(t) for t in (q, k, v, ig, lf)))
    return jnp.moveaxis(hs, 0, 2).reshape(b, h, n, dh), state


def mlstm_mixer(zc, zl, conv_w, i_b, f_b, gn_g):
    def prep(z):
        b, n = z.shape[:2]
        q, k, v, o, ig, fg = split_cols(z, (MLSTM_W,) * 4 + (2 * MLSTM_HEADS,) * 2)
        q, k = jnp.split(jax.nn.silu(centred_dwconv(jnp.concatenate([q, k], axis=-1), conv_w)), 2, axis=-1)
        hd = lambda t: jnp.moveaxis(t.astype(jnp.float32).reshape(b, n, MLSTM_HEADS, MLSTM_HEAD), 1, 2)
        gate = lambda t, bias: jnp.moveaxis(t.astype(jnp.float32).reshape(b, n, 2, MLSTM_HEADS) + bias, 1, -1)
        return (hd(q), hd(k) * MLSTM_HEAD ** -0.5, hd(v), gate(ig, i_b),
                jax.nn.log_sigmoid(gate(fg, f_b)), jax.nn.sigmoid(o.astype(jnp.float32)))
    streams = [prep(zc), prep(zl)]
    b = zl.shape[0]
    state0 = (jnp.zeros((b, MLSTM_HEADS, MLSTM_HEAD, MLSTM_HEAD), jnp.float32),
              jnp.zeros((b, MLSTM_HEADS, MLSTM_HEAD), jnp.float32),
              jnp.zeros((b, MLSTM_HEADS), jnp.float32))
    hsum = [0.0, 0.0]
    for d in range(2):
        flip = (lambda t: jnp.flip(t, axis=2)) if d == 1 else (lambda t: t)
        state = state0
        for i, (q, k, v, ig, lf, _) in enumerate(streams):
            h, state = _mlstm_chunked(flip(q), flip(k), flip(v), flip(ig[:, d]), flip(lf[:, d]), state)
            hsum[i] = hsum[i] + flip(h)
    results = []
    for stream, h in zip(streams, hsum):
        bb, _, n, _ = h.shape
        h = stream[5] * jnp.moveaxis(h, 1, 2).reshape(bb, n, MLSTM_W)
        hh = h.reshape(bb, n, MLSTM_HEADS, MLSTM_HEAD)
        mean = jnp.mean(hh, axis=-1, keepdims=True)
        var = jnp.mean(jnp.square(hh - mean), axis=-1, keepdims=True)
        results.append(((hh - mean) * lax.rsqrt(var + EPS)).reshape(bb, n, MLSTM_W) * gn_g)
    return results[0], results[1]


def _s5_discretise(lam_re, lam_im, log_step, b_re, b_im):
    lam_re, lam_im, b_re, b_im = (t.astype(jnp.float32) for t in (lam_re, lam_im, b_re, b_im))
    dt = jnp.exp(log_step.astype(jnp.float32))[:, None]
    mag = jnp.exp(lam_re * dt)
    a_re = mag * jnp.cos(lam_im * dt)
    a_im = mag * jnp.sin(lam_im * dt)
    den = lam_re * lam_re + lam_im * lam_im
    f_re = ((a_re - 1) * lam_re + a_im * lam_im) / den
    f_im = (a_im * lam_re - (a_re - 1) * lam_im) / den
    bb_re = f_re[..., None] * b_re - f_im[..., None] * b_im
    bb_im = f_re[..., None] * b_im + f_im[..., None] * b_re
    return a_re, a_im, bb_re, bb_im


def _complex_affine_combine(e1, e2):
    a1r, a1i, b1r, b1i = e1
    a2r, a2i, b2r, b2i = e2
    return (a1r * a2r - a1i * a2i, a1r * a2i + a1i * a2r,
            a2r * b1r - a2i * b1i + b2r, a2r * b1i + a2i * b1r + b2i)


def _s5_scan(u, a_re, a_im, bb_re, bb_im, x0_re, x0_im):
    bu_re = jnp.einsum('gpc,btgc->btgp', bb_re, u)
    bu_im = jnp.einsum('gpc,btgc->btgp', bb_im, u)
    bu_re = bu_re.at[:, 0].add(a_re * x0_re - a_im * x0_im)
    bu_im = bu_im.at[:, 0].add(a_re * x0_im + a_im * x0_re)
    elems = (jnp.broadcast_to(a_re, bu_re.shape), jnp.broadcast_to(a_im, bu_im.shape), bu_re, bu_im)
    _, _, x_re, x_im = lax.associative_scan(_complex_affine_combine, elems, axis=1)
    return x_re, x_im


def s5_mixer(zc, zl, lam_re, lam_im, log_step, b_re, b_im, c_re, c_im, d_skip, w_glu):
    us = [z.astype(jnp.float32).reshape(z.shape[0], z.shape[1], S5_GROUPS, S5_GROUP) for z in (zc, zl)]
    b = zl.shape[0]
    zero = jnp.zeros((b, S5_GROUPS, S5_STATE), jnp.float32)
    ys = [0.0, 0.0]
    for d in range(2):
        flip = (lambda t: jnp.flip(t, axis=1)) if d == 1 else (lambda t: t)
        a_re, a_im, bb_re, bb_im = _s5_discretise(lam_re[d], lam_im[d], log_step[d], b_re[d], b_im[d])
        cr = c_re[d].astype(jnp.float32)
        ci = c_im[d].astype(jnp.float32)
        x0 = (zero, zero)
        for i, u in enumerate(us):
            x_re, x_im = _s5_scan(flip(u), a_re, a_im, bb_re, bb_im, x0[0], x0[1])
            x0 = (x_re[:, -1], x_im[:, -1])
            y = jnp.einsum('gcp,btgp->btgc', cr, x_re) - jnp.einsum('gcp,btgp->btgc', ci, x_im)
            ys[i] = ys[i] + flip(y)
    outs = []
    d_g = d_skip.astype(jnp.float32).reshape(S5_GROUPS, S5_GROUP)
    for u, y in zip(us, ys):
        bb, n = u.shape[:2]
        y = (y + u * d_g).reshape(bb, n, S5_W)
        gl, gt = jnp.split(jax.nn.gelu(y) @ w_glu.astype(jnp.float32), 2, axis=-1)
        outs.append(gl * jax.nn.sigmoid(gt))
    return outs[0], outs[1]


def gated_merge(z_gate, b_gate, branches, w_branches, w_out):
    gates = jnp.split(jax.nn.sigmoid((z_gate + b_gate).astype(jnp.float32)), N_BRANCH, axis=-1)
    dt = z_gate.dtype
    y = sum(g.astype(dt) * (br.astype(dt) @ w) for g, br, w in zip(gates, branches, w_branches))
    return y @ w_out


def setup_inputs(seed: int = 0) -> dict:
    key = jax.random.key(seed)
    keys = iter(jax.random.split(key, 64))
    def nrm(shape, scale):
        return scale * jax.random.normal(next(keys), shape, jnp.float32)
    def gain(shape):
        return 1.0 + nrm(shape, 0.05)
    L, D = DEPTH, D_MODEL
    G, P, C5 = S5_GROUPS, S5_STATE, S5_GROUP
    n_idx = jnp.arange(P, dtype=jnp.float32)
    return {
        'x': nrm((BATCH, SEQ, D), 1.0),
        'c': nrm((BATCH, D), 1.0),
        'ctx': nrm((BATCH, CTX_LEN, D), 1.0),
        'c_ctx': nrm((D,), 1.0),
        'w_mod': nrm((L, D, 6 * D), 0.5 * D ** -0.5),
        'b_mod': nrm((L, 6 * D), 0.02),
        'norm1_g': gain((L, D)),
        'norm2_g': gain((L, D)),
        'w_in': nrm((L, D, D_IN), D ** -0.5),
        'b_gate': nrm((L, GATE_IN), 0.02),
        'q_norm_g': gain((L, HEAD_DIM)),
        'k_norm_g': gain((L, HEAD_DIM)),
        'rwkv_mu': jax.random.uniform(next(keys), (L, RWKV_IN), jnp.float32),
        'rwkv_w0': nrm((L, 2, RWKV_W), 1.0),
        'rwkv_w_up': nrm((L, 2, RWKV_DECAY_LORA, RWKV_W), 0.1),
        'rwkv_a0': nrm((L, 2, RWKV_W), 0.5),
        'rwkv_a_up': nrm((L, 2, RWKV_AAA_LORA, RWKV_W), 0.1),
        'rwkv_g_up': nrm((L, RWKV_GATE_LORA, RWKV_W), RWKV_GATE_LORA ** -0.5),
        'rwkv_k_k': 0.85 + nrm((L, RWKV_W), 0.05),
        'rwkv_k_a': 1.0 + nrm((L, RWKV_W), 0.05),
        'rwkv_r_k': nrm((L, RWKV_HEADS, RWKV_HEAD), 0.1),
        'rwkv_gn_g': gain((L, RWKV_W)),
        'rwkv_gn_b': nrm((L, RWKV_W), 0.02),
        'mlstm_conv_w': nrm((L, MLSTM_CONV, 2 * MLSTM_W), MLSTM_CONV ** -0.5),
        'mlstm_i_b': nrm((L, 2, MLSTM_HEADS), 0.1),
        'mlstm_f_b': 3.0 + nrm((L, 2, MLSTM_HEADS), 0.5),
        'mlstm_gn_g': gain((L, MLSTM_W)),
        's5_lam_re': -0.5 * jnp.exp(nrm((L, 2, G, P), 0.05)),
        's5_lam_im': jnp.pi * n_idx * (1.0 + nrm((L, 2, G, P), 0.01)),
        's5_log_step': jax.random.uniform(next(keys), (L, 2, G), jnp.float32, math.log(1e-3), math.log(1e-1)),
        's5_b_re': nrm((L, 2, G, P, C5), (2 * C5) ** -0.5),
        's5_b_im': nrm((L, 2, G, P, C5), (2 * C5) ** -0.5),
        's5_c_re': nrm((L, 2, G, C5, P), P ** -0.5),
        's5_c_im': nrm((L, 2, G, C5, P), P ** -0.5),
        's5_d': nrm((L, S5_W), 1.0),
        's5_w_glu': nrm((L, S5_W, 2 * S5_W), S5_W ** -0.5),
        'w_br_attn': nrm((L, ATTN_HEADS * HEAD_DIM, D), (ATTN_HEADS * HEAD_DIM) ** -0.5),
        'w_br_rwkv': nrm((L, RWKV_W, D), RWKV_W ** -0.5),
        'w_br_mlstm': nrm((L, MLSTM_W, D), MLSTM_W ** -0.5),
        'w_br_s5': nrm((L, S5_W, D), S5_W ** -0.5),
        'w_out': nrm((L, D, D), D ** -0.5),
        'w_ffn_in': nrm((L, D, 2 * FFN_HIDDEN), D ** -0.5),
        'w_ffn_out': nrm((L, FFN_HIDDEN, D), FFN_HIDDEN ** -0.5),
        'final_norm_g': gain((D,)),
    }


def reference(x, c, ctx, c_ctx, w_mod, b_mod, norm1_g, norm2_g, w_in, b_gate, q_norm_g, k_norm_g,
              rwkv_mu, rwkv_w0, rwkv_w_up, rwkv_a0, rwkv_a_up, rwkv_g_up, rwkv_k_k, rwkv_k_a, rwkv_r_k,
              rwkv_gn_g, rwkv_gn_b, mlstm_conv_w, mlstm_i_b, mlstm_f_b, mlstm_gn_g,
              s5_lam_re, s5_lam_im, s5_log_step, s5_b_re, s5_b_im, s5_c_re, s5_c_im, s5_d, s5_w_glu,
              w_br_attn, w_br_rwkv, w_br_mlstm, w_br_s5, w_out, w_ffn_in, w_ffn_out, final_norm_g):
    rope = axial_rope_tables(x.shape[1])
    xc, xl = ctx, x
    for l in range(DEPTH):
        mod_l = jnp.split((jax.nn.silu(c) @ w_mod[l] + b_mod[l])[:, None, :], 6, axis=-1)
        mod_c = jnp.split((jax.nn.silu(c_ctx) @ w_mod[l] + b_mod[l])[None, None, :], 6, axis=-1)
        hc = modulate(rmsnorm(xc, norm1_g[l]), mod_c[0], mod_c[1])
        hl = modulate(rmsnorm(xl, norm1_g[l]), mod_l[0], mod_l[1])
        zc = split_cols(hc @ w_in[l], IN_GROUPS)
        zl = split_cols(hl @ w_in[l], IN_GROUPS)
        ya = attention_mixer(zc[0], zl[0], q_norm_g[l], k_norm_g[l], rope)
        yr = rwkv_mixer(zc[1], zl[1], rwkv_mu[l], rwkv_w0[l], rwkv_w_up[l], rwkv_a0[l], rwkv_a_up[l],
                        rwkv_g_up[l], rwkv_k_k[l], rwkv_k_a[l], rwkv_r_k[l], rwkv_gn_g[l], rwkv_gn_b[l])
        ym = mlstm_mixer(zc[2], zl[2], mlstm_conv_w[l], mlstm_i_b[l], mlstm_f_b[l], mlstm_gn_g[l])
        ys = s5_mixer(zc[3], zl[3], s5_lam_re[l], s5_lam_im[l], s5_log_step[l], s5_b_re[l], s5_b_im[l],
                      s5_c_re[l], s5_c_im[l], s5_d[l], s5_w_glu[l])
        w_branches = (w_br_attn[l], w_br_rwkv[l], w_br_mlstm[l], w_br_s5[l])
        xl = xl + mod_l[2] * gated_merge(zl[4], b_gate[l], (ya[1], yr[1], ym[1], ys[1]), w_branches, w_out[l])
        xl = xl + mod_l[5] * swiglu(modulate(rmsnorm(xl, norm2_g[l]), mod_l[3], mod_l[4]), w_ffn_in[l], w_ffn_out[l])
        if l < DEPTH - 1:
            xc = xc + mod_c[2] * gated_merge(zc[4], b_gate[l], (ya[0], yr[0], ym[0], ys[0]), w_branches, w_out[l])
            xc = xc + mod_c[5] * swiglu(modulate(rmsnorm(xc, norm2_g[l]), mod_c[3], mod_c[4]), w_ffn_in[l], w_ffn_out[l])
    return rmsnorm(xl, final_norm_g)
```

```python
import functools
import math

import numpy as np
import jax
import jax.numpy as jnp
from jax import lax
from jax.experimental import pallas as pl
from jax.experimental.pallas import tpu as pltpu

F32 = jnp.float32
BF16 = jnp.bfloat16
HP = lax.Precision.HIGHEST

D = 2048
NB = 4
SEQ = 2048
CTX = 256
DEPTH = 2
N_LAT = NB * SEQ
N_CTX = NB * CTX
ROWS = N_LAT + N_CTX
EPS = 1e-6
GRID_W = 64

HEAD_DIM = 128
ATTN_HEADS = 8
ATTN_KV = 2
ROPE_THETA = 10000.0
ATTN_Q = ATTN_HEADS * HEAD_DIM
ATTN_IN = (ATTN_HEADS + 2 * ATTN_KV) * HEAD_DIM

RW_H = 8
RW_D = 64
RW_W = 512
RW_IN = 3 * RW_W + 64 + 64 + 128
RW_DECAY = math.exp(-0.5)
RW_GN_EPS = 64e-5

ML_H = 4
ML_D = 128
ML_W = 512
ML_NEG = -1e30
ML_IN = 4 * ML_W + 4 * ML_H

S5_W = 512
S5_C = 16
S5_G = 32
S5_P = 64
S5_L = 16

FFN_H = 5632
GATE_IN = 4 * D

CHUNK = 64
TILE = 256
N_TILES = ROWS // TILE
SEQ_CHUNKS = (SEQ + CTX) // CHUNK
CTX_CHUNKS = CTX // CHUNK
LAT_CHUNKS = SEQ // CHUNK

VMEM_LIMIT_BYTES = 56 * 1024 * 1024


def _cp(*sem):
    return pltpu.CompilerParams(dimension_semantics=sem, vmem_limit_bytes=VMEM_LIMIT_BYTES)


def _bdot(a, b):
    return jnp.dot(a.astype(BF16), b.astype(BF16), preferred_element_type=F32)


def _bdot_nt(a, b):
    return lax.dot_general(a.astype(BF16), b.astype(BF16), (((1,), (1,)), ((), ())),
                           preferred_element_type=F32)


def _bdot_tn(a, b):
    return lax.dot_general(a.astype(BF16), b.astype(BF16), (((0,), (0,)), ((), ())),
                           preferred_element_type=F32)


def _hdot(a, b):
    return jnp.dot(a, b, precision=HP, preferred_element_type=F32)


def _sigmoid(x):
    return 1.0 / (1.0 + jnp.exp(-x))


def _silu(x):
    return x * _sigmoid(x)


def _mod_kernel(c_ref, w_ref, b_ref, o_ref):
    o_ref[...] = _hdot(_silu(c_ref[...]), w_ref[...]) + b_ref[...]


def _modulation(c_all, w, b):
    tn = 1024
    return pl.pallas_call(
        _mod_kernel,
        grid=(6 * D // tn,),
        in_specs=[pl.BlockSpec((8, D), lambda j: (0, 0)),
                  pl.BlockSpec((D, tn), lambda j: (0, j)),
                  pl.BlockSpec((1, tn), lambda j: (0, j))],
        out_specs=pl.BlockSpec((8, tn), lambda j: (0, j)),
        out_shape=jax.ShapeDtypeStruct((8, 6 * D), F32),
        compiler_params=_cp("arbitrary"),
        name="modulation",
    )(c_all, w, b.reshape(1, 6 * D))


def _mod_rows(mod, tm):
    starts = np.arange(ROWS // tm) * tm
    idx = np.where(starts < N_LAT, starts // SEQ, NB)
    return mod[idx][:, None, :]


def _norm_mod_kernel(x_ref, g_ref, sh_ref, sc_ref, o_ref):
    x = x_ref[...]
    y = x * lax.rsqrt(jnp.mean(x * x, axis=-1, keepdims=True) + EPS) * g_ref[...]
    o_ref[...] = (y * (1.0 + sc_ref[0]) + sh_ref[0]).astype(o_ref.dtype)


def _norm_mod(x, g, modr, shift_blk, scale_blk, n_tiles):
    return pl.pallas_call(
        _norm_mod_kernel,
        grid=(n_tiles,),
        in_specs=[pl.BlockSpec((TILE, D), lambda i: (i, 0)),
                  pl.BlockSpec((1, D), lambda i: (0, 0)),
                  pl.BlockSpec((1, 1, D), lambda i: (i, 0, shift_blk)),
                  pl.BlockSpec((1, 1, D), lambda i: (i, 0, scale_blk))],
        out_specs=pl.BlockSpec((TILE, D), lambda i: (i, 0)),
        out_shape=jax.ShapeDtypeStruct((n_tiles * TILE, D), BF16),
        compiler_params=_cp("parallel"),
        name="norm_mod",
    )(x, g.reshape(1, D), modr, modr)


def _mm_kernel(a_ref, w_ref, o_ref):
    o_ref[...] = jnp.dot(a_ref[...], w_ref[...], preferred_element_type=F32).astype(o_ref.dtype)


def _mm(a, w, tm, tn, out_dtype=F32):
    m, k = a.shape
    n = w.shape[1]
    return pl.pallas_call(
        _mm_kernel,
        grid=(n // tn, m // tm),
        in_specs=[pl.BlockSpec((tm, k), lambda j, i: (i, 0)),
                  pl.BlockSpec((k, tn), lambda j, i: (0, j))],
        out_specs=pl.BlockSpec((tm, tn), lambda j, i: (i, j)),
        out_shape=jax.ShapeDtypeStruct((m, n), out_dtype),
        compiler_params=_cp("parallel", "parallel"),
        name="matmul",
    )(a, w)


def _mm_res_kernel(a_ref, w_ref, x_ref, g_ref, o_ref, wb_ref):
    @pl.when(pl.program_id(1) == 0)
    def _():
        wb_ref[...] = w_ref[...].astype(BF16)

    y = jnp.dot(a_ref[...], wb_ref[...], preferred_element_type=F32)
    o_ref[...] = x_ref[...] + g_ref[0] * y


def _mm_res(a, w, x, mod, gate_blk, tm, tn, n_rows):
    modr = _mod_rows(mod, tm)
    k = a.shape[1]
    n = w.shape[1]
    gpb = D // tn
    return pl.pallas_call(
        _mm_res_kernel,
        grid=(n // tn, n_rows // tm),
        in_specs=[pl.BlockSpec((tm, k), lambda j, i: (i, 0)),
                  pl.BlockSpec((k, tn), lambda j, i: (0, j)),
                  pl.BlockSpec((tm, tn), lambda j, i: (i, j)),
                  pl.BlockSpec((1, 1, tn), lambda j, i: (i, 0, gate_blk * gpb + j))],
        out_specs=pl.BlockSpec((tm, tn), lambda j, i: (i, j)),
        out_shape=jax.ShapeDtypeStruct((n_rows, n), F32),
        scratch_shapes=[pltpu.VMEM((k, tn), BF16)],
        compiler_params=_cp("arbitrary", "arbitrary"),
        name="matmul_residual",
    )(a, w, x, modr)


def _ffn_in_kernel(a_ref, wa_ref, wb_ref, o_ref, wab_ref, wbb_ref):
    @pl.when(pl.program_id(1) == 0)
    def _():
        wab_ref[...] = wa_ref[...].astype(BF16)
        wbb_ref[...] = wb_ref[...].astype(BF16)

    a = a_ref[...]
    u = jnp.dot(a, wab_ref[...], preferred_element_type=F32)
    v = jnp.dot(a, wbb_ref[...], preferred_element_type=F32)
    o_ref[...] = (_silu(u) * v).astype(o_ref.dtype)


def _ffn_in(h, w, tm, tn, n_rows):
    nb = FFN_H // tn
    return pl.pallas_call(
        _ffn_in_kernel,
        grid=(nb, n_rows // tm),
        in_specs=[pl.BlockSpec((tm, D), lambda j, i: (i, 0)),
                  pl.BlockSpec((D, tn), lambda j, i: (0, j)),
                  pl.BlockSpec((D, tn), lambda j, i: (0, nb + j))],
        out_specs=pl.BlockSpec((tm, tn), lambda j, i: (i, j)),
        out_shape=jax.ShapeDtypeStruct((n_rows, FFN_H), BF16),
        scratch_shapes=[pltpu.VMEM((D, tn), BF16), pltpu.VMEM((D, tn), BF16)],
        compiler_params=_cp("arbitrary", "arbitrary"),
        name="ffn_in",
    )(h, w, w)


def _merge_kernel(ya_ref, yr_ref, ym_ref, ys_ref, ga_ref, gr_ref, gm_ref, gs_ref,
                  ba_ref, br_ref, bm_ref, bs_ref, wa_ref, wr_ref, wm_ref, ws_ref, o_ref,
                  wab_ref, wrb_ref, wmb_ref, wsb_ref):
    @pl.when(pl.program_id(1) == 0)
    def _():
        wab_ref[...] = wa_ref[...].astype(BF16)
        wrb_ref[...] = wr_ref[...].astype(BF16)
        wmb_ref[...] = wm_ref[...].astype(BF16)
        wsb_ref[...] = ws_ref[...].astype(BF16)

    acc = None
    for y_ref, g_ref, b_ref, w_ref in ((ya_ref, ga_ref, ba_ref, wab_ref), (yr_ref, gr_ref, br_ref, wrb_ref),
                                       (ym_ref, gm_ref, bm_ref, wmb_ref), (ys_ref, gs_ref, bs_ref, wsb_ref)):
        gate = _sigmoid(g_ref[...] + b_ref[...])
        term = gate * jnp.dot(y_ref[...], w_ref[...], preferred_element_type=F32)
        acc = term if acc is None else acc + term
    o_ref[...] = acc.astype(o_ref.dtype)


def _merge(ya, yr, ym, ys, z_gate, b_gate, wa, wr, wm, ws, tm, tn, n_rows):
    nb = D // tn
    b_gate = b_gate.reshape(1, GATE_IN)

    def act(width):
        return pl.BlockSpec((tm, width), lambda j, i: (i, 0))

    def gate(br):
        return pl.BlockSpec((tm, tn), lambda j, i: (i, br * nb + j))

    def bias(br):
        return pl.BlockSpec((1, tn), lambda j, i: (0, br * nb + j))

    def wgt(width):
        return pl.BlockSpec((width, tn), lambda j, i: (0, j))

    return pl.pallas_call(
        _merge_kernel,
        grid=(nb, n_rows // tm),
        in_specs=[act(ATTN_Q), act(RW_W), act(ML_W), act(S5_W),
                  gate(0), gate(1), gate(2), gate(3),
                  bias(0), bias(1), bias(2), bias(3),
                  wgt(ATTN_Q), wgt(RW_W), wgt(ML_W), wgt(S5_W)],
        out_specs=pl.BlockSpec((tm, tn), lambda j, i: (i, j)),
        out_shape=jax.ShapeDtypeStruct((n_rows, D), BF16),
        scratch_shapes=[pltpu.VMEM((ATTN_Q, tn), BF16), pltpu.VMEM((RW_W, tn), BF16),
                        pltpu.VMEM((ML_W, tn), BF16), pltpu.VMEM((S5_W, tn), BF16)],
        compiler_params=_cp("arbitrary", "arbitrary"),
        name="gated_merge",
    )(ya, yr, ym, ys, z_gate, z_gate, z_gate, z_gate, b_gate, b_gate, b_gate, b_gate, wa, wr, wm, ws)


def _final_norm_kernel(x_ref, g_ref, o_ref):
    x = x_ref[...]
    o_ref[...] = x * lax.rsqrt(jnp.mean(x * x, axis=-1, keepdims=True) + EPS) * g_ref[...]


def _final_norm(x, g):
    n_tiles = N_LAT // TILE
    return pl.pallas_call(
        _final_norm_kernel,
        grid=(n_tiles,),
        in_specs=[pl.BlockSpec((TILE, D), lambda i: (i, 0)),
                  pl.BlockSpec((1, D), lambda i: (0, 0))],
        out_specs=pl.BlockSpec((TILE, D), lambda i: (i, 0)),
        out_shape=jax.ShapeDtypeStruct((N_LAT, D), F32),
        compiler_params=_cp("parallel"),
        name="final_norm",
    )(x, g.reshape(1, D))


def _tile_halo_specs(width, col_blk=0):
    last = ROWS // 8 - 1
    per = TILE // 8
    return [pl.BlockSpec((TILE, width), lambda t: (t, col_blk)),
            pl.BlockSpec((8, width), lambda t: (jnp.maximum(t * per - 1, 0), col_blk)),
            pl.BlockSpec((8, width), lambda t: (jnp.minimum((t + 1) * per, last), col_blk))]


def _neighbours(z, prev_blk, next_blk):
    t = pl.program_id(0)
    pos = t % (SEQ // TILE)
    is_lat = t < N_LAT // TILE
    has_prev = jnp.logical_and(is_lat, pos > 0).astype(F32)
    has_next = jnp.logical_and(is_lat, pos < SEQ // TILE - 1).astype(F32)
    row = lax.broadcasted_iota(jnp.int32, z.shape, 0)
    zp = jnp.where(row == 0, prev_blk[7:8, :] * has_prev, pltpu.roll(z, 1, 0))
    zn = jnp.where(row == TILE - 1, next_blk[0:1, :] * has_next, pltpu.roll(z, TILE - 1, 0))
    return zp, zn


def _seq_row_block(b, d, n):
    ctx_c = jnp.where(d == 0, n, CTX_CHUNKS - 1 - n)
    lat_c = jnp.where(d == 0, n - CTX_CHUNKS, SEQ_CHUNKS - 1 - n)
    return jnp.where(n < CTX_CHUNKS, N_LAT // CHUNK + CTX_CHUNKS * b + ctx_c, LAT_CHUNKS * b + lat_c)


def _rope(x, cos, sin):
    lane = lax.broadcasted_iota(jnp.int32, x.shape, 1)
    first = (lane % 64) < 32
    partner = jnp.where(first, pltpu.roll(x, 96, 1), pltpu.roll(x, 32, 1))
    return x * cos + partner * sin


def _rms(x, g):
    return x * lax.rsqrt(jnp.mean(x * x, axis=-1, keepdims=True) + EPS) * g


def _attn_kernel(q_ref, kl_ref, kc_ref, vl_ref, vc_ref, cos_ref, sin_ref, cos_t_ref, sin_t_ref,
                 qg_ref, kg_ref, o_ref, klb_ref, kcb_ref, vlb_ref, vcb_ref):
    qi = pl.program_id(2)
    n_lat_tiles = SEQ // TILE

    @pl.when(qi == 0)
    def _():
        kg = kg_ref[...]
        klb_ref[...] = _rope(_rms(kl_ref[...], kg), cos_ref[...], sin_ref[...]).astype(BF16)
        kcb_ref[...] = _rms(kc_ref[...], kg).astype(BF16)
        vlb_ref[...] = vl_ref[...].astype(BF16)
        vcb_ref[...] = vc_ref[...].astype(BF16)

    scale = HEAD_DIM ** -0.5

    def heads(latent):
        for h in range(ATTN_HEADS // ATTN_KV):
            sl = slice(h * HEAD_DIM, (h + 1) * HEAD_DIM)
            q = _rms(q_ref[:, sl], qg_ref[...])
            if latent:
                q = _rope(q, cos_t_ref[...], sin_t_ref[...])
            q = (q * scale).astype(BF16)
            s_c = lax.dot_general(q, kcb_ref[...], (((1,), (1,)), ((), ())), preferred_element_type=F32)
            m = jnp.max(s_c, axis=-1, keepdims=True)
            if latent:
                s_l = lax.dot_general(q, klb_ref[...], (((1,), (1,)), ((), ())), preferred_element_type=F32)
                m = jnp.maximum(m, jnp.max(s_l, axis=-1, keepdims=True))
            p_c = jnp.exp(s_c - m)
            den = jnp.sum(p_c, axis=-1, keepdims=True)
            acc = jnp.dot(p_c.astype(BF16), vcb_ref[...], preferred_element_type=F32)
            if latent:
                p_l = jnp.exp(s_l - m)
                den = den + jnp.sum(p_l, axis=-1, keepdims=True)
                acc = acc + jnp.dot(p_l.astype(BF16), vlb_ref[...], preferred_element_type=F32)
            o_ref[:, sl] = (acc / den).astype(o_ref.dtype)

    @pl.when(qi < n_lat_tiles)
    def _():
        heads(True)

    @pl.when(qi == n_lat_tiles)
    def _():
        heads(False)


def _rope_tables():
    rows = SEQ // GRID_W
    row = jnp.repeat(jnp.arange(rows, dtype=F32), GRID_W)
    col = jnp.tile(jnp.arange(GRID_W, dtype=F32), rows)
    axis_dim = HEAD_DIM // 2
    inv_freq = ROPE_THETA ** (-jnp.arange(0, axis_dim, 2, dtype=F32) / axis_dim)
    ang_r = row[:, None] * inv_freq[None]
    ang_c = col[:, None] * inv_freq[None]
    cos = jnp.concatenate([jnp.cos(ang_r), jnp.cos(ang_r), jnp.cos(ang_c), jnp.cos(ang_c)], axis=-1)
    sin = jnp.concatenate([-jnp.sin(ang_r), jnp.sin(ang_r), -jnp.sin(ang_c), jnp.sin(ang_c)], axis=-1)
    return cos, sin


def _attention(z, q_g, k_g, cos, sin):
    n_lat_tiles = SEQ // TILE
    qw = ATTN_Q // ATTN_KV
    kcol = ATTN_Q // HEAD_DIM
    vcol = kcol + ATTN_KV

    def q_row(b, g, qi):
        return jnp.where(qi < n_lat_tiles, n_lat_tiles * b + qi, N_LAT // TILE + b)

    return pl.pallas_call(
        _attn_kernel,
        grid=(NB, ATTN_KV, n_lat_tiles + 1),
        in_specs=[pl.BlockSpec((TILE, qw), lambda b, g, qi: (q_row(b, g, qi), g)),
                  pl.BlockSpec((SEQ, HEAD_DIM), lambda b, g, qi: (b, kcol + g)),
                  pl.BlockSpec((CTX, HEAD_DIM), lambda b, g, qi: (N_LAT // CTX + b, kcol + g)),
                  pl.BlockSpec((SEQ, HEAD_DIM), lambda b, g, qi: (b, vcol + g)),
                  pl.BlockSpec((CTX, HEAD_DIM), lambda b, g, qi: (N_LAT // CTX + b, vcol + g)),
                  pl.BlockSpec((SEQ, HEAD_DIM), lambda b, g, qi: (0, 0)),
                  pl.BlockSpec((SEQ, HEAD_DIM), lambda b, g, qi: (0, 0)),
                  pl.BlockSpec((TILE, HEAD_DIM), lambda b, g, qi: (jnp.minimum(qi, n_lat_tiles - 1), 0)),
                  pl.BlockSpec((TILE, HEAD_DIM), lambda b, g, qi: (jnp.minimum(qi, n_lat_tiles - 1), 0)),
                  pl.BlockSpec((1, HEAD_DIM), lambda b, g, qi: (0, 0)),
                  pl.BlockSpec((1, HEAD_DIM), lambda b, g, qi: (0, 0))],
        out_specs=pl.BlockSpec((TILE, qw), lambda b, g, qi: (q_row(b, g, qi), g)),
        out_shape=jax.ShapeDtypeStruct((ROWS, ATTN_Q), BF16),
        scratch_shapes=[pltpu.VMEM((SEQ, HEAD_DIM), BF16), pltpu.VMEM((CTX, HEAD_DIM), BF16),
                        pltpu.VMEM((SEQ, HEAD_DIM), BF16), pltpu.VMEM((CTX, HEAD_DIM), BF16)],
        compiler_params=_cp("arbitrary", "arbitrary", "arbitrary"),
        name="attention",
    )(z, z, z, z, z, cos, sin, cos, sin, q_g.reshape(1, HEAD_DIM), k_g.reshape(1, HEAD_DIM))


def _rwkv_prep_kernel(z_ref, zp_ref, zn_ref, mu_ref, w0_ref, wup_ref, a0_ref, aup_ref, gup_ref,
                      kk_ref, ka_ref, rk_ref, bd_ref,
                      r_out, v_out, kkn_out, g_out, bonus_out, lw_out, b_out, km_out):
    z = z_ref[...]
    zp, zn = _neighbours(z, zp_ref[...], zn_ref[...])
    zs = z + mu_ref[...] * (0.5 * (zp + zn) - z)
    r = zs[:, 0:RW_W]
    k = zs[:, RW_W:2 * RW_W]
    v = zs[:, 2 * RW_W:3 * RW_W]
    w_lo = zs[:, 3 * RW_W:3 * RW_W + 64]
    a_lo = zs[:, 3 * RW_W + 64:3 * RW_W + 128]
    g_lo = zs[:, 3 * RW_W + 128:3 * RW_W + 256]
    bd = bd_ref[...]
    kk = k * kk_ref[...]
    kk = kk * lax.rsqrt(_hdot(kk * kk, bd) + 1e-12)
    r_out[...] = r
    v_out[...] = v
    kkn_out[...] = kk
    g_out[...] = _hdot(_sigmoid(g_lo), gup_ref[...])
    tw = jnp.tanh(w_lo)
    km_sum = None
    for d in range(2):
        lw = -RW_DECAY * _sigmoid(w0_ref[d] + _hdot(tw, wup_ref[d]))
        a = _sigmoid(a0_ref[d] + _hdot(a_lo, aup_ref[d]))
        km = k * (1.0 + (a - 1.0) * ka_ref[...])
        lw_out[d] = lw
        b_out[d] = a * kk
        km_out[d] = km
        km_sum = km if km_sum is None else km_sum + km
    bonus_out[...] = _hdot(r * km_sum * rk_ref[...], bd) * v


def _rwkv_prep(z, p):
    row = lambda a: a.reshape(1, -1)
    full = lambda shape: pl.BlockSpec(shape, lambda t: (0,) * len(shape))
    out_tok = pl.BlockSpec((TILE, RW_W), lambda t: (t, 0))
    out_dir = pl.BlockSpec((2, TILE, RW_W), lambda t: (0, t, 0))
    tok = jax.ShapeDtypeStruct((ROWS, RW_W), F32)
    drn = jax.ShapeDtypeStruct((2, ROWS, RW_W), F32)
    return pl.pallas_call(
        _rwkv_prep_kernel,
        grid=(N_TILES,),
        in_specs=_tile_halo_specs(RW_IN) + [
            full((1, RW_IN)), full((2, 1, RW_W)), full((2, 64, RW_W)), full((2, 1, RW_W)),
            full((2, 64, RW_W)), full((128, RW_W)), full((1, RW_W)), full((1, RW_W)), full((1, RW_W)),
            full((RW_W, RW_W))],
        out_specs=[out_tok] * 5 + [out_dir] * 3,
        out_shape=[tok] * 5 + [drn] * 3,
        compiler_params=_cp("parallel"),
        name="rwkv_prep",
    )(z, z, z, row(p["mu"]), p["w0"].reshape(2, 1, RW_W), p["w_up"], p["a0"].reshape(2, 1, RW_W),
      p["a_up"], p["g_up"], row(p["k_k"]), row(p["k_a"]), row(p["r_k"]), _head_block_ones(RW_W, RW_D))


def _head_block_ones(width, head):
    idx = np.arange(width) // head
    return jnp.asarray((idx[:, None] == idx[None, :]).astype(np.float32))


def _rwkv_scan_kernel(r_ref, v_ref, kk_ref, lw_ref, b_ref, km_ref, o_ref, s_ref):
    d = pl.program_id(1)
    n = pl.program_id(2)

    @pl.when(n == 0)
    def _():
        s_ref[...] = jnp.zeros_like(s_ref)

    c = CHUNK
    ti = lax.broadcasted_iota(jnp.int32, (c, c), 0)
    si = lax.broadcasted_iota(jnp.int32, (c, c), 1)
    delta = (ti - si) * (1 - 2 * d)
    incl = delta >= 0
    strict = delta > 0
    eye = jnp.where(ti == si, 1.0, 0.0)

    lw = lw_ref[0]
    cs = _hdot(jnp.where(incl, 1.0, 0.0), lw)
    tot = jnp.sum(lw, axis=0, keepdims=True)
    r = r_ref[...]
    v = v_ref[...]
    kk = kk_ref[...]
    bb = b_ref[0]
    km = km_ref[0]
    e_neg = jnp.exp(-cs)
    e_rem = jnp.exp(tot - cs)
    kkt = kk * jnp.exp(cs - lw)
    rt = r * jnp.exp(cs)
    bt = bb * e_neg
    kt = km * e_neg
    bh = bb * e_rem
    kh = km * e_rem
    e_tot = jnp.exp(tot)

    for h in range(RW_H):
        sl = slice(h * RW_D, (h + 1) * RW_D)
        s0 = s_ref[h]
        kkt_h, rt_h, bt_h, kt_h, v_h = kkt[:, sl], rt[:, sl], bt[:, sl], kt[:, sl], v[:, sl]
        l_b = jnp.where(strict, _bdot_nt(kkt_h, bt_h), 0.0)
        l_k = jnp.where(strict, _bdot_nt(kkt_h, kt_h), 0.0)
        a_b = jnp.where(incl, _bdot_nt(rt_h, bt_h), 0.0)
        a_k = jnp.where(incl, _bdot_nt(rt_h, kt_h), 0.0)
        rhs = _bdot_nt(kkt_h, s0) + _bdot(l_k, v_h)
        pw = -l_b
        inv = eye + pw
        for _ in range(int(math.log2(c)) - 1):
            pw = _hdot(pw, pw)
            inv = inv + _hdot(inv, pw)
        y = _hdot(inv, rhs)
        o_ref[0, :, sl] = _bdot_nt(rt_h, s0) - _bdot(a_b, y) + _bdot(a_k, v_h)
        s_ref[h] = s0 * e_tot[:, sl] + _bdot_tn(v_h, kh[:, sl]) - _bdot_tn(y, bh[:, sl])


def _rwkv_scan(r, v, kk, lw, bmat, km):
    tok = pl.BlockSpec((CHUNK, RW_W), lambda b, d, n: (_seq_row_block(b, d, n), 0))
    drn = pl.BlockSpec((1, CHUNK, RW_W), lambda b, d, n: (d, _seq_row_block(b, d, n), 0))
    return pl.pallas_call(
        _rwkv_scan_kernel,
        grid=(NB, 2, SEQ_CHUNKS),
        in_specs=[tok, tok, tok, drn, drn, drn],
        out_specs=drn,
        out_shape=jax.ShapeDtypeStruct((2, ROWS, RW_W), F32),
        scratch_shapes=[pltpu.VMEM((RW_H, RW_D, RW_D), F32)],
        compiler_params=_cp("arbitrary", "arbitrary", "arbitrary"),
        name="rwkv_scan",
    )(r, v, kk, lw, bmat, km)


def _rwkv_out_kernel(o_ref, bonus_ref, g_ref, gng_ref, gnb_ref, bd_ref, y_ref):
    o = o_ref[0] + o_ref[1] + bonus_ref[...]
    bd = bd_ref[...] * (1.0 / RW_D)
    cen = o - _hdot(o, bd)
    var = _hdot(cen * cen, bd)
    y = cen * lax.rsqrt(var + RW_GN_EPS) * gng_ref[...] + gnb_ref[...]
    y_ref[...] = (y * g_ref[...]).astype(y_ref.dtype)


def _rwkv_out(o, bonus, g, gn_g, gn_b):
    tok = pl.BlockSpec((TILE, RW_W), lambda t: (t, 0))
    vec = pl.BlockSpec((1, RW_W), lambda t: (0, 0))
    return pl.pallas_call(
        _rwkv_out_kernel,
        grid=(N_TILES,),
        in_specs=[pl.BlockSpec((2, TILE, RW_W), lambda t: (0, t, 0)), tok, tok, vec, vec,
                  pl.BlockSpec((RW_W, RW_W), lambda t: (0, 0))],
        out_specs=tok,
        out_shape=jax.ShapeDtypeStruct((ROWS, RW_W), BF16),
        compiler_params=_cp("parallel"),
        name="rwkv_out",
    )(o, bonus, g, gn_g.reshape(1, RW_W), gn_b.reshape(1, RW_W), _head_block_ones(RW_W, RW_D))


def _mlstm_prep_kernel(z_ref, zp_ref, zn_ref, w_ref, o_ref):
    z = z_ref[...]
    zp, zn = _neighbours(z, zp_ref[...], zn_ref[...])
    y = _silu(zp * w_ref[0:1, :] + z * w_ref[1:2, :] + zn * w_ref[2:3, :])
    col = lax.broadcasted_iota(jnp.int32, y.shape, 1)
    o_ref[...] = jnp.where(col >= ML_W, y * (ML_D ** -0.5), y)


def _mlstm_prep(z, conv_w):
    return pl.pallas_call(
        _mlstm_prep_kernel,
        grid=(N_TILES,),
        in_specs=_tile_halo_specs(2 * ML_W) + [pl.BlockSpec((3, 2 * ML_W), lambda t: (0, 0))],
        out_specs=pl.BlockSpec((TILE, 2 * ML_W), lambda t: (t, 0)),
        out_shape=jax.ShapeDtypeStruct((ROWS, 2 * ML_W), F32),
        compiler_params=_cp("parallel"),
        name="mlstm_prep",
    )(z, z, z, conv_w)


def _log_sigmoid(x):
    return jnp.minimum(x, 0.0) - jnp.log(1.0 + jnp.exp(-jnp.abs(x)))


def _mlstm_scan_kernel(q_ref, k_ref, v_ref, gc_ref, gr_ref, bc_ref, br_ref, o_ref, c_ref, n_ref, m_ref):
    d = pl.program_id(1)
    n = pl.program_id(2)

    @pl.when(n == 0)
    def _():
        c_ref[...] = jnp.zeros_like(c_ref)
        n_ref[...] = jnp.zeros_like(n_ref)
        m_ref[...] = jnp.zeros_like(m_ref)

    c = CHUNK
    ti = lax.broadcasted_iota(jnp.int32, (c, c), 0)
    si = lax.broadcasted_iota(jnp.int32, (c, c), 1)
    delta = (ti - si) * (1 - 2 * d)
    mask = delta >= 0
    mask_t = delta <= 0
    gcol = gc_ref[0, 0] + bc_ref[0]
    grow = gr_ref[0, 0] + br_ref[0]
    for h in range(ML_H):
        sl = slice(h * ML_D, (h + 1) * ML_D)
        q = q_ref[:, sl]
        k = k_ref[:, sl]
        v = v_ref[:, sl]
        i_col = gcol[:, h:h + 1]
        f_col = _log_sigmoid(gcol[:, ML_H + h:ML_H + h + 1])
        i_row = grow[h:h + 1, :]
        f_row = _log_sigmoid(grow[ML_H + h:ML_H + h + 1, :])
        c_mat = c_ref[h]
        n_vec = n_ref[h]
        m_prev = m_ref[h][0:1, 0:1]
        cum_col = jnp.sum(jnp.where(mask, f_row, 0.0), axis=1, keepdims=True)
        cum_row = jnp.sum(jnp.where(mask_t, f_col, 0.0), axis=0, keepdims=True)
        log_w = jnp.where(mask, cum_col - cum_row + i_row, ML_NEG)
        m_inter = cum_col + m_prev
        m_t = jnp.maximum(jnp.max(log_w, axis=1, keepdims=True), m_inter)
        s = _bdot_nt(q, k) * jnp.exp(log_w - m_t)
        w_inter = jnp.exp(m_inter - m_t)
        num = _bdot(s, v) + w_inter * _bdot_nt(q, c_mat)
        den = jnp.sum(s, axis=1, keepdims=True) + w_inter * jnp.sum(q * n_vec, axis=1, keepdims=True)
        o_ref[0, :, sl] = num / jnp.maximum(jnp.abs(den), jnp.exp(-m_t))
        total = jnp.sum(f_row, axis=1, keepdims=True)
        log_src = total - cum_col + i_col
        m_new = jnp.maximum(total + m_prev, jnp.max(log_src, axis=0, keepdims=True))
        src = jnp.exp(log_src - m_new)
        decay = jnp.exp(total + m_prev - m_new)
        c_ref[h] = decay * c_mat + _bdot_tn(v * src, k)
        n_ref[h] = decay * n_vec + jnp.sum(src * k, axis=0, keepdims=True)
        m_ref[h] = jnp.broadcast_to(m_new, m_ref.shape[1:])


def _mlstm_scan(qk, z, gcol, grow, bcol, brow):
    def tok(col_blk):
        return pl.BlockSpec((CHUNK, ML_W), lambda b, d, n: (_seq_row_block(b, d, n), col_blk))

    return pl.pallas_call(
        _mlstm_scan_kernel,
        grid=(NB, 2, SEQ_CHUNKS),
        in_specs=[tok(0), tok(1), tok(2),
                  pl.BlockSpec((1, 1, CHUNK, 2 * ML_H), lambda b, d, n: (d, _seq_row_block(b, d, n), 0, 0)),
                  pl.BlockSpec((1, 1, 2 * ML_H, CHUNK), lambda b, d, n: (d, _seq_row_block(b, d, n), 0, 0)),
                  pl.BlockSpec((1, 1, 2 * ML_H), lambda b, d, n: (d, 0, 0)),
                  pl.BlockSpec((1, 2 * ML_H, 1), lambda b, d, n: (d, 0, 0))],
        out_specs=pl.BlockSpec((1, CHUNK, ML_W), lambda b, d, n: (d, _seq_row_block(b, d, n), 0)),
        out_shape=jax.ShapeDtypeStruct((2, ROWS, ML_W), F32),
        scratch_shapes=[pltpu.VMEM((ML_H, ML_D, ML_D), F32), pltpu.VMEM((ML_H, 1, ML_D), F32),
                        pltpu.VMEM((ML_H, 8, 128), F32)],
        compiler_params=_cp("arbitrary", "arbitrary", "arbitrary"),
        name="mlstm_scan",
    )(qk, qk, z, gcol, grow, bcol, brow)


def _mlstm_out_kernel(h_ref, og_ref, gng_ref, y_ref):
    hsum = _sigmoid(og_ref[...]) * (h_ref[0] + h_ref[1])
    for h in range(ML_H):
        sl = slice(h * ML_D, (h + 1) * ML_D)
        x = hsum[:, sl]
        cen = x - jnp.mean(x, axis=-1, keepdims=True)
        var = jnp.mean(cen * cen, axis=-1, keepdims=True)
        y_ref[:, sl] = (cen * lax.rsqrt(var + EPS) * gng_ref[:, sl]).astype(y_ref.dtype)


def _mlstm_out(hs, z, gn_g):
    return pl.pallas_call(
        _mlstm_out_kernel,
        grid=(N_TILES,),
        in_specs=[pl.BlockSpec((2, TILE, ML_W), lambda t: (0, t, 0)),
                  pl.BlockSpec((TILE, ML_W), lambda t: (t, 3)),
                  pl.BlockSpec((1, ML_W), lambda t: (0, 0))],
        out_specs=pl.BlockSpec((TILE, ML_W), lambda t: (t, 0)),
        out_shape=jax.ShapeDtypeStruct((ROWS, ML_W), BF16),
        compiler_params=_cp("parallel"),
        name="mlstm_out",
    )(hs, z, gn_g.reshape(1, ML_W))


def _mlstm(z_main, z_gates, conv_w, i_b, f_b, gn_g):
    qk = _mlstm_prep(z_main, conv_w)
    n_chunks = ROWS // CHUNK
    gates = z_gates[:, :4 * ML_H].reshape(n_chunks, CHUNK, 2, 2, ML_H)
    gcol = jnp.transpose(gates, (3, 0, 1, 2, 4)).reshape(2, n_chunks, CHUNK, 2 * ML_H)
    grow = jnp.swapaxes(gcol, 2, 3)
    bias = jnp.concatenate([i_b, f_b], axis=-1)
    hs = _mlstm_scan(qk, z_main, gcol, grow, bias.reshape(2, 1, 2 * ML_H), bias.reshape(2, 2 * ML_H, 1))
    return _mlstm_out(hs, z_main, gn_g)


def _s5_matrices(lam_re, lam_im, log_step, b_re, b_im, c_re, c_im, reverse):
    dt = jnp.exp(log_step)[:, None]
    mag = jnp.exp(lam_re * dt)
    a_re = mag * jnp.cos(lam_im * dt)
    a_im = mag * jnp.sin(lam_im * dt)
    den = lam_re * lam_re + lam_im * lam_im
    f_re = ((a_re - 1) * lam_re + a_im * lam_im) / den
    f_im = (a_im * lam_re - (a_re - 1) * lam_im) / den
    bb_re = f_re[..., None] * b_re - f_im[..., None] * b_im
    bb_im = f_re[..., None] * b_im + f_im[..., None] * b_re
    pr = [jnp.ones_like(a_re)]
    pi = [jnp.zeros_like(a_im)]
    for _ in range(S5_L):
        pr.append(pr[-1] * a_re - pi[-1] * a_im)
        pi.append(pr[-2] * a_im + pi[-1] * a_re)
    pr = jnp.stack(pr)
    pi = jnp.stack(pi)
    wr = pr[..., None] * bb_re - pi[..., None] * bb_im
    wi = pr[..., None] * bb_im + pi[..., None] * bb_re
    kern = (jnp.einsum("gcp,tgpk->tgck", c_re, wr, precision=HP)
            - jnp.einsum("gcp,tgpk->tgck", c_im, wi, precision=HP))
    j = np.arange(S5_L)[:, None]
    t = np.arange(S5_L)[None, :]
    lag = (j - t) if reverse else (t - j)
    valid = jnp.asarray((lag >= 0).astype(np.float32))
    m = kern[np.clip(lag, 0, S5_L)] * valid[:, :, None, None, None]
    m = jnp.transpose(m, (2, 0, 4, 1, 3)).reshape(S5_G, S5_L * S5_C, S5_L * S5_C)
    ex_in = np.arange(S5_L) if reverse else S5_L - 1 - np.arange(S5_L)
    e = jnp.concatenate([jnp.transpose(wr[ex_in], (1, 0, 3, 2)), jnp.transpose(wi[ex_in], (1, 0, 3, 2))],
                        axis=-1).reshape(S5_G, S5_L * S5_C, 2 * S5_P)
    ex_out = S5_L - np.arange(S5_L) if reverse else np.arange(S5_L) + 1
    pr_o = pr[ex_out][:, :, None, :]
    pi_o = pi[ex_out][:, :, None, :]
    fr = c_re[None] * pr_o - c_im[None] * pi_o
    fi = -(c_re[None] * pi_o + c_im[None] * pr_o)
    f = jnp.concatenate([jnp.transpose(fr, (1, 3, 0, 2)), jnp.transpose(fi, (1, 3, 0, 2))],
                        axis=1).reshape(S5_G, 2 * S5_P, S5_L * S5_C)
    return jnp.concatenate([m, e], axis=-1), f, pr[S5_L], pi[S5_L]


def _s5_local_kernel(u_ref, me_ref, y_ref, x_ref):
    res = _hdot(u_ref[0], me_ref[0, 0])
    y_ref[0, 0] = res[:, :S5_L * S5_C]
    x_ref[0, 0] = res[:, S5_L * S5_C:]


def _s5_carry_kernel(xc_ref, ar_ref, ai_ref, x0_ref, st_ref):
    @pl.when(pl.program_id(1) == 0)
    def _():
        st_ref[...] = jnp.zeros_like(st_ref)

    ar = ar_ref[0]
    ai = ai_ref[0]
    x = st_ref[...]
    for i in range(xc_ref.shape[1]):
        x0_ref[0, i] = x
        x = x * ar + pltpu.roll(x, S5_P, 1) * ai + xc_ref[0, i]
    st_ref[...] = x


def _s5_state_kernel(y_ref, x0_ref, f_ref, o_ref):
    contrib = y_ref[0, 0] + _hdot(x0_ref[0, 0], f_ref[0, 0])

    @pl.when(pl.program_id(1) == 0)
    def _():
        o_ref[0] = contrib

    @pl.when(pl.program_id(1) == 1)
    def _():
        o_ref[0] = o_ref[0] + contrib


def _s5_glu_kernel(y_ref, u_ref, d_ref, w_ref, o_ref):
    x = y_ref[...] + u_ref[...] * d_ref[...]
    ge = 0.5 * x * (1.0 + jnp.tanh(math.sqrt(2.0 / math.pi) * (x + 0.044715 * (x * x * x))))
    p = _bdot(ge, w_ref[...])
    o_ref[...] = (p[:, :S5_W] * _sigmoid(p[:, S5_W:])).astype(o_ref.dtype)


def _s5(z, lam_re, lam_im, log_step, b_re, b_im, c_re, c_im, d_skip, w_glu):
    n_ch = (SEQ + CTX) // S5_L
    n_ctx_ch = CTX // S5_L
    rows = NB * n_ch
    mats = [_s5_matrices(lam_re[d], lam_im[d], log_step[d], b_re[d], b_im[d], c_re[d], c_im[d], d == 1)
            for d in range(2)]
    me = jnp.stack([m[0] for m in mats])
    f = jnp.stack([m[1] for m in mats])
    a_re = jnp.stack([m[2] for m in mats])
    a_im = jnp.stack([m[3] for m in mats])

    def to_groups(x):
        x = x.reshape(NB, -1, S5_L, S5_G, S5_C)
        return jnp.transpose(x, (3, 0, 1, 2, 4)).reshape(S5_G, NB, -1, S5_L * S5_C)

    u = jnp.concatenate([to_groups(z[N_LAT:]), to_groups(z[:N_LAT])], axis=2).reshape(S5_G, rows, S5_L * S5_C)
    y_loc, x_in = pl.pallas_call(
        _s5_local_kernel,
        grid=(2, S5_G),
        in_specs=[pl.BlockSpec((1, rows, S5_L * S5_C), lambda d, g: (g, 0, 0)),
                  pl.BlockSpec((1, 1, S5_L * S5_C, S5_L * S5_C + 2 * S5_P), lambda d, g: (d, g, 0, 0))],
        out_specs=[pl.BlockSpec((1, 1, rows, S5_L * S5_C), lambda d, g: (d, g, 0, 0)),
                   pl.BlockSpec((1, 1, rows, 2 * S5_P), lambda d, g: (d, g, 0, 0))],
        out_shape=[jax.ShapeDtypeStruct((2, S5_G, rows, S5_L * S5_C), F32),
                   jax.ShapeDtypeStruct((2, S5_G, rows, 2 * S5_P), F32)],
        compiler_params=_cp("parallel", "parallel"),
        name="s5_local",
    )(u, me)

    perm_b = np.concatenate([np.arange(n_ctx_ch)[::-1], np.arange(n_ctx_ch, n_ch)[::-1]])
    inv_b = np.argsort(perm_b)
    x_in = x_in.reshape(2, S5_G, NB, n_ch, 2 * S5_P)
    x_ord = jnp.stack([x_in[0], x_in[1][:, :, perm_b]])
    x_ord = jnp.transpose(x_ord, (0, 3, 1, 2, 4)).reshape(2, n_ch, S5_G * NB, 2 * S5_P)
    coef_r = jnp.repeat(jnp.concatenate([a_re, a_re], axis=-1), NB, axis=1)
    coef_i = jnp.repeat(jnp.concatenate([-a_im, a_im], axis=-1), NB, axis=1)
    step = 16
    x0 = pl.pallas_call(
        _s5_carry_kernel,
        grid=(2, n_ch // step),
        in_specs=[pl.BlockSpec((1, step, S5_G * NB, 2 * S5_P), lambda d, i: (d, i, 0, 0)),
                  pl.BlockSpec((1, S5_G * NB, 2 * S5_P), lambda d, i: (d, 0, 0)),
                  pl.BlockSpec((1, S5_G * NB, 2 * S5_P), lambda d, i: (d, 0, 0))],
        out_specs=pl.BlockSpec((1, step, S5_G * NB, 2 * S5_P), lambda d, i: (d, i, 0, 0)),
        out_shape=jax.ShapeDtypeStruct((2, n_ch, S5_G * NB, 2 * S5_P), F32),
        scratch_shapes=[pltpu.VMEM((S5_G * NB, 2 * S5_P), F32)],
        compiler_params=_cp("arbitrary", "arbitrary"),
        name="s5_carry",
    )(x_ord, coef_r, coef_i)
    x0 = jnp.transpose(x0.reshape(2, n_ch, S5_G, NB, 2 * S5_P), (0, 2, 3, 1, 4))
    x0 = jnp.stack([x0[0], x0[1][:, :, inv_b]]).reshape(2, S5_G, rows, 2 * S5_P)

    y = pl.pallas_call(
        _s5_state_kernel,
        grid=(S5_G, 2),
        in_specs=[pl.BlockSpec((1, 1, rows, S5_L * S5_C), lambda g, d: (d, g, 0, 0)),
                  pl.BlockSpec((1, 1, rows, 2 * S5_P), lambda g, d: (d, g, 0, 0)),
                  pl.BlockSpec((1, 1, 2 * S5_P, S5_L * S5_C), lambda g, d: (d, g, 0, 0))],
        out_specs=pl.BlockSpec((1, rows, S5_L * S5_C), lambda g, d: (g, 0, 0)),
        out_shape=jax.ShapeDtypeStruct((S5_G, rows, S5_L * S5_C), F32),
        compiler_params=_cp("parallel", "arbitrary"),
        name="s5_state",
    )(y_loc, x0, f)

    y = jnp.transpose(y.reshape(S5_G, NB, n_ch, S5_L, S5_C), (1, 2, 3, 0, 4)).reshape(NB, n_ch * S5_L, S5_W)
    y = jnp.concatenate([y[:, CTX:].reshape(N_LAT, S5_W), y[:, :CTX].reshape(N_CTX, S5_W)], axis=0)
    tok = pl.BlockSpec((TILE, S5_W), lambda t: (t, 0))
    return pl.pallas_call(
        _s5_glu_kernel,
        grid=(N_TILES,),
        in_specs=[tok, tok, pl.BlockSpec((1, S5_W), lambda t: (0, 0)),
                  pl.BlockSpec((S5_W, 2 * S5_W), lambda t: (0, 0))],
        out_specs=tok,
        out_shape=jax.ShapeDtypeStruct((ROWS, S5_W), BF16),
        compiler_params=_cp("parallel"),
        name="s5_glu",
    )(y, z, d_skip.reshape(1, S5_W), w_glu)


def kernel(x, c, ctx, c_ctx, w_mod, b_mod, norm1_g, norm2_g, w_in, b_gate, q_norm_g, k_norm_g, rwkv_mu, rwkv_w0, rwkv_w_up, rwkv_a0, rwkv_a_up, rwkv_g_up, rwkv_k_k, rwkv_k_a, rwkv_r_k, rwkv_gn_g, rwkv_gn_b, mlstm_conv_w, mlstm_i_b, mlstm_f_b, mlstm_gn_g, s5_lam_re, s5_lam_im, s5_log_step, s5_b_re, s5_b_im, s5_c_re, s5_c_im, s5_d, s5_w_glu, w_br_attn, w_br_rwkv, w_br_mlstm, w_br_s5, w_out, w_ffn_in, w_ffn_out, final_norm_g):
    cos, sin = _rope_tables()
    xs = jnp.concatenate([x.reshape(N_LAT, D), ctx.reshape(N_CTX, D)], axis=0)
    c_all = jnp.concatenate([c, c_ctx[None], jnp.zeros((3, D), F32)], axis=0)
    off = np.cumsum([0, ATTN_IN, RW_IN, ML_IN, S5_W, GATE_IN])
    tm = 1024
    for l in range(DEPTH):
        last = l == DEPTH - 1
        n_rows = N_LAT if last else ROWS
        mod = _modulation(c_all, w_mod[l], b_mod[l])
        mod_t = _mod_rows(mod, TILE)
        h = _norm_mod(xs, norm1_g[l], mod_t, 0, 1, N_TILES)
        wl = w_in[l]
        w_attn = wl[:, off[0]:off[1]].astype(BF16)
        w_rwkv = wl[:, off[1]:off[2]].astype(BF16)
        w_ml = wl[:, off[2]:off[2] + 4 * ML_W].astype(BF16)
        w_mlg = jnp.pad(wl[:, off[2] + 4 * ML_W:off[3]], ((0, 0), (0, 128 - 4 * ML_H))).astype(BF16)
        w_s5 = wl[:, off[3]:off[4]].astype(BF16)
        w_gate = wl[:, off[4]:off[5]].astype(BF16)
        z_attn = _mm(h, w_attn, tm, 512)
        z_rwkv = _mm(h, w_rwkv, tm, 896)
        z_ml = _mm(h, w_ml, tm, 1024)
        z_mlg = _mm(h, w_mlg, tm, 128)
        z_s5 = _mm(h, w_s5, tm, 512)
        z_gate = _mm(h, w_gate, tm, 1024)

        ya = _attention(z_attn, q_norm_g[l], k_norm_g[l], cos, sin)
        rp = dict(mu=rwkv_mu[l], w0=rwkv_w0[l], w_up=rwkv_w_up[l], a0=rwkv_a0[l], a_up=rwkv_a_up[l],
                  g_up=rwkv_g_up[l], k_k=rwkv_k_k[l], k_a=rwkv_k_a[l], r_k=rwkv_r_k[l].reshape(RW_W))
        r, v, kk, g, bonus, lw, bmat, km = _rwkv_prep(z_rwkv, rp)
        yr = _rwkv_out(_rwkv_scan(r, v, kk, lw, bmat, km), bonus, g, rwkv_gn_g[l], rwkv_gn_b[l])
        ym = _mlstm(z_ml, z_mlg, mlstm_conv_w[l], mlstm_i_b[l], mlstm_f_b[l], mlstm_gn_g[l])
        ys = _s5(z_s5, s5_lam_re[l], s5_lam_im[l], s5_log_step[l], s5_b_re[l], s5_b_im[l],
                 s5_c_re[l], s5_c_im[l], s5_d[l], s5_w_glu[l])

        y = _merge(ya, yr, ym, ys, z_gate, b_gate[l], w_br_attn[l], w_br_rwkv[l], w_br_mlstm[l], w_br_s5[l],
                   tm, 512, n_rows)
        xs = _mm_res(y, w_out[l], xs, mod, 2, tm, 1024, n_rows)
        h2 = _norm_mod(xs, norm2_g[l], mod_t, 3, 4, n_rows // TILE)
        u = _ffn_in(h2, w_ffn_in[l], tm, 512, n_rows)
        xs = _mm_res(u, w_ffn_out[l], xs, mod, 5, 512, 512, n_rows)
    return _final_norm(xs, final_norm_g).reshape(NB, SEQ, D)
```

```python
import functools
import math

import numpy as np
import jax
import jax.numpy as jnp
from jax import lax
from jax.experimental import pallas as pl
from jax.experimental.pallas import tpu as pltpu

F32 = jnp.float32
BF16 = jnp.bfloat16
HP = lax.Precision.HIGHEST

D = 2048
NB = 4
SEQ = 2048
CTX = 256
DEPTH = 2
N_LAT = NB * SEQ
N_CTX = NB * CTX
ROWS = N_LAT + N_CTX
EPS = 1e-6
GRID_W = 64

HEAD_DIM = 128
ATTN_HEADS = 8
ATTN_KV = 2
ROPE_THETA = 10000.0
ATTN_Q = ATTN_HEADS * HEAD_DIM
ATTN_IN = (ATTN_HEADS + 2 * ATTN_KV) * HEAD_DIM

RW_H = 8
RW_D = 64
RW_W = 512
RW_IN = 3 * RW_W + 64 + 64 + 128
RW_DECAY = math.exp(-0.5)
RW_GN_EPS = 64e-5

ML_H = 4
ML_D = 128
ML_W = 512
ML_NEG = -1e30
ML_IN = 4 * ML_W + 4 * ML_H

S5_W = 512
S5_C = 16
S5_G = 32
S5_P = 64
S5_L = 16

FFN_H = 5632
GATE_IN = 4 * D

CHUNK = 64
TILE = 256
N_TILES = ROWS // TILE
SEQ_CHUNKS = (SEQ + CTX) // CHUNK
CTX_CHUNKS = CTX // CHUNK
LAT_CHUNKS = SEQ // CHUNK

VMEM_LIMIT_BYTES = 56 * 1024 * 1024


def _cp(*sem):
    return pltpu.CompilerParams(dimension_semantics=sem, vmem_limit_bytes=VMEM_LIMIT_BYTES)


def _bdot(a, b):
    return jnp.dot(a.astype(BF16), b.astype(BF16), preferred_element_type=F32)


def _bdot_nt(a, b):
    return lax.dot_general(a.astype(BF16), b.astype(BF16), (((1,), (1,)), ((), ())),
                           preferred_element_type=F32)


def _bdot_tn(a, b):
    return lax.dot_general(a.astype(BF16), b.astype(BF16), (((0,), (0,)), ((), ())),
                           preferred_element_type=F32)


def _hdot(a, b):
    return jnp.dot(a, b, precision=HP, preferred_element_type=F32)


def _sigmoid(x):
    return 1.0 / (1.0 + jnp.exp(-x))


def _silu(x):
    return x * _sigmoid(x)


def _mod_kernel(c_ref, w_ref, b_ref, o_ref):
    o_ref[...] = _hdot(_silu(c_ref[...]), w_ref[...]) + b_ref[...]


def _modulation(c_all, w, b):
    tn = 1024
    return pl.pallas_call(
        _mod_kernel,
        grid=(6 * D // tn,),
        in_specs=[pl.BlockSpec((8, D), lambda j: (0, 0)),
                  pl.BlockSpec((D, tn), lambda j: (0, j)),
                  pl.BlockSpec((1, tn), lambda j: (0, j))],
        out_specs=pl.BlockSpec((8, tn), lambda j: (0, j)),
        out_shape=jax.ShapeDtypeStruct((8, 6 * D), F32),
        compiler_params=_cp("arbitrary"),
        name="modulation",
    )(c_all, w, b.reshape(1, 6 * D))


def _mod_rows(mod, tm):
    starts = np.arange(ROWS // tm) * tm
    idx = np.where(starts < N_LAT, starts // SEQ, NB)
    return mod[idx][:, None, :]


def _norm_mod_kernel(x_ref, g_ref, sh_ref, sc_ref, o_ref):
    x = x_ref[...]
    y = x * lax.rsqrt(jnp.mean(x * x, axis=-1, keepdims=True) + EPS) * g_ref[...]
    o_ref[...] = (y * (1.0 + sc_ref[0]) + sh_ref[0]).astype(o_ref.dtype)


def _norm_mod(x, g, modr, shift_blk, scale_blk, n_tiles):
    return pl.pallas_call(
        _norm_mod_kernel,
        grid=(n_tiles,),
        in_specs=[pl.BlockSpec((TILE, D), lambda i: (i, 0)),
                  pl.BlockSpec((1, D), lambda i: (0, 0)),
                  pl.BlockSpec((1, 1, D), lambda i: (i, 0, shift_blk)),
                  pl.BlockSpec((1, 1, D), lambda i: (i, 0, scale_blk))],
        out_specs=pl.BlockSpec((TILE, D), lambda i: (i, 0)),
        out_shape=jax.ShapeDtypeStruct((n_tiles * TILE, D), BF16),
        compiler_params=_cp("parallel"),
        name="norm_mod",
    )(x, g.reshape(1, D), modr, modr)


def _mm_kernel(a_ref, w_ref, o_ref):
    o_ref[...] = jnp.dot(a_ref[...], w_ref[...], preferred_element_type=F32).astype(o_ref.dtype)


def _mm(a, w, tm, tn, out_dtype=F32):
    m, k = a.shape
    n = w.shape[1]
    return pl.pallas_call(
        _mm_kernel,
        grid=(n // tn, m // tm),
        in_specs=[pl.BlockSpec((tm, k), lambda j, i: (i, 0)),
                  pl.BlockSpec((k, tn), lambda j, i: (0, j))],
        out_specs=pl.BlockSpec((tm, tn), lambda j, i: (i, j)),
        out_shape=jax.ShapeDtypeStruct((m, n), out_dtype),
        compiler_params=_cp("parallel", "parallel"),
        name="matmul",
    )(a, w)


def _mm_res_kernel(a_ref, w_ref, x_ref, g_ref, o_ref, wb_ref):
    @pl.when(pl.program_id(1) == 0)
    def _():
        wb_ref[...] = w_ref[...].astype(BF16)

    y = jnp.dot(a_ref[...], wb_ref[...], preferred_element_type=F32)
    o_ref[...] = x_ref[...] + g_ref[0] * y


def _mm_res(a, w, x, mod, gate_blk, tm, tn, n_rows):
    modr = _mod_rows(mod, tm)
    k = a.shape[1]
    n = w.shape[1]
    gpb = D // tn
    return pl.pallas_call(
        _mm_res_kernel,
        grid=(n // tn, n_rows // tm),
        in_specs=[pl.BlockSpec((tm, k), lambda j, i: (i, 0)),
                  pl.BlockSpec((k, tn), lambda j, i: (0, j)),
                  pl.BlockSpec((tm, tn), lambda j, i: (i, j)),
                  pl.BlockSpec((1, 1, tn), lambda j, i: (i, 0, gate_blk * gpb + j))],
        out_specs=pl.BlockSpec((tm, tn), lambda j, i: (i, j)),
        out_shape=jax.ShapeDtypeStruct((n_rows, n), F32),
        scratch_shapes=[pltpu.VMEM((k, tn), BF16)],
        compiler_params=_cp("arbitrary", "arbitrary"),
        name="matmul_residual",
    )(a, w, x, modr)


def _ffn_in_kernel(a_ref, wa_ref, wb_ref, o_ref, wab_ref, wbb_ref):
    @pl.when(pl.program_id(1) == 0)
    def _():
        wab_ref[...] = wa_ref[...].astype(BF16)
        wbb_ref[...] = wb_ref[...].astype(BF16)

    a = a_ref[...]
    u = jnp.dot(a, wab_ref[...], preferred_element_type=F32)
    v = jnp.dot(a, wbb_ref[...], preferred_element_type=F32)
    o_ref[...] = (_silu(u) * v).astype(o_ref.dtype)


def _ffn_in(h, w, tm, tn, n_rows):
    nb = FFN_H // tn
    return pl.pallas_call(
        _ffn_in_kernel,
        grid=(nb, n_rows // tm),
        in_specs=[pl.BlockSpec((tm, D), lambda j, i: (i, 0)),
                  pl.BlockSpec((D, tn), lambda j, i: (0, j)),
                  pl.BlockSpec((D, tn), lambda j, i: (0, nb + j))],
        out_specs=pl.BlockSpec((tm, tn), lambda j, i: (i, j)),
        out_shape=jax.ShapeDtypeStruct((n_rows, FFN_H), BF16),
        scratch_shapes=[pltpu.VMEM((D, tn), BF16), pltpu.VMEM((D, tn), BF16)],
        compiler_params=_cp("arbitrary", "arbitrary"),
        name="ffn_in",
    )(h, w, w)


def _merge_kernel(ya_ref, yr_ref, ym_ref, ys_ref, ga_ref, gr_ref, gm_ref, gs_ref,
                  ba_ref, br_ref, bm_ref, bs_ref, wa_ref, wr_ref, wm_ref, ws_ref, o_ref,
                  wab_ref, wrb_ref, wmb_ref, wsb_ref):
    @pl.when(pl.program_id(1) == 0)
    def _():
        wab_ref[...] = wa_ref[...].astype(BF16)
        wrb_ref[...] = wr_ref[...].astype(BF16)
        wmb_ref[...] = wm_ref[...].astype(BF16)
        wsb_ref[...] = ws_ref[...].astype(BF16)

    acc = None
    for y_ref, g_ref, b_ref, w_ref in ((ya_ref, ga_ref, ba_ref, wab_ref), (yr_ref, gr_ref, br_ref, wrb_ref),
                                       (ym_ref, gm_ref, bm_ref, wmb_ref), (ys_ref, gs_ref, bs_ref, wsb_ref)):
        gate = _sigmoid(g_ref[...] + b_ref[...])
        term = gate * jnp.dot(y_ref[...], w_ref[...], preferred_element_type=F32)
        acc = term if acc is None else acc + term
    o_ref[...] = acc.astype(o_ref.dtype)


def _merge(ya, yr, ym, ys, z_gate, b_gate, wa, wr, wm, ws, tm, tn, n_rows):
    nb = D // tn
    b_gate = b_gate.reshape(1, GATE_IN)

    def act(width):
        return pl.BlockSpec((tm, width), lambda j, i: (i, 0))

    def gate(br):
        return pl.BlockSpec((tm, tn), lambda j, i: (i, br * nb + j))

    def bias(br):
        return pl.BlockSpec((1, tn), lambda j, i: (0, br * nb + j))

    def wgt(width):
        return pl.BlockSpec((width, tn), lambda j, i: (0, j))

    return pl.pallas_call(
        _merge_kernel,
        grid=(nb, n_rows // tm),
        in_specs=[act(ATTN_Q), act(RW_W), act(ML_W), act(S5_W),
                  gate(0), gate(1), gate(2), gate(3),
                  bias(0), bias(1), bias(2), bias(3),
                  wgt(ATTN_Q), wgt(RW_W), wgt(ML_W), wgt(S5_W)],
        out_specs=pl.BlockSpec((tm, tn), lambda j, i: (i, j)),
        out_shape=jax.ShapeDtypeStruct((n_rows, D), BF16),
        scratch_shapes=[pltpu.VMEM((ATTN_Q, tn), BF16), pltpu.VMEM((RW_W, tn), BF16),
                        pltpu.VMEM((ML_W, tn), BF16), pltpu.VMEM((S5_W, tn), BF16)],
        compiler_params=_cp("arbitrary", "arbitrary"),
        name="gated_merge",
    )(ya, yr, ym, ys, z_gate, z_gate, z_gate, z_gate, b_gate, b_gate, b_gate, b_gate, wa, wr, wm, ws)


def _final_norm_kernel(x_ref, g_ref, o_ref):
    x = x_ref[...]
    o_ref[...] = x * lax.rsqrt(jnp.mean(x * x, axis=-1, keepdims=True) + EPS) * g_ref[...]


def _final_norm(x, g):
    n_tiles = N_LAT // TILE
    return pl.pallas_call(
        _final_norm_kernel,
        grid=(n_tiles,),
        in_specs=[pl.BlockSpec((TILE, D), lambda i: (i, 0)),
                  pl.BlockSpec((1, D), lambda i: (0, 0))],
        out_specs=pl.BlockSpec((TILE, D), lambda i: (i, 0)),
        out_shape=jax.ShapeDtypeStruct((N_LAT, D), F32),
        compiler_params=_cp("parallel"),
        name="final_norm",
    )(x, g.reshape(1, D))


def _tile_halo_specs(width, col_blk=0):
    last = ROWS // 8 - 1
    per = TILE // 8
    return [pl.BlockSpec((TILE, width), lambda t: (t, col_blk)),
            pl.BlockSpec((8, width), lambda t: (jnp.maximum(t * per - 1, 0), col_blk)),
            pl.BlockSpec((8, width), lambda t: (jnp.minimum((t + 1) * per, last), col_blk))]


def _neighbours(z, prev_blk, next_blk):
    t = pl.program_id(0)
    pos = t % (SEQ // TILE)
    is_lat = t < N_LAT // TILE
    has_prev = jnp.logical_and(is_lat, pos > 0).astype(F32)
    has_next = jnp.logical_and(is_lat, pos < SEQ // TILE - 1).astype(F32)
    row = lax.broadcasted_iota(jnp.int32, z.shape, 0)
    zp = jnp.where(row == 0, prev_blk[7:8, :] * has_prev, pltpu.roll(z, 1, 0))
    zn = jnp.where(row == TILE - 1, next_blk[0:1, :] * has_next, pltpu.roll(z, TILE - 1, 0))
    return zp, zn


def _seq_row_block(b, d, n):
    ctx_c = jnp.where(d == 0, n, CTX_CHUNKS - 1 - n)
    lat_c = jnp.where(d == 0, n - CTX_CHUNKS, SEQ_CHUNKS - 1 - n)
    return jnp.where(n < CTX_CHUNKS, N_LAT // CHUNK + CTX_CHUNKS * b + ctx_c, LAT_CHUNKS * b + lat_c)


def _rope(x, cos, sin):
    lane = lax.broadcasted_iota(jnp.int32, x.shape, 1)
    first = (lane % 64) < 32
    partner = jnp.where(first, pltpu.roll(x, 96, 1), pltpu.roll(x, 32, 1))
    return x * cos + partner * sin


def _rms(x, g):
    return x * lax.rsqrt(jnp.mean(x * x, axis=-1, keepdims=True) + EPS) * g


def _attn_kernel(q_ref, kl_ref, kc_ref, vl_ref, vc_ref, cos_ref, sin_ref, cos_t_ref, sin_t_ref,
                 qg_ref, kg_ref, o_ref, klb_ref, kcb_ref, vlb_ref, vcb_ref):
    qi = pl.program_id(2)
    n_lat_tiles = SEQ // TILE

    @pl.when(qi == 0)
    def _():
        kg = kg_ref[...]
        klb_ref[...] = _rope(_rms(kl_ref[...], kg), cos_ref[...], sin_ref[...]).astype(BF16)
        kcb_ref[...] = _rms(kc_ref[...], kg).astype(BF16)
        vlb_ref[...] = vl_ref[...].astype(BF16)
        vcb_ref[...] = vc_ref[...].astype(BF16)

    scale = HEAD_DIM ** -0.5

    def heads(latent):
        for h in range(ATTN_HEADS // ATTN_KV):
            sl = slice(h * HEAD_DIM, (h + 1) * HEAD_DIM)
            q = _rms(q_ref[:, sl], qg_ref[...])
            if latent:
                q = _rope(q, cos_t_ref[...], sin_t_ref[...])
            q = (q * scale).astype(BF16)
            s_c = lax.dot_general(q, kcb_ref[...], (((1,), (1,)), ((), ())), preferred_element_type=F32)
            m = jnp.max(s_c, axis=-1, keepdims=True)
            if latent:
                s_l = lax.dot_general(q, klb_ref[...], (((1,), (1,)), ((), ())), preferred_element_type=F32)
                m = jnp.maximum(m, jnp.max(s_l, axis=-1, keepdims=True))
            p_c = jnp.exp(s_c - m)
            den = jnp.sum(p_c, axis=-1, keepdims=True)
            acc = jnp.dot(p_c.astype(BF16), vcb_ref[...], preferred_element_type=F32)
            if latent:
                p_l = jnp.exp(s_l - m)
                den = den + jnp.sum(p_l, axis=-1, keepdims=True)
                acc = acc + jnp.dot(p_l.astype(BF16), vlb_ref[...], preferred_element_type=F32)
            o_ref[:, sl] = (acc / den).astype(o_ref.dtype)

    @pl.when(qi < n_lat_tiles)
    def _():
        heads(True)

    @pl.when(qi == n_lat_tiles)
    def _():
        heads(False)


def _rope_tables():
    rows = SEQ // GRID_W
    row = jnp.repeat(jnp.arange(rows, dtype=F32), GRID_W)
    col = jnp.tile(jnp.arange(GRID_W, dtype=F32), rows)
    axis_dim = HEAD_DIM // 2
    inv_freq = ROPE_THETA ** (-jnp.arange(0, axis_dim, 2, dtype=F32) / axis_dim)
    ang_r = row[:, None] * inv_freq[None]
    ang_c = col[:, None] * inv_freq[None]
    cos = jnp.concatenate([jnp.cos(ang_r), jnp.cos(ang_r), jnp.cos(ang_c), jnp.cos(ang_c)], axis=-1)
    sin = jnp.concatenate([-jnp.sin(ang_r), jnp.sin(ang_r), -jnp.sin(ang_c), jnp.sin(ang_c)], axis=-1)
    return cos, sin


def _attention(z, q_g, k_g, cos, sin):
    n_lat_tiles = SEQ // TILE
    qw = ATTN_Q // ATTN_KV
    kcol = ATTN_Q // HEAD_DIM
    vcol = kcol + ATTN_KV

    def q_row(b, g, qi):
        return jnp.where(qi < n_lat_tiles, n_lat_tiles * b + qi, N_LAT // TILE + b)

    return pl.pallas_call(
        _attn_kernel,
        grid=(NB, ATTN_KV, n_lat_tiles + 1),
        in_specs=[pl.BlockSpec((TILE, qw), lambda b, g, qi: (q_row(b, g, qi), g)),
                  pl.BlockSpec((SEQ, HEAD_DIM), lambda b, g, qi: (b, kcol + g)),
                  pl.BlockSpec((CTX, HEAD_DIM), lambda b, g, qi: (N_LAT // CTX + b, kcol + g)),
                  pl.BlockSpec((SEQ, HEAD_DIM), lambda b, g, qi: (b, vcol + g)),
                  pl.BlockSpec((CTX, HEAD_DIM), lambda b, g, qi: (N_LAT // CTX + b, vcol + g)),
                  pl.BlockSpec((SEQ, HEAD_DIM), lambda b, g, qi: (0, 0)),
                  pl.BlockSpec((SEQ, HEAD_DIM), lambda b, g, qi: (0, 0)),
                  pl.BlockSpec((TILE, HEAD_DIM), lambda b, g, qi: (jnp.minimum(qi, n_lat_tiles - 1), 0)),
                  pl.BlockSpec((TILE, HEAD_DIM), lambda b, g, qi: (jnp.minimum(qi, n_lat_tiles - 1), 0)),
                  pl.BlockSpec((1, HEAD_DIM), lambda b, g, qi: (0, 0)),
                  pl.BlockSpec((1, HEAD_DIM), lambda b, g, qi: (0, 0))],
        out_specs=pl.BlockSpec((TILE, qw), lambda b, g, qi: (q_row(b, g, qi), g)),
        out_shape=jax.ShapeDtypeStruct((ROWS, ATTN_Q), BF16),
        scratch_shapes=[pltpu.VMEM((SEQ, HEAD_DIM), BF16), pltpu.VMEM((CTX, HEAD_DIM), BF16),
                        pltpu.VMEM((SEQ, HEAD_DIM), BF16), pltpu.VMEM((CTX, HEAD_DIM), BF16)],
        compiler_params=_cp("arbitrary", "arbitrary", "arbitrary"),
        name="attention",
    )(z, z, z, z, z, cos, sin, cos, sin, q_g.reshape(1, HEAD_DIM), k_g.reshape(1, HEAD_DIM))


def _rwkv_prep_kernel(z_ref, zp_ref, zn_ref, mu_ref, w0_ref, wup_ref, a0_ref, aup_ref, gup_ref,
                      kk_ref, ka_ref, rk_ref, bd_ref,
                      r_out, v_out, kkn_out, g_out, bonus_out, lw_out, b_out, km_out):
    z = z_ref[...]
    zp, zn = _neighbours(z, zp_ref[...], zn_ref[...])
    zs = z + mu_ref[...] * (0.5 * (zp + zn) - z)
    r = zs[:, 0:RW_W]
    k = zs[:, RW_W:2 * RW_W]
    v = zs[:, 2 * RW_W:3 * RW_W]
    w_lo = zs[:, 3 * RW_W:3 * RW_W + 64]
    a_lo = zs[:, 3 * RW_W + 64:3 * RW_W + 128]
    g_lo = zs[:, 3 * RW_W + 128:3 * RW_W + 256]
    bd = bd_ref[...]
    kk = k * kk_ref[...]
    kk = kk * lax.rsqrt(_hdot(kk * kk, bd) + 1e-12)
    r_out[...] = r
    v_out[...] = v
    kkn_out[...] = kk
    g_out[...] = _hdot(_sigmoid(g_lo), gup_ref[...])
    tw = jnp.tanh(w_lo)
    km_sum = None
    for d in range(2):
        lw = -RW_DECAY * _sigmoid(w0_ref[d] + _hdot(tw, wup_ref[d]))
        a = _sigmoid(a0_ref[d] + _hdot(a_lo, aup_ref[d]))
        km = k * (1.0 + (a - 1.0) * ka_ref[...])
        lw_out[d] = lw
        b_out[d] = a * kk
        km_out[d] = km
        km_sum = km if km_sum is None else km_sum + km
    bonus_out[...] = _hdot(r * km_sum * rk_ref[...], bd) * v


def _rwkv_prep(z, p):
    row = lambda a: a.reshape(1, -1)
    full = lambda shape: pl.BlockSpec(shape, lambda t: (0,) * len(shape))
    out_tok = pl.BlockSpec((TILE, RW_W), lambda t: (t, 0))
    out_dir = pl.BlockSpec((2, TILE, RW_W), lambda t: (0, t, 0))
    tok = jax.ShapeDtypeStruct((ROWS, RW_W), F32)
    drn = jax.ShapeDtypeStruct((2, ROWS, RW_W), F32)
    return pl.pallas_call(
        _rwkv_prep_kernel,
        grid=(N_TILES,),
        in_specs=_tile_halo_specs(RW_IN) + [
            full((1, RW_IN)), full((2, 1, RW_W)), full((2, 64, RW_W)), full((2, 1, RW_W)),
            full((2, 64, RW_W)), full((128, RW_W)), full((1, RW_W)), full((1, RW_W)), full((1, RW_W)),
            full((RW_W, RW_W))],
        out_specs=[out_tok] * 5 + [out_dir] * 3,
        out_shape=[tok] * 5 + [drn] * 3,
        compiler_params=_cp("parallel"),
        name="rwkv_prep",
    )(z, z, z, row(p["mu"]), p["w0"].reshape(2, 1, RW_W), p["w_up"], p["a0"].reshape(2, 1, RW_W),
      p["a_up"], p["g_up"], row(p["k_k"]), row(p["k_a"]), row(p["r_k"]), _head_block_ones(RW_W, RW_D))


def _head_block_ones(width, head):
    idx = np.arange(width) // head
    return jnp.asarray((idx[:, None] == idx[None, :]).astype(np.float32))


def _rwkv_chunk_kernel(r_ref, v_ref, kk_ref, lw_ref, b_ref, km_ref, p_ref, sl_ref, re_ref, ol_ref):
    d = pl.program_id(0)
    c = CHUNK
    ti = lax.broadcasted_iota(jnp.int32, (c, c), 0)
    si = lax.broadcasted_iota(jnp.int32, (c, c), 1)
    delta = (ti - si) * (1 - 2 * d)
    incl = delta >= 0
    strict = delta > 0
    eye = jnp.where(ti == si, 1.0, 0.0)

    lw = lw_ref[0]
    cs = _hdot(jnp.where(incl, 1.0, 0.0), lw)
    tot = jnp.sum(lw, axis=0, keepdims=True)
    r = r_ref[...]
    v = v_ref[...]
    kk = kk_ref[...]
    bb = b_ref[0]
    km = km_ref[0]
    e_neg = jnp.exp(-cs)
    e_rem = jnp.exp(tot - cs)
    kkt = kk * jnp.exp(cs - lw)
    rt = r * jnp.exp(cs)
    bt = bb * e_neg
    kt = km * e_neg
    bh = bb * e_rem
    kh = km * e_rem
    e_tot = jnp.exp(tot)

    heads = range(RW_H)
    sls = [slice(h * RW_D, (h + 1) * RW_D) for h in heads]
    l_b = [jnp.where(strict, _bdot_nt(kkt[:, s], bt[:, s]), 0.0) for s in sls]
    l_k = [jnp.where(strict, _bdot_nt(kkt[:, s], kt[:, s]), 0.0) for s in sls]
    a_b = [jnp.where(incl, _bdot_nt(rt[:, s], bt[:, s]), 0.0) for s in sls]
    a_k = [jnp.where(incl, _bdot_nt(rt[:, s], kt[:, s]), 0.0) for s in sls]
    pw = [-x for x in l_b]
    inv = [eye + x for x in pw]
    for _ in range(int(math.log2(c)) - 1):
        pw = [_bdot(x, x) for x in pw]
        inv = [i + _bdot(i, x) for i, x in zip(inv, pw)]
    lkv = [_bdot(l_k[h], v[:, sls[h]]) for h in heads]
    w = [_bdot(inv[h], kkt[:, sls[h]]) for h in heads]
    y_loc = [_bdot(inv[h], lkv[h]) for h in heads]
    for h in heads:
        s = sls[h]
        re_ref[0, 0, 0, :, s] = rt[:, s] - _bdot(a_b[h], w[h])
        ol_ref[0, 0, 0, :, s] = _bdot(a_k[h], v[:, s]) - _bdot(a_b[h], y_loc[h])
        sl_ref[0, 0, 0, :, s] = _bdot_tn(v[:, s], kh[:, s]) - _bdot_tn(y_loc[h], bh[:, s])
        p_ref[0, 0, 0, :, s] = eye * e_tot[:, s] - _bdot_tn(w[h], bh[:, s])


def _rwkv_carry_kernel(p_ref, sl_ref, s_out_ref, st_ref):
    @pl.when(pl.program_id(0) == 0)
    def _():
        st_ref[...] = jnp.zeros_like(st_ref)

    for d in range(2):
        for b in range(NB):
            s = st_ref[d, b]
            s_out_ref[d, b, 0] = s
            for h in range(RW_H):
                sl = slice(h * RW_D, (h + 1) * RW_D)
                st_ref[d, b, :, sl] = _hdot(s[:, sl], p_ref[d, b, 0, :, sl]) + sl_ref[d, b, 0, :, sl]


def _rwkv_out_kernel(ol0_ref, re0_ref, s0_ref, ol1_ref, re1_ref, s1_ref, bonus_ref, g_ref, gng_ref, gnb_ref,
                     bd_ref, y_ref, o_scr):
    for h in range(RW_H):
        sl = slice(h * RW_D, (h + 1) * RW_D)
        o_scr[:, sl] = (ol0_ref[0, 0, 0, :, sl] + ol1_ref[0, 0, 0, :, sl]
                        + _bdot_nt(re0_ref[0, 0, 0, :, sl], s0_ref[0, 0, 0, :, sl])
                        + _bdot_nt(re1_ref[0, 0, 0, :, sl], s1_ref[0, 0, 0, :, sl]))
    o = o_scr[...] + bonus_ref[...]
    bd = bd_ref[...] * (1.0 / RW_D)
    cen = o - _hdot(o, bd)
    var = _hdot(cen * cen, bd)
    y = cen * lax.rsqrt(var + RW_GN_EPS) * gng_ref[...] + gnb_ref[...]
    y_ref[...] = (y * g_ref[...]).astype(y_ref.dtype)


def _rwkv_scan(r, v, kk, lw, bmat, km, bonus, g, gn_g, gn_b):
    tok = pl.BlockSpec((CHUNK, RW_W), lambda d, b, n: (_seq_row_block(b, d, n), 0))
    drn = pl.BlockSpec((1, CHUNK, RW_W), lambda d, b, n: (d, _seq_row_block(b, d, n), 0))
    step_shape = jax.ShapeDtypeStruct((2, NB, SEQ_CHUNKS, CHUNK, RW_W), F32)
    step_blk = pl.BlockSpec((1, 1, 1, CHUNK, RW_W), lambda d, b, n: (d, b, n, 0, 0))
    p, s_loc, r_eff, o_loc = pl.pallas_call(
        _rwkv_chunk_kernel,
        grid=(2, NB, SEQ_CHUNKS),
        in_specs=[tok, tok, tok, drn, drn, drn],
        out_specs=[step_blk] * 4,
        out_shape=[step_shape] * 4,
        compiler_params=_cp("parallel", "parallel", "parallel"),
        name="rwkv_chunk",
    )(r, v, kk, lw, bmat, km)

    all_blk = pl.BlockSpec((2, NB, 1, CHUNK, RW_W), lambda n: (0, 0, n, 0, 0))
    s_in = pl.pallas_call(
        _rwkv_carry_kernel,
        grid=(SEQ_CHUNKS,),
        in_specs=[all_blk, all_blk],
        out_specs=all_blk,
        out_shape=step_shape,
        scratch_shapes=[pltpu.VMEM((2, NB, CHUNK, RW_W), F32)],
        compiler_params=_cp("arbitrary"),
        name="rwkv_carry",
    )(p, s_loc)

    fwd = pl.BlockSpec((1, 1, 1, CHUNK, RW_W), lambda b, pos: (0, b, pos, 0, 0))
    bwd = pl.BlockSpec((1, 1, 1, CHUNK, RW_W), lambda b, pos: (1, b, _bwd_step(pos), 0, 0))
    rows = pl.BlockSpec((CHUNK, RW_W), lambda b, pos: (_seq_row_block(b, 0, pos), 0))
    vec = pl.BlockSpec((1, RW_W), lambda b, pos: (0, 0))
    return pl.pallas_call(
        _rwkv_out_kernel,
        grid=(NB, SEQ_CHUNKS),
        in_specs=[fwd, fwd, fwd, bwd, bwd, bwd, rows, rows, vec, vec,
                  pl.BlockSpec((RW_W, RW_W), lambda b, pos: (0, 0))],
        out_specs=rows,
        out_shape=jax.ShapeDtypeStruct((ROWS, RW_W), BF16),
        scratch_shapes=[pltpu.VMEM((CHUNK, RW_W), F32)],
        compiler_params=_cp("parallel", "parallel"),
        name="rwkv_out",
    )(o_loc, r_eff, s_in, o_loc, r_eff, s_in, bonus, g, gn_g.reshape(1, RW_W), gn_b.reshape(1, RW_W),
      _head_block_ones(RW_W, RW_D))


def _mlstm_prep_kernel(z_ref, zp_ref, zn_ref, w_ref, o_ref):
    z = z_ref[...]
    zp, zn = _neighbours(z, zp_ref[...], zn_ref[...])
    y = _silu(zp * w_ref[0:1, :] + z * w_ref[1:2, :] + zn * w_ref[2:3, :])
    col = lax.broadcasted_iota(jnp.int32, y.shape, 1)
    o_ref[...] = jnp.where(col >= ML_W, y * (ML_D ** -0.5), y)


def _mlstm_prep(z, conv_w):
    return pl.pallas_call(
        _mlstm_prep_kernel,
        grid=(N_TILES,),
        in_specs=_tile_halo_specs(2 * ML_W) + [pl.BlockSpec((3, 2 * ML_W), lambda t: (0, 0))],
        out_specs=pl.BlockSpec((TILE, 2 * ML_W), lambda t: (t, 0)),
        out_shape=jax.ShapeDtypeStruct((ROWS, 2 * ML_W), F32),
        compiler_params=_cp("parallel"),
        name="mlstm_prep",
    )(z, z, z, conv_w)


def _log_sigmoid(x):
    return jnp.minimum(x, 0.0) - jnp.log(1.0 + jnp.exp(-jnp.abs(x)))


N_CHAINS = 2 * NB


def _mlstm_scan_kernel(*refs):
    nc = N_CHAINS
    q_refs, k_refs, v_refs = refs[0:nc], refs[nc:2 * nc], refs[2 * nc:3 * nc]
    gc_refs, gr_refs = refs[3 * nc:4 * nc], refs[4 * nc:5 * nc]
    bc_ref, br_ref, o_ref, c_ref, n_ref, m_ref = refs[5 * nc:]

    @pl.when(pl.program_id(0) == 0)
    def _():
        c_ref[...] = jnp.zeros_like(c_ref)
        n_ref[...] = jnp.zeros_like(n_ref)
        m_ref[...] = jnp.zeros_like(m_ref)

    c = CHUNK
    ti = lax.broadcasted_iota(jnp.int32, (c, c), 0)
    si = lax.broadcasted_iota(jnp.int32, (c, c), 1)
    masks = (ti >= si, ti <= si)
    items = [(ci, h) for ci in range(nc) for h in range(ML_H)]
    sls = [slice(h * ML_D, (h + 1) * ML_D) for h in range(ML_H)]
    gcol = [gc_refs[ci][0, 0] + bc_ref[ci // NB] for ci in range(nc)]
    grow = [gr_refs[ci][0, 0] + br_ref[ci // NB] for ci in range(nc)]

    q = [q_refs[ci][:, sls[h]] for ci, h in items]
    k = [k_refs[ci][:, sls[h]] for ci, h in items]
    v = [v_refs[ci][:, sls[h]] for ci, h in items]
    qk = [_bdot_nt(a, b) for a, b in zip(q, k)]
    c_mat = [c_ref[ci, h] for ci, h in items]
    n_vec = [n_ref[ci, h] for ci, h in items]
    m_prev = [m_ref[ci, h][0:1, 0:1] for ci, h in items]
    qc = [_bdot_nt(a, b) for a, b in zip(q, c_mat)]
    qn = [jnp.sum(a * b, axis=1, keepdims=True) for a, b in zip(q, n_vec)]

    log_w, m_inter, cum_col, i_col, total = [], [], [], [], []
    for ci, h in items:
        mask = masks[ci // NB]
        mask_t = masks[1 - ci // NB]
        f_col = _log_sigmoid(gcol[ci][:, ML_H + h:ML_H + h + 1])
        f_row = _log_sigmoid(grow[ci][ML_H + h:ML_H + h + 1, :])
        cc = jnp.sum(jnp.where(mask, f_row, 0.0), axis=1, keepdims=True)
        cr = jnp.sum(jnp.where(mask_t, f_col, 0.0), axis=0, keepdims=True)
        log_w.append(jnp.where(mask, cc - cr + grow[ci][h:h + 1, :], ML_NEG))
        cum_col.append(cc)
        i_col.append(gcol[ci][:, h:h + 1])
        total.append(jnp.sum(f_row, axis=1, keepdims=True))
    m_inter = [a + b for a, b in zip(cum_col, m_prev)]
    m_t = [jnp.maximum(jnp.max(a, axis=1, keepdims=True), b) for a, b in zip(log_w, m_inter)]
    s = [a * jnp.exp(b - m) for a, b, m in zip(qk, log_w, m_t)]
    w_inter = [jnp.exp(a - m) for a, m in zip(m_inter, m_t)]
    sv = [_bdot(a, b) for a, b in zip(s, v)]
    for i, (ci, h) in enumerate(items):
        num = sv[i] + w_inter[i] * qc[i]
        den = jnp.sum(s[i], axis=1, keepdims=True) + w_inter[i] * qn[i]
        o_ref[ci // NB, ci % NB, 0, :, sls[h]] = num / jnp.maximum(jnp.abs(den), jnp.exp(-m_t[i]))
    log_src = [t - a + b for t, a, b in zip(total, cum_col, i_col)]
    m_new = [jnp.maximum(t + mp, jnp.max(ls, axis=0, keepdims=True)) for t, mp, ls in zip(total, m_prev, log_src)]
    src = [jnp.exp(ls - mn) for ls, mn in zip(log_src, m_new)]
    decay = [jnp.exp(t + mp - mn) for t, mp, mn in zip(total, m_prev, m_new)]
    vk = [_bdot_tn(a * sr, b) for a, sr, b in zip(v, src, k)]
    for i, (ci, h) in enumerate(items):
        c_ref[ci, h] = decay[i] * c_mat[i] + vk[i]
        n_ref[ci, h] = decay[i] * n_vec[i] + jnp.sum(src[i] * k[i], axis=0, keepdims=True)
        m_ref[ci, h] = jnp.broadcast_to(m_new[i], m_ref.shape[2:])


def _bwd_step(pos):
    return jnp.where(pos < CTX_CHUNKS, CTX_CHUNKS - 1 - pos, SEQ_CHUNKS - 1 + CTX_CHUNKS - pos)


def _mlstm_scan(qk, z, gcol, grow, bcol, brow):
    chains = [(d, b) for d in range(2) for b in range(NB)]

    def tok(col_blk):
        return [pl.BlockSpec((CHUNK, ML_W), lambda n, d=d, b=b: (_seq_row_block(b, d, n), col_blk))
                for d, b in chains]

    gc_specs = [pl.BlockSpec((1, 1, CHUNK, 2 * ML_H), lambda n, d=d, b=b: (d, _seq_row_block(b, d, n), 0, 0))
                for d, b in chains]
    gr_specs = [pl.BlockSpec((1, 1, 2 * ML_H, CHUNK), lambda n, d=d, b=b: (d, _seq_row_block(b, d, n), 0, 0))
                for d, b in chains]
    nc = N_CHAINS
    return pl.pallas_call(
        _mlstm_scan_kernel,
        grid=(SEQ_CHUNKS,),
        in_specs=tok(0) + tok(1) + tok(2) + gc_specs + gr_specs + [
            pl.BlockSpec((2, 1, 2 * ML_H), lambda n: (0, 0, 0)),
            pl.BlockSpec((2, 2 * ML_H, 1), lambda n: (0, 0, 0))],
        out_specs=pl.BlockSpec((2, NB, 1, CHUNK, ML_W), lambda n: (0, 0, n, 0, 0)),
        out_shape=jax.ShapeDtypeStruct((2, NB, SEQ_CHUNKS, CHUNK, ML_W), F32),
        scratch_shapes=[pltpu.VMEM((nc, ML_H, ML_D, ML_D), F32), pltpu.VMEM((nc, ML_H, 1, ML_D), F32),
                        pltpu.VMEM((nc, ML_H, 8, 128), F32)],
        compiler_params=_cp("arbitrary"),
        name="mlstm_scan",
    )(*([qk] * (2 * nc) + [z] * nc + [gcol] * nc + [grow] * nc + [bcol, brow]))


def _mlstm_out_kernel(hf_ref, hb_ref, og_ref, gng_ref, y_ref):
    hsum = _sigmoid(og_ref[...]) * (hf_ref[0, 0, 0] + hb_ref[0, 0, 0])
    for h in range(ML_H):
        sl = slice(h * ML_D, (h + 1) * ML_D)
        x = hsum[:, sl]
        cen = x - jnp.mean(x, axis=-1, keepdims=True)
        var = jnp.mean(cen * cen, axis=-1, keepdims=True)
        y_ref[:, sl] = (cen * lax.rsqrt(var + EPS) * gng_ref[:, sl]).astype(y_ref.dtype)


def _mlstm_out(hs, z, gn_g):
    return pl.pallas_call(
        _mlstm_out_kernel,
        grid=(NB, SEQ_CHUNKS),
        in_specs=[pl.BlockSpec((1, 1, 1, CHUNK, ML_W), lambda b, pos: (0, b, pos, 0, 0)),
                  pl.BlockSpec((1, 1, 1, CHUNK, ML_W), lambda b, pos: (1, b, _bwd_step(pos), 0, 0)),
                  pl.BlockSpec((CHUNK, ML_W), lambda b, pos: (_seq_row_block(b, 0, pos), 3)),
                  pl.BlockSpec((1, ML_W), lambda b, pos: (0, 0))],
        out_specs=pl.BlockSpec((CHUNK, ML_W), lambda b, pos: (_seq_row_block(b, 0, pos), 0)),
        out_shape=jax.ShapeDtypeStruct((ROWS, ML_W), BF16),
        compiler_params=_cp("parallel", "parallel"),
        name="mlstm_out",
    )(hs, hs, z, gn_g.reshape(1, ML_W))


def _mlstm(z_main, z_gates, conv_w, i_b, f_b, gn_g):
    qk = _mlstm_prep(z_main, conv_w)
    n_chunks = ROWS // CHUNK
    gates = z_gates[:, :4 * ML_H].reshape(n_chunks, CHUNK, 2, 2, ML_H)
    gcol = jnp.transpose(gates, (3, 0, 1, 2, 4)).reshape(2, n_chunks, CHUNK, 2 * ML_H)
    grow = jnp.swapaxes(gcol, 2, 3)
    bias = jnp.concatenate([i_b, f_b], axis=-1)
    hs = _mlstm_scan(qk, z_main, gcol, grow, bias.reshape(2, 1, 2 * ML_H), bias.reshape(2, 2 * ML_H, 1))
    return _mlstm_out(hs, z_main, gn_g)


def _s5_matrices(lam_re, lam_im, log_step, b_re, b_im, c_re, c_im, reverse):
    dt = jnp.exp(log_step)[:, None]
    mag = jnp.exp(lam_re * dt)
    a_re = mag * jnp.cos(lam_im * dt)
    a_im = mag * jnp.sin(lam_im * dt)
    den = lam_re * lam_re + lam_im * lam_im
    f_re = ((a_re - 1) * lam_re + a_im * lam_im) / den
    f_im = (a_im * lam_re - (a_re - 1) * lam_im) / den
    bb_re = f_re[..., None] * b_re - f_im[..., None] * b_im
    bb_im = f_re[..., None] * b_im + f_im[..., None] * b_re
    pr = [jnp.ones_like(a_re)]
    pi = [jnp.zeros_like(a_im)]
    for _ in range(S5_L):
        pr.append(pr[-1] * a_re - pi[-1] * a_im)
        pi.append(pr[-2] * a_im + pi[-1] * a_re)
    pr = jnp.stack(pr)
    pi = jnp.stack(pi)
    wr = pr[..., None] * bb_re - pi[..., None] * bb_im
    wi = pr[..., None] * bb_im + pi[..., None] * bb_re
    kern = (jnp.einsum("gcp,tgpk->tgck", c_re, wr, precision=HP)
            - jnp.einsum("gcp,tgpk->tgck", c_im, wi, precision=HP))
    j = np.arange(S5_L)[:, None]
    t = np.arange(S5_L)[None, :]
    lag = (j - t) if reverse else (t - j)
    valid = jnp.asarray((lag >= 0).astype(np.float32))
    m = kern[np.clip(lag, 0, S5_L)] * valid[:, :, None, None, None]
    m = jnp.transpose(m, (2, 0, 4, 1, 3)).reshape(S5_G, S5_L * S5_C, S5_L * S5_C)
    ex_in = np.arange(S5_L) if reverse else S5_L - 1 - np.arange(S5_L)
    e = jnp.concatenate([jnp.transpose(wr[ex_in], (1, 0, 3, 2)), jnp.transpose(wi[ex_in], (1, 0, 3, 2))],
                        axis=-1).reshape(S5_G, S5_L * S5_C, 2 * S5_P)
    ex_out = S5_L - np.arange(S5_L) if reverse else np.arange(S5_L) + 1
    pr_o = pr[ex_out][:, :, None, :]
    pi_o = pi[ex_out][:, :, None, :]
    fr = c_re[None] * pr_o - c_im[None] * pi_o
    fi = -(c_re[None] * pi_o + c_im[None] * pr_o)
    f = jnp.concatenate([jnp.transpose(fr, (1, 3, 0, 2)), jnp.transpose(fi, (1, 3, 0, 2))],
                        axis=1).reshape(S5_G, 2 * S5_P, S5_L * S5_C)
    return jnp.concatenate([m, e], axis=-1), f, pr[S5_L], pi[S5_L]


def _s5_local_kernel(u_ref, me_ref, y_ref, x_ref):
    res = _hdot(u_ref[0], me_ref[0, 0])
    y_ref[0, 0] = res[:, :S5_L * S5_C]
    x_ref[0, 0] = res[:, S5_L * S5_C:]


def _s5_carry_kernel(xc_ref, ar_ref, ai_ref, x0_ref, st_ref):
    @pl.when(pl.program_id(1) == 0)
    def _():
        st_ref[...] = jnp.zeros_like(st_ref)

    ar = ar_ref[0]
    ai = ai_ref[0]
    x = st_ref[...]
    for i in range(xc_ref.shape[1]):
        x0_ref[0, i] = x
        x = x * ar + pltpu.roll(x, S5_P, 1) * ai + xc_ref[0, i]
    st_ref[...] = x


def _s5_state_kernel(y_ref, x0_ref, f_ref, o_ref):
    contrib = y_ref[0, 0] + _hdot(x0_ref[0, 0], f_ref[0, 0])

    @pl.when(pl.program_id(1) == 0)
    def _():
        o_ref[0] = contrib

    @pl.when(pl.program_id(1) == 1)
    def _():
        o_ref[0] = o_ref[0] + contrib


def _s5_glu_kernel(y_ref, u_ref, d_ref, w_ref, o_ref):
    x = y_ref[...] + u_ref[...] * d_ref[...]
    ge = 0.5 * x * (1.0 + jnp.tanh(math.sqrt(2.0 / math.pi) * (x + 0.044715 * (x * x * x))))
    p = _bdot(ge, w_ref[...])
    o_ref[...] = (p[:, :S5_W] * _sigmoid(p[:, S5_W:])).astype(o_ref.dtype)


def _s5(z, lam_re, lam_im, log_step, b_re, b_im, c_re, c_im, d_skip, w_glu):
    n_ch = (SEQ + CTX) // S5_L
    n_ctx_ch = CTX // S5_L
    rows = NB * n_ch
    mats = [_s5_matrices(lam_re[d], lam_im[d], log_step[d], b_re[d], b_im[d], c_re[d], c_im[d], d == 1)
            for d in range(2)]
    me = jnp.stack([m[0] for m in mats])
    f = jnp.stack([m[1] for m in mats])
    a_re = jnp.stack([m[2] for m in mats])
    a_im = jnp.stack([m[3] for m in mats])

    def to_groups(x):
        x = x.reshape(NB, -1, S5_L, S5_G, S5_C)
        return jnp.transpose(x, (3, 0, 1, 2, 4)).reshape(S5_G, NB, -1, S5_L * S5_C)

    u = jnp.concatenate([to_groups(z[N_LAT:]), to_groups(z[:N_LAT])], axis=2).reshape(S5_G, rows, S5_L * S5_C)
    y_loc, x_in = pl.pallas_call(
        _s5_local_kernel,
        grid=(2, S5_G),
        in_specs=[pl.BlockSpec((1, rows, S5_L * S5_C), lambda d, g: (g, 0, 0)),
                  pl.BlockSpec((1, 1, S5_L * S5_C, S5_L * S5_C + 2 * S5_P), lambda d, g: (d, g, 0, 0))],
        out_specs=[pl.BlockSpec((1, 1, rows, S5_L * S5_C), lambda d, g: (d, g, 0, 0)),
                   pl.BlockSpec((1, 1, rows, 2 * S5_P), lambda d, g: (d, g, 0, 0))],
        out_shape=[jax.ShapeDtypeStruct((2, S5_G, rows, S5_L * S5_C), F32),
                   jax.ShapeDtypeStruct((2, S5_G, rows, 2 * S5_P), F32)],
        compiler_params=_cp("parallel", "parallel"),
        name="s5_local",
    )(u, me)

    perm_b = np.concatenate([np.arange(n_ctx_ch)[::-1], np.arange(n_ctx_ch, n_ch)[::-1]])
    inv_b = np.argsort(perm_b)
    x_in = x_in.reshape(2, S5_G, NB, n_ch, 2 * S5_P)
    x_ord = jnp.stack([x_in[0], x_in[1][:, :, perm_b]])
    x_ord = jnp.transpose(x_ord, (0, 3, 1, 2, 4)).reshape(2, n_ch, S5_G * NB, 2 * S5_P)
    coef_r = jnp.repeat(jnp.concatenate([a_re, a_re], axis=-1), NB, axis=1)
    coef_i = jnp.repeat(jnp.concatenate([-a_im, a_im], axis=-1), NB, axis=1)
    step = 16
    x0 = pl.pallas_call(
        _s5_carry_kernel,
        grid=(2, n_ch // step),
        in_specs=[pl.BlockSpec((1, step, S5_G * NB, 2 * S5_P), lambda d, i: (d, i, 0, 0)),
                  pl.BlockSpec((1, S5_G * NB, 2 * S5_P), lambda d, i: (d, 0, 0)),
                  pl.BlockSpec((1, S5_G * NB, 2 * S5_P), lambda d, i: (d, 0, 0))],
        out_specs=pl.BlockSpec((1, step, S5_G * NB, 2 * S5_P), lambda d, i: (d, i, 0, 0)),
        out_shape=jax.ShapeDtypeStruct((2, n_ch, S5_G * NB, 2 * S5_P), F32),
        scratch_shapes=[pltpu.VMEM((S5_G * NB, 2 * S5_P), F32)],
        compiler_params=_cp("arbitrary", "arbitrary"),
        name="s5_carry",
    )(x_ord, coef_r, coef_i)
    x0 = jnp.transpose(x0.reshape(2, n_ch, S5_G, NB, 2 * S5_P), (0, 2, 3, 1, 4))
    x0 = jnp.stack([x0[0], x0[1][:, :, inv_b]]).reshape(2, S5_G, rows, 2 * S5_P)

    y = pl.pallas_call(
        _s5_state_kernel,
        grid=(S5_G, 2),
        in_specs=[pl.BlockSpec((1, 1, rows, S5_L * S5_C), lambda g, d: (d, g, 0, 0)),
                  pl.BlockSpec((1, 1, rows, 2 * S5_P), lambda g, d: (d, g, 0, 0)),
                  pl.BlockSpec((1, 1, 2 * S5_P, S5_L * S5_C), lambda g, d: (d, g, 0, 0))],
        out_specs=pl.BlockSpec((1, rows, S5_L * S5_C), lambda g, d: (g, 0, 0)),
        out_shape=jax.ShapeDtypeStruct((S5_G, rows, S5_L * S5_C), F32),
        compiler_params=_cp("parallel", "arbitrary"),
        name="s5_state",
    )(y_loc, x0, f)

    y = jnp.transpose(y.reshape(S5_G, NB, n_ch, S5_L, S5_C), (1, 2, 3, 0, 4)).reshape(NB, n_ch * S5_L, S5_W)
    y = jnp.concatenate([y[:, CTX:].reshape(N_LAT, S5_W), y[:, :CTX].reshape(N_CTX, S5_W)], axis=0)
    tok = pl.BlockSpec((TILE, S5_W), lambda t: (t, 0))
    return pl.pallas_call(
        _s5_glu_kernel,
        grid=(N_TILES,),
        in_specs=[tok, tok, pl.BlockSpec((1, S5_W), lambda t: (0, 0)),
                  pl.BlockSpec((S5_W, 2 * S5_W), lambda t: (0, 0))],
        out_specs=tok,
        out_shape=jax.ShapeDtypeStruct((ROWS, S5_W), BF16),
        compiler_params=_cp("parallel"),
        name="s5_glu",
    )(y, z, d_skip.reshape(1, S5_W), w_glu)


def kernel(x, c, ctx, c_ctx, w_mod, b_mod, norm1_g, norm2_g, w_in, b_gate, q_norm_g, k_norm_g, rwkv_mu, rwkv_w0, rwkv_w_up, rwkv_a0, rwkv_a_up, rwkv_g_up, rwkv_k_k, rwkv_k_a, rwkv_r_k, rwkv_gn_g, rwkv_gn_b, mlstm_conv_w, mlstm_i_b, mlstm_f_b, mlstm_gn_g, s5_lam_re, s5_lam_im, s5_log_step, s5_b_re, s5_b_im, s5_c_re, s5_c_im, s5_d, s5_w_glu, w_br_attn, w_br_rwkv, w_br_mlstm, w_br_s5, w_out, w_ffn_in, w_ffn_out, final_norm_g):
    cos, sin = _rope_tables()
    xs = jnp.concatenate([x.reshape(N_LAT, D), ctx.reshape(N_CTX, D)], axis=0)
    c_all = jnp.concatenate([c, c_ctx[None], jnp.zeros((3, D), F32)], axis=0)
    off = np.cumsum([0, ATTN_IN, RW_IN, ML_IN, S5_W, GATE_IN])
    tm = 1024
    for l in range(DEPTH):
        last = l == DEPTH - 1
        n_rows = N_LAT if last else ROWS
        mod = _modulation(c_all, w_mod[l], b_mod[l])
        mod_t = _mod_rows(mod, TILE)
        h = _norm_mod(xs, norm1_g[l], mod_t, 0, 1, N_TILES)
        wl = w_in[l]
        w_attn = wl[:, off[0]:off[1]].astype(BF16)
        w_rwkv = wl[:, off[1]:off[2]].astype(BF16)
        w_ml = wl[:, off[2]:off[2] + 4 * ML_W].astype(BF16)
        w_mlg = jnp.pad(wl[:, off[2] + 4 * ML_W:off[3]], ((0, 0), (0, 128 - 4 * ML_H))).astype(BF16)
        w_s5 = wl[:, off[3]:off[4]].astype(BF16)
        w_gate = wl[:, off[4]:off[5]].astype(BF16)
        z_attn = _mm(h, w_attn, tm, 512)
        z_rwkv = _mm(h, w_rwkv, tm, 896)
        z_ml = _mm(h, w_ml, tm, 1024)
        z_mlg = _mm(h, w_mlg, tm, 128)
        z_s5 = _mm(h, w_s5, tm, 512)
        z_gate = _mm(h, w_gate, tm, 1024)

        ya = _attention(z_attn, q_norm_g[l], k_norm_g[l], cos, sin)
        rp = dict(mu=rwkv_mu[l], w0=rwkv_w0[l], w_up=rwkv_w_up[l], a0=rwkv_a0[l], a_up=rwkv_a_up[l],
                  g_up=rwkv_g_up[l], k_k=rwkv_k_k[l], k_a=rwkv_k_a[l], r_k=rwkv_r_k[l].reshape(RW_W))
        r, v, kk, g, bonus, lw, bmat, km = _rwkv_prep(z_rwkv, rp)
        yr = _rwkv_scan(r, v, kk, lw, bmat, km, bonus, g, rwkv_gn_g[l], rwkv_gn_b[l])
        ym = _mlstm(z_ml, z_mlg, mlstm_conv_w[l], mlstm_i_b[l], mlstm_f_b[l], mlstm_gn_g[l])
        ys = _s5(z_s5, s5_lam_re[l], s5_lam_im[l], s5_log_step[l], s5_b_re[l], s5_b_im[l],
                 s5_c_re[l], s5_c_im[l], s5_d[l], s5_w_glu[l])

        y = _merge(ya, yr, ym, ys, z_gate, b_gate[l], w_br_attn[l], w_br_rwkv[l], w_br_mlstm[l], w_br_s5[l],
                   tm, 512, n_rows)
        xs = _mm_res(y, w_out[l], xs, mod, 2, tm, 1024, n_rows)
        h2 = _norm_mod(xs, norm2_g[l], mod_t, 3, 4, n_rows // TILE)
        u = _ffn_in(h2, w_ffn_in[l], tm, 512, n_rows)
        xs = _mm_res(u, w_ffn_out[l], xs, mod, 5, 512, 512, n_rows)
    return _final_norm(xs, final_norm_g).reshape(NB, SEQ, D)
```

```python
import functools
import math

import numpy as np
import jax
import jax.numpy as jnp
from jax import lax
from jax.experimental import pallas as pl
from jax.experimental.pallas import tpu as pltpu

F32 = jnp.float32
BF16 = jnp.bfloat16
HP = lax.Precision.HIGHEST

D = 2048
NB = 4
SEQ = 2048
CTX = 256
DEPTH = 2
N_LAT = NB * SEQ
N_CTX = NB * CTX
ROWS = N_LAT + N_CTX
EPS = 1e-6
GRID_W = 64

HEAD_DIM = 128
ATTN_HEADS = 8
ATTN_KV = 2
ROPE_THETA = 10000.0
ATTN_Q = ATTN_HEADS * HEAD_DIM
ATTN_IN = (ATTN_HEADS + 2 * ATTN_KV) * HEAD_DIM

RW_H = 8
RW_D = 64
RW_W = 512
RW_IN = 3 * RW_W + 64 + 64 + 128
RW_DECAY = math.exp(-0.5)
RW_GN_EPS = 64e-5

ML_H = 4
ML_D = 128
ML_W = 512
ML_NEG = -1e30
ML_IN = 4 * ML_W + 4 * ML_H

S5_W = 512
S5_C = 16
S5_G = 32
S5_P = 64
S5_L = 16

FFN_H = 5632
GATE_IN = 4 * D

CHUNK = 64
TILE = 256
N_TILES = ROWS // TILE
SEQ_CHUNKS = (SEQ + CTX) // CHUNK
CTX_CHUNKS = CTX // CHUNK
LAT_CHUNKS = SEQ // CHUNK

VMEM_LIMIT_BYTES = 56 * 1024 * 1024


def _cp(*sem):
    return pltpu.CompilerParams(dimension_semantics=sem, vmem_limit_bytes=VMEM_LIMIT_BYTES)


def _bdot(a, b):
    return jnp.dot(a.astype(BF16), b.astype(BF16), preferred_element_type=F32)


def _bdot_nt(a, b):
    return lax.dot_general(a.astype(BF16), b.astype(BF16), (((1,), (1,)), ((), ())),
                           preferred_element_type=F32)


def _bdot_tn(a, b):
    return lax.dot_general(a.astype(BF16), b.astype(BF16), (((0,), (0,)), ((), ())),
                           preferred_element_type=F32)


def _hdot(a, b):
    return jnp.dot(a, b, precision=HP, preferred_element_type=F32)


def _sigmoid(x):
    return 1.0 / (1.0 + jnp.exp(-x))


def _silu(x):
    return x * _sigmoid(x)


def _mod_kernel(c_ref, w_ref, b_ref, o_ref):
    o_ref[...] = _hdot(_silu(c_ref[...]), w_ref[...]) + b_ref[...]


def _modulation(c_all, w, b):
    tn = 1024
    return pl.pallas_call(
        _mod_kernel,
        grid=(6 * D // tn,),
        in_specs=[pl.BlockSpec((8, D), lambda j: (0, 0)),
                  pl.BlockSpec((D, tn), lambda j: (0, j)),
                  pl.BlockSpec((1, tn), lambda j: (0, j))],
        out_specs=pl.BlockSpec((8, tn), lambda j: (0, j)),
        out_shape=jax.ShapeDtypeStruct((8, 6 * D), F32),
        compiler_params=_cp("arbitrary"),
        name="modulation",
    )(c_all, w, b.reshape(1, 6 * D))


def _mod_row(i, tm):
    return jnp.where(i * tm < N_LAT, (i * tm) // SEQ, NB)


def _norm_mod_kernel(x_ref, g_ref, sh_ref, sc_ref, o_ref):
    x = x_ref[...]
    y = x * lax.rsqrt(jnp.mean(x * x, axis=-1, keepdims=True) + EPS) * g_ref[...]
    o_ref[...] = (y * (1.0 + sc_ref[0]) + sh_ref[0]).astype(o_ref.dtype)


def _norm_mod(x, g, modr, shift_blk, scale_blk, n_tiles):
    return pl.pallas_call(
        _norm_mod_kernel,
        grid=(n_tiles,),
        in_specs=[pl.BlockSpec((TILE, D), lambda i: (i, 0)),
                  pl.BlockSpec((1, D), lambda i: (0, 0)),
                  pl.BlockSpec((1, 1, D), lambda i: (_mod_row(i, TILE), 0, shift_blk)),
                  pl.BlockSpec((1, 1, D), lambda i: (_mod_row(i, TILE), 0, scale_blk))],
        out_specs=pl.BlockSpec((TILE, D), lambda i: (i, 0)),
        out_shape=jax.ShapeDtypeStruct((n_tiles * TILE, D), BF16),
        compiler_params=_cp("parallel"),
        name="norm_mod",
    )(x, g.reshape(1, D), modr, modr)


def _mm_kernel(a_ref, w_ref, o_ref):
    o_ref[...] = jnp.dot(a_ref[...], w_ref[...], preferred_element_type=F32).astype(o_ref.dtype)


def _mm(a, w, tm, tn, out_dtype=F32):
    m, k = a.shape
    n = w.shape[1]
    return pl.pallas_call(
        _mm_kernel,
        grid=(n // tn, m // tm),
        in_specs=[pl.BlockSpec((tm, k), lambda j, i: (i, 0)),
                  pl.BlockSpec((k, tn), lambda j, i: (0, j))],
        out_specs=pl.BlockSpec((tm, tn), lambda j, i: (i, j)),
        out_shape=jax.ShapeDtypeStruct((m, n), out_dtype),
        compiler_params=_cp("parallel", "parallel"),
        name="matmul",
    )(a, w)


def _mm_res_kernel(a_ref, w_ref, x_ref, g_ref, o_ref, wb_ref):
    @pl.when(pl.program_id(1) == 0)
    def _():
        wb_ref[...] = w_ref[...].astype(BF16)

    y = jnp.dot(a_ref[...], wb_ref[...], preferred_element_type=F32)
    o_ref[...] = x_ref[...] + g_ref[0] * y


def _mm_res(a, w, x, mod, gate_blk, tm, tn, n_rows):
    k = a.shape[1]
    n = w.shape[1]
    gpb = D // tn
    return pl.pallas_call(
        _mm_res_kernel,
        grid=(n // tn, n_rows // tm),
        in_specs=[pl.BlockSpec((tm, k), lambda j, i: (i, 0)),
                  pl.BlockSpec((k, tn), lambda j, i: (0, j)),
                  pl.BlockSpec((tm, tn), lambda j, i: (i, j)),
                  pl.BlockSpec((1, 1, tn), lambda j, i: (_mod_row(i, tm), 0, gate_blk * gpb + j))],
        out_specs=pl.BlockSpec((tm, tn), lambda j, i: (i, j)),
        out_shape=jax.ShapeDtypeStruct((n_rows, n), F32),
        scratch_shapes=[pltpu.VMEM((k, tn), BF16)],
        compiler_params=_cp("arbitrary", "arbitrary"),
        name="matmul_residual",
    )(a, w, x, mod)


def _ffn_in_kernel(a_ref, wa_ref, wb_ref, o_ref, wab_ref, wbb_ref):
    @pl.when(pl.program_id(1) == 0)
    def _():
        wab_ref[...] = wa_ref[...].astype(BF16)
        wbb_ref[...] = wb_ref[...].astype(BF16)

    a = a_ref[...]
    u = jnp.dot(a, wab_ref[...], preferred_element_type=F32)
    v = jnp.dot(a, wbb_ref[...], preferred_element_type=F32)
    o_ref[...] = (_silu(u) * v).astype(o_ref.dtype)


def _ffn_in(h, w, tm, tn, n_rows):
    nb = FFN_H // tn
    return pl.pallas_call(
        _ffn_in_kernel,
        grid=(nb, n_rows // tm),
        in_specs=[pl.BlockSpec((tm, D), lambda j, i: (i, 0)),
                  pl.BlockSpec((D, tn), lambda j, i: (0, j)),
                  pl.BlockSpec((D, tn), lambda j, i: (0, nb + j))],
        out_specs=pl.BlockSpec((tm, tn), lambda j, i: (i, j)),
        out_shape=jax.ShapeDtypeStruct((n_rows, FFN_H), BF16),
        scratch_shapes=[pltpu.VMEM((D, tn), BF16), pltpu.VMEM((D, tn), BF16)],
        compiler_params=_cp("arbitrary", "arbitrary"),
        name="ffn_in",
    )(h, w, w)


def _merge_kernel(ya_ref, yr_ref, ym_ref, ys_ref, ga_ref, gr_ref, gm_ref, gs_ref,
                  ba_ref, br_ref, bm_ref, bs_ref, wa_ref, wr_ref, wm_ref, ws_ref, o_ref,
                  wab_ref, wrb_ref, wmb_ref, wsb_ref):
    @pl.when(pl.program_id(1) == 0)
    def _():
        wab_ref[...] = wa_ref[...].astype(BF16)
        wrb_ref[...] = wr_ref[...].astype(BF16)
        wmb_ref[...] = wm_ref[...].astype(BF16)
        wsb_ref[...] = ws_ref[...].astype(BF16)

    acc = None
    for y_ref, g_ref, b_ref, w_ref in ((ya_ref, ga_ref, ba_ref, wab_ref), (yr_ref, gr_ref, br_ref, wrb_ref),
                                       (ym_ref, gm_ref, bm_ref, wmb_ref), (ys_ref, gs_ref, bs_ref, wsb_ref)):
        gate = _sigmoid(g_ref[...] + b_ref[...])
        term = gate * jnp.dot(y_ref[...], w_ref[...], preferred_element_type=F32)
        acc = term if acc is None else acc + term
    o_ref[...] = acc.astype(o_ref.dtype)


def _merge(ya, yr, ym, ys, z_gate, b_gate, wa, wr, wm, ws, tm, tn, n_rows):
    nb = D // tn
    b_gate = b_gate.reshape(1, GATE_IN)

    def act(width):
        return pl.BlockSpec((tm, width), lambda j, i: (i, 0))

    def gate(br):
        return pl.BlockSpec((tm, tn), lambda j, i: (i, br * nb + j))

    def bias(br):
        return pl.BlockSpec((1, tn), lambda j, i: (0, br * nb + j))

    def wgt(width):
        return pl.BlockSpec((width, tn), lambda j, i: (0, j))

    return pl.pallas_call(
        _merge_kernel,
        grid=(nb, n_rows // tm),
        in_specs=[act(ATTN_Q), act(RW_W), act(ML_W), act(S5_W),
                  gate(0), gate(1), gate(2), gate(3),
                  bias(0), bias(1), bias(2), bias(3),
                  wgt(ATTN_Q), wgt(RW_W), wgt(ML_W), wgt(S5_W)],
        out_specs=pl.BlockSpec((tm, tn), lambda j, i: (i, j)),
        out_shape=jax.ShapeDtypeStruct((n_rows, D), BF16),
        scratch_shapes=[pltpu.VMEM((ATTN_Q, tn), BF16), pltpu.VMEM((RW_W, tn), BF16),
                        pltpu.VMEM((ML_W, tn), BF16), pltpu.VMEM((S5_W, tn), BF16)],
        compiler_params=_cp("arbitrary", "arbitrary"),
        name="gated_merge",
    )(ya, yr, ym, ys, z_gate, z_gate, z_gate, z_gate, b_gate, b_gate, b_gate, b_gate, wa, wr, wm, ws)


def _final_norm_kernel(x_ref, g_ref, o_ref):
    x = x_ref[...]
    o_ref[...] = x * lax.rsqrt(jnp.mean(x * x, axis=-1, keepdims=True) + EPS) * g_ref[...]


def _final_norm(x, g):
    n_tiles = N_LAT // TILE
    return pl.pallas_call(
        _final_norm_kernel,
        grid=(n_tiles,),
        in_specs=[pl.BlockSpec((TILE, D), lambda i: (i, 0)),
                  pl.BlockSpec((1, D), lambda i: (0, 0))],
        out_specs=pl.BlockSpec((TILE, D), lambda i: (i, 0)),
        out_shape=jax.ShapeDtypeStruct((N_LAT, D), F32),
        compiler_params=_cp("parallel"),
        name="final_norm",
    )(x, g.reshape(1, D))


def _tile_halo_specs(width, col_blk=0):
    last = ROWS // 8 - 1
    per = TILE // 8
    return [pl.BlockSpec((TILE, width), lambda t: (t, col_blk)),
            pl.BlockSpec((8, width), lambda t: (jnp.maximum(t * per - 1, 0), col_blk)),
            pl.BlockSpec((8, width), lambda t: (jnp.minimum((t + 1) * per, last), col_blk))]


def _neighbours(z, prev_blk, next_blk):
    t = pl.program_id(0)
    pos = t % (SEQ // TILE)
    is_lat = t < N_LAT // TILE
    has_prev = jnp.logical_and(is_lat, pos > 0).astype(F32)
    has_next = jnp.logical_and(is_lat, pos < SEQ // TILE - 1).astype(F32)
    row = lax.broadcasted_iota(jnp.int32, z.shape, 0)
    zp = jnp.where(row == 0, prev_blk[7:8, :] * has_prev, pltpu.roll(z, 1, 0))
    zn = jnp.where(row == TILE - 1, next_blk[0:1, :] * has_next, pltpu.roll(z, TILE - 1, 0))
    return zp, zn


def _seq_row_block(b, d, n):
    ctx_c = jnp.where(d == 0, n, CTX_CHUNKS - 1 - n)
    lat_c = jnp.where(d == 0, n - CTX_CHUNKS, SEQ_CHUNKS - 1 - n)
    return jnp.where(n < CTX_CHUNKS, N_LAT // CHUNK + CTX_CHUNKS * b + ctx_c, LAT_CHUNKS * b + lat_c)


def _rope(x, cos, sin):
    lane = lax.broadcasted_iota(jnp.int32, x.shape, 1)
    first = (lane % 64) < 32
    partner = jnp.where(first, pltpu.roll(x, 96, 1), pltpu.roll(x, 32, 1))
    return x * cos + partner * sin


def _rms(x, g):
    return x * lax.rsqrt(jnp.mean(x * x, axis=-1, keepdims=True) + EPS) * g


def _attn_kernel(q_ref, kl_ref, kc_ref, vl_ref, vc_ref, cos_ref, sin_ref, cos_t_ref, sin_t_ref,
                 qg_ref, kg_ref, o_ref, klb_ref, kcb_ref, vlb_ref, vcb_ref):
    qi = pl.program_id(2)
    n_lat_tiles = SEQ // TILE

    @pl.when(qi == 0)
    def _():
        kg = kg_ref[...]
        klb_ref[...] = _rope(_rms(kl_ref[...], kg), cos_ref[...], sin_ref[...]).astype(BF16)
        kcb_ref[...] = _rms(kc_ref[...], kg).astype(BF16)
        vlb_ref[...] = vl_ref[...].astype(BF16)
        vcb_ref[...] = vc_ref[...].astype(BF16)

    scale = HEAD_DIM ** -0.5

    def heads(latent):
        for h in range(ATTN_HEADS // ATTN_KV):
            sl = slice(h * HEAD_DIM, (h + 1) * HEAD_DIM)
            q = _rms(q_ref[:, sl], qg_ref[...])
            if latent:
                q = _rope(q, cos_t_ref[...], sin_t_ref[...])
            q = (q * scale).astype(BF16)
            s_c = lax.dot_general(q, kcb_ref[...], (((1,), (1,)), ((), ())), preferred_element_type=F32)
            m = jnp.max(s_c, axis=-1, keepdims=True)
            if latent:
                s_l = lax.dot_general(q, klb_ref[...], (((1,), (1,)), ((), ())), preferred_element_type=F32)
                m = jnp.maximum(m, jnp.max(s_l, axis=-1, keepdims=True))
            p_c = jnp.exp(s_c - m)
            den = jnp.sum(p_c, axis=-1, keepdims=True)
            acc = jnp.dot(p_c.astype(BF16), vcb_ref[...], preferred_element_type=F32)
            if latent:
                p_l = jnp.exp(s_l - m)
                den = den + jnp.sum(p_l, axis=-1, keepdims=True)
                acc = acc + jnp.dot(p_l.astype(BF16), vlb_ref[...], preferred_element_type=F32)
            o_ref[:, sl] = (acc / den).astype(o_ref.dtype)

    @pl.when(qi < n_lat_tiles)
    def _():
        heads(True)

    @pl.when(qi == n_lat_tiles)
    def _():
        heads(False)


def _rope_tables():
    rows = SEQ // GRID_W
    row = jnp.repeat(jnp.arange(rows, dtype=F32), GRID_W)
    col = jnp.tile(jnp.arange(GRID_W, dtype=F32), rows)
    axis_dim = HEAD_DIM // 2
    inv_freq = ROPE_THETA ** (-jnp.arange(0, axis_dim, 2, dtype=F32) / axis_dim)
    ang_r = row[:, None] * inv_freq[None]
    ang_c = col[:, None] * inv_freq[None]
    cos = jnp.concatenate([jnp.cos(ang_r), jnp.cos(ang_r), jnp.cos(ang_c), jnp.cos(ang_c)], axis=-1)
    sin = jnp.concatenate([-jnp.sin(ang_r), jnp.sin(ang_r), -jnp.sin(ang_c), jnp.sin(ang_c)], axis=-1)
    return cos, sin


def _attention(z, q_g, k_g, cos, sin):
    n_lat_tiles = SEQ // TILE
    qw = ATTN_Q // ATTN_KV
    kcol = ATTN_Q // HEAD_DIM
    vcol = kcol + ATTN_KV

    def q_row(b, g, qi):
        return jnp.where(qi < n_lat_tiles, n_lat_tiles * b + qi, N_LAT // TILE + b)

    return pl.pallas_call(
        _attn_kernel,
        grid=(NB, ATTN_KV, n_lat_tiles + 1),
        in_specs=[pl.BlockSpec((TILE, qw), lambda b, g, qi: (q_row(b, g, qi), g)),
                  pl.BlockSpec((SEQ, HEAD_DIM), lambda b, g, qi: (b, kcol + g)),
                  pl.BlockSpec((CTX, HEAD_DIM), lambda b, g, qi: (N_LAT // CTX + b, kcol + g)),
                  pl.BlockSpec((SEQ, HEAD_DIM), lambda b, g, qi: (b, vcol + g)),
                  pl.BlockSpec((CTX, HEAD_DIM), lambda b, g, qi: (N_LAT // CTX + b, vcol + g)),
                  pl.BlockSpec((SEQ, HEAD_DIM), lambda b, g, qi: (0, 0)),
                  pl.BlockSpec((SEQ, HEAD_DIM), lambda b, g, qi: (0, 0)),
                  pl.BlockSpec((TILE, HEAD_DIM), lambda b, g, qi: (jnp.minimum(qi, n_lat_tiles - 1), 0)),
                  pl.BlockSpec((TILE, HEAD_DIM), lambda b, g, qi: (jnp.minimum(qi, n_lat_tiles - 1), 0)),
                  pl.BlockSpec((1, HEAD_DIM), lambda b, g, qi: (0, 0)),
                  pl.BlockSpec((1, HEAD_DIM), lambda b, g, qi: (0, 0))],
        out_specs=pl.BlockSpec((TILE, qw), lambda b, g, qi: (q_row(b, g, qi), g)),
        out_shape=jax.ShapeDtypeStruct((ROWS, ATTN_Q), BF16),
        scratch_shapes=[pltpu.VMEM((SEQ, HEAD_DIM), BF16), pltpu.VMEM((CTX, HEAD_DIM), BF16),
                        pltpu.VMEM((SEQ, HEAD_DIM), BF16), pltpu.VMEM((CTX, HEAD_DIM), BF16)],
        compiler_params=_cp("arbitrary", "arbitrary", "arbitrary"),
        name="attention",
    )(z, z, z, z, z, cos, sin, cos, sin, q_g.reshape(1, HEAD_DIM), k_g.reshape(1, HEAD_DIM))


def _rwkv_prep_kernel(z_ref, zp_ref, zn_ref, mu_ref, w0_ref, wup_ref, a0_ref, aup_ref, gup_ref,
                      kk_ref, ka_ref, rk_ref, bd_ref,
                      r_out, v_out, kkn_out, g_out, bonus_out, lw_out, b_out, km_out):
    z = z_ref[...]
    zp, zn = _neighbours(z, zp_ref[...], zn_ref[...])
    zs = z + mu_ref[...] * (0.5 * (zp + zn) - z)
    r = zs[:, 0:RW_W]
    k = zs[:, RW_W:2 * RW_W]
    v = zs[:, 2 * RW_W:3 * RW_W]
    w_lo = zs[:, 3 * RW_W:3 * RW_W + 64]
    a_lo = zs[:, 3 * RW_W + 64:3 * RW_W + 128]
    g_lo = zs[:, 3 * RW_W + 128:3 * RW_W + 256]
    bd = bd_ref[...]
    kk = k * kk_ref[...]
    kk = kk * lax.rsqrt(_dot_rhs_exact(kk * kk, bd) + 1e-12)
    r_out[...] = r
    v_out[...] = v
    kkn_out[...] = kk
    g_out[...] = _bdot(_sigmoid(g_lo), gup_ref[...])
    tw = jnp.tanh(w_lo)
    km_sum = None
    for d in range(2):
        lw = -RW_DECAY * _sigmoid(w0_ref[d] + _bdot(tw, wup_ref[d]))
        a = _sigmoid(a0_ref[d] + _bdot(a_lo, aup_ref[d]))
        km = k * (1.0 + (a - 1.0) * ka_ref[...])
        lw_out[d] = lw
        b_out[d] = a * kk
        km_out[d] = km
        km_sum = km if km_sum is None else km_sum + km
    bonus_out[...] = _dot_rhs_exact(r * km_sum * rk_ref[...], bd) * v


def _rwkv_prep(z, p):
    row = lambda a: a.reshape(1, -1)
    full = lambda shape: pl.BlockSpec(shape, lambda t: (0,) * len(shape))
    out_tok = pl.BlockSpec((TILE, RW_W), lambda t: (t, 0))
    out_dir = pl.BlockSpec((2, TILE, RW_W), lambda t: (0, t, 0))
    tok = jax.ShapeDtypeStruct((ROWS, RW_W), F32)
    drn = jax.ShapeDtypeStruct((2, ROWS, RW_W), F32)
    return pl.pallas_call(
        _rwkv_prep_kernel,
        grid=(N_TILES,),
        in_specs=_tile_halo_specs(RW_IN) + [
            full((1, RW_IN)), full((2, 1, RW_W)), full((2, 64, RW_W)), full((2, 1, RW_W)),
            full((2, 64, RW_W)), full((128, RW_W)), full((1, RW_W)), full((1, RW_W)), full((1, RW_W)),
            full((RW_W, RW_W))],
        out_specs=[out_tok] * 5 + [out_dir] * 3,
        out_shape=[tok] * 5 + [drn] * 3,
        compiler_params=_cp("parallel"),
        name="rwkv_prep",
    )(z, z, z, row(p["mu"]), p["w0"].reshape(2, 1, RW_W), p["w_up"], p["a0"].reshape(2, 1, RW_W),
      p["a_up"], p["g_up"], row(p["k_k"]), row(p["k_a"]), row(p["r_k"]),
      _head_block_ones(RW_W, RW_D).astype(BF16))


def _head_block_ones(width, head):
    idx = np.arange(width) // head
    return jnp.asarray((idx[:, None] == idx[None, :]).astype(np.float32))


def _split_bf16(x, pieces):
    out = []
    for _ in range(pieces):
        p = x.astype(BF16)
        out.append(p)
        x = x - p.astype(F32)
    return out


def _dot_rhs_exact(x, m, pieces=2):
    return sum(jnp.dot(p, m, preferred_element_type=F32) for p in _split_bf16(x, pieces))


def _dot_lhs_exact(m, x, pieces=3):
    return sum(jnp.dot(m, p, preferred_element_type=F32) for p in _split_bf16(x, pieces))


RW_SUB = 2


def _rwkv_chunk_kernel(r_ref, v_ref, kk_ref, lw_ref, b_ref, km_ref, p_ref, sl_ref, re_ref, ol_ref):
    d = pl.program_id(0)
    c = CHUNK
    hd = RW_D
    ti = lax.broadcasted_iota(jnp.int32, (c, c), 0)
    si = lax.broadcasted_iota(jnp.int32, (c, c), 1)
    delta = (ti - si) * (1 - 2 * d)
    incl = delta >= 0
    strict = delta > 0
    eye = jnp.where(ti == si, 1.0, 0.0)
    tri = jnp.where(incl, 1.0, 0.0).astype(BF16)
    gr = lax.broadcasted_iota(jnp.int32, (2 * c, 2 * c), 0)
    gc = lax.broadcasted_iota(jnp.int32, (2 * c, 2 * c), 1)
    gdelta = (gr % c - gc % c) * (1 - 2 * d)
    gmask = gdelta >= jnp.where(gr < c, 1, 0)

    items = []
    for sub in range(RW_SUB):
        rows = slice(sub * c, (sub + 1) * c)
        lw = lw_ref[0, rows, :]
        cs = _dot_lhs_exact(tri, lw)
        tot = jnp.sum(lw, axis=0, keepdims=True)
        r = r_ref[rows, :]
        v = v_ref[rows, :]
        kk = kk_ref[rows, :]
        bb = b_ref[0, rows, :]
        km = km_ref[0, rows, :]
        e_neg = jnp.exp(-cs)
        e_rem = jnp.exp(tot - cs)
        kkt = kk * jnp.exp(cs - lw)
        rt = r * jnp.exp(cs)
        bt = bb * e_neg
        kt = km * e_neg
        bh = bb * e_rem
        kh = km * e_rem
        e_tot = jnp.exp(tot)
        for h in range(RW_H):
            s = slice(h * hd, (h + 1) * hd)
            items.append(dict(sub=sub, s=s, kkt=kkt[:, s], rt=rt[:, s], bt=bt[:, s], kt=kt[:, s], v=v[:, s],
                              bh=bh[:, s], kh=kh[:, s], e_tot=e_tot[:, s]))

    g = [jnp.where(gmask, _bdot_nt(jnp.concatenate([it["kkt"], it["rt"]], axis=0),
                                   jnp.concatenate([it["bt"], it["kt"]], axis=0)), 0.0) for it in items]
    l_b = [x[:c, :c] for x in g]
    a_b = [x[c:, :c] for x in g]
    lkv_akv = [_bdot(x[:, c:], it["v"]) for x, it in zip(g, items)]
    pw = [_bdot(x, x) for x in l_b]
    inv = [eye - x for x in l_b]
    for _ in range(int(math.log2(c)) - 2):
        res = [_bdot(p, jnp.concatenate([p, i], axis=1)) for p, i in zip(pw, inv)]
        pw = [x[:, :c] for x in res]
        inv = [i + x[:, c:] for i, x in zip(inv, res)]
    inv = [i + _bdot(p, i) for p, i in zip(pw, inv)]
    wy = [_bdot(i, jnp.concatenate([it["kkt"], x[:c]], axis=1)) for i, it, x in zip(inv, items, lkv_akv)]
    ab_wy = [_bdot(a, x) for a, x in zip(a_b, wy)]
    cross = [_bdot_tn(jnp.concatenate([it["v"], x], axis=1), jnp.concatenate([it["kh"], it["bh"]], axis=1))
             for it, x in zip(items, wy)]
    for it, x_ab, x_lk, x_cr in zip(items, ab_wy, lkv_akv, cross):
        s = it["s"]
        j = jnp.where(d == 0, it["sub"], RW_SUB - 1 - it["sub"])
        re_ref[0, 0, j, :, s] = it["rt"] - x_ab[:, :hd]
        ol_ref[0, 0, j, :, s] = x_lk[c:] - x_ab[:, hd:]
        sl_ref[0, 0, j, :, s] = x_cr[:hd, :hd] - x_cr[2 * hd:, hd:]
        p_ref[0, 0, j, :, s] = eye * it["e_tot"] - x_cr[hd:2 * hd, hd:]


def _rwkv_carry_kernel(p_ref, sl_ref, s_out_ref, st_ref):
    @pl.when(pl.program_id(0) == 0)
    def _():
        st_ref[...] = jnp.zeros_like(st_ref)

    for d in range(2):
        for b in range(NB):
            s = st_ref[d, b]
            s_out_ref[d, b, 0] = s
            for h in range(RW_H):
                sl = slice(h * RW_D, (h + 1) * RW_D)
                st_ref[d, b, :, sl] = _hdot(s[:, sl], p_ref[d, b, 0, :, sl]) + sl_ref[d, b, 0, :, sl]


TILE_CHUNKS = TILE // CHUNK


def _tile_of_group(b, grp):
    return jnp.where(grp == 0, N_LAT // TILE + b, (SEQ // TILE) * b + grp - 1)


def _bwd_group(grp):
    return jnp.where(grp == 0, 0, SEQ_CHUNKS // TILE_CHUNKS - grp)


def _rwkv_out_kernel(olf_ref, ref_ref, sf_ref, olb_ref, reb_ref, sb_ref, bonus_ref, g_ref, gng_ref, gnb_ref,
                     bd_ref, y_ref, o_scr):
    n = TILE_CHUNKS
    items = [(i, slice(h * RW_D, (h + 1) * RW_D)) for i in range(n) for h in range(RW_H)]
    pf = [_bdot_nt(ref_ref[0, 0, i, :, s], sf_ref[0, 0, i, :, s]) for i, s in items]
    pb = [_bdot_nt(reb_ref[0, 0, n - 1 - i, :, s], sb_ref[0, 0, n - 1 - i, :, s]) for i, s in items]
    for (i, s), a, b in zip(items, pf, pb):
        o_scr[i * CHUNK:(i + 1) * CHUNK, s] = olf_ref[0, 0, i, :, s] + olb_ref[0, 0, n - 1 - i, :, s] + a + b
    o = o_scr[...] + bonus_ref[...]
    bd = bd_ref[...]
    cen = o - _dot_rhs_exact(o, bd) * (1.0 / RW_D)
    var = _dot_rhs_exact(cen * cen, bd) * (1.0 / RW_D)
    y = cen * lax.rsqrt(var + RW_GN_EPS) * gng_ref[...] + gnb_ref[...]
    y_ref[...] = (y * g_ref[...]).astype(y_ref.dtype)


def _rwkv_scan(r, v, kk, lw, bmat, km, bonus, g, gn_g, gn_b):
    sub_rows = RW_SUB * CHUNK
    lat_blocks = N_LAT // sub_rows
    lat_per_b = SEQ // sub_rows
    ctx_per_b = CTX // sub_rows

    def step_block(d, rb):
        is_lat = rb < lat_blocks
        b = jnp.where(is_lat, rb // lat_per_b, (rb - lat_blocks) // ctx_per_b)
        i = jnp.where(is_lat, rb % lat_per_b, (rb - lat_blocks) % ctx_per_b)
        fwd = jnp.where(is_lat, ctx_per_b + i, i)
        bwd = jnp.where(is_lat, ctx_per_b + lat_per_b - 1 - i, ctx_per_b - 1 - i)
        return b, jnp.where(d == 0, fwd, bwd)

    tok = pl.BlockSpec((sub_rows, RW_W), lambda d, rb: (rb, 0))
    drn = pl.BlockSpec((1, sub_rows, RW_W), lambda d, rb: (d, rb, 0))
    step_shape = jax.ShapeDtypeStruct((2, NB, SEQ_CHUNKS, CHUNK, RW_W), F32)
    step_blk = pl.BlockSpec((1, 1, RW_SUB, CHUNK, RW_W), lambda d, rb: (d,) + step_block(d, rb) + (0, 0))
    p, s_loc, r_eff, o_loc = pl.pallas_call(
        _rwkv_chunk_kernel,
        grid=(2, ROWS // sub_rows),
        in_specs=[tok, tok, tok, drn, drn, drn],
        out_specs=[step_blk] * 4,
        out_shape=[step_shape] * 4,
        compiler_params=_cp("parallel", "parallel"),
        name="rwkv_chunk",
    )(r, v, kk, lw, bmat, km)

    all_blk = pl.BlockSpec((2, NB, 1, CHUNK, RW_W), lambda n: (0, 0, n, 0, 0))
    s_in = pl.pallas_call(
        _rwkv_carry_kernel,
        grid=(SEQ_CHUNKS,),
        in_specs=[all_blk, all_blk],
        out_specs=all_blk,
        out_shape=step_shape,
        scratch_shapes=[pltpu.VMEM((2, NB, CHUNK, RW_W), F32)],
        compiler_params=_cp("arbitrary"),
        name="rwkv_carry",
    )(p, s_loc)

    fwd = pl.BlockSpec((1, 1, TILE_CHUNKS, CHUNK, RW_W), lambda b, grp: (0, b, grp, 0, 0))
    bwd = pl.BlockSpec((1, 1, TILE_CHUNKS, CHUNK, RW_W), lambda b, grp: (1, b, _bwd_group(grp), 0, 0))
    rows = pl.BlockSpec((TILE, RW_W), lambda b, grp: (_tile_of_group(b, grp), 0))
    vec = pl.BlockSpec((1, RW_W), lambda b, grp: (0, 0))
    return pl.pallas_call(
        _rwkv_out_kernel,
        grid=(NB, SEQ_CHUNKS // TILE_CHUNKS),
        in_specs=[fwd, fwd, fwd, bwd, bwd, bwd, rows, rows, vec, vec,
                  pl.BlockSpec((RW_W, RW_W), lambda b, grp: (0, 0))],
        out_specs=rows,
        out_shape=jax.ShapeDtypeStruct((ROWS, RW_W), BF16),
        scratch_shapes=[pltpu.VMEM((TILE, RW_W), F32)],
        compiler_params=_cp("parallel", "parallel"),
        name="rwkv_out",
    )(o_loc, r_eff, s_in, o_loc, r_eff, s_in, bonus, g, gn_g.reshape(1, RW_W), gn_b.reshape(1, RW_W),
      _head_block_ones(RW_W, RW_D).astype(BF16))


def _mlstm_prep_kernel(z_ref, zp_ref, zn_ref, w_ref, o_ref):
    z = z_ref[...]
    zp, zn = _neighbours(z, zp_ref[...], zn_ref[...])
    y = _silu(zp * w_ref[0:1, :] + z * w_ref[1:2, :] + zn * w_ref[2:3, :])
    col = lax.broadcasted_iota(jnp.int32, y.shape, 1)
    o_ref[...] = jnp.where(col >= ML_W, y * (ML_D ** -0.5), y)


def _mlstm_prep(z, conv_w):
    return pl.pallas_call(
        _mlstm_prep_kernel,
        grid=(N_TILES,),
        in_specs=_tile_halo_specs(2 * ML_W) + [pl.BlockSpec((3, 2 * ML_W), lambda t: (0, 0))],
        out_specs=pl.BlockSpec((TILE, 2 * ML_W), lambda t: (t, 0)),
        out_shape=jax.ShapeDtypeStruct((ROWS, 2 * ML_W), F32),
        compiler_params=_cp("parallel"),
        name="mlstm_prep",
    )(z, z, z, conv_w)


def _log_sigmoid(x):
    return jnp.minimum(x, 0.0) - jnp.log(1.0 + jnp.exp(-jnp.abs(x)))


N_CHAINS = 2 * NB


def _mlstm_scan_kernel(*refs):
    nc = N_CHAINS
    q_refs, k_refs, v_refs = refs[0:nc], refs[nc:2 * nc], refs[2 * nc:3 * nc]
    gc_refs, gr_refs = refs[3 * nc:4 * nc], refs[4 * nc:5 * nc]
    bc_ref, br_ref, o_ref, c_ref, n_ref, m_ref = refs[5 * nc:]

    @pl.when(pl.program_id(0) == 0)
    def _():
        c_ref[...] = jnp.zeros_like(c_ref)
        n_ref[...] = jnp.zeros_like(n_ref)
        m_ref[...] = jnp.zeros_like(m_ref)

    c = CHUNK
    ti = lax.broadcasted_iota(jnp.int32, (c, c), 0)
    si = lax.broadcasted_iota(jnp.int32, (c, c), 1)
    masks = (ti >= si, ti <= si)
    items = [(ci, h) for ci in range(nc) for h in range(ML_H)]
    sls = [slice(h * ML_D, (h + 1) * ML_D) for h in range(ML_H)]
    gcol = [gc_refs[ci][0, 0] + bc_ref[ci // NB] for ci in range(nc)]
    grow = [gr_refs[ci][0, 0] + br_ref[ci // NB] for ci in range(nc)]

    q = [q_refs[ci][:, sls[h]] for ci, h in items]
    k = [k_refs[ci][:, sls[h]] for ci, h in items]
    v = [v_refs[ci][:, sls[h]] for ci, h in items]
    qk = [_bdot_nt(a, b) for a, b in zip(q, k)]
    c_mat = [c_ref[ci, h] for ci, h in items]
    n_vec = [n_ref[ci, h] for ci, h in items]
    m_prev = [m_ref[ci, h][0:1, 0:1] for ci, h in items]
    qc = [_bdot_nt(a, b) for a, b in zip(q, c_mat)]
    qn = [jnp.sum(a * b, axis=1, keepdims=True) for a, b in zip(q, n_vec)]

    log_w, m_inter, cum_col, i_col, total = [], [], [], [], []
    for ci, h in items:
        mask = masks[ci // NB]
        mask_t = masks[1 - ci // NB]
        f_col = _log_sigmoid(gcol[ci][:, ML_H + h:ML_H + h + 1])
        f_row = _log_sigmoid(grow[ci][ML_H + h:ML_H + h + 1, :])
        cc = jnp.sum(jnp.where(mask, f_row, 0.0), axis=1, keepdims=True)
        cr = jnp.sum(jnp.where(mask_t, f_col, 0.0), axis=0, keepdims=True)
        log_w.append(jnp.where(mask, cc - cr + grow[ci][h:h + 1, :], ML_NEG))
        cum_col.append(cc)
        i_col.append(gcol[ci][:, h:h + 1])
        total.append(jnp.sum(f_row, axis=1, keepdims=True))
    m_inter = [a + b for a, b in zip(cum_col, m_prev)]
    m_t = [jnp.maximum(jnp.max(a, axis=1, keepdims=True), b) for a, b in zip(log_w, m_inter)]
    s = [a * jnp.exp(b - m) for a, b, m in zip(qk, log_w, m_t)]
    w_inter = [jnp.exp(a - m) for a, m in zip(m_inter, m_t)]
    sv = [_bdot(a, b) for a, b in zip(s, v)]
    for i, (ci, h) in enumerate(items):
        num = sv[i] + w_inter[i] * qc[i]
        den = jnp.sum(s[i], axis=1, keepdims=True) + w_inter[i] * qn[i]
        o_ref[ci // NB, ci % NB, 0, :, sls[h]] = num / jnp.maximum(jnp.abs(den), jnp.exp(-m_t[i]))
    log_src = [t - a + b for t, a, b in zip(total, cum_col, i_col)]
    m_new = [jnp.maximum(t + mp, jnp.max(ls, axis=0, keepdims=True)) for t, mp, ls in zip(total, m_prev, log_src)]
    src = [jnp.exp(ls - mn) for ls, mn in zip(log_src, m_new)]
    decay = [jnp.exp(t + mp - mn) for t, mp, mn in zip(total, m_prev, m_new)]
    vk = [_bdot_tn(a * sr, b) for a, sr, b in zip(v, src, k)]
    for i, (ci, h) in enumerate(items):
        c_ref[ci, h] = decay[i] * c_mat[i] + vk[i]
        n_ref[ci, h] = decay[i] * n_vec[i] + jnp.sum(src[i] * k[i], axis=0, keepdims=True)
        m_ref[ci, h] = jnp.broadcast_to(m_new[i], m_ref.shape[2:])


def _bwd_step(pos):
    return jnp.where(pos < CTX_CHUNKS, CTX_CHUNKS - 1 - pos, SEQ_CHUNKS - 1 + CTX_CHUNKS - pos)


def _mlstm_scan(qk, z, gcol, grow, bcol, brow):
    chains = [(d, b) for d in range(2) for b in range(NB)]

    def tok(col_blk):
        return [pl.BlockSpec((CHUNK, ML_W), lambda n, d=d, b=b: (_seq_row_block(b, d, n), col_blk))
                for d, b in chains]

    gc_specs = [pl.BlockSpec((1, 1, CHUNK, 2 * ML_H), lambda n, d=d, b=b: (d, _seq_row_block(b, d, n), 0, 0))
                for d, b in chains]
    gr_specs = [pl.BlockSpec((1, 1, 2 * ML_H, CHUNK), lambda n, d=d, b=b: (d, _seq_row_block(b, d, n), 0, 0))
                for d, b in chains]
    nc = N_CHAINS
    return pl.pallas_call(
        _mlstm_scan_kernel,
        grid=(SEQ_CHUNKS,),
        in_specs=tok(0) + tok(1) + tok(2) + gc_specs + gr_specs + [
            pl.BlockSpec((2, 1, 2 * ML_H), lambda n: (0, 0, 0)),
            pl.BlockSpec((2, 2 * ML_H, 1), lambda n: (0, 0, 0))],
        out_specs=pl.BlockSpec((2, NB, 1, CHUNK, ML_W), lambda n: (0, 0, n, 0, 0)),
        out_shape=jax.ShapeDtypeStruct((2, NB, SEQ_CHUNKS, CHUNK, ML_W), F32),
        scratch_shapes=[pltpu.VMEM((nc, ML_H, ML_D, ML_D), F32), pltpu.VMEM((nc, ML_H, 1, ML_D), F32),
                        pltpu.VMEM((nc, ML_H, 8, 128), F32)],
        compiler_params=_cp("arbitrary"),
        name="mlstm_scan",
    )(*([qk] * (2 * nc) + [z] * nc + [gcol] * nc + [grow] * nc + [bcol, brow]))


def _mlstm_out_kernel(hf_ref, hb_ref, og_ref, gng_ref, y_ref):
    n = TILE_CHUNKS
    for i in range(n):
        rows = slice(i * CHUNK, (i + 1) * CHUNK)
        hsum = _sigmoid(og_ref[rows, :]) * (hf_ref[0, 0, i] + hb_ref[0, 0, n - 1 - i])
        for h in range(ML_H):
            sl = slice(h * ML_D, (h + 1) * ML_D)
            x = hsum[:, sl]
            cen = x - jnp.mean(x, axis=-1, keepdims=True)
            var = jnp.mean(cen * cen, axis=-1, keepdims=True)
            y_ref[rows, sl] = (cen * lax.rsqrt(var + EPS) * gng_ref[:, sl]).astype(y_ref.dtype)


def _mlstm_out(hs, z, gn_g):
    blk = (1, 1, TILE_CHUNKS, CHUNK, ML_W)
    return pl.pallas_call(
        _mlstm_out_kernel,
        grid=(NB, SEQ_CHUNKS // TILE_CHUNKS),
        in_specs=[pl.BlockSpec(blk, lambda b, grp: (0, b, grp, 0, 0)),
                  pl.BlockSpec(blk, lambda b, grp: (1, b, _bwd_group(grp), 0, 0)),
                  pl.BlockSpec((TILE, ML_W), lambda b, grp: (_tile_of_group(b, grp), 3)),
                  pl.BlockSpec((1, ML_W), lambda b, grp: (0, 0))],
        out_specs=pl.BlockSpec((TILE, ML_W), lambda b, grp: (_tile_of_group(b, grp), 0)),
        out_shape=jax.ShapeDtypeStruct((ROWS, ML_W), BF16),
        compiler_params=_cp("parallel", "parallel"),
        name="mlstm_out",
    )(hs, hs, z, gn_g.reshape(1, ML_W))


def _mlstm(z_main, z_gates, conv_w, i_b, f_b, gn_g):
    qk = _mlstm_prep(z_main, conv_w)
    n_chunks = ROWS // CHUNK
    gates = z_gates[:, :4 * ML_H].reshape(n_chunks, CHUNK, 2, 2, ML_H)
    gcol = jnp.transpose(gates, (3, 0, 1, 2, 4)).reshape(2, n_chunks, CHUNK, 2 * ML_H)
    grow = jnp.swapaxes(gcol, 2, 3)
    bias = jnp.concatenate([i_b, f_b], axis=-1)
    hs = _mlstm_scan(qk, z_main, gcol, grow, bias.reshape(2, 1, 2 * ML_H), bias.reshape(2, 2 * ML_H, 1))
    return _mlstm_out(hs, z_main, gn_g)


def _s5_matrices(lam_re, lam_im, log_step, b_re, b_im, c_re, c_im, reverse):
    dt = jnp.exp(log_step)[:, None]
    mag = jnp.exp(lam_re * dt)
    a_re = mag * jnp.cos(lam_im * dt)
    a_im = mag * jnp.sin(lam_im * dt)
    den = lam_re * lam_re + lam_im * lam_im
    f_re = ((a_re - 1) * lam_re + a_im * lam_im) / den
    f_im = (a_im * lam_re - (a_re - 1) * lam_im) / den
    bb_re = f_re[..., None] * b_re - f_im[..., None] * b_im
    bb_im = f_re[..., None] * b_im + f_im[..., None] * b_re
    pr = [jnp.ones_like(a_re)]
    pi = [jnp.zeros_like(a_im)]
    for _ in range(S5_L):
        pr.append(pr[-1] * a_re - pi[-1] * a_im)
        pi.append(pr[-2] * a_im + pi[-1] * a_re)
    pr = jnp.stack(pr)
    pi = jnp.stack(pi)
    wr = pr[..., None] * bb_re - pi[..., None] * bb_im
    wi = pr[..., None] * bb_im + pi[..., None] * bb_re
    kern = (jnp.einsum("gcp,tgpk->tgck", c_re, wr, precision=HP)
            - jnp.einsum("gcp,tgpk->tgck", c_im, wi, precision=HP))
    j = np.arange(S5_L)[:, None]
    t = np.arange(S5_L)[None, :]
    lag = (j - t) if reverse else (t - j)
    valid = jnp.asarray((lag >= 0).astype(np.float32))
    m = kern[np.clip(lag, 0, S5_L)] * valid[:, :, None, None, None]
    m = jnp.transpose(m, (2, 0, 4, 1, 3)).reshape(S5_G, S5_L * S5_C, S5_L * S5_C)
    ex_in = np.arange(S5_L) if reverse else S5_L - 1 - np.arange(S5_L)
    e = jnp.concatenate([jnp.transpose(wr[ex_in], (1, 0, 3, 2)), jnp.transpose(wi[ex_in], (1, 0, 3, 2))],
                        axis=-1).reshape(S5_G, S5_L * S5_C, 2 * S5_P)
    ex_out = S5_L - np.arange(S5_L) if reverse else np.arange(S5_L) + 1
    pr_o = pr[ex_out][:, :, None, :]
    pi_o = pi[ex_out][:, :, None, :]
    fr = c_re[None] * pr_o - c_im[None] * pi_o
    fi = -(c_re[None] * pi_o + c_im[None] * pr_o)
    f = jnp.concatenate([jnp.transpose(fr, (1, 3, 0, 2)), jnp.transpose(fi, (1, 3, 0, 2))],
                        axis=1).reshape(S5_G, 2 * S5_P, S5_L * S5_C)
    return jnp.concatenate([m, e], axis=-1), f, pr[S5_L], pi[S5_L]


def _s5_local_kernel(u_ref, me_ref, y_ref, x_ref):
    res = _bdot(u_ref[0], me_ref[0, 0])
    y_ref[0, 0] = res[:, :S5_L * S5_C]
    x_ref[0, 0] = res[:, S5_L * S5_C:]


def _s5_carry_kernel(xc_ref, ar_ref, ai_ref, x0_ref, st_ref):
    @pl.when(pl.program_id(1) == 0)
    def _():
        st_ref[...] = jnp.zeros_like(st_ref)

    ar = ar_ref[0]
    ai = ai_ref[0]
    x = st_ref[...]
    for i in range(xc_ref.shape[1]):
        x0_ref[0, i] = x
        x = x * ar + pltpu.roll(x, S5_P, 1) * ai + xc_ref[0, i]
    st_ref[...] = x


def _s5_state_kernel(y_ref, x0_ref, f_ref, o_ref):
    contrib = y_ref[0, 0] + _bdot(x0_ref[0, 0], f_ref[0, 0])

    @pl.when(pl.program_id(1) == 0)
    def _():
        o_ref[0] = contrib

    @pl.when(pl.program_id(1) == 1)
    def _():
        o_ref[0] = o_ref[0] + contrib


def _s5_glu_kernel(y_ref, u_ref, d_ref, w_ref, o_ref):
    x = y_ref[...] + u_ref[...] * d_ref[...]
    ge = 0.5 * x * (1.0 + jnp.tanh(math.sqrt(2.0 / math.pi) * (x + 0.044715 * (x * x * x))))
    p = _bdot(ge, w_ref[...])
    o_ref[...] = (p[:, :S5_W] * _sigmoid(p[:, S5_W:])).astype(o_ref.dtype)


def _s5(z, lam_re, lam_im, log_step, b_re, b_im, c_re, c_im, d_skip, w_glu):
    n_ch = (SEQ + CTX) // S5_L
    n_ctx_ch = CTX // S5_L
    rows = NB * n_ch
    mats = [_s5_matrices(lam_re[d], lam_im[d], log_step[d], b_re[d], b_im[d], c_re[d], c_im[d], d == 1)
            for d in range(2)]
    me = jnp.stack([m[0] for m in mats]).astype(BF16)
    f = jnp.stack([m[1] for m in mats]).astype(BF16)
    a_re = jnp.stack([m[2] for m in mats])
    a_im = jnp.stack([m[3] for m in mats])

    def to_groups(x):
        x = x.reshape(NB, -1, S5_L, S5_G, S5_C)
        return jnp.transpose(x, (3, 0, 1, 2, 4)).reshape(S5_G, NB, -1, S5_L * S5_C)

    zb = z.astype(BF16)
    u = jnp.concatenate([to_groups(zb[N_LAT:]), to_groups(zb[:N_LAT])], axis=2).reshape(S5_G, rows, S5_L * S5_C)
    y_loc, x_in = pl.pallas_call(
        _s5_local_kernel,
        grid=(2, S5_G),
        in_specs=[pl.BlockSpec((1, rows, S5_L * S5_C), lambda d, g: (g, 0, 0)),
                  pl.BlockSpec((1, 1, S5_L * S5_C, S5_L * S5_C + 2 * S5_P), lambda d, g: (d, g, 0, 0))],
        out_specs=[pl.BlockSpec((1, 1, rows, S5_L * S5_C), lambda d, g: (d, g, 0, 0)),
                   pl.BlockSpec((1, 1, rows, 2 * S5_P), lambda d, g: (d, g, 0, 0))],
        out_shape=[jax.ShapeDtypeStruct((2, S5_G, rows, S5_L * S5_C), F32),
                   jax.ShapeDtypeStruct((2, S5_G, rows, 2 * S5_P), F32)],
        compiler_params=_cp("parallel", "parallel"),
        name="s5_local",
    )(u, me)

    perm_b = np.concatenate([np.arange(n_ctx_ch)[::-1], np.arange(n_ctx_ch, n_ch)[::-1]])
    inv_b = np.argsort(perm_b)
    x_in = x_in.reshape(2, S5_G, NB, n_ch, 2 * S5_P)
    x_ord = jnp.stack([x_in[0], x_in[1][:, :, perm_b]])
    x_ord = jnp.transpose(x_ord, (0, 3, 1, 2, 4)).reshape(2, n_ch, S5_G * NB, 2 * S5_P)
    coef_r = jnp.repeat(jnp.concatenate([a_re, a_re], axis=-1), NB, axis=1)
    coef_i = jnp.repeat(jnp.concatenate([-a_im, a_im], axis=-1), NB, axis=1)
    step = 16
    x0 = pl.pallas_call(
        _s5_carry_kernel,
        grid=(2, n_ch // step),
        in_specs=[pl.BlockSpec((1, step, S5_G * NB, 2 * S5_P), lambda d, i: (d, i, 0, 0)),
                  pl.BlockSpec((1, S5_G * NB, 2 * S5_P), lambda d, i: (d, 0, 0)),
                  pl.BlockSpec((1, S5_G * NB, 2 * S5_P), lambda d, i: (d, 0, 0))],
        out_specs=pl.BlockSpec((1, step, S5_G * NB, 2 * S5_P), lambda d, i: (d, i, 0, 0)),
        out_shape=jax.ShapeDtypeStruct((2, n_ch, S5_G * NB, 2 * S5_P), F32),
        scratch_shapes=[pltpu.VMEM((S5_G * NB, 2 * S5_P), F32)],
        compiler_params=_cp("arbitrary", "arbitrary"),
        name="s5_carry",
    )(x_ord, coef_r, coef_i)
    x0 = jnp.transpose(x0.reshape(2, n_ch, S5_G, NB, 2 * S5_P), (0, 2, 3, 1, 4))
    x0 = jnp.stack([x0[0], x0[1][:, :, inv_b]]).reshape(2, S5_G, rows, 2 * S5_P)

    y = pl.pallas_call(
        _s5_state_kernel,
        grid=(S5_G, 2),
        in_specs=[pl.BlockSpec((1, 1, rows, S5_L * S5_C), lambda g, d: (d, g, 0, 0)),
                  pl.BlockSpec((1, 1, rows, 2 * S5_P), lambda g, d: (d, g, 0, 0)),
                  pl.BlockSpec((1, 1, 2 * S5_P, S5_L * S5_C), lambda g, d: (d, g, 0, 0))],
        out_specs=pl.BlockSpec((1, rows, S5_L * S5_C), lambda g, d: (g, 0, 0)),
        out_shape=jax.ShapeDtypeStruct((S5_G, rows, S5_L * S5_C), F32),
        compiler_params=_cp("parallel", "arbitrary"),
        name="s5_state",
    )(y_loc, x0, f)

    y = jnp.transpose(y.reshape(S5_G, NB, n_ch, S5_L, S5_C), (1, 2, 3, 0, 4)).reshape(NB, n_ch * S5_L, S5_W)
    y = jnp.concatenate([y[:, CTX:].reshape(N_LAT, S5_W), y[:, :CTX].reshape(N_CTX, S5_W)], axis=0)
    tok = pl.BlockSpec((TILE, S5_W), lambda t: (t, 0))
    return pl.pallas_call(
        _s5_glu_kernel,
        grid=(N_TILES,),
        in_specs=[tok, tok, pl.BlockSpec((1, S5_W), lambda t: (0, 0)),
                  pl.BlockSpec((S5_W, 2 * S5_W), lambda t: (0, 0))],
        out_specs=tok,
        out_shape=jax.ShapeDtypeStruct((ROWS, S5_W), BF16),
        compiler_params=_cp("parallel"),
        name="s5_glu",
    )(y, z, d_skip.reshape(1, S5_W), w_glu)


def kernel(x, c, ctx, c_ctx, w_mod, b_mod, norm1_g, norm2_g, w_in, b_gate, q_norm_g, k_norm_g, rwkv_mu, rwkv_w0, rwkv_w_up, rwkv_a0, rwkv_a_up, rwkv_g_up, rwkv_k_k, rwkv_k_a, rwkv_r_k, rwkv_gn_g, rwkv_gn_b, mlstm_conv_w, mlstm_i_b, mlstm_f_b, mlstm_gn_g, s5_lam_re, s5_lam_im, s5_log_step, s5_b_re, s5_b_im, s5_c_re, s5_c_im, s5_d, s5_w_glu, w_br_attn, w_br_rwkv, w_br_mlstm, w_br_s5, w_out, w_ffn_in, w_ffn_out, final_norm_g):
    cos, sin = _rope_tables()
    xs = jnp.concatenate([x.reshape(N_LAT, D), ctx.reshape(N_CTX, D)], axis=0)
    c_all = jnp.concatenate([c, c_ctx[None], jnp.zeros((3, D), F32)], axis=0)
    off = np.cumsum([0, ATTN_IN, RW_IN, ML_IN, S5_W, GATE_IN])
    tm = 1024
    for l in range(DEPTH):
        last = l == DEPTH - 1
        n_rows = N_LAT if last else ROWS
        mod = _modulation(c_all, w_mod[l], b_mod[l]).reshape(8, 1, 6 * D)
        h = _norm_mod(xs, norm1_g[l], mod, 0, 1, N_TILES)
        wl = w_in[l]
        w_attn = wl[:, off[0]:off[1]].astype(BF16)
        w_rwkv = wl[:, off[1]:off[2]].astype(BF16)
        w_ml = wl[:, off[2]:off[2] + 4 * ML_W].astype(BF16)
        w_mlg = jnp.pad(wl[:, off[2] + 4 * ML_W:off[3]], ((0, 0), (0, 128 - 4 * ML_H))).astype(BF16)
        w_s5 = wl[:, off[3]:off[4]].astype(BF16)
        w_gate = wl[:, off[4]:off[5]].astype(BF16)
        z_attn = _mm(h, w_attn, tm, 512)
        z_rwkv = _mm(h, w_rwkv, tm, 896)
        z_ml = _mm(h, w_ml, tm, 1024)
        z_mlg = _mm(h, w_mlg, tm, 128)
        z_s5 = _mm(h, w_s5, tm, 512)
        z_gate = _mm(h, w_gate, tm, 1024)

        ya = _attention(z_attn, q_norm_g[l], k_norm_g[l], cos, sin)
        rp = dict(mu=rwkv_mu[l], w0=rwkv_w0[l], w_up=rwkv_w_up[l], a0=rwkv_a0[l], a_up=rwkv_a_up[l],
                  g_up=rwkv_g_up[l], k_k=rwkv_k_k[l], k_a=rwkv_k_a[l], r_k=rwkv_r_k[l].reshape(RW_W))
        r, v, kk, g, bonus, lw, bmat, km = _rwkv_prep(z_rwkv, rp)
        yr = _rwkv_scan(r, v, kk, lw, bmat, km, bonus, g, rwkv_gn_g[l], rwkv_gn_b[l])
        ym = _mlstm(z_ml, z_mlg, mlstm_conv_w[l], mlstm_i_b[l], mlstm_f_b[l], mlstm_gn_g[l])
        ys = _s5(z_s5, s5_lam_re[l], s5_lam_im[l], s5_log_step[l], s5_b_re[l], s5_b_im[l],
                 s5_c_re[l], s5_c_im[l], s5_d[l], s5_w_glu[l])

        y = _merge(ya, yr, ym, ys, z_gate, b_gate[l], w_br_attn[l], w_br_rwkv[l], w_br_mlstm[l], w_br_s5[l],
                   tm, 512, n_rows)
        xs = _mm_res(y, w_out[l], xs, mod, 2, tm, 1024, n_rows)
        h2 = _norm_mod(xs, norm2_g[l], mod, 3, 4, n_rows // TILE)
        u = _ffn_in(h2, w_ffn_in[l], tm, 512, n_rows)
        xs = _mm_res(u, w_ffn_out[l], xs, mod, 5, 512, 512, n_rows)
    return _final_norm(xs, final_norm_g).reshape(NB, SEQ, D)
```

```python
import functools
import math

import numpy as np
import jax
import jax.numpy as jnp
from jax import lax
from jax.experimental import pallas as pl
from jax.experimental.pallas import tpu as pltpu

F32 = jnp.float32
BF16 = jnp.bfloat16
HP = lax.Precision.HIGHEST

D = 2048
NB = 4
SEQ = 2048
CTX = 256
DEPTH = 2
N_LAT = NB * SEQ
N_CTX = NB * CTX
ROWS = N_LAT + N_CTX
EPS = 1e-6
GRID_W = 64

HEAD_DIM = 128
ATTN_HEADS = 8
ATTN_KV = 2
ROPE_THETA = 10000.0
ATTN_Q = ATTN_HEADS * HEAD_DIM
ATTN_IN = (ATTN_HEADS + 2 * ATTN_KV) * HEAD_DIM

RW_H = 8
RW_D = 64
RW_W = 512
RW_IN = 3 * RW_W + 64 + 64 + 128
RW_DECAY = math.exp(-0.5)
RW_GN_EPS = 64e-5

ML_H = 4
ML_D = 128
ML_W = 512
ML_NEG = -1e30
ML_IN = 4 * ML_W + 4 * ML_H

S5_W = 512
S5_C = 16
S5_G = 32
S5_P = 64
S5_L = 16

FFN_H = 5632
GATE_IN = 4 * D

CHUNK = 64
TILE = 256
N_TILES = ROWS // TILE
SEQ_CHUNKS = (SEQ + CTX) // CHUNK
CTX_CHUNKS = CTX // CHUNK
LAT_CHUNKS = SEQ // CHUNK

VMEM_LIMIT_BYTES = 56 * 1024 * 1024


def _cp(*sem):
    return pltpu.CompilerParams(dimension_semantics=sem, vmem_limit_bytes=VMEM_LIMIT_BYTES)


def _bdot(a, b):
    return jnp.dot(a.astype(BF16), b.astype(BF16), preferred_element_type=F32)


def _bdot_nt(a, b):
    return lax.dot_general(a.astype(BF16), b.astype(BF16), (((1,), (1,)), ((), ())),
                           preferred_element_type=F32)


def _bdot_tn(a, b):
    return lax.dot_general(a.astype(BF16), b.astype(BF16), (((0,), (0,)), ((), ())),
                           preferred_element_type=F32)


def _hdot(a, b):
    return jnp.dot(a, b, precision=HP, preferred_element_type=F32)


def _sigmoid(x):
    return 1.0 / (1.0 + jnp.exp(-x))


def _silu(x):
    return x * _sigmoid(x)


def _mod_kernel(c_ref, w_ref, b_ref, o_ref):
    o_ref[...] = _hdot(_silu(c_ref[...]), w_ref[...]) + b_ref[...]


def _modulation(c_all, w, b):
    tn = 1024
    return pl.pallas_call(
        _mod_kernel,
        grid=(6 * D // tn,),
        in_specs=[pl.BlockSpec((8, D), lambda j: (0, 0)),
                  pl.BlockSpec((D, tn), lambda j: (0, j)),
                  pl.BlockSpec((1, tn), lambda j: (0, j))],
        out_specs=pl.BlockSpec((8, tn), lambda j: (0, j)),
        out_shape=jax.ShapeDtypeStruct((8, 6 * D), F32),
        compiler_params=_cp("arbitrary"),
        name="modulation",
    )(c_all, w, b.reshape(1, 6 * D))


def _mod_row(i, tm):
    return jnp.where(i * tm < N_LAT, (i * tm) // SEQ, NB)


def _norm_mod_kernel(x_ref, g_ref, sh_ref, sc_ref, o_ref):
    x = x_ref[...]
    y = x * lax.rsqrt(jnp.mean(x * x, axis=-1, keepdims=True) + EPS) * g_ref[...]
    o_ref[...] = (y * (1.0 + sc_ref[0]) + sh_ref[0]).astype(o_ref.dtype)


def _norm_mod(x, g, modr, shift_blk, scale_blk, n_tiles):
    return pl.pallas_call(
        _norm_mod_kernel,
        grid=(n_tiles,),
        in_specs=[pl.BlockSpec((TILE, D), lambda i: (i, 0)),
                  pl.BlockSpec((1, D), lambda i: (0, 0)),
                  pl.BlockSpec((1, 1, D), lambda i: (_mod_row(i, TILE), 0, shift_blk)),
                  pl.BlockSpec((1, 1, D), lambda i: (_mod_row(i, TILE), 0, scale_blk))],
        out_specs=pl.BlockSpec((TILE, D), lambda i: (i, 0)),
        out_shape=jax.ShapeDtypeStruct((n_tiles * TILE, D), BF16),
        compiler_params=_cp("parallel"),
        name="norm_mod",
    )(x, g.reshape(1, D), modr, modr)


W_IN_OFFSETS = tuple(int(v) for v in np.cumsum([0, ATTN_IN, RW_IN, 4 * ML_W, 4 * ML_H, S5_W, GATE_IN]))
ML_GATE_PAD = 128


def _w_in_split_kernel(w_ref, attn_ref, rwkv_ref, ml_ref, mlg_ref, s5_ref, gate_ref):
    o = W_IN_OFFSETS
    attn_ref[...] = w_ref[:, o[0]:o[1]].astype(BF16)
    rwkv_ref[...] = w_ref[:, o[1]:o[2]].astype(BF16)
    ml_ref[...] = w_ref[:, o[2]:o[3]].astype(BF16)
    tail = w_ref[:, o[3]:o[3] + ML_GATE_PAD]
    lane = lax.broadcasted_iota(jnp.int32, tail.shape, 1)
    mlg_ref[...] = jnp.where(lane < o[4] - o[3], tail, 0.0).astype(BF16)
    s5_ref[...] = w_ref[:, o[4]:o[5]].astype(BF16)
    gate_ref[...] = w_ref[:, o[5]:o[6]].astype(BF16)


def _w_in_split(w):
    tr = 128
    widths = (ATTN_IN, RW_IN, 4 * ML_W, ML_GATE_PAD, S5_W, GATE_IN)
    return pl.pallas_call(
        _w_in_split_kernel,
        grid=(D // tr,),
        in_specs=[pl.BlockSpec((tr, w.shape[1]), lambda i: (i, 0))],
        out_specs=[pl.BlockSpec((tr, n), lambda i: (i, 0)) for n in widths],
        out_shape=[jax.ShapeDtypeStruct((D, n), BF16) for n in widths],
        compiler_params=_cp("parallel"),
        name="w_in_split",
    )(w)


def _mm_kernel(a_ref, w_ref, o_ref):
    o_ref[...] = jnp.dot(a_ref[...], w_ref[...], preferred_element_type=F32).astype(o_ref.dtype)


def _mm(a, w, tm, tn, out_dtype=F32):
    m, k = a.shape
    n = w.shape[1]
    return pl.pallas_call(
        _mm_kernel,
        grid=(n // tn, m // tm),
        in_specs=[pl.BlockSpec((tm, k), lambda j, i: (i, 0)),
                  pl.BlockSpec((k, tn), lambda j, i: (0, j))],
        out_specs=pl.BlockSpec((tm, tn), lambda j, i: (i, j)),
        out_shape=jax.ShapeDtypeStruct((m, n), out_dtype),
        compiler_params=_cp("parallel", "parallel"),
        name="matmul",
    )(a, w)


def _mm_res_kernel(a_ref, w_ref, x_ref, g_ref, o_ref, wb_ref):
    @pl.when(pl.program_id(1) == 0)
    def _():
        wb_ref[...] = w_ref[...].astype(BF16)

    y = jnp.dot(a_ref[...], wb_ref[...], preferred_element_type=F32)
    o_ref[...] = x_ref[...] + g_ref[0] * y


def _mm_res(a, w, x, mod, gate_blk, tm, tn, n_rows):
    k = a.shape[1]
    n = w.shape[1]
    gpb = D // tn
    return pl.pallas_call(
        _mm_res_kernel,
        grid=(n // tn, n_rows // tm),
        in_specs=[pl.BlockSpec((tm, k), lambda j, i: (i, 0)),
                  pl.BlockSpec((k, tn), lambda j, i: (0, j)),
                  pl.BlockSpec((tm, tn), lambda j, i: (i, j)),
                  pl.BlockSpec((1, 1, tn), lambda j, i: (_mod_row(i, tm), 0, gate_blk * gpb + j))],
        out_specs=pl.BlockSpec((tm, tn), lambda j, i: (i, j)),
        out_shape=jax.ShapeDtypeStruct((n_rows, n), F32),
        scratch_shapes=[pltpu.VMEM((k, tn), BF16)],
        compiler_params=_cp("arbitrary", "arbitrary"),
        name="matmul_residual",
    )(a, w, x, mod)


def _ffn_in_kernel(a_ref, wa_ref, wb_ref, o_ref, wab_ref, wbb_ref):
    @pl.when(pl.program_id(1) == 0)
    def _():
        wab_ref[...] = wa_ref[...].astype(BF16)
        wbb_ref[...] = wb_ref[...].astype(BF16)

    a = a_ref[...]
    u = jnp.dot(a, wab_ref[...], preferred_element_type=F32)
    v = jnp.dot(a, wbb_ref[...], preferred_element_type=F32)
    o_ref[...] = (_silu(u) * v).astype(o_ref.dtype)


def _ffn_in(h, w, tm, tn, n_rows):
    nb = FFN_H // tn
    return pl.pallas_call(
        _ffn_in_kernel,
        grid=(nb, n_rows // tm),
        in_specs=[pl.BlockSpec((tm, D), lambda j, i: (i, 0)),
                  pl.BlockSpec((D, tn), lambda j, i: (0, j)),
                  pl.BlockSpec((D, tn), lambda j, i: (0, nb + j))],
        out_specs=pl.BlockSpec((tm, tn), lambda j, i: (i, j)),
        out_shape=jax.ShapeDtypeStruct((n_rows, FFN_H), BF16),
        scratch_shapes=[pltpu.VMEM((D, tn), BF16), pltpu.VMEM((D, tn), BF16)],
        compiler_params=_cp("arbitrary", "arbitrary"),
        name="ffn_in",
    )(h, w, w)


def _merge_kernel(ya_ref, yr_ref, ym_ref, ys_ref, ga_ref, gr_ref, gm_ref, gs_ref,
                  ba_ref, br_ref, bm_ref, bs_ref, wa_ref, wr_ref, wm_ref, ws_ref, o_ref,
                  wab_ref, wrb_ref, wmb_ref, wsb_ref):
    @pl.when(pl.program_id(1) == 0)
    def _():
        wab_ref[...] = wa_ref[...].astype(BF16)
        wrb_ref[...] = wr_ref[...].astype(BF16)
        wmb_ref[...] = wm_ref[...].astype(BF16)
        wsb_ref[...] = ws_ref[...].astype(BF16)

    acc = None
    for y_ref, g_ref, b_ref, w_ref in ((ya_ref, ga_ref, ba_ref, wab_ref), (yr_ref, gr_ref, br_ref, wrb_ref),
                                       (ym_ref, gm_ref, bm_ref, wmb_ref), (ys_ref, gs_ref, bs_ref, wsb_ref)):
        gate = _sigmoid(g_ref[...] + b_ref[...])
        term = gate * jnp.dot(y_ref[...], w_ref[...], preferred_element_type=F32)
        acc = term if acc is None else acc + term
    o_ref[...] = acc.astype(o_ref.dtype)


def _merge(ya, yr, ym, ys, z_gate, b_gate, wa, wr, wm, ws, tm, tn, n_rows):
    nb = D // tn
    b_gate = b_gate.reshape(1, GATE_IN)

    def act(width):
        return pl.BlockSpec((tm, width), lambda j, i: (i, 0))

    def gate(br):
        return pl.BlockSpec((tm, tn), lambda j, i: (i, br * nb + j))

    def bias(br):
        return pl.BlockSpec((1, tn), lambda j, i: (0, br * nb + j))

    def wgt(width):
        return pl.BlockSpec((width, tn), lambda j, i: (0, j))

    return pl.pallas_call(
        _merge_kernel,
        grid=(nb, n_rows // tm),
        in_specs=[act(ATTN_Q), act(RW_W), act(ML_W), act(S5_W),
                  gate(0), gate(1), gate(2), gate(3),
                  bias(0), bias(1), bias(2), bias(3),
                  wgt(ATTN_Q), wgt(RW_W), wgt(ML_W), wgt(S5_W)],
        out_specs=pl.BlockSpec((tm, tn), lambda j, i: (i, j)),
        out_shape=jax.ShapeDtypeStruct((n_rows, D), BF16),
        scratch_shapes=[pltpu.VMEM((ATTN_Q, tn), BF16), pltpu.VMEM((RW_W, tn), BF16),
                        pltpu.VMEM((ML_W, tn), BF16), pltpu.VMEM((S5_W, tn), BF16)],
        compiler_params=_cp("arbitrary", "arbitrary"),
        name="gated_merge",
    )(ya, yr, ym, ys, z_gate, z_gate, z_gate, z_gate, b_gate, b_gate, b_gate, b_gate, wa, wr, wm, ws)


def _final_norm_kernel(x_ref, g_ref, o_ref):
    x = x_ref[...]
    o_ref[...] = x * lax.rsqrt(jnp.mean(x * x, axis=-1, keepdims=True) + EPS) * g_ref[...]


def _final_norm(x, g):
    n_tiles = N_LAT // TILE
    return pl.pallas_call(
        _final_norm_kernel,
        grid=(n_tiles,),
        in_specs=[pl.BlockSpec((TILE, D), lambda i: (i, 0)),
                  pl.BlockSpec((1, D), lambda i: (0, 0))],
        out_specs=pl.BlockSpec((TILE, D), lambda i: (i, 0)),
        out_shape=jax.ShapeDtypeStruct((N_LAT, D), F32),
        compiler_params=_cp("parallel"),
        name="final_norm",
    )(x, g.reshape(1, D))


def _tile_halo_specs(width, col_blk=0):
    last = ROWS // 8 - 1
    per = TILE // 8
    return [pl.BlockSpec((TILE, width), lambda t: (t, col_blk)),
            pl.BlockSpec((8, width), lambda t: (jnp.maximum(t * per - 1, 0), col_blk)),
            pl.BlockSpec((8, width), lambda t: (jnp.minimum((t + 1) * per, last), col_blk))]


def _neighbours(z, prev_blk, next_blk):
    t = pl.program_id(0)
    pos = t % (SEQ // TILE)
    is_lat = t < N_LAT // TILE
    has_prev = jnp.logical_and(is_lat, pos > 0).astype(F32)
    has_next = jnp.logical_and(is_lat, pos < SEQ // TILE - 1).astype(F32)
    row = lax.broadcasted_iota(jnp.int32, z.shape, 0)
    zp = jnp.where(row == 0, prev_blk[7:8, :] * has_prev, pltpu.roll(z, 1, 0))
    zn = jnp.where(row == TILE - 1, next_blk[0:1, :] * has_next, pltpu.roll(z, TILE - 1, 0))
    return zp, zn


def _seq_row_block(b, d, n):
    ctx_c = jnp.where(d == 0, n, CTX_CHUNKS - 1 - n)
    lat_c = jnp.where(d == 0, n - CTX_CHUNKS, SEQ_CHUNKS - 1 - n)
    return jnp.where(n < CTX_CHUNKS, N_LAT // CHUNK + CTX_CHUNKS * b + ctx_c, LAT_CHUNKS * b + lat_c)


def _rope(x, cos, sin):
    lane = lax.broadcasted_iota(jnp.int32, x.shape, 1)
    first = (lane % 64) < 32
    partner = jnp.where(first, pltpu.roll(x, 96, 1), pltpu.roll(x, 32, 1))
    return x * cos + partner * sin


def _rms(x, g):
    return x * lax.rsqrt(jnp.mean(x * x, axis=-1, keepdims=True) + EPS) * g


def _attn_kernel(q_ref, kl_ref, kc_ref, vl_ref, vc_ref, cos_ref, sin_ref, cos_t_ref, sin_t_ref,
                 qg_ref, kg_ref, o_ref, klb_ref, kcb_ref, vlb_ref, vcb_ref):
    qi = pl.program_id(2)
    n_lat_tiles = SEQ // TILE

    @pl.when(qi == 0)
    def _():
        kg = kg_ref[...]
        klb_ref[...] = _rope(_rms(kl_ref[...], kg), cos_ref[...], sin_ref[...]).astype(BF16)
        kcb_ref[...] = _rms(kc_ref[...], kg).astype(BF16)
        vlb_ref[...] = vl_ref[...].astype(BF16)
        vcb_ref[...] = vc_ref[...].astype(BF16)

    scale = HEAD_DIM ** -0.5

    def heads(latent):
        for h in range(ATTN_HEADS // ATTN_KV):
            sl = slice(h * HEAD_DIM, (h + 1) * HEAD_DIM)
            q = _rms(q_ref[:, sl], qg_ref[...])
            if latent:
                q = _rope(q, cos_t_ref[...], sin_t_ref[...])
            q = (q * scale).astype(BF16)
            s_c = lax.dot_general(q, kcb_ref[...], (((1,), (1,)), ((), ())), preferred_element_type=F32)
            m = jnp.max(s_c, axis=-1, keepdims=True)
            if latent:
                s_l = lax.dot_general(q, klb_ref[...], (((1,), (1,)), ((), ())), preferred_element_type=F32)
                m = jnp.maximum(m, jnp.max(s_l, axis=-1, keepdims=True))
            p_c = jnp.exp(s_c - m)
            den = jnp.sum(p_c, axis=-1, keepdims=True)
            acc = jnp.dot(p_c.astype(BF16), vcb_ref[...], preferred_element_type=F32)
            if latent:
                p_l = jnp.exp(s_l - m)
                den = den + jnp.sum(p_l, axis=-1, keepdims=True)
                acc = acc + jnp.dot(p_l.astype(BF16), vlb_ref[...], preferred_element_type=F32)
            o_ref[:, sl] = (acc / den).astype(o_ref.dtype)

    @pl.when(qi < n_lat_tiles)
    def _():
        heads(True)

    @pl.when(qi == n_lat_tiles)
    def _():
        heads(False)


def _rope_tables():
    rows = SEQ // GRID_W
    row = jnp.repeat(jnp.arange(rows, dtype=F32), GRID_W)
    col = jnp.tile(jnp.arange(GRID_W, dtype=F32), rows)
    axis_dim = HEAD_DIM // 2
    inv_freq = ROPE_THETA ** (-jnp.arange(0, axis_dim, 2, dtype=F32) / axis_dim)
    ang_r = row[:, None] * inv_freq[None]
    ang_c = col[:, None] * inv_freq[None]
    cos = jnp.concatenate([jnp.cos(ang_r), jnp.cos(ang_r), jnp.cos(ang_c), jnp.cos(ang_c)], axis=-1)
    sin = jnp.concatenate([-jnp.sin(ang_r), jnp.sin(ang_r), -jnp.sin(ang_c), jnp.sin(ang_c)], axis=-1)
    return cos, sin


def _attention(z, q_g, k_g, cos, sin):
    n_lat_tiles = SEQ // TILE
    qw = ATTN_Q // ATTN_KV
    kcol = ATTN_Q // HEAD_DIM
    vcol = kcol + ATTN_KV

    def q_row(b, g, qi):
        return jnp.where(qi < n_lat_tiles, n_lat_tiles * b + qi, N_LAT // TILE + b)

    return pl.pallas_call(
        _attn_kernel,
        grid=(NB, ATTN_KV, n_lat_tiles + 1),
        in_specs=[pl.BlockSpec((TILE, qw), lambda b, g, qi: (q_row(b, g, qi), g)),
                  pl.BlockSpec((SEQ, HEAD_DIM), lambda b, g, qi: (b, kcol + g)),
                  pl.BlockSpec((CTX, HEAD_DIM), lambda b, g, qi: (N_LAT // CTX + b, kcol + g)),
                  pl.BlockSpec((SEQ, HEAD_DIM), lambda b, g, qi: (b, vcol + g)),
                  pl.BlockSpec((CTX, HEAD_DIM), lambda b, g, qi: (N_LAT // CTX + b, vcol + g)),
                  pl.BlockSpec((SEQ, HEAD_DIM), lambda b, g, qi: (0, 0)),
                  pl.BlockSpec((SEQ, HEAD_DIM), lambda b, g, qi: (0, 0)),
                  pl.BlockSpec((TILE, HEAD_DIM), lambda b, g, qi: (jnp.minimum(qi, n_lat_tiles - 1), 0)),
                  pl.BlockSpec((TILE, HEAD_DIM), lambda b, g, qi: (jnp.minimum(qi, n_lat_tiles - 1), 0)),
                  pl.BlockSpec((1, HEAD_DIM), lambda b, g, qi: (0, 0)),
                  pl.BlockSpec((1, HEAD_DIM), lambda b, g, qi: (0, 0))],
        out_specs=pl.BlockSpec((TILE, qw), lambda b, g, qi: (q_row(b, g, qi), g)),
        out_shape=jax.ShapeDtypeStruct((ROWS, ATTN_Q), BF16),
        scratch_shapes=[pltpu.VMEM((SEQ, HEAD_DIM), BF16), pltpu.VMEM((CTX, HEAD_DIM), BF16),
                        pltpu.VMEM((SEQ, HEAD_DIM), BF16), pltpu.VMEM((CTX, HEAD_DIM), BF16)],
        compiler_params=_cp("arbitrary", "arbitrary", "arbitrary"),
        name="attention",
    )(z, z, z, z, z, cos, sin, cos, sin, q_g.reshape(1, HEAD_DIM), k_g.reshape(1, HEAD_DIM))


def _rwkv_prep_kernel(z_ref, zp_ref, zn_ref, mu_ref, w0_ref, wup_ref, a0_ref, aup_ref, gup_ref,
                      kk_ref, ka_ref, rk_ref, bd_ref,
                      r_out, v_out, kkn_out, g_out, bonus_out, lw_out, b_out, km_out):
    z = z_ref[...]
    zp, zn = _neighbours(z, zp_ref[...], zn_ref[...])
    zs = z + mu_ref[...] * (0.5 * (zp + zn) - z)
    r = zs[:, 0:RW_W]
    k = zs[:, RW_W:2 * RW_W]
    v = zs[:, 2 * RW_W:3 * RW_W]
    w_lo = zs[:, 3 * RW_W:3 * RW_W + 64]
    a_lo = zs[:, 3 * RW_W + 64:3 * RW_W + 128]
    g_lo = zs[:, 3 * RW_W + 128:3 * RW_W + 256]
    bd = bd_ref[...]
    kk = k * kk_ref[...]
    kk = kk * lax.rsqrt(_dot_rhs_exact(kk * kk, bd) + 1e-12)
    r_out[...] = r
    v_out[...] = v
    kkn_out[...] = kk
    g_out[...] = _bdot(_sigmoid(g_lo), gup_ref[...])
    tw = jnp.tanh(w_lo)
    km_sum = None
    for d in range(2):
        lw = -RW_DECAY * _sigmoid(w0_ref[d] + _bdot(tw, wup_ref[d]))
        a = _sigmoid(a0_ref[d] + _bdot(a_lo, aup_ref[d]))
        km = k * (1.0 + (a - 1.0) * ka_ref[...])
        lw_out[d] = lw
        b_out[d] = a * kk
        km_out[d] = km
        km_sum = km if km_sum is None else km_sum + km
    bonus_out[...] = _dot_rhs_exact(r * km_sum * rk_ref[...], bd) * v


def _rwkv_prep(z, p):
    row = lambda a: a.reshape(1, -1)
    full = lambda shape: pl.BlockSpec(shape, lambda t: (0,) * len(shape))
    out_tok = pl.BlockSpec((TILE, RW_W), lambda t: (t, 0))
    out_dir = pl.BlockSpec((2, TILE, RW_W), lambda t: (0, t, 0))
    tok = jax.ShapeDtypeStruct((ROWS, RW_W), F32)
    drn = jax.ShapeDtypeStruct((2, ROWS, RW_W), F32)
    return pl.pallas_call(
        _rwkv_prep_kernel,
        grid=(N_TILES,),
        in_specs=_tile_halo_specs(RW_IN) + [
            full((1, RW_IN)), full((2, 1, RW_W)), full((2, 64, RW_W)), full((2, 1, RW_W)),
            full((2, 64, RW_W)), full((128, RW_W)), full((1, RW_W)), full((1, RW_W)), full((1, RW_W)),
            full((RW_W, RW_W))],
        out_specs=[out_tok] * 5 + [out_dir] * 3,
        out_shape=[tok] * 5 + [drn] * 3,
        compiler_params=_cp("parallel"),
        name="rwkv_prep",
    )(z, z, z, row(p["mu"]), p["w0"].reshape(2, 1, RW_W), p["w_up"], p["a0"].reshape(2, 1, RW_W),
      p["a_up"], p["g_up"], row(p["k_k"]), row(p["k_a"]), row(p["r_k"]),
      _head_block_ones(RW_W, RW_D).astype(BF16))


def _head_block_ones(width, head):
    idx = np.arange(width) // head
    return jnp.asarray((idx[:, None] == idx[None, :]).astype(np.float32))


def _split_bf16(x, pieces):
    out = []
    for _ in range(pieces):
        p = x.astype(BF16)
        out.append(p)
        x = x - p.astype(F32)
    return out


def _dot_rhs_exact(x, m, pieces=2):
    return sum(jnp.dot(p, m, preferred_element_type=F32) for p in _split_bf16(x, pieces))


def _dot_lhs_exact(m, x, pieces=3):
    return sum(jnp.dot(m, p, preferred_element_type=F32) for p in _split_bf16(x, pieces))


RW_SUB = 2


def _rwkv_chunk_kernel(r_ref, v_ref, kk_ref, lw_ref, b_ref, km_ref, p_ref, sl_ref, re_ref, ol_ref):
    d = pl.program_id(0)
    c = CHUNK
    hd = RW_D
    ti = lax.broadcasted_iota(jnp.int32, (c, c), 0)
    si = lax.broadcasted_iota(jnp.int32, (c, c), 1)
    delta = (ti - si) * (1 - 2 * d)
    incl = delta >= 0
    strict = delta > 0
    eye = jnp.where(ti == si, 1.0, 0.0)
    tri = jnp.where(incl, 1.0, 0.0).astype(BF16)
    gr = lax.broadcasted_iota(jnp.int32, (2 * c, 2 * c), 0)
    gc = lax.broadcasted_iota(jnp.int32, (2 * c, 2 * c), 1)
    gdelta = (gr % c - gc % c) * (1 - 2 * d)
    gmask = gdelta >= jnp.where(gr < c, 1, 0)

    items = []
    for sub in range(RW_SUB):
        rows = slice(sub * c, (sub + 1) * c)
        lw = lw_ref[0, rows, :]
        cs = _dot_lhs_exact(tri, lw)
        tot = jnp.sum(lw, axis=0, keepdims=True)
        r = r_ref[rows, :]
        v = v_ref[rows, :]
        kk = kk_ref[rows, :]
        bb = b_ref[0, rows, :]
        km = km_ref[0, rows, :]
        e_neg = jnp.exp(-cs)
        e_rem = jnp.exp(tot - cs)
        kkt = kk * jnp.exp(cs - lw)
        rt = r * jnp.exp(cs)
        bt = bb * e_neg
        kt = km * e_neg
        bh = bb * e_rem
        kh = km * e_rem
        e_tot = jnp.exp(tot)
        for h in range(RW_H):
            s = slice(h * hd, (h + 1) * hd)
            items.append(dict(sub=sub, s=s, kkt=kkt[:, s], rt=rt[:, s], bt=bt[:, s], kt=kt[:, s], v=v[:, s],
                              bh=bh[:, s], kh=kh[:, s], e_tot=e_tot[:, s]))

    g = [jnp.where(gmask, _bdot_nt(jnp.concatenate([it["kkt"], it["rt"]], axis=0),
                                   jnp.concatenate([it["bt"], it["kt"]], axis=0)), 0.0) for it in items]
    l_b = [x[:c, :c] for x in g]
    a_b = [x[c:, :c] for x in g]
    lkv_akv = [_bdot(x[:, c:], it["v"]) for x, it in zip(g, items)]
    pw = [_bdot(x, x) for x in l_b]
    inv = [eye - x for x in l_b]
    for _ in range(int(math.log2(c)) - 2):
        res = [_bdot(p, jnp.concatenate([p, i], axis=1)) for p, i in zip(pw, inv)]
        pw = [x[:, :c] for x in res]
        inv = [i + x[:, c:] for i, x in zip(inv, res)]
    inv = [i + _bdot(p, i) for p, i in zip(pw, inv)]
    wy = [_bdot(i, jnp.concatenate([it["kkt"], x[:c]], axis=1)) for i, it, x in zip(inv, items, lkv_akv)]
    ab_wy = [_bdot(a, x) for a, x in zip(a_b, wy)]
    cross = [_bdot_tn(jnp.concatenate([it["v"], x], axis=1), jnp.concatenate([it["kh"], it["bh"]], axis=1))
             for it, x in zip(items, wy)]
    for it, x_ab, x_lk, x_cr in zip(items, ab_wy, lkv_akv, cross):
        s = it["s"]
        j = jnp.where(d == 0, it["sub"], RW_SUB - 1 - it["sub"])
        re_ref[0, 0, j, :, s] = it["rt"] - x_ab[:, :hd]
        ol_ref[0, 0, j, :, s] = x_lk[c:] - x_ab[:, hd:]
        sl_ref[0, 0, j, :, s] = x_cr[:hd, :hd] - x_cr[2 * hd:, hd:]
        p_ref[0, 0, j, :, s] = eye * it["e_tot"] - x_cr[hd:2 * hd, hd:]


def _rwkv_carry_kernel(p_ref, sl_ref, s_out_ref, st_ref):
    @pl.when(pl.program_id(0) == 0)
    def _():
        st_ref[...] = jnp.zeros_like(st_ref)

    for d in range(2):
        for b in range(NB):
            s = st_ref[d, b]
            s_out_ref[d, b, 0] = s
            for h in range(RW_H):
                sl = slice(h * RW_D, (h + 1) * RW_D)
                st_ref[d, b, :, sl] = _hdot(s[:, sl], p_ref[d, b, 0, :, sl]) + sl_ref[d, b, 0, :, sl]


TILE_CHUNKS = TILE // CHUNK


def _tile_of_group(b, grp):
    return jnp.where(grp == 0, N_LAT // TILE + b, (SEQ // TILE) * b + grp - 1)


def _bwd_group(grp):
    return jnp.where(grp == 0, 0, SEQ_CHUNKS // TILE_CHUNKS - grp)


def _rwkv_out_kernel(olf_ref, ref_ref, sf_ref, olb_ref, reb_ref, sb_ref, bonus_ref, g_ref, gng_ref, gnb_ref,
                     bd_ref, y_ref, o_scr):
    n = TILE_CHUNKS
    items = [(i, slice(h * RW_D, (h + 1) * RW_D)) for i in range(n) for h in range(RW_H)]
    pf = [_bdot_nt(ref_ref[0, 0, i, :, s], sf_ref[0, 0, i, :, s]) for i, s in items]
    pb = [_bdot_nt(reb_ref[0, 0, n - 1 - i, :, s], sb_ref[0, 0, n - 1 - i, :, s]) for i, s in items]
    for (i, s), a, b in zip(items, pf, pb):
        o_scr[i * CHUNK:(i + 1) * CHUNK, s] = olf_ref[0, 0, i, :, s] + olb_ref[0, 0, n - 1 - i, :, s] + a + b
    o = o_scr[...] + bonus_ref[...]
    bd = bd_ref[...]
    cen = o - _dot_rhs_exact(o, bd) * (1.0 / RW_D)
    var = _dot_rhs_exact(cen * cen, bd) * (1.0 / RW_D)
    y = cen * lax.rsqrt(var + RW_GN_EPS) * gng_ref[...] + gnb_ref[...]
    y_ref[...] = (y * g_ref[...]).astype(y_ref.dtype)


def _rwkv_scan(r, v, kk, lw, bmat, km, bonus, g, gn_g, gn_b):
    sub_rows = RW_SUB * CHUNK
    lat_blocks = N_LAT // sub_rows
    lat_per_b = SEQ // sub_rows
    ctx_per_b = CTX // sub_rows

    def step_block(d, rb):
        is_lat = rb < lat_blocks
        b = jnp.where(is_lat, rb // lat_per_b, (rb - lat_blocks) // ctx_per_b)
        i = jnp.where(is_lat, rb % lat_per_b, (rb - lat_blocks) % ctx_per_b)
        fwd = jnp.where(is_lat, ctx_per_b + i, i)
        bwd = jnp.where(is_lat, ctx_per_b + lat_per_b - 1 - i, ctx_per_b - 1 - i)
        return b, jnp.where(d == 0, fwd, bwd)

    tok = pl.BlockSpec((sub_rows, RW_W), lambda d, rb: (rb, 0))
    drn = pl.BlockSpec((1, sub_rows, RW_W), lambda d, rb: (d, rb, 0))
    step_shape = jax.ShapeDtypeStruct((2, NB, SEQ_CHUNKS, CHUNK, RW_W), F32)
    step_blk = pl.BlockSpec((1, 1, RW_SUB, CHUNK, RW_W), lambda d, rb: (d,) + step_block(d, rb) + (0, 0))
    p, s_loc, r_eff, o_loc = pl.pallas_call(
        _rwkv_chunk_kernel,
        grid=(2, ROWS // sub_rows),
        in_specs=[tok, tok, tok, drn, drn, drn],
        out_specs=[step_blk] * 4,
        out_shape=[step_shape] * 4,
        compiler_params=_cp("parallel", "parallel"),
        name="rwkv_chunk",
    )(r, v, kk, lw, bmat, km)

    all_blk = pl.BlockSpec((2, NB, 1, CHUNK, RW_W), lambda n: (0, 0, n, 0, 0))
    s_in = pl.pallas_call(
        _rwkv_carry_kernel,
        grid=(SEQ_CHUNKS,),
        in_specs=[all_blk, all_blk],
        out_specs=all_blk,
        out_shape=step_shape,
        scratch_shapes=[pltpu.VMEM((2, NB, CHUNK, RW_W), F32)],
        compiler_params=_cp("arbitrary"),
        name="rwkv_carry",
    )(p, s_loc)

    fwd = pl.BlockSpec((1, 1, TILE_CHUNKS, CHUNK, RW_W), lambda b, grp: (0, b, grp, 0, 0))
    bwd = pl.BlockSpec((1, 1, TILE_CHUNKS, CHUNK, RW_W), lambda b, grp: (1, b, _bwd_group(grp), 0, 0))
    rows = pl.BlockSpec((TILE, RW_W), lambda b, grp: (_tile_of_group(b, grp), 0))
    vec = pl.BlockSpec((1, RW_W), lambda b, grp: (0, 0))
    return pl.pallas_call(
        _rwkv_out_kernel,
        grid=(NB, SEQ_CHUNKS // TILE_CHUNKS),
        in_specs=[fwd, fwd, fwd, bwd, bwd, bwd, rows, rows, vec, vec,
                  pl.BlockSpec((RW_W, RW_W), lambda b, grp: (0, 0))],
        out_specs=rows,
        out_shape=jax.ShapeDtypeStruct((ROWS, RW_W), BF16),
        scratch_shapes=[pltpu.VMEM((TILE, RW_W), F32)],
        compiler_params=_cp("parallel", "parallel"),
        name="rwkv_out",
    )(o_loc, r_eff, s_in, o_loc, r_eff, s_in, bonus, g, gn_g.reshape(1, RW_W), gn_b.reshape(1, RW_W),
      _head_block_ones(RW_W, RW_D).astype(BF16))


def _mlstm_prep_kernel(z_ref, zp_ref, zn_ref, w_ref, o_ref):
    z = z_ref[...]
    zp, zn = _neighbours(z, zp_ref[...], zn_ref[...])
    y = _silu(zp * w_ref[0:1, :] + z * w_ref[1:2, :] + zn * w_ref[2:3, :])
    col = lax.broadcasted_iota(jnp.int32, y.shape, 1)
    o_ref[...] = jnp.where(col >= ML_W, y * (ML_D ** -0.5), y)


def _mlstm_prep(z, conv_w):
    return pl.pallas_call(
        _mlstm_prep_kernel,
        grid=(N_TILES,),
        in_specs=_tile_halo_specs(2 * ML_W) + [pl.BlockSpec((3, 2 * ML_W), lambda t: (0, 0))],
        out_specs=pl.BlockSpec((TILE, 2 * ML_W), lambda t: (t, 0)),
        out_shape=jax.ShapeDtypeStruct((ROWS, 2 * ML_W), F32),
        compiler_params=_cp("parallel"),
        name="mlstm_prep",
    )(z, z, z, conv_w)


def _log_sigmoid(x):
    return jnp.minimum(x, 0.0) - jnp.log(1.0 + jnp.exp(-jnp.abs(x)))


N_CHAINS = 2 * NB


def _mlstm_scan_kernel(*refs):
    nc = N_CHAINS
    q_refs, k_refs, v_refs = refs[0:nc], refs[nc:2 * nc], refs[2 * nc:3 * nc]
    gc_refs, gr_refs = refs[3 * nc:4 * nc], refs[4 * nc:5 * nc]
    bc_ref, br_ref, o_ref, c_ref, n_ref, m_ref = refs[5 * nc:]

    @pl.when(pl.program_id(0) == 0)
    def _():
        c_ref[...] = jnp.zeros_like(c_ref)
        n_ref[...] = jnp.zeros_like(n_ref)
        m_ref[...] = jnp.zeros_like(m_ref)

    c = CHUNK
    ti = lax.broadcasted_iota(jnp.int32, (c, c), 0)
    si = lax.broadcasted_iota(jnp.int32, (c, c), 1)
    masks = (ti >= si, ti <= si)
    items = [(ci, h) for ci in range(nc) for h in range(ML_H)]
    sls = [slice(h * ML_D, (h + 1) * ML_D) for h in range(ML_H)]
    gcol = [gc_refs[ci][0, 0] + bc_ref[ci // NB] for ci in range(nc)]
    grow = [gr_refs[ci][0, 0] + br_ref[ci // NB] for ci in range(nc)]

    q = [q_refs[ci][:, sls[h]] for ci, h in items]
    k = [k_refs[ci][:, sls[h]] for ci, h in items]
    v = [v_refs[ci][:, sls[h]] for ci, h in items]
    qk = [_bdot_nt(a, b) for a, b in zip(q, k)]
    c_mat = [c_ref[ci, h] for ci, h in items]
    n_vec = [n_ref[ci, h] for ci, h in items]
    m_prev = [m_ref[ci, h][0:1, 0:1] for ci, h in items]
    qc = [_bdot_nt(a, b) for a, b in zip(q, c_mat)]
    qn = [jnp.sum(a * b, axis=1, keepdims=True) for a, b in zip(q, n_vec)]

    log_w, m_inter, cum_col, i_col, total = [], [], [], [], []
    for ci, h in items:
        mask = masks[ci // NB]
        mask_t = masks[1 - ci // NB]
        f_col = _log_sigmoid(gcol[ci][:, ML_H + h:ML_H + h + 1])
        f_row = _log_sigmoid(grow[ci][ML_H + h:ML_H + h + 1, :])
        cc = jnp.sum(jnp.where(mask, f_row, 0.0), axis=1, keepdims=True)
        cr = jnp.sum(jnp.where(mask_t, f_col, 0.0), axis=0, keepdims=True)
        log_w.append(jnp.where(mask, cc - cr + grow[ci][h:h + 1, :], ML_NEG))
        cum_col.append(cc)
        i_col.append(gcol[ci][:, h:h + 1])
        total.append(jnp.sum(f_row, axis=1, keepdims=True))
    m_inter = [a + b for a, b in zip(cum_col, m_prev)]
    m_t = [jnp.maximum(jnp.max(a, axis=1, keepdims=True), b) for a, b in zip(log_w, m_inter)]
    s = [a * jnp.exp(b - m) for a, b, m in zip(qk, log_w, m_t)]
    w_inter = [jnp.exp(a - m) for a, m in zip(m_inter, m_t)]
    sv = [_bdot(a, b) for a, b in zip(s, v)]
    for i, (ci, h) in enumerate(items):
        num = sv[i] + w_inter[i] * qc[i]
        den = jnp.sum(s[i], axis=1, keepdims=True) + w_inter[i] * qn[i]
        o_ref[ci // NB, ci % NB, 0, :, sls[h]] = num / jnp.maximum(jnp.abs(den), jnp.exp(-m_t[i]))
    log_src = [t - a + b for t, a, b in zip(total, cum_col, i_col)]
    m_new = [jnp.maximum(t + mp, jnp.max(ls, axis=0, keepdims=True)) for t, mp, ls in zip(total, m_prev, log_src)]
    src = [jnp.exp(ls - mn) for ls, mn in zip(log_src, m_new)]
    decay = [jnp.exp(t + mp - mn) for t, mp, mn in zip(total, m_prev, m_new)]
    vk = [_bdot_tn(a * sr, b) for a, sr, b in zip(v, src, k)]
    for i, (ci, h) in enumerate(items):
        c_ref[ci, h] = decay[i] * c_mat[i] + vk[i]
        n_ref[ci, h] = decay[i] * n_vec[i] + jnp.sum(src[i] * k[i], axis=0, keepdims=True)
        m_ref[ci, h] = jnp.broadcast_to(m_new[i], m_ref.shape[2:])


def _bwd_step(pos):
    return jnp.where(pos < CTX_CHUNKS, CTX_CHUNKS - 1 - pos, SEQ_CHUNKS - 1 + CTX_CHUNKS - pos)


def _mlstm_scan(qk, z, gcol, grow, bcol, brow):
    chains = [(d, b) for d in range(2) for b in range(NB)]

    def tok(col_blk):
        return [pl.BlockSpec((CHUNK, ML_W), lambda n, d=d, b=b: (_seq_row_block(b, d, n), col_blk))
                for d, b in chains]

    gc_specs = [pl.BlockSpec((1, 1, CHUNK, 2 * ML_H), lambda n, d=d, b=b: (d, _seq_row_block(b, d, n), 0, 0))
                for d, b in chains]
    gr_specs = [pl.BlockSpec((1, 1, 2 * ML_H, CHUNK), lambda n, d=d, b=b: (d, _seq_row_block(b, d, n), 0, 0))
                for d, b in chains]
    nc = N_CHAINS
    return pl.pallas_call(
        _mlstm_scan_kernel,
        grid=(SEQ_CHUNKS,),
        in_specs=tok(0) + tok(1) + tok(2) + gc_specs + gr_specs + [
            pl.BlockSpec((2, 1, 2 * ML_H), lambda n: (0, 0, 0)),
            pl.BlockSpec((2, 2 * ML_H, 1), lambda n: (0, 0, 0))],
        out_specs=pl.BlockSpec((2, NB, 1, CHUNK, ML_W), lambda n: (0, 0, n, 0, 0)),
        out_shape=jax.ShapeDtypeStruct((2, NB, SEQ_CHUNKS, CHUNK, ML_W), F32),
        scratch_shapes=[pltpu.VMEM((nc, ML_H, ML_D, ML_D), F32), pltpu.VMEM((nc, ML_H, 1, ML_D), F32),
                        pltpu.VMEM((nc, ML_H, 8, 128), F32)],
        compiler_params=_cp("arbitrary"),
        name="mlstm_scan",
    )(*([qk] * (2 * nc) + [z] * nc + [gcol] * nc + [grow] * nc + [bcol, brow]))


def _mlstm_out_kernel(hf_ref, hb_ref, og_ref, gng_ref, y_ref):
    n = TILE_CHUNKS
    for i in range(n):
        rows = slice(i * CHUNK, (i + 1) * CHUNK)
        hsum = _sigmoid(og_ref[rows, :]) * (hf_ref[0, 0, i] + hb_ref[0, 0, n - 1 - i])
        for h in range(ML_H):
            sl = slice(h * ML_D, (h + 1) * ML_D)
            x = hsum[:, sl]
            cen = x - jnp.mean(x, axis=-1, keepdims=True)
            var = jnp.mean(cen * cen, axis=-1, keepdims=True)
            y_ref[rows, sl] = (cen * lax.rsqrt(var + EPS) * gng_ref[:, sl]).astype(y_ref.dtype)


def _mlstm_out(hs, z, gn_g):
    blk = (1, 1, TILE_CHUNKS, CHUNK, ML_W)
    return pl.pallas_call(
        _mlstm_out_kernel,
        grid=(NB, SEQ_CHUNKS // TILE_CHUNKS),
        in_specs=[pl.BlockSpec(blk, lambda b, grp: (0, b, grp, 0, 0)),
                  pl.BlockSpec(blk, lambda b, grp: (1, b, _bwd_group(grp), 0, 0)),
                  pl.BlockSpec((TILE, ML_W), lambda b, grp: (_tile_of_group(b, grp), 3)),
                  pl.BlockSpec((1, ML_W), lambda b, grp: (0, 0))],
        out_specs=pl.BlockSpec((TILE, ML_W), lambda b, grp: (_tile_of_group(b, grp), 0)),
        out_shape=jax.ShapeDtypeStruct((ROWS, ML_W), BF16),
        compiler_params=_cp("parallel", "parallel"),
        name="mlstm_out",
    )(hs, hs, z, gn_g.reshape(1, ML_W))


def _mlstm(z_main, z_gates, conv_w, i_b, f_b, gn_g):
    qk = _mlstm_prep(z_main, conv_w)
    n_chunks = ROWS // CHUNK
    gates = z_gates[:, :4 * ML_H].reshape(n_chunks, CHUNK, 2, 2, ML_H)
    gcol = jnp.transpose(gates, (3, 0, 1, 2, 4)).reshape(2, n_chunks, CHUNK, 2 * ML_H)
    grow = jnp.swapaxes(gcol, 2, 3)
    bias = jnp.concatenate([i_b, f_b], axis=-1)
    hs = _mlstm_scan(qk, z_main, gcol, grow, bias.reshape(2, 1, 2 * ML_H), bias.reshape(2, 2 * ML_H, 1))
    return _mlstm_out(hs, z_main, gn_g)


def _s5_toeplitz_kernel(kf_ref, kb_ref, o_ref):
    kf = kf_ref[0]
    kb = kb_ref[0]
    lane = lax.broadcasted_iota(jnp.int32, kf.shape, 1)
    width = S5_L * S5_C
    for j in range(S5_L):
        f = kf if j == 0 else jnp.where(lane >= S5_C * j, pltpu.roll(kf, S5_C * j, 1), 0.0)
        back = S5_L - 1 - j
        b = kb if back == 0 else jnp.where(lane < S5_C * (j + 1), pltpu.roll(kb, width - S5_C * back, 1), 0.0)
        o_ref[0, 0, :, j] = f.reshape(S5_G, S5_C, width).astype(o_ref.dtype)
        o_ref[0, 1, :, j] = b.reshape(S5_G, S5_C, width).astype(o_ref.dtype)


def _s5_operators(lam_re, lam_im, log_step, b_re, b_im, c_re, c_im):
    nl = lam_re.shape[0]
    dt = jnp.exp(log_step)[..., None]
    mag = jnp.exp(lam_re * dt)
    a_re = mag * jnp.cos(lam_im * dt)
    a_im = mag * jnp.sin(lam_im * dt)
    den = lam_re * lam_re + lam_im * lam_im
    f_re = ((a_re - 1) * lam_re + a_im * lam_im) / den
    f_im = (a_im * lam_re - (a_re - 1) * lam_im) / den
    bb_re = f_re[..., None] * b_re - f_im[..., None] * b_im
    bb_im = f_re[..., None] * b_im + f_im[..., None] * b_re
    bt_re = jnp.swapaxes(bb_re, -1, -2)
    bt_im = jnp.swapaxes(bb_im, -1, -2)
    pr = [jnp.ones_like(a_re)]
    pi = [jnp.zeros_like(a_im)]
    for _ in range(S5_L):
        pr.append(pr[-1] * a_re - pi[-1] * a_im)
        pi.append(pr[-2] * a_im + pi[-1] * a_re)
    pr = jnp.stack(pr, axis=3)
    pi = jnp.stack(pi, axis=3)

    def times_b(qr, qi):
        qr, qi = qr[..., None, :], qi[..., None, :]
        br, bi = bt_re[:, :, :, None], bt_im[:, :, :, None]
        return qr * br - qi * bi, qr * bi + qi * br

    wr, wi = times_b(pr[:, :, :, :S5_L], pi[:, :, :, :S5_L])
    wr = jnp.swapaxes(wr, 3, 4)
    wi = jnp.swapaxes(wi, 3, 4)
    kern = (jnp.einsum("ldgktp,ldgcp->ldgktc", wr, c_re, precision=HP)
            - jnp.einsum("ldgktp,ldgcp->ldgktc", wi, c_im, precision=HP))
    kf = kern[:, 0].reshape(nl, S5_G * S5_C, S5_L * S5_C)
    kb = jnp.flip(kern[:, 1], axis=3).reshape(nl, S5_G * S5_C, S5_L * S5_C)
    rows_blk = pl.BlockSpec((1, S5_G * S5_C, S5_L * S5_C), lambda l: (l, 0, 0))
    m = pl.pallas_call(
        _s5_toeplitz_kernel,
        grid=(nl,),
        in_specs=[rows_blk, rows_blk],
        out_specs=pl.BlockSpec((1, 2, S5_G, S5_L, S5_C, S5_L * S5_C), lambda l: (l, 0, 0, 0, 0, 0)),
        out_shape=jax.ShapeDtypeStruct((nl, 2, S5_G, S5_L, S5_C, S5_L * S5_C), BF16),
        compiler_params=_cp("parallel"),
        name="s5_toeplitz",
    )(kf, kb).reshape(nl, 2, S5_G, S5_L * S5_C, S5_L * S5_C)

    def stack_dirs(fwd, bwd):
        return jnp.stack([fwd[:, 0], bwd[:, 1]], axis=1)

    er, ei = times_b(stack_dirs(jnp.flip(pr[:, :, :, :S5_L], axis=3), pr[:, :, :, :S5_L]),
                     stack_dirs(jnp.flip(pi[:, :, :, :S5_L], axis=3), pi[:, :, :, :S5_L]))
    e = jnp.concatenate([er, ei], axis=-1).reshape(nl, 2, S5_G, S5_L * S5_C, 2 * S5_P).astype(BF16)
    qr = stack_dirs(pr[:, :, :, 1:], jnp.flip(pr[:, :, :, 1:], axis=3))[..., None, :]
    qi = stack_dirs(pi[:, :, :, 1:], jnp.flip(pi[:, :, :, 1:], axis=3))[..., None, :]
    cr, ci = c_re[:, :, :, None], c_im[:, :, :, None]
    ft = jnp.concatenate([cr * qr - ci * qi, -(cr * qi + ci * qr)], axis=-1)
    ft = ft.reshape(nl, 2, S5_G, S5_L * S5_C, 2 * S5_P).astype(BF16)
    return m, e, ft, pr[:, :, :, S5_L], pi[:, :, :, S5_L]


def _s5_local_kernel(u_ref, m_ref, e_ref, y_ref, x_ref):
    u = u_ref[0]
    for d in range(2):
        y_ref[d, 0] = jnp.dot(u, m_ref[d, 0], preferred_element_type=F32)
        x_ref[d, 0] = jnp.dot(u, e_ref[d, 0], preferred_element_type=F32)


def _s5_carry_kernel(xc_ref, ar_ref, ai_ref, x0_ref, st_ref):
    @pl.when(pl.program_id(1) == 0)
    def _():
        st_ref[...] = jnp.zeros_like(st_ref)

    ar = ar_ref[0]
    ai = ai_ref[0]
    x = st_ref[...]
    for i in range(xc_ref.shape[1]):
        x0_ref[0, i] = x
        x = x * ar + pltpu.roll(x, S5_P, 1) * ai + xc_ref[0, i]
    st_ref[...] = x


def _s5_state_kernel(y_ref, x0_ref, ft_ref, o_ref):
    acc = y_ref[0, 0] + y_ref[1, 0]
    for d in range(2):
        acc = acc + _bdot_nt(x0_ref[d, 0], ft_ref[d, 0])
    o_ref[0] = acc.astype(o_ref.dtype)


def _s5_glu_kernel(y_ref, u_ref, d_ref, w_ref, o_ref):
    x = y_ref[...] + u_ref[...] * d_ref[...]
    ge = 0.5 * x * (1.0 + jnp.tanh(math.sqrt(2.0 / math.pi) * (x + 0.044715 * (x * x * x))))
    p = _bdot(ge, w_ref[...])
    o_ref[...] = (p[:, :S5_W] * _sigmoid(p[:, S5_W:])).astype(o_ref.dtype)


def _s5(z, m, e, ft, a_re, a_im, d_skip, w_glu):
    n_ch = (SEQ + CTX) // S5_L
    n_ctx_ch = CTX // S5_L
    rows = NB * n_ch
    width = S5_L * S5_C

    def to_groups(x):
        x = x.reshape(NB, -1, S5_L, S5_G, S5_C)
        return jnp.transpose(x, (3, 0, 1, 2, 4)).reshape(S5_G, NB, -1, width)

    zb = z.astype(BF16)
    u = jnp.concatenate([to_groups(zb[N_LAT:]), to_groups(zb[:N_LAT])], axis=2).reshape(S5_G, rows, width)
    y_loc, x_in = pl.pallas_call(
        _s5_local_kernel,
        grid=(S5_G,),
        in_specs=[pl.BlockSpec((1, rows, width), lambda g: (g, 0, 0)),
                  pl.BlockSpec((2, 1, width, width), lambda g: (0, g, 0, 0)),
                  pl.BlockSpec((2, 1, width, 2 * S5_P), lambda g: (0, g, 0, 0))],
        out_specs=[pl.BlockSpec((2, 1, rows, width), lambda g: (0, g, 0, 0)),
                   pl.BlockSpec((2, 1, rows, 2 * S5_P), lambda g: (0, g, 0, 0))],
        out_shape=[jax.ShapeDtypeStruct((2, S5_G, rows, width), F32),
                   jax.ShapeDtypeStruct((2, S5_G, rows, 2 * S5_P), F32)],
        compiler_params=_cp("parallel"),
        name="s5_local",
    )(u, m, e)

    def scan_order(x):
        return jnp.concatenate([jnp.flip(x[:, :, :n_ctx_ch], axis=2), jnp.flip(x[:, :, n_ctx_ch:], axis=2)], axis=2)

    x_in = x_in.reshape(2, S5_G, NB, n_ch, 2 * S5_P)
    x_ord = jnp.stack([x_in[0], scan_order(x_in[1])])
    x_ord = jnp.transpose(x_ord, (0, 3, 1, 2, 4)).reshape(2, n_ch, S5_G * NB, 2 * S5_P)
    coef_r = jnp.repeat(jnp.concatenate([a_re, a_re], axis=-1), NB, axis=1)
    coef_i = jnp.repeat(jnp.concatenate([-a_im, a_im], axis=-1), NB, axis=1)
    step = 16
    x0 = pl.pallas_call(
        _s5_carry_kernel,
        grid=(2, n_ch // step),
        in_specs=[pl.BlockSpec((1, step, S5_G * NB, 2 * S5_P), lambda d, i: (d, i, 0, 0)),
                  pl.BlockSpec((1, S5_G * NB, 2 * S5_P), lambda d, i: (d, 0, 0)),
                  pl.BlockSpec((1, S5_G * NB, 2 * S5_P), lambda d, i: (d, 0, 0))],
        out_specs=pl.BlockSpec((1, step, S5_G * NB, 2 * S5_P), lambda d, i: (d, i, 0, 0)),
        out_shape=jax.ShapeDtypeStruct((2, n_ch, S5_G * NB, 2 * S5_P), F32),
        scratch_shapes=[pltpu.VMEM((S5_G * NB, 2 * S5_P), F32)],
        compiler_params=_cp("arbitrary", "arbitrary"),
        name="s5_carry",
    )(x_ord, coef_r, coef_i)
    x0 = jnp.transpose(x0.reshape(2, n_ch, S5_G, NB, 2 * S5_P), (0, 2, 3, 1, 4))
    x0 = jnp.stack([x0[0], scan_order(x0[1])]).reshape(2, S5_G, rows, 2 * S5_P)

    y = pl.pallas_call(
        _s5_state_kernel,
        grid=(S5_G,),
        in_specs=[pl.BlockSpec((2, 1, rows, width), lambda g: (0, g, 0, 0)),
                  pl.BlockSpec((2, 1, rows, 2 * S5_P), lambda g: (0, g, 0, 0)),
                  pl.BlockSpec((2, 1, width, 2 * S5_P), lambda g: (0, g, 0, 0))],
        out_specs=pl.BlockSpec((1, rows, width), lambda g: (g, 0, 0)),
        out_shape=jax.ShapeDtypeStruct((S5_G, rows, width), BF16),
        compiler_params=_cp("parallel"),
        name="s5_state",
    )(y_loc, x0, ft)

    y = jnp.transpose(y.reshape(S5_G, NB, n_ch, S5_L, S5_C), (1, 2, 3, 0, 4)).reshape(NB, n_ch * S5_L, S5_W)
    y = jnp.concatenate([y[:, CTX:].reshape(N_LAT, S5_W), y[:, :CTX].reshape(N_CTX, S5_W)], axis=0)
    tok = pl.BlockSpec((TILE, S5_W), lambda t: (t, 0))
    return pl.pallas_call(
        _s5_glu_kernel,
        grid=(N_TILES,),
        in_specs=[tok, tok, pl.BlockSpec((1, S5_W), lambda t: (0, 0)),
                  pl.BlockSpec((S5_W, 2 * S5_W), lambda t: (0, 0))],
        out_specs=tok,
        out_shape=jax.ShapeDtypeStruct((ROWS, S5_W), BF16),
        compiler_params=_cp("parallel"),
        name="s5_glu",
    )(y, z, d_skip.reshape(1, S5_W), w_glu)


def kernel(x, c, ctx, c_ctx, w_mod, b_mod, norm1_g, norm2_g, w_in, b_gate, q_norm_g, k_norm_g, rwkv_mu, rwkv_w0, rwkv_w_up, rwkv_a0, rwkv_a_up, rwkv_g_up, rwkv_k_k, rwkv_k_a, rwkv_r_k, rwkv_gn_g, rwkv_gn_b, mlstm_conv_w, mlstm_i_b, mlstm_f_b, mlstm_gn_g, s5_lam_re, s5_lam_im, s5_log_step, s5_b_re, s5_b_im, s5_c_re, s5_c_im, s5_d, s5_w_glu, w_br_attn, w_br_rwkv, w_br_mlstm, w_br_s5, w_out, w_ffn_in, w_ffn_out, final_norm_g):
    cos, sin = _rope_tables()
    xs = jnp.concatenate([x.reshape(N_LAT, D), ctx.reshape(N_CTX, D)], axis=0)
    c_all = jnp.concatenate([c, c_ctx[None], jnp.zeros((3, D), F32)], axis=0)
    s5_m, s5_e, s5_ft, s5_ar, s5_ai = _s5_operators(s5_lam_re, s5_lam_im, s5_log_step, s5_b_re, s5_b_im,
                                                    s5_c_re, s5_c_im)
    tm = 1024
    for l in range(DEPTH):
        last = l == DEPTH - 1
        n_rows = N_LAT if last else ROWS
        mod = _modulation(c_all, w_mod[l], b_mod[l]).reshape(8, 1, 6 * D)
        h = _norm_mod(xs, norm1_g[l], mod, 0, 1, N_TILES)
        w_attn, w_rwkv, w_ml, w_mlg, w_s5, w_gate = _w_in_split(w_in[l])
        z_attn = _mm(h, w_attn, tm, 512)
        z_rwkv = _mm(h, w_rwkv, tm, 896)
        z_ml = _mm(h, w_ml, tm, 1024)
        z_mlg = _mm(h, w_mlg, tm, 128)
        z_s5 = _mm(h, w_s5, tm, 512)
        z_gate = _mm(h, w_gate, tm, 1024)

        ya = _attention(z_attn, q_norm_g[l], k_norm_g[l], cos, sin)
        rp = dict(mu=rwkv_mu[l], w0=rwkv_w0[l], w_up=rwkv_w_up[l], a0=rwkv_a0[l], a_up=rwkv_a_up[l],
                  g_up=rwkv_g_up[l], k_k=rwkv_k_k[l], k_a=rwkv_k_a[l], r_k=rwkv_r_k[l].reshape(RW_W))
        r, v, kk, g, bonus, lw, bmat, km = _rwkv_prep(z_rwkv, rp)
        yr = _rwkv_scan(r, v, kk, lw, bmat, km, bonus, g, rwkv_gn_g[l], rwkv_gn_b[l])
        ym = _mlstm(z_ml, z_mlg, mlstm_conv_w[l], mlstm_i_b[l], mlstm_f_b[l], mlstm_gn_g[l])
        ys = _s5(z_s5, s5_m[l], s5_e[l], s5_ft[l], s5_ar[l], s5_ai[l], s5_d[l], s5_w_glu[l])

        y = _merge(ya, yr, ym, ys, z_gate, b_gate[l], w_br_attn[l], w_br_rwkv[l], w_br_mlstm[l], w_br_s5[l],
                   tm, 512, n_rows)
        xs = _mm_res(y, w_out[l], xs, mod, 2, tm, 1024, n_rows)
        h2 = _norm_mod(xs, norm2_g[l], mod, 3, 4, n_rows // TILE)
        u = _ffn_in(h2, w_ffn_in[l], tm, 512, n_rows)
        xs = _mm_res(u, w_ffn_out[l], xs, mod, 5, 512, 512, n_rows)
    return _final_norm(xs, final_norm_g).reshape(NB, SEQ, D)
```

```python
import functools
import math

import numpy as np
import jax
import jax.numpy as jnp
from jax import lax
from jax.experimental import pallas as pl
from jax.experimental.pallas import tpu as pltpu

F32 = jnp.float32
BF16 = jnp.bfloat16
HP = lax.Precision.HIGHEST

D = 2048
NB = 4
SEQ = 2048
CTX = 256
DEPTH = 2
N_LAT = NB * SEQ
N_CTX = NB * CTX
ROWS = N_LAT + N_CTX
EPS = 1e-6
GRID_W = 64

HEAD_DIM = 128
ATTN_HEADS = 8
ATTN_KV = 2
ROPE_THETA = 10000.0
ATTN_Q = ATTN_HEADS * HEAD_DIM
ATTN_IN = (ATTN_HEADS + 2 * ATTN_KV) * HEAD_DIM

RW_H = 8
RW_D = 64
RW_W = 512
RW_IN = 3 * RW_W + 64 + 64 + 128
RW_DECAY = math.exp(-0.5)
RW_GN_EPS = 64e-5

ML_H = 4
ML_D = 128
ML_W = 512
ML_NEG = -1e30
ML_IN = 4 * ML_W + 4 * ML_H

S5_W = 512
S5_C = 16
S5_G = 32
S5_P = 64
S5_L = 16

FFN_H = 5632
GATE_IN = 4 * D

CHUNK = 64
TILE = 256
N_TILES = ROWS // TILE
SEQ_CHUNKS = (SEQ + CTX) // CHUNK
CTX_CHUNKS = CTX // CHUNK
LAT_CHUNKS = SEQ // CHUNK

VMEM_LIMIT_BYTES = 56 * 1024 * 1024


def _cp(*sem):
    return pltpu.CompilerParams(dimension_semantics=sem, vmem_limit_bytes=VMEM_LIMIT_BYTES)


def _bdot(a, b):
    return jnp.dot(a.astype(BF16), b.astype(BF16), preferred_element_type=F32)


def _bdot_nt(a, b):
    return lax.dot_general(a.astype(BF16), b.astype(BF16), (((1,), (1,)), ((), ())),
                           preferred_element_type=F32)


def _bdot_tn(a, b):
    return lax.dot_general(a.astype(BF16), b.astype(BF16), (((0,), (0,)), ((), ())),
                           preferred_element_type=F32)


def _hdot(a, b):
    return jnp.dot(a, b, precision=HP, preferred_element_type=F32)


def _sigmoid(x):
    return 1.0 / (1.0 + jnp.exp(-x))


def _silu(x):
    return x * _sigmoid(x)


def _mod_kernel(c_ref, w_ref, b_ref, o_ref):
    o_ref[...] = _hdot(_silu(c_ref[...]), w_ref[...]) + b_ref[...]


def _modulation(c_all, w, b, l):
    tn = 1024
    return pl.pallas_call(
        _mod_kernel,
        grid=(6 * D // tn,),
        in_specs=[pl.BlockSpec((8, D), lambda j: (0, 0)),
                  pl.BlockSpec((None, D, tn), lambda j: (l, 0, j)),
                  pl.BlockSpec((None, 1, tn), lambda j: (l, 0, j))],
        out_specs=pl.BlockSpec((8, tn), lambda j: (0, j)),
        out_shape=jax.ShapeDtypeStruct((8, 6 * D), F32),
        compiler_params=_cp("arbitrary"),
        name="modulation",
    )(c_all, w, b.reshape(DEPTH, 1, 6 * D))


def _mod_row(i, tm):
    return jnp.where(i * tm < N_LAT, (i * tm) // SEQ, NB)


def _norm_mod_kernel(x_ref, g_ref, sh_ref, sc_ref, o_ref):
    x = x_ref[...]
    y = x * lax.rsqrt(jnp.mean(x * x, axis=-1, keepdims=True) + EPS) * g_ref[...]
    o_ref[...] = (y * (1.0 + sc_ref[0]) + sh_ref[0]).astype(o_ref.dtype)


def _norm_mod(x, g, modr, shift_blk, scale_blk, n_tiles):
    return pl.pallas_call(
        _norm_mod_kernel,
        grid=(n_tiles,),
        in_specs=[pl.BlockSpec((TILE, D), lambda i: (i, 0)),
                  pl.BlockSpec((1, D), lambda i: (0, 0)),
                  pl.BlockSpec((1, 1, D), lambda i: (_mod_row(i, TILE), 0, shift_blk)),
                  pl.BlockSpec((1, 1, D), lambda i: (_mod_row(i, TILE), 0, scale_blk))],
        out_specs=pl.BlockSpec((TILE, D), lambda i: (i, 0)),
        out_shape=jax.ShapeDtypeStruct((n_tiles * TILE, D), BF16),
        compiler_params=_cp("parallel"),
        name="norm_mod",
    )(x, g.reshape(1, D), modr, modr)


W_IN_OFFSETS = tuple(int(v) for v in np.cumsum([0, ATTN_IN, RW_IN, 4 * ML_W, 4 * ML_H, S5_W, GATE_IN]))
ML_GATE_PAD = 128


def _w_in_split_kernel(w_ref, attn_ref, rwkv_ref, ml_ref, mlg_ref, s5_ref, gate_ref):
    o = W_IN_OFFSETS
    attn_ref[...] = w_ref[:, o[0]:o[1]].astype(BF16)
    rwkv_ref[...] = w_ref[:, o[1]:o[2]].astype(BF16)
    ml_ref[...] = w_ref[:, o[2]:o[3]].astype(BF16)
    tail = w_ref[:, o[3]:o[3] + ML_GATE_PAD]
    lane = lax.broadcasted_iota(jnp.int32, tail.shape, 1)
    mlg_ref[...] = jnp.where(lane < o[4] - o[3], tail, 0.0).astype(BF16)
    s5_ref[...] = w_ref[:, o[4]:o[5]].astype(BF16)
    gate_ref[...] = w_ref[:, o[5]:o[6]].astype(BF16)


def _w_in_split(w, l):
    tr = 128
    widths = (ATTN_IN, RW_IN, 4 * ML_W, ML_GATE_PAD, S5_W, GATE_IN)
    return pl.pallas_call(
        _w_in_split_kernel,
        grid=(D // tr,),
        in_specs=[pl.BlockSpec((None, tr, w.shape[2]), lambda i: (l, i, 0))],
        out_specs=[pl.BlockSpec((tr, n), lambda i: (i, 0)) for n in widths],
        out_shape=[jax.ShapeDtypeStruct((D, n), BF16) for n in widths],
        compiler_params=_cp("parallel"),
        name="w_in_split",
    )(w)


def _mm_kernel(a_ref, w_ref, o_ref):
    o_ref[...] = jnp.dot(a_ref[...], w_ref[...], preferred_element_type=F32).astype(o_ref.dtype)


def _mm(a, w, tm, tn, out_dtype=F32):
    m, k = a.shape
    n = w.shape[1]
    return pl.pallas_call(
        _mm_kernel,
        grid=(n // tn, m // tm),
        in_specs=[pl.BlockSpec((tm, k), lambda j, i: (i, 0)),
                  pl.BlockSpec((k, tn), lambda j, i: (0, j))],
        out_specs=pl.BlockSpec((tm, tn), lambda j, i: (i, j)),
        out_shape=jax.ShapeDtypeStruct((m, n), out_dtype),
        compiler_params=_cp("parallel", "parallel"),
        name="matmul",
    )(a, w)


def _mm_res_kernel(a_ref, w_ref, x_ref, g_ref, o_ref, wb_ref):
    @pl.when(pl.program_id(1) == 0)
    def _():
        wb_ref[...] = w_ref[...].astype(BF16)

    y = jnp.dot(a_ref[...], wb_ref[...], preferred_element_type=F32)
    o_ref[...] = x_ref[...] + g_ref[0] * y


def _mm_res(a, w, l, x, mod, gate_blk, tm, tn, n_rows):
    k = a.shape[1]
    n = w.shape[2]
    gpb = D // tn
    return pl.pallas_call(
        _mm_res_kernel,
        grid=(n // tn, n_rows // tm),
        in_specs=[pl.BlockSpec((tm, k), lambda j, i: (i, 0)),
                  pl.BlockSpec((None, k, tn), lambda j, i: (l, 0, j)),
                  pl.BlockSpec((tm, tn), lambda j, i: (i, j)),
                  pl.BlockSpec((1, 1, tn), lambda j, i: (_mod_row(i, tm), 0, gate_blk * gpb + j))],
        out_specs=pl.BlockSpec((tm, tn), lambda j, i: (i, j)),
        out_shape=jax.ShapeDtypeStruct((n_rows, n), F32),
        scratch_shapes=[pltpu.VMEM((k, tn), BF16)],
        compiler_params=_cp("arbitrary", "arbitrary"),
        name="matmul_residual",
    )(a, w, x, mod)


def _ffn_in_kernel(a_ref, wa_ref, wb_ref, o_ref, wab_ref, wbb_ref):
    @pl.when(pl.program_id(1) == 0)
    def _():
        wab_ref[...] = wa_ref[...].astype(BF16)
        wbb_ref[...] = wb_ref[...].astype(BF16)

    a = a_ref[...]
    u = jnp.dot(a, wab_ref[...], preferred_element_type=F32)
    v = jnp.dot(a, wbb_ref[...], preferred_element_type=F32)
    o_ref[...] = (_silu(u) * v).astype(o_ref.dtype)


def _ffn_in(h, w, l, tm, tn, n_rows):
    nb = FFN_H // tn
    return pl.pallas_call(
        _ffn_in_kernel,
        grid=(nb, n_rows // tm),
        in_specs=[pl.BlockSpec((tm, D), lambda j, i: (i, 0)),
                  pl.BlockSpec((None, D, tn), lambda j, i: (l, 0, j)),
                  pl.BlockSpec((None, D, tn), lambda j, i: (l, 0, nb + j))],
        out_specs=pl.BlockSpec((tm, tn), lambda j, i: (i, j)),
        out_shape=jax.ShapeDtypeStruct((n_rows, FFN_H), BF16),
        scratch_shapes=[pltpu.VMEM((D, tn), BF16), pltpu.VMEM((D, tn), BF16)],
        compiler_params=_cp("arbitrary", "arbitrary"),
        name="ffn_in",
    )(h, w, w)


def _merge_kernel(ya_ref, yr_ref, ym_ref, ys_ref, ga_ref, gr_ref, gm_ref, gs_ref,
                  ba_ref, br_ref, bm_ref, bs_ref, wa_ref, wr_ref, wm_ref, ws_ref, o_ref,
                  wab_ref, wrb_ref, wmb_ref, wsb_ref):
    @pl.when(pl.program_id(1) == 0)
    def _():
        wab_ref[...] = wa_ref[...].astype(BF16)
        wrb_ref[...] = wr_ref[...].astype(BF16)
        wmb_ref[...] = wm_ref[...].astype(BF16)
        wsb_ref[...] = ws_ref[...].astype(BF16)

    acc = None
    for y_ref, g_ref, b_ref, w_ref in ((ya_ref, ga_ref, ba_ref, wab_ref), (yr_ref, gr_ref, br_ref, wrb_ref),
                                       (ym_ref, gm_ref, bm_ref, wmb_ref), (ys_ref, gs_ref, bs_ref, wsb_ref)):
        gate = _sigmoid(g_ref[...] + b_ref[...])
        term = gate * jnp.dot(y_ref[...], w_ref[...], preferred_element_type=F32)
        acc = term if acc is None else acc + term
    o_ref[...] = acc.astype(o_ref.dtype)


def _merge(ya, yr, ym, ys, z_gate, b_gate, wa, wr, wm, ws, l, tm, tn, n_rows):
    nb = D // tn
    b_gate = b_gate.reshape(DEPTH, 1, GATE_IN)

    def act(width):
        return pl.BlockSpec((tm, width), lambda j, i: (i, 0))

    def gate(br):
        return pl.BlockSpec((tm, tn), lambda j, i: (i, br * nb + j))

    def bias(br):
        return pl.BlockSpec((None, 1, tn), lambda j, i: (l, 0, br * nb + j))

    def wgt(width):
        return pl.BlockSpec((None, width, tn), lambda j, i: (l, 0, j))

    return pl.pallas_call(
        _merge_kernel,
        grid=(nb, n_rows // tm),
        in_specs=[act(ATTN_Q), act(RW_W), act(ML_W), act(S5_W),
                  gate(0), gate(1), gate(2), gate(3),
                  bias(0), bias(1), bias(2), bias(3),
                  wgt(ATTN_Q), wgt(RW_W), wgt(ML_W), wgt(S5_W)],
        out_specs=pl.BlockSpec((tm, tn), lambda j, i: (i, j)),
        out_shape=jax.ShapeDtypeStruct((n_rows, D), BF16),
        scratch_shapes=[pltpu.VMEM((ATTN_Q, tn), BF16), pltpu.VMEM((RW_W, tn), BF16),
                        pltpu.VMEM((ML_W, tn), BF16), pltpu.VMEM((S5_W, tn), BF16)],
        compiler_params=_cp("arbitrary", "arbitrary"),
        name="gated_merge",
    )(ya, yr, ym, ys, z_gate, z_gate, z_gate, z_gate, b_gate, b_gate, b_gate, b_gate, wa, wr, wm, ws)


def _final_norm_kernel(x_ref, g_ref, o_ref):
    x = x_ref[...]
    o_ref[...] = x * lax.rsqrt(jnp.mean(x * x, axis=-1, keepdims=True) + EPS) * g_ref[...]


def _final_norm(x, g):
    n_tiles = N_LAT // TILE
    return pl.pallas_call(
        _final_norm_kernel,
        grid=(n_tiles,),
        in_specs=[pl.BlockSpec((TILE, D), lambda i: (i, 0)),
                  pl.BlockSpec((1, D), lambda i: (0, 0))],
        out_specs=pl.BlockSpec((TILE, D), lambda i: (i, 0)),
        out_shape=jax.ShapeDtypeStruct((N_LAT, D), F32),
        compiler_params=_cp("parallel"),
        name="final_norm",
    )(x, g.reshape(1, D))


def _tile_halo_specs(width, col_blk=0):
    last = ROWS // 8 - 1
    per = TILE // 8
    return [pl.BlockSpec((TILE, width), lambda t: (t, col_blk)),
            pl.BlockSpec((8, width), lambda t: (jnp.maximum(t * per - 1, 0), col_blk)),
            pl.BlockSpec((8, width), lambda t: (jnp.minimum((t + 1) * per, last), col_blk))]


def _neighbours(z, prev_blk, next_blk):
    t = pl.program_id(0)
    pos = t % (SEQ // TILE)
    is_lat = t < N_LAT // TILE
    has_prev = jnp.logical_and(is_lat, pos > 0).astype(F32)
    has_next = jnp.logical_and(is_lat, pos < SEQ // TILE - 1).astype(F32)
    row = lax.broadcasted_iota(jnp.int32, z.shape, 0)
    zp = jnp.where(row == 0, prev_blk[7:8, :] * has_prev, pltpu.roll(z, 1, 0))
    zn = jnp.where(row == TILE - 1, next_blk[0:1, :] * has_next, pltpu.roll(z, TILE - 1, 0))
    return zp, zn


def _seq_row_block(b, d, n):
    ctx_c = jnp.where(d == 0, n, CTX_CHUNKS - 1 - n)
    lat_c = jnp.where(d == 0, n - CTX_CHUNKS, SEQ_CHUNKS - 1 - n)
    return jnp.where(n < CTX_CHUNKS, N_LAT // CHUNK + CTX_CHUNKS * b + ctx_c, LAT_CHUNKS * b + lat_c)


def _rope(x, cos, sin):
    lane = lax.broadcasted_iota(jnp.int32, x.shape, 1)
    first = (lane % 64) < 32
    partner = jnp.where(first, pltpu.roll(x, 96, 1), pltpu.roll(x, 32, 1))
    return x * cos + partner * sin


def _rms(x, g):
    return x * lax.rsqrt(jnp.mean(x * x, axis=-1, keepdims=True) + EPS) * g


def _attn_kernel(q_ref, kl_ref, kc_ref, vl_ref, vc_ref, cos_ref, sin_ref, cos_t_ref, sin_t_ref,
                 qg_ref, kg_ref, o_ref, klb_ref, kcb_ref, vlb_ref, vcb_ref):
    qi = pl.program_id(2)
    n_lat_tiles = SEQ // TILE

    @pl.when(qi == 0)
    def _():
        kg = kg_ref[...]
        klb_ref[...] = _rope(_rms(kl_ref[...], kg), cos_ref[...], sin_ref[...]).astype(BF16)
        kcb_ref[...] = _rms(kc_ref[...], kg).astype(BF16)
        vlb_ref[...] = vl_ref[...].astype(BF16)
        vcb_ref[...] = vc_ref[...].astype(BF16)

    scale = HEAD_DIM ** -0.5 * math.log2(math.e)
    nt = (((1,), (1,)), ((), ()))

    def heads(latent):
        sls = [slice(h * HEAD_DIM, (h + 1) * HEAD_DIM) for h in range(ATTN_HEADS // ATTN_KV)]
        q = [_rms(q_ref[:, sl], qg_ref[...]) for sl in sls]
        if latent:
            q = [_rope(x, cos_t_ref[...], sin_t_ref[...]) for x in q]
        q = [(x * scale).astype(BF16) for x in q]
        s_c = [lax.dot_general(x, kcb_ref[...], nt, preferred_element_type=F32) for x in q]
        m = [jnp.max(x, axis=-1, keepdims=True) for x in s_c]
        if latent:
            s_l = [lax.dot_general(x, klb_ref[...], nt, preferred_element_type=F32) for x in q]
            m = [jnp.maximum(a, jnp.max(x, axis=-1, keepdims=True)) for a, x in zip(m, s_l)]
        p_c = [jnp.exp2(x - a) for x, a in zip(s_c, m)]
        den = [jnp.sum(x, axis=-1, keepdims=True) for x in p_c]
        acc = [jnp.dot(x.astype(BF16), vcb_ref[...], preferred_element_type=F32) for x in p_c]
        if latent:
            p_l = [jnp.exp2(x - a) for x, a in zip(s_l, m)]
            den = [a + jnp.sum(x, axis=-1, keepdims=True) for a, x in zip(den, p_l)]
            acc = [a + jnp.dot(x.astype(BF16), vlb_ref[...], preferred_element_type=F32) for a, x in zip(acc, p_l)]
        for sl, a, dn in zip(sls, acc, den):
            o_ref[:, sl] = (a / dn).astype(o_ref.dtype)

    @pl.when(qi < n_lat_tiles)
    def _():
        heads(True)

    @pl.when(qi == n_lat_tiles)
    def _():
        heads(False)


def _rope_tables():
    rows = SEQ // GRID_W
    row = jnp.repeat(jnp.arange(rows, dtype=F32), GRID_W)
    col = jnp.tile(jnp.arange(GRID_W, dtype=F32), rows)
    axis_dim = HEAD_DIM // 2
    inv_freq = ROPE_THETA ** (-jnp.arange(0, axis_dim, 2, dtype=F32) / axis_dim)
    ang_r = row[:, None] * inv_freq[None]
    ang_c = col[:, None] * inv_freq[None]
    cos = jnp.concatenate([jnp.cos(ang_r), jnp.cos(ang_r), jnp.cos(ang_c), jnp.cos(ang_c)], axis=-1)
    sin = jnp.concatenate([-jnp.sin(ang_r), jnp.sin(ang_r), -jnp.sin(ang_c), jnp.sin(ang_c)], axis=-1)
    return cos, sin


def _attention(z, q_g, k_g, cos, sin):
    n_lat_tiles = SEQ // TILE
    qw = ATTN_Q // ATTN_KV
    kcol = ATTN_Q // HEAD_DIM
    vcol = kcol + ATTN_KV

    def q_row(b, g, qi):
        return jnp.where(qi < n_lat_tiles, n_lat_tiles * b + qi, N_LAT // TILE + b)

    return pl.pallas_call(
        _attn_kernel,
        grid=(NB, ATTN_KV, n_lat_tiles + 1),
        in_specs=[pl.BlockSpec((TILE, qw), lambda b, g, qi: (q_row(b, g, qi), g)),
                  pl.BlockSpec((SEQ, HEAD_DIM), lambda b, g, qi: (b, kcol + g)),
                  pl.BlockSpec((CTX, HEAD_DIM), lambda b, g, qi: (N_LAT // CTX + b, kcol + g)),
                  pl.BlockSpec((SEQ, HEAD_DIM), lambda b, g, qi: (b, vcol + g)),
                  pl.BlockSpec((CTX, HEAD_DIM), lambda b, g, qi: (N_LAT // CTX + b, vcol + g)),
                  pl.BlockSpec((SEQ, HEAD_DIM), lambda b, g, qi: (0, 0)),
                  pl.BlockSpec((SEQ, HEAD_DIM), lambda b, g, qi: (0, 0)),
                  pl.BlockSpec((TILE, HEAD_DIM), lambda b, g, qi: (jnp.minimum(qi, n_lat_tiles - 1), 0)),
                  pl.BlockSpec((TILE, HEAD_DIM), lambda b, g, qi: (jnp.minimum(qi, n_lat_tiles - 1), 0)),
                  pl.BlockSpec((1, HEAD_DIM), lambda b, g, qi: (0, 0)),
                  pl.BlockSpec((1, HEAD_DIM), lambda b, g, qi: (0, 0))],
        out_specs=pl.BlockSpec((TILE, qw), lambda b, g, qi: (q_row(b, g, qi), g)),
        out_shape=jax.ShapeDtypeStruct((ROWS, ATTN_Q), BF16),
        scratch_shapes=[pltpu.VMEM((SEQ, HEAD_DIM), BF16), pltpu.VMEM((CTX, HEAD_DIM), BF16),
                        pltpu.VMEM((SEQ, HEAD_DIM), BF16), pltpu.VMEM((CTX, HEAD_DIM), BF16)],
        compiler_params=_cp("arbitrary", "arbitrary", "arbitrary"),
        name="attention",
    )(z, z, z, z, z, cos, sin, cos, sin, q_g.reshape(1, HEAD_DIM), k_g.reshape(1, HEAD_DIM))


def _rwkv_prep_kernel(z_ref, zp_ref, zn_ref, mu_ref, w0_ref, wup_ref, a0_ref, aup_ref, gup_ref,
                      kk_ref, ka_ref, rk_ref, bd_ref,
                      r_out, v_out, kkn_out, g_out, bonus_out, lw_out, b_out, km_out):
    z = z_ref[...]
    zp, zn = _neighbours(z, zp_ref[...], zn_ref[...])
    zs = z + mu_ref[...] * (0.5 * (zp + zn) - z)
    r = zs[:, 0:RW_W]
    k = zs[:, RW_W:2 * RW_W]
    v = zs[:, 2 * RW_W:3 * RW_W]
    w_lo = zs[:, 3 * RW_W:3 * RW_W + 64]
    a_lo = zs[:, 3 * RW_W + 64:3 * RW_W + 128]
    g_lo = zs[:, 3 * RW_W + 128:3 * RW_W + 256]
    bd = bd_ref[...]
    kk = k * kk_ref[...]
    kk = kk * lax.rsqrt(_dot_rhs_exact(kk * kk, bd) + 1e-12)
    r_out[...] = r
    v_out[...] = v
    kkn_out[...] = kk
    g_out[...] = _bdot(_sigmoid(g_lo), gup_ref[...])
    tw = jnp.tanh(w_lo)
    km_sum = None
    for d in range(2):
        lw = -RW_DECAY * _sigmoid(w0_ref[d] + _bdot(tw, wup_ref[d]))
        a = _sigmoid(a0_ref[d] + _bdot(a_lo, aup_ref[d]))
        km = k * (1.0 + (a - 1.0) * ka_ref[...])
        lw_out[d] = lw
        b_out[d] = a * kk
        km_out[d] = km
        km_sum = km if km_sum is None else km_sum + km
    bonus_out[...] = _dot_rhs_exact(r * km_sum * rk_ref[...], bd) * v


def _rwkv_prep(z, p):
    row = lambda a: a.reshape(1, -1)
    full = lambda shape: pl.BlockSpec(shape, lambda t: (0,) * len(shape))
    out_tok = pl.BlockSpec((TILE, RW_W), lambda t: (t, 0))
    out_dir = pl.BlockSpec((2, TILE, RW_W), lambda t: (0, t, 0))
    tok = jax.ShapeDtypeStruct((ROWS, RW_W), F32)
    drn = jax.ShapeDtypeStruct((2, ROWS, RW_W), F32)
    return pl.pallas_call(
        _rwkv_prep_kernel,
        grid=(N_TILES,),
        in_specs=_tile_halo_specs(RW_IN) + [
            full((1, RW_IN)), full((2, 1, RW_W)), full((2, 64, RW_W)), full((2, 1, RW_W)),
            full((2, 64, RW_W)), full((128, RW_W)), full((1, RW_W)), full((1, RW_W)), full((1, RW_W)),
            full((RW_W, RW_W))],
        out_specs=[out_tok] * 5 + [out_dir] * 3,
        out_shape=[tok] * 5 + [drn] * 3,
        compiler_params=_cp("parallel"),
        name="rwkv_prep",
    )(z, z, z, row(p["mu"]), p["w0"].reshape(2, 1, RW_W), p["w_up"], p["a0"].reshape(2, 1, RW_W),
      p["a_up"], p["g_up"], row(p["k_k"]), row(p["k_a"]), row(p["r_k"]),
      _head_block_ones(RW_W, RW_D).astype(BF16))


def _head_block_ones(width, head):
    idx = np.arange(width) // head
    return jnp.asarray((idx[:, None] == idx[None, :]).astype(np.float32))


def _split_bf16(x, pieces):
    out = []
    for _ in range(pieces):
        p = x.astype(BF16)
        out.append(p)
        x = x - p.astype(F32)
    return out


def _dot_rhs_exact(x, m, pieces=2):
    return sum(jnp.dot(p, m, preferred_element_type=F32) for p in _split_bf16(x, pieces))


def _dot_lhs_exact(m, x, pieces=3):
    return sum(jnp.dot(m, p, preferred_element_type=F32) for p in _split_bf16(x, pieces))


RW_SUB = 2


def _rwkv_chunk_kernel(r_ref, v_ref, kk_ref, lw_ref, b_ref, km_ref, p_ref, sl_ref, re_ref, ol_ref):
    d = pl.program_id(0)
    c = CHUNK
    hd = RW_D
    ti = lax.broadcasted_iota(jnp.int32, (c, c), 0)
    si = lax.broadcasted_iota(jnp.int32, (c, c), 1)
    delta = (ti - si) * (1 - 2 * d)
    incl = delta >= 0
    strict = delta > 0
    eye = jnp.where(ti == si, 1.0, 0.0)
    tri = jnp.where(incl, 1.0, 0.0).astype(BF16)
    gr = lax.broadcasted_iota(jnp.int32, (2 * c, 2 * c), 0)
    gc = lax.broadcasted_iota(jnp.int32, (2 * c, 2 * c), 1)
    gdelta = (gr % c - gc % c) * (1 - 2 * d)
    gmask = gdelta >= jnp.where(gr < c, 1, 0)

    items = []
    for sub in range(RW_SUB):
        rows = slice(sub * c, (sub + 1) * c)
        lw = lw_ref[0, rows, :]
        cs = _dot_lhs_exact(tri, lw)
        tot = jnp.sum(lw, axis=0, keepdims=True)
        r = r_ref[rows, :]
        v = v_ref[rows, :]
        kk = kk_ref[rows, :]
        bb = b_ref[0, rows, :]
        km = km_ref[0, rows, :]
        e_neg = jnp.exp(-cs)
        e_rem = jnp.exp(tot - cs)
        kkt = kk * jnp.exp(cs - lw)
        rt = r * jnp.exp(cs)
        bt = bb * e_neg
        kt = km * e_neg
        bh = bb * e_rem
        kh = km * e_rem
        e_tot = jnp.exp(tot)
        for h in range(RW_H):
            s = slice(h * hd, (h + 1) * hd)
            items.append(dict(sub=sub, s=s, kkt=kkt[:, s], rt=rt[:, s], bt=bt[:, s], kt=kt[:, s], v=v[:, s],
                              bh=bh[:, s], kh=kh[:, s], e_tot=e_tot[:, s]))

    g = [jnp.where(gmask, _bdot_nt(jnp.concatenate([it["kkt"], it["rt"]], axis=0),
                                   jnp.concatenate([it["bt"], it["kt"]], axis=0)), 0.0) for it in items]
    l_b = [x[:c, :c] for x in g]
    a_b = [x[c:, :c] for x in g]
    lkv_akv = [_bdot(x[:, c:], it["v"]) for x, it in zip(g, items)]
    pw = [_bdot(x, x) for x in l_b]
    inv = [eye - x for x in l_b]
    for _ in range(int(math.log2(c)) - 2):
        res = [_bdot(p, jnp.concatenate([p, i], axis=1)) for p, i in zip(pw, inv)]
        pw = [x[:, :c] for x in res]
        inv = [i + x[:, c:] for i, x in zip(inv, res)]
    inv = [i + _bdot(p, i) for p, i in zip(pw, inv)]
    wy = [_bdot(i, jnp.concatenate([it["kkt"], x[:c]], axis=1)) for i, it, x in zip(inv, items, lkv_akv)]
    ab_wy = [_bdot(a, x) for a, x in zip(a_b, wy)]
    cross = [_bdot_tn(jnp.concatenate([it["v"], x], axis=1), jnp.concatenate([it["kh"], it["bh"]], axis=1))
             for it, x in zip(items, wy)]
    for it, x_ab, x_lk, x_cr in zip(items, ab_wy, lkv_akv, cross):
        s = it["s"]
        j = jnp.where(d == 0, it["sub"], RW_SUB - 1 - it["sub"])
        re_ref[0, 0, j, :, s] = it["rt"] - x_ab[:, :hd]
        ol_ref[0, 0, j, :, s] = x_lk[c:] - x_ab[:, hd:]
        sl_ref[0, 0, j, :, s] = x_cr[:hd, :hd] - x_cr[2 * hd:, hd:]
        p_ref[0, 0, j, :, s] = eye * it["e_tot"] - x_cr[hd:2 * hd, hd:]


def _rwkv_carry_kernel(p_ref, sl_ref, s_out_ref, st_ref):
    @pl.when(pl.program_id(0) == 0)
    def _():
        st_ref[...] = jnp.zeros_like(st_ref)

    for d in range(2):
        for b in range(NB):
            s = st_ref[d, b]
            s_out_ref[d, b, 0] = s
            for h in range(RW_H):
                sl = slice(h * RW_D, (h + 1) * RW_D)
                st_ref[d, b, :, sl] = _hdot(s[:, sl], p_ref[d, b, 0, :, sl]) + sl_ref[d, b, 0, :, sl]


TILE_CHUNKS = TILE // CHUNK


def _tile_of_group(b, grp):
    return jnp.where(grp == 0, N_LAT // TILE + b, (SEQ // TILE) * b + grp - 1)


def _bwd_group(grp):
    return jnp.where(grp == 0, 0, SEQ_CHUNKS // TILE_CHUNKS - grp)


def _rwkv_out_kernel(olf_ref, ref_ref, sf_ref, olb_ref, reb_ref, sb_ref, bonus_ref, g_ref, gng_ref, gnb_ref,
                     bd_ref, y_ref, o_scr):
    n = TILE_CHUNKS
    items = [(i, slice(h * RW_D, (h + 1) * RW_D)) for i in range(n) for h in range(RW_H)]
    pf = [_bdot_nt(ref_ref[0, 0, i, :, s], sf_ref[0, 0, i, :, s]) for i, s in items]
    pb = [_bdot_nt(reb_ref[0, 0, n - 1 - i, :, s], sb_ref[0, 0, n - 1 - i, :, s]) for i, s in items]
    for (i, s), a, b in zip(items, pf, pb):
        o_scr[i * CHUNK:(i + 1) * CHUNK, s] = olf_ref[0, 0, i, :, s] + olb_ref[0, 0, n - 1 - i, :, s] + a + b
    o = o_scr[...] + bonus_ref[...]
    bd = bd_ref[...]
    cen = o - _dot_rhs_exact(o, bd) * (1.0 / RW_D)
    var = _dot_rhs_exact(cen * cen, bd) * (1.0 / RW_D)
    y = cen * lax.rsqrt(var + RW_GN_EPS) * gng_ref[...] + gnb_ref[...]
    y_ref[...] = (y * g_ref[...]).astype(y_ref.dtype)


def _rwkv_scan(r, v, kk, lw, bmat, km, bonus, g, gn_g, gn_b):
    sub_rows = RW_SUB * CHUNK
    lat_blocks = N_LAT // sub_rows
    lat_per_b = SEQ // sub_rows
    ctx_per_b = CTX // sub_rows

    def step_block(d, rb):
        is_lat = rb < lat_blocks
        b = jnp.where(is_lat, rb // lat_per_b, (rb - lat_blocks) // ctx_per_b)
        i = jnp.where(is_lat, rb % lat_per_b, (rb - lat_blocks) % ctx_per_b)
        fwd = jnp.where(is_lat, ctx_per_b + i, i)
        bwd = jnp.where(is_lat, ctx_per_b + lat_per_b - 1 - i, ctx_per_b - 1 - i)
        return b, jnp.where(d == 0, fwd, bwd)

    tok = pl.BlockSpec((sub_rows, RW_W), lambda d, rb: (rb, 0))
    drn = pl.BlockSpec((1, sub_rows, RW_W), lambda d, rb: (d, rb, 0))
    step_shape = jax.ShapeDtypeStruct((2, NB, SEQ_CHUNKS, CHUNK, RW_W), F32)
    step_blk = pl.BlockSpec((1, 1, RW_SUB, CHUNK, RW_W), lambda d, rb: (d,) + step_block(d, rb) + (0, 0))
    p, s_loc, r_eff, o_loc = pl.pallas_call(
        _rwkv_chunk_kernel,
        grid=(2, ROWS // sub_rows),
        in_specs=[tok, tok, tok, drn, drn, drn],
        out_specs=[step_blk] * 4,
        out_shape=[step_shape] * 4,
        compiler_params=_cp("parallel", "parallel"),
        name="rwkv_chunk",
    )(r, v, kk, lw, bmat, km)

    all_blk = pl.BlockSpec((2, NB, 1, CHUNK, RW_W), lambda n: (0, 0, n, 0, 0))
    s_in = pl.pallas_call(
        _rwkv_carry_kernel,
        grid=(SEQ_CHUNKS,),
        in_specs=[all_blk, all_blk],
        out_specs=all_blk,
        out_shape=step_shape,
        scratch_shapes=[pltpu.VMEM((2, NB, CHUNK, RW_W), F32)],
        compiler_params=_cp("arbitrary"),
        name="rwkv_carry",
    )(p, s_loc)

    fwd = pl.BlockSpec((1, 1, TILE_CHUNKS, CHUNK, RW_W), lambda b, grp: (0, b, grp, 0, 0))
    bwd = pl.BlockSpec((1, 1, TILE_CHUNKS, CHUNK, RW_W), lambda b, grp: (1, b, _bwd_group(grp), 0, 0))
    rows = pl.BlockSpec((TILE, RW_W), lambda b, grp: (_tile_of_group(b, grp), 0))
    vec = pl.BlockSpec((1, RW_W), lambda b, grp: (0, 0))
    return pl.pallas_call(
        _rwkv_out_kernel,
        grid=(NB, SEQ_CHUNKS // TILE_CHUNKS),
        in_specs=[fwd, fwd, fwd, bwd, bwd, bwd, rows, rows, vec, vec,
                  pl.BlockSpec((RW_W, RW_W), lambda b, grp: (0, 0))],
        out_specs=rows,
        out_shape=jax.ShapeDtypeStruct((ROWS, RW_W), BF16),
        scratch_shapes=[pltpu.VMEM((TILE, RW_W), F32)],
        compiler_params=_cp("parallel", "parallel"),
        name="rwkv_out",
    )(o_loc, r_eff, s_in, o_loc, r_eff, s_in, bonus, g, gn_g.reshape(1, RW_W), gn_b.reshape(1, RW_W),
      _head_block_ones(RW_W, RW_D).astype(BF16))


def _mlstm_prep_kernel(z_ref, zp_ref, zn_ref, w_ref, o_ref):
    z = z_ref[...]
    zp, zn = _neighbours(z, zp_ref[...], zn_ref[...])
    y = _silu(zp * w_ref[0:1, :] + z * w_ref[1:2, :] + zn * w_ref[2:3, :])
    col = lax.broadcasted_iota(jnp.int32, y.shape, 1)
    o_ref[...] = jnp.where(col >= ML_W, y * (ML_D ** -0.5), y)


def _mlstm_prep(z, conv_w):
    return pl.pallas_call(
        _mlstm_prep_kernel,
        grid=(N_TILES,),
        in_specs=_tile_halo_specs(2 * ML_W) + [pl.BlockSpec((3, 2 * ML_W), lambda t: (0, 0))],
        out_specs=pl.BlockSpec((TILE, 2 * ML_W), lambda t: (t, 0)),
        out_shape=jax.ShapeDtypeStruct((ROWS, 2 * ML_W), F32),
        compiler_params=_cp("parallel"),
        name="mlstm_prep",
    )(z, z, z, conv_w)


def _log_sigmoid(x):
    return jnp.minimum(x, 0.0) - jnp.log(1.0 + jnp.exp(-jnp.abs(x)))


N_CHAINS = 2 * NB


def _mlstm_scan_kernel(*refs):
    nc = N_CHAINS
    q_refs, k_refs, v_refs = refs[0:nc], refs[nc:2 * nc], refs[2 * nc:3 * nc]
    gc_refs, gr_refs = refs[3 * nc:4 * nc], refs[4 * nc:5 * nc]
    bc_ref, br_ref, o_ref, c_ref, n_ref, m_ref = refs[5 * nc:]

    @pl.when(pl.program_id(0) == 0)
    def _():
        c_ref[...] = jnp.zeros_like(c_ref)
        n_ref[...] = jnp.zeros_like(n_ref)
        m_ref[...] = jnp.zeros_like(m_ref)

    c = CHUNK
    ti = lax.broadcasted_iota(jnp.int32, (c, c), 0)
    si = lax.broadcasted_iota(jnp.int32, (c, c), 1)
    masks = (ti >= si, ti <= si)
    items = [(ci, h) for ci in range(nc) for h in range(ML_H)]
    sls = [slice(h * ML_D, (h + 1) * ML_D) for h in range(ML_H)]
    gcol = [gc_refs[ci][0, 0] + bc_ref[ci // NB] for ci in range(nc)]
    grow = [gr_refs[ci][0, 0] + br_ref[ci // NB] for ci in range(nc)]

    q = [q_refs[ci][:, sls[h]] for ci, h in items]
    k = [k_refs[ci][:, sls[h]] for ci, h in items]
    v = [v_refs[ci][:, sls[h]] for ci, h in items]
    qk = [_bdot_nt(a, b) for a, b in zip(q, k)]
    c_mat = [c_ref[ci, h] for ci, h in items]
    n_vec = [n_ref[ci, h] for ci, h in items]
    m_prev = [m_ref[ci, h][0:1, 0:1] for ci, h in items]
    qc = [_bdot_nt(a, b) for a, b in zip(q, c_mat)]
    qn = [jnp.sum(a * b, axis=1, keepdims=True) for a, b in zip(q, n_vec)]

    log_w, m_inter, cum_col, i_col, total = [], [], [], [], []
    for ci, h in items:
        mask = masks[ci // NB]
        mask_t = masks[1 - ci // NB]
        f_col = _log_sigmoid(gcol[ci][:, ML_H + h:ML_H + h + 1])
        f_row = _log_sigmoid(grow[ci][ML_H + h:ML_H + h + 1, :])
        cc = jnp.sum(jnp.where(mask, f_row, 0.0), axis=1, keepdims=True)
        cr = jnp.sum(jnp.where(mask_t, f_col, 0.0), axis=0, keepdims=True)
        log_w.append(jnp.where(mask, cc - cr + grow[ci][h:h + 1, :], ML_NEG))
        cum_col.append(cc)
        i_col.append(gcol[ci][:, h:h + 1])
        total.append(jnp.sum(f_row, axis=1, keepdims=True))
    m_inter = [a + b for a, b in zip(cum_col, m_prev)]
    m_t = [jnp.maximum(jnp.max(a, axis=1, keepdims=True), b) for a, b in zip(log_w, m_inter)]
    s = [a * jnp.exp(b - m) for a, b, m in zip(qk, log_w, m_t)]
    w_inter = [jnp.exp(a - m) for a, m in zip(m_inter, m_t)]
    sv = [_bdot(a, b) for a, b in zip(s, v)]
    for i, (ci, h) in enumerate(items):
        num = sv[i] + w_inter[i] * qc[i]
        den = jnp.sum(s[i], axis=1, keepdims=True) + w_inter[i] * qn[i]
        o_ref[ci // NB, ci % NB, 0, :, sls[h]] = num / jnp.maximum(jnp.abs(den), jnp.exp(-m_t[i]))
    log_src = [t - a + b for t, a, b in zip(total, cum_col, i_col)]
    m_new = [jnp.maximum(t + mp, jnp.max(ls, axis=0, keepdims=True)) for t, mp, ls in zip(total, m_prev, log_src)]
    src = [jnp.exp(ls - mn) for ls, mn in zip(log_src, m_new)]
    decay = [jnp.exp(t + mp - mn) for t, mp, mn in zip(total, m_prev, m_new)]
    vk = [_bdot_tn(a * sr, b) for a, sr, b in zip(v, src, k)]
    for i, (ci, h) in enumerate(items):
        c_ref[ci, h] = decay[i] * c_mat[i] + vk[i]
        n_ref[ci, h] = decay[i] * n_vec[i] + jnp.sum(src[i] * k[i], axis=0, keepdims=True)
        m_ref[ci, h] = jnp.broadcast_to(m_new[i], m_ref.shape[2:])


def _bwd_step(pos):
    return jnp.where(pos < CTX_CHUNKS, CTX_CHUNKS - 1 - pos, SEQ_CHUNKS - 1 + CTX_CHUNKS - pos)


def _mlstm_scan(qk, z, gcol, grow, bcol, brow):
    chains = [(d, b) for d in range(2) for b in range(NB)]

    def tok(col_blk):
        return [pl.BlockSpec((CHUNK, ML_W), lambda n, d=d, b=b: (_seq_row_block(b, d, n), col_blk))
                for d, b in chains]

    gc_specs = [pl.BlockSpec((1, 1, CHUNK, 2 * ML_H), lambda n, d=d, b=b: (d, _seq_row_block(b, d, n), 0, 0))
                for d, b in chains]
    gr_specs = [pl.BlockSpec((1, 1, 2 * ML_H, CHUNK), lambda n, d=d, b=b: (d, _seq_row_block(b, d, n), 0, 0))
                for d, b in chains]
    nc = N_CHAINS
    return pl.pallas_call(
        _mlstm_scan_kernel,
        grid=(SEQ_CHUNKS,),
        in_specs=tok(0) + tok(1) + tok(2) + gc_specs + gr_specs + [
            pl.BlockSpec((2, 1, 2 * ML_H), lambda n: (0, 0, 0)),
            pl.BlockSpec((2, 2 * ML_H, 1), lambda n: (0, 0, 0))],
        out_specs=pl.BlockSpec((2, NB, 1, CHUNK, ML_W), lambda n: (0, 0, n, 0, 0)),
        out_shape=jax.ShapeDtypeStruct((2, NB, SEQ_CHUNKS, CHUNK, ML_W), F32),
        scratch_shapes=[pltpu.VMEM((nc, ML_H, ML_D, ML_D), F32), pltpu.VMEM((nc, ML_H, 1, ML_D), F32),
                        pltpu.VMEM((nc, ML_H, 8, 128), F32)],
        compiler_params=_cp("arbitrary"),
        name="mlstm_scan",
    )(*([qk] * (2 * nc) + [z] * nc + [gcol] * nc + [grow] * nc + [bcol, brow]))


def _mlstm_out_kernel(hf_ref, hb_ref, og_ref, gng_ref, y_ref):
    n = TILE_CHUNKS
    for i in range(n):
        rows = slice(i * CHUNK, (i + 1) * CHUNK)
        hsum = _sigmoid(og_ref[rows, :]) * (hf_ref[0, 0, i] + hb_ref[0, 0, n - 1 - i])
        for h in range(ML_H):
            sl = slice(h * ML_D, (h + 1) * ML_D)
            x = hsum[:, sl]
            cen = x - jnp.mean(x, axis=-1, keepdims=True)
            var = jnp.mean(cen * cen, axis=-1, keepdims=True)
            y_ref[rows, sl] = (cen * lax.rsqrt(var + EPS) * gng_ref[:, sl]).astype(y_ref.dtype)


def _mlstm_out(hs, z, gn_g):
    blk = (1, 1, TILE_CHUNKS, CHUNK, ML_W)
    return pl.pallas_call(
        _mlstm_out_kernel,
        grid=(NB, SEQ_CHUNKS // TILE_CHUNKS),
        in_specs=[pl.BlockSpec(blk, lambda b, grp: (0, b, grp, 0, 0)),
                  pl.BlockSpec(blk, lambda b, grp: (1, b, _bwd_group(grp), 0, 0)),
                  pl.BlockSpec((TILE, ML_W), lambda b, grp: (_tile_of_group(b, grp), 3)),
                  pl.BlockSpec((1, ML_W), lambda b, grp: (0, 0))],
        out_specs=pl.BlockSpec((TILE, ML_W), lambda b, grp: (_tile_of_group(b, grp), 0)),
        out_shape=jax.ShapeDtypeStruct((ROWS, ML_W), BF16),
        compiler_params=_cp("parallel", "parallel"),
        name="mlstm_out",
    )(hs, hs, z, gn_g.reshape(1, ML_W))


def _mlstm(z_main, z_gates, conv_w, i_b, f_b, gn_g):
    qk = _mlstm_prep(z_main, conv_w)
    n_chunks = ROWS // CHUNK
    gates = z_gates[:, :4 * ML_H].reshape(n_chunks, CHUNK, 2, 2, ML_H)
    gcol = jnp.transpose(gates, (3, 0, 1, 2, 4)).reshape(2, n_chunks, CHUNK, 2 * ML_H)
    grow = jnp.swapaxes(gcol, 2, 3)
    bias = jnp.concatenate([i_b, f_b], axis=-1)
    hs = _mlstm_scan(qk, z_main, gcol, grow, bias.reshape(2, 1, 2 * ML_H), bias.reshape(2, 2 * ML_H, 1))
    return _mlstm_out(hs, z_main, gn_g)


def _s5_toeplitz_kernel(kf_ref, kb_ref, o_ref):
    kf = kf_ref[0]
    kb = kb_ref[0]
    lane = lax.broadcasted_iota(jnp.int32, kf.shape, 1)
    width = S5_L * S5_C
    for j in range(S5_L):
        f = kf if j == 0 else jnp.where(lane >= S5_C * j, pltpu.roll(kf, S5_C * j, 1), 0.0)
        back = S5_L - 1 - j
        b = kb if back == 0 else jnp.where(lane < S5_C * (j + 1), pltpu.roll(kb, width - S5_C * back, 1), 0.0)
        o_ref[0, 0, :, j] = f.reshape(S5_G, S5_C, width).astype(o_ref.dtype)
        o_ref[0, 1, :, j] = b.reshape(S5_G, S5_C, width).astype(o_ref.dtype)


def _s5_operators(lam_re, lam_im, log_step, b_re, b_im, c_re, c_im):
    nl = lam_re.shape[0]
    dt = jnp.exp(log_step)[..., None]
    mag = jnp.exp(lam_re * dt)
    a_re = mag * jnp.cos(lam_im * dt)
    a_im = mag * jnp.sin(lam_im * dt)
    den = lam_re * lam_re + lam_im * lam_im
    f_re = ((a_re - 1) * lam_re + a_im * lam_im) / den
    f_im = (a_im * lam_re - (a_re - 1) * lam_im) / den
    bb_re = f_re[..., None] * b_re - f_im[..., None] * b_im
    bb_im = f_re[..., None] * b_im + f_im[..., None] * b_re
    bt_re = jnp.swapaxes(bb_re, -1, -2)
    bt_im = jnp.swapaxes(bb_im, -1, -2)
    pr = [jnp.ones_like(a_re)]
    pi = [jnp.zeros_like(a_im)]
    for _ in range(S5_L):
        pr.append(pr[-1] * a_re - pi[-1] * a_im)
        pi.append(pr[-2] * a_im + pi[-1] * a_re)
    pr = jnp.stack(pr, axis=3)
    pi = jnp.stack(pi, axis=3)

    def times_b(qr, qi):
        qr, qi = qr[..., None, :], qi[..., None, :]
        br, bi = bt_re[:, :, :, None], bt_im[:, :, :, None]
        return qr * br - qi * bi, qr * bi + qi * br

    wr, wi = times_b(pr[:, :, :, :S5_L], pi[:, :, :, :S5_L])
    wr = jnp.swapaxes(wr, 3, 4)
    wi = jnp.swapaxes(wi, 3, 4)
    kern = (jnp.einsum("ldgktp,ldgcp->ldgktc", wr, c_re, precision=HP)
            - jnp.einsum("ldgktp,ldgcp->ldgktc", wi, c_im, precision=HP))
    kf = kern[:, 0].reshape(nl, S5_G * S5_C, S5_L * S5_C)
    kb = jnp.flip(kern[:, 1], axis=3).reshape(nl, S5_G * S5_C, S5_L * S5_C)
    rows_blk = pl.BlockSpec((1, S5_G * S5_C, S5_L * S5_C), lambda l: (l, 0, 0))
    m = pl.pallas_call(
        _s5_toeplitz_kernel,
        grid=(nl,),
        in_specs=[rows_blk, rows_blk],
        out_specs=pl.BlockSpec((1, 2, S5_G, S5_L, S5_C, S5_L * S5_C), lambda l: (l, 0, 0, 0, 0, 0)),
        out_shape=jax.ShapeDtypeStruct((nl, 2, S5_G, S5_L, S5_C, S5_L * S5_C), BF16),
        compiler_params=_cp("parallel"),
        name="s5_toeplitz",
    )(kf, kb).reshape(nl, 2, S5_G, S5_L * S5_C, S5_L * S5_C)

    def stack_dirs(fwd, bwd):
        return jnp.stack([fwd[:, 0], bwd[:, 1]], axis=1)

    er, ei = times_b(stack_dirs(jnp.flip(pr[:, :, :, :S5_L], axis=3), pr[:, :, :, :S5_L]),
                     stack_dirs(jnp.flip(pi[:, :, :, :S5_L], axis=3), pi[:, :, :, :S5_L]))
    e = jnp.concatenate([er, ei], axis=-1).reshape(nl, 2, S5_G, S5_L * S5_C, 2 * S5_P).astype(BF16)
    qr = stack_dirs(pr[:, :, :, 1:], jnp.flip(pr[:, :, :, 1:], axis=3))[..., None, :]
    qi = stack_dirs(pi[:, :, :, 1:], jnp.flip(pi[:, :, :, 1:], axis=3))[..., None, :]
    cr, ci = c_re[:, :, :, None], c_im[:, :, :, None]
    ft = jnp.concatenate([cr * qr - ci * qi, -(cr * qi + ci * qr)], axis=-1)
    ft = ft.reshape(nl, 2, S5_G, S5_L * S5_C, 2 * S5_P).astype(BF16)
    return m, e, ft, pr[:, :, :, S5_L], pi[:, :, :, S5_L]


def _s5_local_kernel(u_ref, m_ref, e_ref, y_ref, x_ref):
    u = u_ref[0]
    for d in range(2):
        y_ref[d, 0] = jnp.dot(u, m_ref[d, 0], preferred_element_type=F32)
        x_ref[d, 0] = jnp.dot(u, e_ref[d, 0], preferred_element_type=F32)


def _s5_carry_kernel(xc_ref, ar_ref, ai_ref, x0_ref, st_ref):
    @pl.when(pl.program_id(1) == 0)
    def _():
        st_ref[...] = jnp.zeros_like(st_ref)

    ar = ar_ref[0]
    ai = ai_ref[0]

    def run(order):
        x = st_ref[...]
        for i in order:
            x0_ref[0, :, :, i, :] = x
            x = x * ar + pltpu.roll(x, S5_P, 2) * ai + xc_ref[0, :, :, i, :]
        st_ref[...] = x

    n = xc_ref.shape[3]

    @pl.when(pl.program_id(0) == 0)
    def _():
        run(range(n))

    @pl.when(pl.program_id(0) == 1)
    def _():
        run(reversed(range(n)))


def _s5_state_kernel(y_ref, x0_ref, ft_ref, o_ref):
    acc = y_ref[0, 0] + y_ref[1, 0]
    for d in range(2):
        acc = acc + _bdot_nt(x0_ref[d, 0], ft_ref[d, 0])
    o_ref[0] = acc.astype(o_ref.dtype)


def _s5_glu_kernel(y_ref, u_ref, d_ref, w_ref, o_ref):
    x = y_ref[...] + u_ref[...] * d_ref[...]
    ge = 0.5 * x * (1.0 + jnp.tanh(math.sqrt(2.0 / math.pi) * (x + 0.044715 * (x * x * x))))
    p = _bdot(ge, w_ref[...])
    o_ref[...] = (p[:, :S5_W] * _sigmoid(p[:, S5_W:])).astype(o_ref.dtype)


def _s5(z, m, e, ft, a_re, a_im, d_skip, w_glu):
    n_ch = (SEQ + CTX) // S5_L
    n_ctx_ch = CTX // S5_L
    rows = NB * n_ch
    width = S5_L * S5_C

    def to_groups(x):
        x = x.reshape(NB, -1, S5_L, S5_G, S5_C)
        return jnp.transpose(x, (3, 0, 1, 2, 4)).reshape(S5_G, NB, -1, width)

    zb = z.astype(BF16)
    u = jnp.concatenate([to_groups(zb[N_LAT:]), to_groups(zb[:N_LAT])], axis=2).reshape(S5_G, rows, width)
    y_loc, x_in = pl.pallas_call(
        _s5_local_kernel,
        grid=(S5_G,),
        in_specs=[pl.BlockSpec((1, rows, width), lambda g: (g, 0, 0)),
                  pl.BlockSpec((2, 1, width, width), lambda g: (0, g, 0, 0)),
                  pl.BlockSpec((2, 1, width, 2 * S5_P), lambda g: (0, g, 0, 0))],
        out_specs=[pl.BlockSpec((2, 1, rows, width), lambda g: (0, g, 0, 0)),
                   pl.BlockSpec((2, 1, rows, 2 * S5_P), lambda g: (0, g, 0, 0))],
        out_shape=[jax.ShapeDtypeStruct((2, S5_G, rows, width), F32),
                   jax.ShapeDtypeStruct((2, S5_G, rows, 2 * S5_P), F32)],
        compiler_params=_cp("parallel"),
        name="s5_local",
    )(u, m, e)

    step = n_ctx_ch
    n_blk = n_ch // step
    coef_r = jnp.concatenate([a_re, a_re], axis=-1)[:, :, None, :]
    coef_i = jnp.concatenate([-a_im, a_im], axis=-1)[:, :, None, :]

    def chunk_blk(d, i):
        return jnp.where(d == 0, i, jnp.where(i == 0, 0, n_blk - i))

    st_blk = pl.BlockSpec((1, S5_G, NB, step, 2 * S5_P), lambda d, i: (d, 0, 0, chunk_blk(d, i), 0))
    coef_blk = pl.BlockSpec((1, S5_G, 1, 2 * S5_P), lambda d, i: (d, 0, 0, 0))
    x0 = pl.pallas_call(
        _s5_carry_kernel,
        grid=(2, n_blk),
        in_specs=[st_blk, coef_blk, coef_blk],
        out_specs=st_blk,
        out_shape=jax.ShapeDtypeStruct((2, S5_G, NB, n_ch, 2 * S5_P), F32),
        scratch_shapes=[pltpu.VMEM((S5_G, NB, 2 * S5_P), F32)],
        compiler_params=_cp("arbitrary", "arbitrary"),
        name="s5_carry",
    )(x_in.reshape(2, S5_G, NB, n_ch, 2 * S5_P), coef_r, coef_i).reshape(2, S5_G, rows, 2 * S5_P)

    y = pl.pallas_call(
        _s5_state_kernel,
        grid=(S5_G,),
        in_specs=[pl.BlockSpec((2, 1, rows, width), lambda g: (0, g, 0, 0)),
                  pl.BlockSpec((2, 1, rows, 2 * S5_P), lambda g: (0, g, 0, 0)),
                  pl.BlockSpec((2, 1, width, 2 * S5_P), lambda g: (0, g, 0, 0))],
        out_specs=pl.BlockSpec((1, rows, width), lambda g: (g, 0, 0)),
        out_shape=jax.ShapeDtypeStruct((S5_G, rows, width), BF16),
        compiler_params=_cp("parallel"),
        name="s5_state",
    )(y_loc, x0, ft)

    y = jnp.transpose(y.reshape(S5_G, NB, n_ch, S5_L, S5_C), (1, 2, 3, 0, 4)).reshape(NB, n_ch * S5_L, S5_W)
    y = jnp.concatenate([y[:, CTX:].reshape(N_LAT, S5_W), y[:, :CTX].reshape(N_CTX, S5_W)], axis=0)
    tok = pl.BlockSpec((TILE, S5_W), lambda t: (t, 0))
    return pl.pallas_call(
        _s5_glu_kernel,
        grid=(N_TILES,),
        in_specs=[tok, tok, pl.BlockSpec((1, S5_W), lambda t: (0, 0)),
                  pl.BlockSpec((S5_W, 2 * S5_W), lambda t: (0, 0))],
        out_specs=tok,
        out_shape=jax.ShapeDtypeStruct((ROWS, S5_W), BF16),
        compiler_params=_cp("parallel"),
        name="s5_glu",
    )(y, z, d_skip.reshape(1, S5_W), w_glu)


def kernel(x, c, ctx, c_ctx, w_mod, b_mod, norm1_g, norm2_g, w_in, b_gate, q_norm_g, k_norm_g, rwkv_mu, rwkv_w0, rwkv_w_up, rwkv_a0, rwkv_a_up, rwkv_g_up, rwkv_k_k, rwkv_k_a, rwkv_r_k, rwkv_gn_g, rwkv_gn_b, mlstm_conv_w, mlstm_i_b, mlstm_f_b, mlstm_gn_g, s5_lam_re, s5_lam_im, s5_log_step, s5_b_re, s5_b_im, s5_c_re, s5_c_im, s5_d, s5_w_glu, w_br_attn, w_br_rwkv, w_br_mlstm, w_br_s5, w_out, w_ffn_in, w_ffn_out, final_norm_g):
    cos, sin = _rope_tables()
    xs = jnp.concatenate([x.reshape(N_LAT, D), ctx.reshape(N_CTX, D)], axis=0)
    c_all = jnp.concatenate([c, c_ctx[None], jnp.zeros((3, D), F32)], axis=0)
    s5_m, s5_e, s5_ft, s5_ar, s5_ai = _s5_operators(s5_lam_re, s5_lam_im, s5_log_step, s5_b_re, s5_b_im,
                                                    s5_c_re, s5_c_im)
    tm = 1024
    for l in range(DEPTH):
        last = l == DEPTH - 1
        n_rows = N_LAT if last else ROWS
        mod = _modulation(c_all, w_mod, b_mod, l).reshape(8, 1, 6 * D)
        h = _norm_mod(xs, norm1_g[l], mod, 0, 1, N_TILES)
        w_attn, w_rwkv, w_ml, w_mlg, w_s5, w_gate = _w_in_split(w_in, l)
        z_attn = _mm(h, w_attn, tm, 512)
        z_rwkv = _mm(h, w_rwkv, tm, 896)
        z_ml = _mm(h, w_ml, tm, 1024)
        z_mlg = _mm(h, w_mlg, tm, 128)
        z_s5 = _mm(h, w_s5, tm, 512)
        z_gate = _mm(h, w_gate, tm, 1024)

        ya = _attention(z_attn, q_norm_g[l], k_norm_g[l], cos, sin)
        rp = dict(mu=rwkv_mu[l], w0=rwkv_w0[l], w_up=rwkv_w_up[l], a0=rwkv_a0[l], a_up=rwkv_a_up[l],
                  g_up=rwkv_g_up[l], k_k=rwkv_k_k[l], k_a=rwkv_k_a[l], r_k=rwkv_r_k[l].reshape(RW_W))
        r, v, kk, g, bonus, lw, bmat, km = _rwkv_prep(z_rwkv, rp)
        yr = _rwkv_scan(r, v, kk, lw, bmat, km, bonus, g, rwkv_gn_g[l], rwkv_gn_b[l])
        ym = _mlstm(z_ml, z_mlg, mlstm_conv_w[l], mlstm_i_b[l], mlstm_f_b[l], mlstm_gn_g[l])
        ys = _s5(z_s5, s5_m[l], s5_e[l], s5_ft[l], s5_ar[l], s5_ai[l], s5_d[l], s5_w_glu[l])

        y = _merge(ya, yr, ym, ys, z_gate, b_gate, w_br_attn, w_br_rwkv, w_br_mlstm, w_br_s5, l, tm, 512, n_rows)
        xs = _mm_res(y, w_out, l, xs, mod, 2, tm, 1024, n_rows)
        h2 = _norm_mod(xs, norm2_g[l], mod, 3, 4, n_rows // TILE)
        u = _ffn_in(h2, w_ffn_in, l, tm, 512, n_rows)
        xs = _mm_res(u, w_ffn_out, l, xs, mod, 5, 512, 512, n_rows)
    return _final_norm(xs, final_norm_g).reshape(NB, SEQ, D)
```

```python
import functools
import math

import numpy as np
import jax
import jax.numpy as jnp
from jax import lax
from jax.experimental import pallas as pl
from jax.experimental.pallas import tpu as pltpu

F32 = jnp.float32
BF16 = jnp.bfloat16
HP = lax.Precision.HIGHEST

D = 2048
NB = 4
SEQ = 2048
CTX = 256
DEPTH = 2
N_LAT = NB * SEQ
N_CTX = NB * CTX
ROWS = N_LAT + N_CTX
EPS = 1e-6
GRID_W = 64

HEAD_DIM = 128
ATTN_HEADS = 8
ATTN_KV = 2
ROPE_THETA = 10000.0
ATTN_Q = ATTN_HEADS * HEAD_DIM
ATTN_IN = (ATTN_HEADS + 2 * ATTN_KV) * HEAD_DIM

RW_H = 8
RW_D = 64
RW_W = 512
RW_IN = 3 * RW_W + 64 + 64 + 128
RW_DECAY = math.exp(-0.5)
RW_GN_EPS = 64e-5

ML_H = 4
ML_D = 128
ML_W = 512
ML_NEG = -1e30
ML_IN = 4 * ML_W + 4 * ML_H

S5_W = 512
S5_C = 16
S5_G = 32
S5_P = 64
S5_L = 16

FFN_H = 5632
GATE_IN = 4 * D

CHUNK = 64
TILE = 256
N_TILES = ROWS // TILE
SEQ_CHUNKS = (SEQ + CTX) // CHUNK
CTX_CHUNKS = CTX // CHUNK
LAT_CHUNKS = SEQ // CHUNK

VMEM_LIMIT_BYTES = 56 * 1024 * 1024


def _cp(*sem):
    return pltpu.CompilerParams(dimension_semantics=sem, vmem_limit_bytes=VMEM_LIMIT_BYTES)


def _bdot(a, b):
    return jnp.dot(a.astype(BF16), b.astype(BF16), preferred_element_type=F32)


def _bdot_nt(a, b):
    return lax.dot_general(a.astype(BF16), b.astype(BF16), (((1,), (1,)), ((), ())),
                           preferred_element_type=F32)


def _bdot_tn(a, b):
    return lax.dot_general(a.astype(BF16), b.astype(BF16), (((0,), (0,)), ((), ())),
                           preferred_element_type=F32)


def _hdot(a, b):
    return jnp.dot(a, b, precision=HP, preferred_element_type=F32)


def _sigmoid(x):
    return 1.0 / (1.0 + jnp.exp(-x))


def _silu(x):
    return x * _sigmoid(x)


def _mod_kernel(c_ref, w_ref, b_ref, o_ref):
    c_hi, c_lo = _split_bf16(_silu(c_ref[...]), 2)
    w_hi, w_lo = _split_bf16(w_ref[...], 2)
    rows = c_hi.shape[0]
    both = jnp.dot(jnp.concatenate([c_hi, c_lo], axis=0), w_hi, preferred_element_type=F32)
    o_ref[...] = both[:rows] + both[rows:] + jnp.dot(c_hi, w_lo, preferred_element_type=F32) + b_ref[...]


def _modulation(c_all, w, b, l):
    tn = 1024
    return pl.pallas_call(
        _mod_kernel,
        grid=(6 * D // tn,),
        in_specs=[pl.BlockSpec((8, D), lambda j: (0, 0)),
                  pl.BlockSpec((None, D, tn), lambda j: (l, 0, j)),
                  pl.BlockSpec((None, 1, tn), lambda j: (l, 0, j))],
        out_specs=pl.BlockSpec((8, tn), lambda j: (0, j)),
        out_shape=jax.ShapeDtypeStruct((8, 6 * D), F32),
        compiler_params=_cp("arbitrary"),
        name="modulation",
    )(c_all, w, b.reshape(DEPTH, 1, 6 * D))


def _mod_row(i, tm):
    return jnp.where(i * tm < N_LAT, (i * tm) // SEQ, NB)


def _norm_mod_kernel(x_ref, g_ref, sh_ref, sc_ref, o_ref):
    x = x_ref[...]
    y = x * lax.rsqrt(jnp.mean(x * x, axis=-1, keepdims=True) + EPS) * g_ref[...]
    o_ref[...] = (y * (1.0 + sc_ref[0]) + sh_ref[0]).astype(o_ref.dtype)


def _norm_mod(x, g, modr, shift_blk, scale_blk, n_tiles):
    return pl.pallas_call(
        _norm_mod_kernel,
        grid=(n_tiles,),
        in_specs=[pl.BlockSpec((TILE, D), lambda i: (i, 0)),
                  pl.BlockSpec((1, D), lambda i: (0, 0)),
                  pl.BlockSpec((1, 1, D), lambda i: (_mod_row(i, TILE), 0, shift_blk)),
                  pl.BlockSpec((1, 1, D), lambda i: (_mod_row(i, TILE), 0, scale_blk))],
        out_specs=pl.BlockSpec((TILE, D), lambda i: (i, 0)),
        out_shape=jax.ShapeDtypeStruct((n_tiles * TILE, D), BF16),
        compiler_params=_cp("parallel"),
        name="norm_mod",
    )(x, g.reshape(1, D), modr, modr)


W_IN_OFFSETS = tuple(int(v) for v in np.cumsum([0, ATTN_IN, RW_IN, 4 * ML_W, 4 * ML_H, S5_W, GATE_IN]))
ML_GATE_PAD = 128


def _w_in_split_kernel(w_ref, attn_ref, rwkv_ref, ml_ref, mlg_ref, s5_ref, gate_ref):
    o = W_IN_OFFSETS
    attn_ref[...] = w_ref[:, o[0]:o[1]].astype(BF16)
    rwkv_ref[...] = w_ref[:, o[1]:o[2]].astype(BF16)
    ml_ref[...] = w_ref[:, o[2]:o[3]].astype(BF16)
    tail = w_ref[:, o[3]:o[3] + ML_GATE_PAD]
    lane = lax.broadcasted_iota(jnp.int32, tail.shape, 1)
    mlg_ref[...] = jnp.where(lane < o[4] - o[3], tail, 0.0).astype(BF16)
    s5_ref[...] = w_ref[:, o[4]:o[5]].astype(BF16)
    gate_ref[...] = w_ref[:, o[5]:o[6]].astype(BF16)


def _w_in_split(w, l):
    tr = 128
    widths = (ATTN_IN, RW_IN, 4 * ML_W, ML_GATE_PAD, S5_W, GATE_IN)
    return pl.pallas_call(
        _w_in_split_kernel,
        grid=(D // tr,),
        in_specs=[pl.BlockSpec((None, tr, w.shape[2]), lambda i: (l, i, 0))],
        out_specs=[pl.BlockSpec((tr, n), lambda i: (i, 0)) for n in widths],
        out_shape=[jax.ShapeDtypeStruct((D, n), BF16) for n in widths],
        compiler_params=_cp("parallel"),
        name="w_in_split",
    )(w)


def _mm_kernel(a_ref, w_ref, o_ref):
    o_ref[...] = jnp.dot(a_ref[...], w_ref[...], preferred_element_type=F32).astype(o_ref.dtype)


def _mm(a, w, tm, tn, out_dtype=F32):
    m, k = a.shape
    n = w.shape[1]
    return pl.pallas_call(
        _mm_kernel,
        grid=(n // tn, m // tm),
        in_specs=[pl.BlockSpec((tm, k), lambda j, i: (i, 0)),
                  pl.BlockSpec((k, tn), lambda j, i: (0, j))],
        out_specs=pl.BlockSpec((tm, tn), lambda j, i: (i, j)),
        out_shape=jax.ShapeDtypeStruct((m, n), out_dtype),
        compiler_params=_cp("parallel", "parallel"),
        name="matmul",
    )(a, w)


def _mm_gate_kernel(a_ref, w_ref, b_ref, o_ref):
    z = jnp.dot(a_ref[...], w_ref[...], preferred_element_type=F32)
    o_ref[...] = _sigmoid(z + b_ref[...]).astype(o_ref.dtype)


def _mm_gate(a, w, b_gate, l, tm, tn, m):
    k = a.shape[1]
    n = w.shape[1]
    return pl.pallas_call(
        _mm_gate_kernel,
        grid=(n // tn, m // tm),
        in_specs=[pl.BlockSpec((tm, k), lambda j, i: (i, 0)),
                  pl.BlockSpec((k, tn), lambda j, i: (0, j)),
                  pl.BlockSpec((None, 1, tn), lambda j, i: (l, 0, j))],
        out_specs=pl.BlockSpec((tm, tn), lambda j, i: (i, j)),
        out_shape=jax.ShapeDtypeStruct((m, n), BF16),
        compiler_params=_cp("parallel", "parallel"),
        name="matmul_gate",
    )(a, w, b_gate.reshape(DEPTH, 1, GATE_IN))


def _mm_res_kernel(a_ref, w_ref, x_ref, g_ref, o_ref, wb_ref):
    @pl.when(pl.program_id(1) == 0)
    def _():
        wb_ref[...] = w_ref[...].astype(BF16)

    y = jnp.dot(a_ref[...], wb_ref[...], preferred_element_type=F32)
    o_ref[...] = x_ref[...] + g_ref[0] * y


def _mm_res(a, w, l, x, mod, gate_blk, tm, tn, n_rows):
    k = a.shape[1]
    n = w.shape[2]
    gpb = D // tn
    return pl.pallas_call(
        _mm_res_kernel,
        grid=(n // tn, n_rows // tm),
        in_specs=[pl.BlockSpec((tm, k), lambda j, i: (i, 0)),
                  pl.BlockSpec((None, k, tn), lambda j, i: (l, 0, j)),
                  pl.BlockSpec((tm, tn), lambda j, i: (i, j)),
                  pl.BlockSpec((1, 1, tn), lambda j, i: (_mod_row(i, tm), 0, gate_blk * gpb + j))],
        out_specs=pl.BlockSpec((tm, tn), lambda j, i: (i, j)),
        out_shape=jax.ShapeDtypeStruct((n_rows, n), F32),
        scratch_shapes=[pltpu.VMEM((k, tn), BF16)],
        compiler_params=_cp("arbitrary", "arbitrary"),
        name="matmul_residual",
    )(a, w, x, mod)


def _ffn_in_kernel(a_ref, wa_ref, wb_ref, o_ref, wab_ref, wbb_ref):
    @pl.when(pl.program_id(1) == 0)
    def _():
        wab_ref[...] = wa_ref[...].astype(BF16)
        wbb_ref[...] = wb_ref[...].astype(BF16)

    a = a_ref[...]
    u = jnp.dot(a, wab_ref[...], preferred_element_type=F32)
    v = jnp.dot(a, wbb_ref[...], preferred_element_type=F32)
    o_ref[...] = (_silu(u) * v).astype(o_ref.dtype)


def _ffn_in(h, w, l, tm, tn, n_rows):
    nb = FFN_H // tn
    return pl.pallas_call(
        _ffn_in_kernel,
        grid=(nb, n_rows // tm),
        in_specs=[pl.BlockSpec((tm, D), lambda j, i: (i, 0)),
                  pl.BlockSpec((None, D, tn), lambda j, i: (l, 0, j)),
                  pl.BlockSpec((None, D, tn), lambda j, i: (l, 0, nb + j))],
        out_specs=pl.BlockSpec((tm, tn), lambda j, i: (i, j)),
        out_shape=jax.ShapeDtypeStruct((n_rows, FFN_H), BF16),
        scratch_shapes=[pltpu.VMEM((D, tn), BF16), pltpu.VMEM((D, tn), BF16)],
        compiler_params=_cp("arbitrary", "arbitrary"),
        name="ffn_in",
    )(h, w, w)


def _merge_kernel(ya_ref, yr_ref, ym_ref, ys_ref, ga_ref, gr_ref, gm_ref, gs_ref,
                  wa_ref, wr_ref, wm_ref, ws_ref, o_ref, wab_ref, wrb_ref, wmb_ref, wsb_ref):
    @pl.when(pl.program_id(1) == 0)
    def _():
        wab_ref[...] = wa_ref[...].astype(BF16)
        wrb_ref[...] = wr_ref[...].astype(BF16)
        wmb_ref[...] = wm_ref[...].astype(BF16)
        wsb_ref[...] = ws_ref[...].astype(BF16)

    acc = None
    for y_ref, g_ref, w_ref in ((ya_ref, ga_ref, wab_ref), (yr_ref, gr_ref, wrb_ref),
                                (ym_ref, gm_ref, wmb_ref), (ys_ref, gs_ref, wsb_ref)):
        term = g_ref[...].astype(F32) * jnp.dot(y_ref[...], w_ref[...], preferred_element_type=F32)
        acc = term if acc is None else acc + term
    o_ref[...] = acc.astype(o_ref.dtype)


def _merge(ya, yr, ym, ys, gates, wa, wr, wm, ws, l, tm, tn, n_rows):
    nb = D // tn

    def act(width):
        return pl.BlockSpec((tm, width), lambda j, i: (i, 0))

    def gate(br):
        return pl.BlockSpec((tm, tn), lambda j, i: (i, br * nb + j))

    def wgt(width):
        return pl.BlockSpec((None, width, tn), lambda j, i: (l, 0, j))

    return pl.pallas_call(
        _merge_kernel,
        grid=(nb, n_rows // tm),
        in_specs=[act(ATTN_Q), act(RW_W), act(ML_W), act(S5_W),
                  gate(0), gate(1), gate(2), gate(3),
                  wgt(ATTN_Q), wgt(RW_W), wgt(ML_W), wgt(S5_W)],
        out_specs=pl.BlockSpec((tm, tn), lambda j, i: (i, j)),
        out_shape=jax.ShapeDtypeStruct((n_rows, D), BF16),
        scratch_shapes=[pltpu.VMEM((ATTN_Q, tn), BF16), pltpu.VMEM((RW_W, tn), BF16),
                        pltpu.VMEM((ML_W, tn), BF16), pltpu.VMEM((S5_W, tn), BF16)],
        compiler_params=_cp("arbitrary", "arbitrary"),
        name="gated_merge",
    )(ya, yr, ym, ys, gates, gates, gates, gates, wa, wr, wm, ws)


def _final_norm_kernel(x_ref, g_ref, o_ref):
    x = x_ref[...]
    o_ref[...] = x * lax.rsqrt(jnp.mean(x * x, axis=-1, keepdims=True) + EPS) * g_ref[...]


def _final_norm(x, g):
    n_tiles = N_LAT // TILE
    return pl.pallas_call(
        _final_norm_kernel,
        grid=(n_tiles,),
        in_specs=[pl.BlockSpec((TILE, D), lambda i: (i, 0)),
                  pl.BlockSpec((1, D), lambda i: (0, 0))],
        out_specs=pl.BlockSpec((TILE, D), lambda i: (i, 0)),
        out_shape=jax.ShapeDtypeStruct((N_LAT, D), F32),
        compiler_params=_cp("parallel"),
        name="final_norm",
    )(x, g.reshape(1, D))


def _tile_halo_specs(width, col_blk=0):
    last = ROWS // 8 - 1
    per = TILE // 8
    return [pl.BlockSpec((TILE, width), lambda t: (t, col_blk)),
            pl.BlockSpec((8, width), lambda t: (jnp.maximum(t * per - 1, 0), col_blk)),
            pl.BlockSpec((8, width), lambda t: (jnp.minimum((t + 1) * per, last), col_blk))]


def _neighbours(z, prev_blk, next_blk):
    t = pl.program_id(0)
    pos = t % (SEQ // TILE)
    is_lat = t < N_LAT // TILE
    has_prev = jnp.logical_and(is_lat, pos > 0).astype(F32)
    has_next = jnp.logical_and(is_lat, pos < SEQ // TILE - 1).astype(F32)
    row = lax.broadcasted_iota(jnp.int32, z.shape, 0)
    zp = jnp.where(row == 0, prev_blk[7:8, :] * has_prev, pltpu.roll(z, 1, 0))
    zn = jnp.where(row == TILE - 1, next_blk[0:1, :] * has_next, pltpu.roll(z, TILE - 1, 0))
    return zp, zn


def _seq_row_block(b, d, n):
    ctx_c = jnp.where(d == 0, n, CTX_CHUNKS - 1 - n)
    lat_c = jnp.where(d == 0, n - CTX_CHUNKS, SEQ_CHUNKS - 1 - n)
    return jnp.where(n < CTX_CHUNKS, N_LAT // CHUNK + CTX_CHUNKS * b + ctx_c, LAT_CHUNKS * b + lat_c)


def _rope(x, cos, sin):
    lane = lax.broadcasted_iota(jnp.int32, x.shape, 1)
    first = (lane % 64) < 32
    partner = jnp.where(first, pltpu.roll(x, 96, 1), pltpu.roll(x, 32, 1))
    return x * cos + partner * sin


def _rms(x, g):
    return x * lax.rsqrt(jnp.mean(x * x, axis=-1, keepdims=True) + EPS) * g


def _attn_kernel(q_ref, kl_ref, kc_ref, vl_ref, vc_ref, cos_ref, sin_ref, cos_t_ref, sin_t_ref,
                 qg_ref, kg_ref, o_ref, klb_ref, kcb_ref, vlb_ref, vcb_ref):
    qi = pl.program_id(2)
    n_lat_tiles = SEQ // TILE

    @pl.when(qi == 0)
    def _():
        kg = kg_ref[...]
        klb_ref[...] = _rope(_rms(kl_ref[...], kg), cos_ref[...], sin_ref[...]).astype(BF16)
        kcb_ref[...] = _rms(kc_ref[...], kg).astype(BF16)
        vlb_ref[...] = vl_ref[...].astype(BF16)
        vcb_ref[...] = vc_ref[...].astype(BF16)

    scale = HEAD_DIM ** -0.5 * math.log2(math.e)
    nt = (((1,), (1,)), ((), ()))

    def heads(latent):
        sls = [slice(h * HEAD_DIM, (h + 1) * HEAD_DIM) for h in range(ATTN_HEADS // ATTN_KV)]
        q = [_rms(q_ref[:, sl], qg_ref[...]) for sl in sls]
        if latent:
            q = [_rope(x, cos_t_ref[...], sin_t_ref[...]) for x in q]
        q = [(x * scale).astype(BF16) for x in q]
        s_c = [lax.dot_general(x, kcb_ref[...], nt, preferred_element_type=F32) for x in q]
        m = [jnp.max(x, axis=-1, keepdims=True) for x in s_c]
        if latent:
            s_l = [lax.dot_general(x, klb_ref[...], nt, preferred_element_type=F32) for x in q]
            m = [jnp.maximum(a, jnp.max(x, axis=-1, keepdims=True)) for a, x in zip(m, s_l)]
        p_c = [jnp.exp2(x - a) for x, a in zip(s_c, m)]
        den = [jnp.sum(x, axis=-1, keepdims=True) for x in p_c]
        acc = [jnp.dot(x.astype(BF16), vcb_ref[...], preferred_element_type=F32) for x in p_c]
        if latent:
            p_l = [jnp.exp2(x - a) for x, a in zip(s_l, m)]
            den = [a + jnp.sum(x, axis=-1, keepdims=True) for a, x in zip(den, p_l)]
            acc = [a + jnp.dot(x.astype(BF16), vlb_ref[...], preferred_element_type=F32) for a, x in zip(acc, p_l)]
        for sl, a, dn in zip(sls, acc, den):
            o_ref[:, sl] = (a / dn).astype(o_ref.dtype)

    @pl.when(qi < n_lat_tiles)
    def _():
        heads(True)

    @pl.when(qi == n_lat_tiles)
    def _():
        heads(False)


def _rope_tables():
    rows = SEQ // GRID_W
    row = jnp.repeat(jnp.arange(rows, dtype=F32), GRID_W)
    col = jnp.tile(jnp.arange(GRID_W, dtype=F32), rows)
    axis_dim = HEAD_DIM // 2
    inv_freq = ROPE_THETA ** (-jnp.arange(0, axis_dim, 2, dtype=F32) / axis_dim)
    ang_r = row[:, None] * inv_freq[None]
    ang_c = col[:, None] * inv_freq[None]
    cos = jnp.concatenate([jnp.cos(ang_r), jnp.cos(ang_r), jnp.cos(ang_c), jnp.cos(ang_c)], axis=-1)
    sin = jnp.concatenate([-jnp.sin(ang_r), jnp.sin(ang_r), -jnp.sin(ang_c), jnp.sin(ang_c)], axis=-1)
    return cos, sin


def _attention(z, q_g, k_g, cos, sin):
    n_lat_tiles = SEQ // TILE
    qw = ATTN_Q // ATTN_KV
    kcol = ATTN_Q // HEAD_DIM
    vcol = kcol + ATTN_KV

    def q_row(b, g, qi):
        return jnp.where(qi < n_lat_tiles, n_lat_tiles * b + qi, N_LAT // TILE + b)

    return pl.pallas_call(
        _attn_kernel,
        grid=(NB, ATTN_KV, n_lat_tiles + 1),
        in_specs=[pl.BlockSpec((TILE, qw), lambda b, g, qi: (q_row(b, g, qi), g)),
                  pl.BlockSpec((SEQ, HEAD_DIM), lambda b, g, qi: (b, kcol + g)),
                  pl.BlockSpec((CTX, HEAD_DIM), lambda b, g, qi: (N_LAT // CTX + b, kcol + g)),
                  pl.BlockSpec((SEQ, HEAD_DIM), lambda b, g, qi: (b, vcol + g)),
                  pl.BlockSpec((CTX, HEAD_DIM), lambda b, g, qi: (N_LAT // CTX + b, vcol + g)),
                  pl.BlockSpec((SEQ, HEAD_DIM), lambda b, g, qi: (0, 0)),
                  pl.BlockSpec((SEQ, HEAD_DIM), lambda b, g, qi: (0, 0)),
                  pl.BlockSpec((TILE, HEAD_DIM), lambda b, g, qi: (jnp.minimum(qi, n_lat_tiles - 1), 0)),
                  pl.BlockSpec((TILE, HEAD_DIM), lambda b, g, qi: (jnp.minimum(qi, n_lat_tiles - 1), 0)),
                  pl.BlockSpec((1, HEAD_DIM), lambda b, g, qi: (0, 0)),
                  pl.BlockSpec((1, HEAD_DIM), lambda b, g, qi: (0, 0))],
        out_specs=pl.BlockSpec((TILE, qw), lambda b, g, qi: (q_row(b, g, qi), g)),
        out_shape=jax.ShapeDtypeStruct((ROWS, ATTN_Q), BF16),
        scratch_shapes=[pltpu.VMEM((SEQ, HEAD_DIM), BF16), pltpu.VMEM((CTX, HEAD_DIM), BF16),
                        pltpu.VMEM((SEQ, HEAD_DIM), BF16), pltpu.VMEM((CTX, HEAD_DIM), BF16)],
        compiler_params=_cp("arbitrary", "arbitrary", "arbitrary"),
        name="attention",
    )(z, z, z, z, z, cos, sin, cos, sin, q_g.reshape(1, HEAD_DIM), k_g.reshape(1, HEAD_DIM))


def _rwkv_prep_kernel(z_ref, zp_ref, zn_ref, mu_ref, w0_ref, wup_ref, a0_ref, aup_ref, gup_ref,
                      kk_ref, ka_ref, rk_ref, bd_ref,
                      r_out, v_out, kkn_out, g_out, bonus_out, lw_out, b_out, km_out):
    z = z_ref[...]
    zp, zn = _neighbours(z, zp_ref[...], zn_ref[...])
    zs = z + mu_ref[...] * (0.5 * (zp + zn) - z)
    r = zs[:, 0:RW_W]
    k = zs[:, RW_W:2 * RW_W]
    v = zs[:, 2 * RW_W:3 * RW_W]
    w_lo = zs[:, 3 * RW_W:3 * RW_W + 64]
    a_lo = zs[:, 3 * RW_W + 64:3 * RW_W + 128]
    g_lo = zs[:, 3 * RW_W + 128:3 * RW_W + 256]
    bd = bd_ref[...]
    kk = k * kk_ref[...]
    kk = kk * lax.rsqrt(_dot_rhs_exact(kk * kk, bd) + 1e-12)
    r_out[...] = r
    v_out[...] = v
    kkn_out[...] = kk
    g_out[...] = _bdot(_sigmoid(g_lo), gup_ref[...])
    tw = jnp.tanh(w_lo)
    km_sum = None
    for d in range(2):
        lw = -RW_DECAY * _sigmoid(w0_ref[d] + _bdot(tw, wup_ref[d]))
        a = _sigmoid(a0_ref[d] + _bdot(a_lo, aup_ref[d]))
        km = k * (1.0 + (a - 1.0) * ka_ref[...])
        lw_out[d] = lw
        b_out[d] = a * kk
        km_out[d] = km
        km_sum = km if km_sum is None else km_sum + km
    bonus_out[...] = _dot_rhs_exact(r * km_sum * rk_ref[...], bd) * v


def _rwkv_prep(z, p):
    row = lambda a: a.reshape(1, -1)
    full = lambda shape: pl.BlockSpec(shape, lambda t: (0,) * len(shape))
    out_tok = pl.BlockSpec((TILE, RW_W), lambda t: (t, 0))
    out_dir = pl.BlockSpec((2, TILE, RW_W), lambda t: (0, t, 0))
    tok = jax.ShapeDtypeStruct((ROWS, RW_W), F32)
    drn = jax.ShapeDtypeStruct((2, ROWS, RW_W), F32)
    return pl.pallas_call(
        _rwkv_prep_kernel,
        grid=(N_TILES,),
        in_specs=_tile_halo_specs(RW_IN) + [
            full((1, RW_IN)), full((2, 1, RW_W)), full((2, 64, RW_W)), full((2, 1, RW_W)),
            full((2, 64, RW_W)), full((128, RW_W)), full((1, RW_W)), full((1, RW_W)), full((1, RW_W)),
            full((RW_W, RW_W))],
        out_specs=[out_tok] * 5 + [out_dir] * 3,
        out_shape=[tok] * 5 + [drn] * 3,
        compiler_params=_cp("parallel"),
        name="rwkv_prep",
    )(z, z, z, row(p["mu"]), p["w0"].reshape(2, 1, RW_W), p["w_up"], p["a0"].reshape(2, 1, RW_W),
      p["a_up"], p["g_up"], row(p["k_k"]), row(p["k_a"]), row(p["r_k"]),
      _head_block_ones(RW_W, RW_D).astype(BF16))


def _head_block_ones(width, head):
    idx = np.arange(width) // head
    return jnp.asarray((idx[:, None] == idx[None, :]).astype(np.float32))


def _split_bf16(x, pieces):
    out = []
    for _ in range(pieces):
        p = x.astype(BF16)
        out.append(p)
        x = x - p.astype(F32)
    return out


def _dot_rhs_exact(x, m, pieces=2):
    return sum(jnp.dot(p, m, preferred_element_type=F32) for p in _split_bf16(x, pieces))


def _dot_lhs_exact(m, x, pieces=3):
    return sum(jnp.dot(m, p, preferred_element_type=F32) for p in _split_bf16(x, pieces))


RW_SUB = 2


def _rwkv_chunk_kernel(r_ref, v_ref, kk_ref, lw_ref, b_ref, km_ref, p_ref, sl_ref, re_ref, ol_ref):
    d = pl.program_id(0)
    c = CHUNK
    hd = RW_D
    ti = lax.broadcasted_iota(jnp.int32, (c, c), 0)
    si = lax.broadcasted_iota(jnp.int32, (c, c), 1)
    delta = (ti - si) * (1 - 2 * d)
    incl = delta >= 0
    strict = delta > 0
    eye = jnp.where(ti == si, 1.0, 0.0)
    tri = jnp.where(incl, 1.0, 0.0).astype(BF16)
    gr = lax.broadcasted_iota(jnp.int32, (2 * c, 2 * c), 0)
    gc = lax.broadcasted_iota(jnp.int32, (2 * c, 2 * c), 1)
    gdelta = (gr % c - gc % c) * (1 - 2 * d)
    gmask = gdelta >= jnp.where(gr < c, 1, 0)

    items = []
    for sub in range(RW_SUB):
        rows = slice(sub * c, (sub + 1) * c)
        lw = lw_ref[0, rows, :]
        cs = _dot_lhs_exact(tri, lw)
        tot = jnp.sum(lw, axis=0, keepdims=True)
        r = r_ref[rows, :]
        v = v_ref[rows, :]
        kk = kk_ref[rows, :]
        bb = b_ref[0, rows, :]
        km = km_ref[0, rows, :]
        e_neg = jnp.exp(-cs)
        e_rem = jnp.exp(tot - cs)
        kkt = kk * jnp.exp(cs - lw)
        rt = r * jnp.exp(cs)
        bt = bb * e_neg
        kt = km * e_neg
        bh = bb * e_rem
        kh = km * e_rem
        e_tot = jnp.exp(tot)
        for h in range(RW_H):
            s = slice(h * hd, (h + 1) * hd)
            items.append(dict(sub=sub, s=s, kkt=kkt[:, s], rt=rt[:, s], bt=bt[:, s], kt=kt[:, s], v=v[:, s],
                              bh=bh[:, s], kh=kh[:, s], e_tot=e_tot[:, s]))

    g = [jnp.where(gmask, _bdot_nt(jnp.concatenate([it["kkt"], it["rt"]], axis=0),
                                   jnp.concatenate([it["bt"], it["kt"]], axis=0)), 0.0) for it in items]
    l_b = [x[:c, :c] for x in g]
    a_b = [x[c:, :c] for x in g]
    lkv_akv = [_bdot(x[:, c:], it["v"]) for x, it in zip(g, items)]
    pw = [_bdot(x, x) for x in l_b]
    inv = [eye - x for x in l_b]
    for _ in range(int(math.log2(c)) - 2):
        res = [_bdot(p, jnp.concatenate([p, i], axis=1)) for p, i in zip(pw, inv)]
        pw = [x[:, :c] for x in res]
        inv = [i + x[:, c:] for i, x in zip(inv, res)]
    inv = [i + _bdot(p, i) for p, i in zip(pw, inv)]
    wy = [_bdot(i, jnp.concatenate([it["kkt"], x[:c]], axis=1)) for i, it, x in zip(inv, items, lkv_akv)]
    ab_wy = [_bdot(a, x) for a, x in zip(a_b, wy)]
    cross = [_bdot_tn(jnp.concatenate([it["v"], x], axis=1), jnp.concatenate([it["kh"], it["bh"]], axis=1))
             for it, x in zip(items, wy)]
    for it, x_ab, x_lk, x_cr in zip(items, ab_wy, lkv_akv, cross):
        s = it["s"]
        j = jnp.where(d == 0, it["sub"], RW_SUB - 1 - it["sub"])
        re_ref[0, 0, j, :, s] = it["rt"] - x_ab[:, :hd]
        ol_ref[0, 0, j, :, s] = x_lk[c:] - x_ab[:, hd:]
        sl_ref[0, 0, j, :, s] = x_cr[:hd, :hd] - x_cr[2 * hd:, hd:]
        p_ref[0, 0, j, :, s] = eye * it["e_tot"] - x_cr[hd:2 * hd, hd:]


def _rwkv_carry_kernel(p_ref, sl_ref, s_out_ref, st_ref):
    @pl.when(pl.program_id(0) == 0)
    def _():
        st_ref[...] = jnp.zeros_like(st_ref)

    for d in range(2):
        for b in range(NB):
            s = st_ref[d, b]
            s_out_ref[d, b, 0] = s
            for h in range(RW_H):
                sl = slice(h * RW_D, (h + 1) * RW_D)
                st_ref[d, b, :, sl] = _hdot(s[:, sl], p_ref[d, b, 0, :, sl]) + sl_ref[d, b, 0, :, sl]


TILE_CHUNKS = TILE // CHUNK


def _tile_of_group(b, grp):
    return jnp.where(grp == 0, N_LAT // TILE + b, (SEQ // TILE) * b + grp - 1)


def _bwd_group(grp):
    return jnp.where(grp == 0, 0, SEQ_CHUNKS // TILE_CHUNKS - grp)


def _rwkv_out_kernel(olf_ref, ref_ref, sf_ref, olb_ref, reb_ref, sb_ref, bonus_ref, g_ref, gng_ref, gnb_ref,
                     bd_ref, y_ref, o_scr):
    n = TILE_CHUNKS
    items = [(i, slice(h * RW_D, (h + 1) * RW_D)) for i in range(n) for h in range(RW_H)]
    pf = [_bdot_nt(ref_ref[0, 0, i, :, s], sf_ref[0, 0, i, :, s]) for i, s in items]
    pb = [_bdot_nt(reb_ref[0, 0, n - 1 - i, :, s], sb_ref[0, 0, n - 1 - i, :, s]) for i, s in items]
    for (i, s), a, b in zip(items, pf, pb):
        o_scr[i * CHUNK:(i + 1) * CHUNK, s] = olf_ref[0, 0, i, :, s] + olb_ref[0, 0, n - 1 - i, :, s] + a + b
    o = o_scr[...] + bonus_ref[...]
    bd = bd_ref[...]
    cen = o - _dot_rhs_exact(o, bd) * (1.0 / RW_D)
    var = _dot_rhs_exact(cen * cen, bd) * (1.0 / RW_D)
    y = cen * lax.rsqrt(var + RW_GN_EPS) * gng_ref[...] + gnb_ref[...]
    y_ref[...] = (y * g_ref[...]).astype(y_ref.dtype)


def _rwkv_scan(r, v, kk, lw, bmat, km, bonus, g, gn_g, gn_b):
    sub_rows = RW_SUB * CHUNK
    lat_blocks = N_LAT // sub_rows
    lat_per_b = SEQ // sub_rows
    ctx_per_b = CTX // sub_rows

    def step_block(d, rb):
        is_lat = rb < lat_blocks
        b = jnp.where(is_lat, rb // lat_per_b, (rb - lat_blocks) // ctx_per_b)
        i = jnp.where(is_lat, rb % lat_per_b, (rb - lat_blocks) % ctx_per_b)
        fwd = jnp.where(is_lat, ctx_per_b + i, i)
        bwd = jnp.where(is_lat, ctx_per_b + lat_per_b - 1 - i, ctx_per_b - 1 - i)
        return b, jnp.where(d == 0, fwd, bwd)

    tok = pl.BlockSpec((sub_rows, RW_W), lambda d, rb: (rb, 0))
    drn = pl.BlockSpec((1, sub_rows, RW_W), lambda d, rb: (d, rb, 0))
    step_shape = jax.ShapeDtypeStruct((2, NB, SEQ_CHUNKS, CHUNK, RW_W), F32)
    step_blk = pl.BlockSpec((1, 1, RW_SUB, CHUNK, RW_W), lambda d, rb: (d,) + step_block(d, rb) + (0, 0))
    p, s_loc, r_eff, o_loc = pl.pallas_call(
        _rwkv_chunk_kernel,
        grid=(2, ROWS // sub_rows),
        in_specs=[tok, tok, tok, drn, drn, drn],
        out_specs=[step_blk] * 4,
        out_shape=[step_shape] * 4,
        compiler_params=_cp("parallel", "parallel"),
        name="rwkv_chunk",
    )(r, v, kk, lw, bmat, km)

    all_blk = pl.BlockSpec((2, NB, 1, CHUNK, RW_W), lambda n: (0, 0, n, 0, 0))
    s_in = pl.pallas_call(
        _rwkv_carry_kernel,
        grid=(SEQ_CHUNKS,),
        in_specs=[all_blk, all_blk],
        out_specs=all_blk,
        out_shape=step_shape,
        scratch_shapes=[pltpu.VMEM((2, NB, CHUNK, RW_W), F32)],
        compiler_params=_cp("arbitrary"),
        name="rwkv_carry",
    )(p, s_loc)

    fwd = pl.BlockSpec((1, 1, TILE_CHUNKS, CHUNK, RW_W), lambda b, grp: (0, b, grp, 0, 0))
    bwd = pl.BlockSpec((1, 1, TILE_CHUNKS, CHUNK, RW_W), lambda b, grp: (1, b, _bwd_group(grp), 0, 0))
    rows = pl.BlockSpec((TILE, RW_W), lambda b, grp: (_tile_of_group(b, grp), 0))
    vec = pl.BlockSpec((1, RW_W), lambda b, grp: (0, 0))
    return pl.pallas_call(
        _rwkv_out_kernel,
        grid=(NB, SEQ_CHUNKS // TILE_CHUNKS),
        in_specs=[fwd, fwd, fwd, bwd, bwd, bwd, rows, rows, vec, vec,
                  pl.BlockSpec((RW_W, RW_W), lambda b, grp: (0, 0))],
        out_specs=rows,
        out_shape=jax.ShapeDtypeStruct((ROWS, RW_W), BF16),
        scratch_shapes=[pltpu.VMEM((TILE, RW_W), F32)],
        compiler_params=_cp("parallel", "parallel"),
        name="rwkv_out",
    )(o_loc, r_eff, s_in, o_loc, r_eff, s_in, bonus, g, gn_g.reshape(1, RW_W), gn_b.reshape(1, RW_W),
      _head_block_ones(RW_W, RW_D).astype(BF16))


def _mlstm_prep_kernel(z_ref, zp_ref, zn_ref, w_ref, o_ref):
    z = z_ref[...]
    zp, zn = _neighbours(z, zp_ref[...], zn_ref[...])
    y = _silu(zp * w_ref[0:1, :] + z * w_ref[1:2, :] + zn * w_ref[2:3, :])
    col = lax.broadcasted_iota(jnp.int32, y.shape, 1)
    o_ref[...] = jnp.where(col >= ML_W, y * (ML_D ** -0.5), y)


def _mlstm_prep(z, conv_w):
    return pl.pallas_call(
        _mlstm_prep_kernel,
        grid=(N_TILES,),
        in_specs=_tile_halo_specs(2 * ML_W) + [pl.BlockSpec((3, 2 * ML_W), lambda t: (0, 0))],
        out_specs=pl.BlockSpec((TILE, 2 * ML_W), lambda t: (t, 0)),
        out_shape=jax.ShapeDtypeStruct((ROWS, 2 * ML_W), F32),
        compiler_params=_cp("parallel"),
        name="mlstm_prep",
    )(z, z, z, conv_w)


def _log_sigmoid(x):
    return jnp.minimum(x, 0.0) - jnp.log(1.0 + jnp.exp(-jnp.abs(x)))


N_CHAINS = 2 * NB


def _mlstm_scan_kernel(*refs):
    nc = N_CHAINS
    q_refs, k_refs, v_refs = refs[0:nc], refs[nc:2 * nc], refs[2 * nc:3 * nc]
    gc_refs, gr_refs = refs[3 * nc:4 * nc], refs[4 * nc:5 * nc]
    bc_ref, br_ref, o_ref, c_ref, n_ref, m_ref = refs[5 * nc:]

    @pl.when(pl.program_id(0) == 0)
    def _():
        c_ref[...] = jnp.zeros_like(c_ref)
        n_ref[...] = jnp.zeros_like(n_ref)
        m_ref[...] = jnp.zeros_like(m_ref)

    c = CHUNK
    ti = lax.broadcasted_iota(jnp.int32, (c, c), 0)
    si = lax.broadcasted_iota(jnp.int32, (c, c), 1)
    masks = (ti >= si, ti <= si)
    items = [(ci, h) for ci in range(nc) for h in range(ML_H)]
    sls = [slice(h * ML_D, (h + 1) * ML_D) for h in range(ML_H)]
    gcol = [gc_refs[ci][0, 0] + bc_ref[ci // NB] for ci in range(nc)]
    grow = [gr_refs[ci][0, 0] + br_ref[ci // NB] for ci in range(nc)]

    q = [q_refs[ci][:, sls[h]] for ci, h in items]
    k = [k_refs[ci][:, sls[h]] for ci, h in items]
    v = [v_refs[ci][:, sls[h]] for ci, h in items]
    qk = [_bdot_nt(a, b) for a, b in zip(q, k)]
    c_mat = [c_ref[ci, h] for ci, h in items]
    n_vec = [n_ref[ci, h] for ci, h in items]
    m_prev = [m_ref[ci, h][0:1, 0:1] for ci, h in items]
    qc = [_bdot_nt(a, b) for a, b in zip(q, c_mat)]
    qn = [jnp.sum(a * b, axis=1, keepdims=True) for a, b in zip(q, n_vec)]

    log_w, m_inter, cum_col, i_col, total = [], [], [], [], []
    for ci, h in items:
        mask = masks[ci // NB]
        mask_t = masks[1 - ci // NB]
        f_col = _log_sigmoid(gcol[ci][:, ML_H + h:ML_H + h + 1])
        f_row = _log_sigmoid(grow[ci][ML_H + h:ML_H + h + 1, :])
        cc = jnp.sum(jnp.where(mask, f_row, 0.0), axis=1, keepdims=True)
        cr = jnp.sum(jnp.where(mask_t, f_col, 0.0), axis=0, keepdims=True)
        log_w.append(jnp.where(mask, cc - cr + grow[ci][h:h + 1, :], ML_NEG))
        cum_col.append(cc)
        i_col.append(gcol[ci][:, h:h + 1])
        total.append(jnp.sum(f_row, axis=1, keepdims=True))
    m_inter = [a + b for a, b in zip(cum_col, m_prev)]
    m_t = [jnp.maximum(jnp.max(a, axis=1, keepdims=True), b) for a, b in zip(log_w, m_inter)]
    s = [a * jnp.exp(b - m) for a, b, m in zip(qk, log_w, m_t)]
    w_inter = [jnp.exp(a - m) for a, m in zip(m_inter, m_t)]
    sv = [_bdot(a, b) for a, b in zip(s, v)]
    for i, (ci, h) in enumerate(items):
        num = sv[i] + w_inter[i] * qc[i]
        den = jnp.sum(s[i], axis=1, keepdims=True) + w_inter[i] * qn[i]
        o_ref[ci // NB, ci % NB, 0, :, sls[h]] = num / jnp.maximum(jnp.abs(den), jnp.exp(-m_t[i]))
    log_src = [t - a + b for t, a, b in zip(total, cum_col, i_col)]
    m_new = [jnp.maximum(t + mp, jnp.max(ls, axis=0, keepdims=True)) for t, mp, ls in zip(total, m_prev, log_src)]
    src = [jnp.exp(ls - mn) for ls, mn in zip(log_src, m_new)]
    decay = [jnp.exp(t + mp - mn) for t, mp, mn in zip(total, m_prev, m_new)]
    vk = [_bdot_tn(a * sr, b) for a, sr, b in zip(v, src, k)]
    for i, (ci, h) in enumerate(items):
        c_ref[ci, h] = decay[i] * c_mat[i] + vk[i]
        n_ref[ci, h] = decay[i] * n_vec[i] + jnp.sum(src[i] * k[i], axis=0, keepdims=True)
        m_ref[ci, h] = jnp.broadcast_to(m_new[i], m_ref.shape[2:])


def _bwd_step(pos):
    return jnp.where(pos < CTX_CHUNKS, CTX_CHUNKS - 1 - pos, SEQ_CHUNKS - 1 + CTX_CHUNKS - pos)


def _mlstm_scan(qk, z, gcol, grow, bcol, brow):
    chains = [(d, b) for d in range(2) for b in range(NB)]

    def tok(col_blk):
        return [pl.BlockSpec((CHUNK, ML_W), lambda n, d=d, b=b: (_seq_row_block(b, d, n), col_blk))
                for d, b in chains]

    gc_specs = [pl.BlockSpec((1, 1, CHUNK, 2 * ML_H), lambda n, d=d, b=b: (d, _seq_row_block(b, d, n), 0, 0))
                for d, b in chains]
    gr_specs = [pl.BlockSpec((1, 1, 2 * ML_H, CHUNK), lambda n, d=d, b=b: (d, _seq_row_block(b, d, n), 0, 0))
                for d, b in chains]
    nc = N_CHAINS
    return pl.pallas_call(
        _mlstm_scan_kernel,
        grid=(SEQ_CHUNKS,),
        in_specs=tok(0) + tok(1) + tok(2) + gc_specs + gr_specs + [
            pl.BlockSpec((2, 1, 2 * ML_H), lambda n: (0, 0, 0)),
            pl.BlockSpec((2, 2 * ML_H, 1), lambda n: (0, 0, 0))],
        out_specs=pl.BlockSpec((2, NB, 1, CHUNK, ML_W), lambda n: (0, 0, n, 0, 0)),
        out_shape=jax.ShapeDtypeStruct((2, NB, SEQ_CHUNKS, CHUNK, ML_W), F32),
        scratch_shapes=[pltpu.VMEM((nc, ML_H, ML_D, ML_D), F32), pltpu.VMEM((nc, ML_H, 1, ML_D), F32),
                        pltpu.VMEM((nc, ML_H, 8, 128), F32)],
        compiler_params=_cp("arbitrary"),
        name="mlstm_scan",
    )(*([qk] * (2 * nc) + [z] * nc + [gcol] * nc + [grow] * nc + [bcol, brow]))


def _mlstm_out_kernel(hf_ref, hb_ref, og_ref, gng_ref, y_ref):
    n = TILE_CHUNKS
    for i in range(n):
        rows = slice(i * CHUNK, (i + 1) * CHUNK)
        hsum = _sigmoid(og_ref[rows, :]) * (hf_ref[0, 0, i] + hb_ref[0, 0, n - 1 - i])
        for h in range(ML_H):
            sl = slice(h * ML_D, (h + 1) * ML_D)
            x = hsum[:, sl]
            cen = x - jnp.mean(x, axis=-1, keepdims=True)
            var = jnp.mean(cen * cen, axis=-1, keepdims=True)
            y_ref[rows, sl] = (cen * lax.rsqrt(var + EPS) * gng_ref[:, sl]).astype(y_ref.dtype)


def _mlstm_out(hs, z, gn_g):
    blk = (1, 1, TILE_CHUNKS, CHUNK, ML_W)
    return pl.pallas_call(
        _mlstm_out_kernel,
        grid=(NB, SEQ_CHUNKS // TILE_CHUNKS),
        in_specs=[pl.BlockSpec(blk, lambda b, grp: (0, b, grp, 0, 0)),
                  pl.BlockSpec(blk, lambda b, grp: (1, b, _bwd_group(grp), 0, 0)),
                  pl.BlockSpec((TILE, ML_W), lambda b, grp: (_tile_of_group(b, grp), 3)),
                  pl.BlockSpec((1, ML_W), lambda b, grp: (0, 0))],
        out_specs=pl.BlockSpec((TILE, ML_W), lambda b, grp: (_tile_of_group(b, grp), 0)),
        out_shape=jax.ShapeDtypeStruct((ROWS, ML_W), BF16),
        compiler_params=_cp("parallel", "parallel"),
        name="mlstm_out",
    )(hs, hs, z, gn_g.reshape(1, ML_W))


def _mlstm(z_main, z_gates, conv_w, i_b, f_b, gn_g):
    qk = _mlstm_prep(z_main, conv_w)
    n_chunks = ROWS // CHUNK
    gates = z_gates[:, :4 * ML_H].reshape(n_chunks, CHUNK, 2, 2, ML_H)
    gcol = jnp.transpose(gates, (3, 0, 1, 2, 4)).reshape(2, n_chunks, CHUNK, 2 * ML_H)
    grow = jnp.swapaxes(gcol, 2, 3)
    bias = jnp.concatenate([i_b, f_b], axis=-1)
    hs = _mlstm_scan(qk, z_main, gcol, grow, bias.reshape(2, 1, 2 * ML_H), bias.reshape(2, 2 * ML_H, 1))
    return _mlstm_out(hs, z_main, gn_g)


def _s5_toeplitz_kernel(kf_ref, kb_ref, o_ref):
    kf = kf_ref[0]
    kb = kb_ref[0]
    lane = lax.broadcasted_iota(jnp.int32, kf.shape, 1)
    width = S5_L * S5_C
    for j in range(S5_L):
        f = kf if j == 0 else jnp.where(lane >= S5_C * j, pltpu.roll(kf, S5_C * j, 1), 0.0)
        back = S5_L - 1 - j
        b = kb if back == 0 else jnp.where(lane < S5_C * (j + 1), pltpu.roll(kb, width - S5_C * back, 1), 0.0)
        o_ref[0, 0, :, j] = f.reshape(S5_G, S5_C, width).astype(o_ref.dtype)
        o_ref[0, 1, :, j] = b.reshape(S5_G, S5_C, width).astype(o_ref.dtype)


def _s5_operators(lam_re, lam_im, log_step, b_re, b_im, c_re, c_im):
    nl = lam_re.shape[0]
    dt = jnp.exp(log_step)[..., None]
    mag = jnp.exp(lam_re * dt)
    a_re = mag * jnp.cos(lam_im * dt)
    a_im = mag * jnp.sin(lam_im * dt)
    den = lam_re * lam_re + lam_im * lam_im
    f_re = ((a_re - 1) * lam_re + a_im * lam_im) / den
    f_im = (a_im * lam_re - (a_re - 1) * lam_im) / den
    bb_re = f_re[..., None] * b_re - f_im[..., None] * b_im
    bb_im = f_re[..., None] * b_im + f_im[..., None] * b_re
    bt_re = jnp.swapaxes(bb_re, -1, -2)
    bt_im = jnp.swapaxes(bb_im, -1, -2)
    pr = [jnp.ones_like(a_re)]
    pi = [jnp.zeros_like(a_im)]
    for _ in range(S5_L):
        pr.append(pr[-1] * a_re - pi[-1] * a_im)
        pi.append(pr[-2] * a_im + pi[-1] * a_re)
    pr = jnp.stack(pr, axis=3)
    pi = jnp.stack(pi, axis=3)

    def times_b(qr, qi):
        qr, qi = qr[..., None, :], qi[..., None, :]
        br, bi = bt_re[:, :, :, None], bt_im[:, :, :, None]
        return qr * br - qi * bi, qr * bi + qi * br

    wr, wi = times_b(pr[:, :, :, :S5_L], pi[:, :, :, :S5_L])
    wr = jnp.swapaxes(wr, 3, 4)
    wi = jnp.swapaxes(wi, 3, 4)
    kern = (jnp.einsum("ldgktp,ldgcp->ldgktc", wr, c_re, precision=HP)
            - jnp.einsum("ldgktp,ldgcp->ldgktc", wi, c_im, precision=HP))
    kf = kern[:, 0].reshape(nl, S5_G * S5_C, S5_L * S5_C)
    kb = jnp.flip(kern[:, 1], axis=3).reshape(nl, S5_G * S5_C, S5_L * S5_C)
    rows_blk = pl.BlockSpec((1, S5_G * S5_C, S5_L * S5_C), lambda l: (l, 0, 0))
    m = pl.pallas_call(
        _s5_toeplitz_kernel,
        grid=(nl,),
        in_specs=[rows_blk, rows_blk],
        out_specs=pl.BlockSpec((1, 2, S5_G, S5_L, S5_C, S5_L * S5_C), lambda l: (l, 0, 0, 0, 0, 0)),
        out_shape=jax.ShapeDtypeStruct((nl, 2, S5_G, S5_L, S5_C, S5_L * S5_C), BF16),
        compiler_params=_cp("parallel"),
        name="s5_toeplitz",
    )(kf, kb).reshape(nl, 2, S5_G, S5_L * S5_C, S5_L * S5_C)

    def stack_dirs(fwd, bwd):
        return jnp.stack([fwd[:, 0], bwd[:, 1]], axis=1)

    er, ei = times_b(stack_dirs(jnp.flip(pr[:, :, :, :S5_L], axis=3), pr[:, :, :, :S5_L]),
                     stack_dirs(jnp.flip(pi[:, :, :, :S5_L], axis=3), pi[:, :, :, :S5_L]))
    e = jnp.concatenate([er, ei], axis=-1).reshape(nl, 2, S5_G, S5_L * S5_C, 2 * S5_P).astype(BF16)
    qr = stack_dirs(pr[:, :, :, 1:], jnp.flip(pr[:, :, :, 1:], axis=3))[..., None, :]
    qi = stack_dirs(pi[:, :, :, 1:], jnp.flip(pi[:, :, :, 1:], axis=3))[..., None, :]
    cr, ci = c_re[:, :, :, None], c_im[:, :, :, None]
    ft = jnp.concatenate([cr * qr - ci * qi, -(cr * qi + ci * qr)], axis=-1)
    ft = ft.reshape(nl, 2, S5_G, S5_L * S5_C, 2 * S5_P).astype(BF16)
    return m, e, ft, pr[:, :, :, S5_L], pi[:, :, :, S5_L]


def _s5_local_kernel(u_ref, m_ref, e_ref, y_ref, x_ref):
    u = u_ref[0]
    for d in range(2):
        y_ref[d, 0] = jnp.dot(u, m_ref[d, 0], preferred_element_type=F32)
        x_ref[d, 0] = jnp.dot(u, e_ref[d, 0], preferred_element_type=F32)


def _s5_carry_kernel(xc_ref, ar_ref, ai_ref, x0_ref, st_ref):
    @pl.when(pl.program_id(1) == 0)
    def _():
        st_ref[...] = jnp.zeros_like(st_ref)

    ar = ar_ref[0]
    ai = ai_ref[0]

    def run(order):
        x = st_ref[...]
        for i in order:
            x0_ref[0, :, :, i, :] = x
            x = x * ar + pltpu.roll(x, S5_P, 2) * ai + xc_ref[0, :, :, i, :]
        st_ref[...] = x

    n = xc_ref.shape[3]

    @pl.when(pl.program_id(0) == 0)
    def _():
        run(range(n))

    @pl.when(pl.program_id(0) == 1)
    def _():
        run(reversed(range(n)))


def _s5_state_kernel(y_ref, x0_ref, ft_ref, o_ref):
    acc = y_ref[0, 0] + y_ref[1, 0]
    for d in range(2):
        acc = acc + _bdot_nt(x0_ref[d, 0], ft_ref[d, 0])
    o_ref[0] = acc.astype(o_ref.dtype)


def _s5_glu_kernel(y_ref, u_ref, d_ref, w_ref, o_ref):
    x = y_ref[...] + u_ref[...] * d_ref[...]
    ge = 0.5 * x * (1.0 + jnp.tanh(math.sqrt(2.0 / math.pi) * (x + 0.044715 * (x * x * x))))
    p = _bdot(ge, w_ref[...])
    o_ref[...] = (p[:, :S5_W] * _sigmoid(p[:, S5_W:])).astype(o_ref.dtype)


def _s5(z, m, e, ft, a_re, a_im, d_skip, w_glu):
    n_ch = (SEQ + CTX) // S5_L
    n_ctx_ch = CTX // S5_L
    rows = NB * n_ch
    width = S5_L * S5_C

    def to_groups(x):
        x = x.reshape(NB, -1, S5_L, S5_G, S5_C)
        return jnp.transpose(x, (3, 0, 1, 2, 4)).reshape(S5_G, NB, -1, width)

    zb = z.astype(BF16)
    u = jnp.concatenate([to_groups(zb[N_LAT:]), to_groups(zb[:N_LAT])], axis=2).reshape(S5_G, rows, width)
    y_loc, x_in = pl.pallas_call(
        _s5_local_kernel,
        grid=(S5_G,),
        in_specs=[pl.BlockSpec((1, rows, width), lambda g: (g, 0, 0)),
                  pl.BlockSpec((2, 1, width, width), lambda g: (0, g, 0, 0)),
                  pl.BlockSpec((2, 1, width, 2 * S5_P), lambda g: (0, g, 0, 0))],
        out_specs=[pl.BlockSpec((2, 1, rows, width), lambda g: (0, g, 0, 0)),
                   pl.BlockSpec((2, 1, rows, 2 * S5_P), lambda g: (0, g, 0, 0))],
        out_shape=[jax.ShapeDtypeStruct((2, S5_G, rows, width), F32),
                   jax.ShapeDtypeStruct((2, S5_G, rows, 2 * S5_P), F32)],
        compiler_params=_cp("parallel"),
        name="s5_local",
    )(u, m, e)

    step = n_ctx_ch
    n_blk = n_ch // step
    coef_r = jnp.concatenate([a_re, a_re], axis=-1)[:, :, None, :]
    coef_i = jnp.concatenate([-a_im, a_im], axis=-1)[:, :, None, :]

    def chunk_blk(d, i):
        return jnp.where(d == 0, i, jnp.where(i == 0, 0, n_blk - i))

    st_blk = pl.BlockSpec((1, S5_G, NB, step, 2 * S5_P), lambda d, i: (d, 0, 0, chunk_blk(d, i), 0))
    coef_blk = pl.BlockSpec((1, S5_G, 1, 2 * S5_P), lambda d, i: (d, 0, 0, 0))
    x0 = pl.pallas_call(
        _s5_carry_kernel,
        grid=(2, n_blk),
        in_specs=[st_blk, coef_blk, coef_blk],
        out_specs=st_blk,
        out_shape=jax.ShapeDtypeStruct((2, S5_G, NB, n_ch, 2 * S5_P), F32),
        scratch_shapes=[pltpu.VMEM((S5_G, NB, 2 * S5_P), F32)],
        compiler_params=_cp("arbitrary", "arbitrary"),
        name="s5_carry",
    )(x_in.reshape(2, S5_G, NB, n_ch, 2 * S5_P), coef_r, coef_i).reshape(2, S5_G, rows, 2 * S5_P)

    y = pl.pallas_call(
        _s5_state_kernel,
        grid=(S5_G,),
        in_specs=[pl.BlockSpec((2, 1, rows, width), lambda g: (0, g, 0, 0)),
                  pl.BlockSpec((2, 1, rows, 2 * S5_P), lambda g: (0, g, 0, 0)),
                  pl.BlockSpec((2, 1, width, 2 * S5_P), lambda g: (0, g, 0, 0))],
        out_specs=pl.BlockSpec((1, rows, width), lambda g: (g, 0, 0)),
        out_shape=jax.ShapeDtypeStruct((S5_G, rows, width), BF16),
        compiler_params=_cp("parallel"),
        name="s5_state",
    )(y_loc, x0, ft)

    y = jnp.transpose(y.reshape(S5_G, NB, n_ch, S5_L, S5_C), (1, 2, 3, 0, 4)).reshape(NB, n_ch * S5_L, S5_W)
    y = jnp.concatenate([y[:, CTX:].reshape(N_LAT, S5_W), y[:, :CTX].reshape(N_CTX, S5_W)], axis=0)
    tok = pl.BlockSpec((TILE, S5_W), lambda t: (t, 0))
    return pl.pallas_call(
        _s5_glu_kernel,
        grid=(N_TILES,),
        in_specs=[tok, tok, pl.BlockSpec((1, S5_W), lambda t: (0, 0)),
                  pl.BlockSpec((S5_W, 2 * S5_W), lambda t: (0, 0))],
        out_specs=tok,
        out_shape=jax.ShapeDtypeStruct((ROWS, S5_W), BF16),
        compiler_params=_cp("parallel"),
        name="s5_glu",
    )(y, z, d_skip.reshape(1, S5_W), w_glu)


def kernel(x, c, ctx, c_ctx, w_mod, b_mod, norm1_g, norm2_g, w_in, b_gate, q_norm_g, k_norm_g, rwkv_mu, rwkv_w0, rwkv_w_up, rwkv_a0, rwkv_a_up, rwkv_g_up, rwkv_k_k, rwkv_k_a, rwkv_r_k, rwkv_gn_g, rwkv_gn_b, mlstm_conv_w, mlstm_i_b, mlstm_f_b, mlstm_gn_g, s5_lam_re, s5_lam_im, s5_log_step, s5_b_re, s5_b_im, s5_c_re, s5_c_im, s5_d, s5_w_glu, w_br_attn, w_br_rwkv, w_br_mlstm, w_br_s5, w_out, w_ffn_in, w_ffn_out, final_norm_g):
    cos, sin = _rope_tables()
    xs = jnp.concatenate([x.reshape(N_LAT, D), ctx.reshape(N_CTX, D)], axis=0)
    c_all = jnp.concatenate([c, c_ctx[None], jnp.zeros((3, D), F32)], axis=0)
    s5_m, s5_e, s5_ft, s5_ar, s5_ai = _s5_operators(s5_lam_re, s5_lam_im, s5_log_step, s5_b_re, s5_b_im,
                                                    s5_c_re, s5_c_im)
    tm = 1024
    for l in range(DEPTH):
        last = l == DEPTH - 1
        n_rows = N_LAT if last else ROWS
        mod = _modulation(c_all, w_mod, b_mod, l).reshape(8, 1, 6 * D)
        h = _norm_mod(xs, norm1_g[l], mod, 0, 1, N_TILES)
        w_attn, w_rwkv, w_ml, w_mlg, w_s5, w_gate = _w_in_split(w_in, l)
        z_attn = _mm(h, w_attn, tm, 512)
        z_rwkv = _mm(h, w_rwkv, tm, 896)
        z_ml = _mm(h, w_ml, tm, 1024)
        z_mlg = _mm(h, w_mlg, tm, 128)
        z_s5 = _mm(h, w_s5, tm, 512)
        gates = _mm_gate(h, w_gate, b_gate, l, tm, 1024, n_rows)

        ya = _attention(z_attn, q_norm_g[l], k_norm_g[l], cos, sin)
        rp = dict(mu=rwkv_mu[l], w0=rwkv_w0[l], w_up=rwkv_w_up[l], a0=rwkv_a0[l], a_up=rwkv_a_up[l],
                  g_up=rwkv_g_up[l], k_k=rwkv_k_k[l], k_a=rwkv_k_a[l], r_k=rwkv_r_k[l].reshape(RW_W))
        r, v, kk, g, bonus, lw, bmat, km = _rwkv_prep(z_rwkv, rp)
        yr = _rwkv_scan(r, v, kk, lw, bmat, km, bonus, g, rwkv_gn_g[l], rwkv_gn_b[l])
        ym = _mlstm(z_ml, z_mlg, mlstm_conv_w[l], mlstm_i_b[l], mlstm_f_b[l], mlstm_gn_g[l])
        ys = _s5(z_s5, s5_m[l], s5_e[l], s5_ft[l], s5_ar[l], s5_ai[l], s5_d[l], s5_w_glu[l])

        y = _merge(ya, yr, ym, ys, gates, w_br_attn, w_br_rwkv, w_br_mlstm, w_br_s5, l, tm, 512, n_rows)
        xs = _mm_res(y, w_out, l, xs, mod, 2, tm, 1024, n_rows)
        h2 = _norm_mod(xs, norm2_g[l], mod, 3, 4, n_rows // TILE)
        u = _ffn_in(h2, w_ffn_in, l, tm, 512, n_rows)
        xs = _mm_res(u, w_ffn_out, l, xs, mod, 5, 512, 512, n_rows)
    return _final_norm(xs, final_norm_g).reshape(NB, SEQ, D)
```

```python
import functools
import math

import numpy as np
import jax
import jax.numpy as jnp
from jax import lax
from jax.experimental import pallas as pl
from jax.experimental.pallas import tpu as pltpu

F32 = jnp.float32
BF16 = jnp.bfloat16
HP = lax.Precision.HIGHEST

D = 2048
NB = 4
SEQ = 2048
CTX = 256
DEPTH = 2
N_LAT = NB * SEQ
N_CTX = NB * CTX
ROWS = N_LAT + N_CTX
EPS = 1e-6
GRID_W = 64

HEAD_DIM = 128
ATTN_HEADS = 8
ATTN_KV = 2
ROPE_THETA = 10000.0
ATTN_Q = ATTN_HEADS * HEAD_DIM
ATTN_IN = (ATTN_HEADS + 2 * ATTN_KV) * HEAD_DIM

RW_H = 8
RW_D = 64
RW_W = 512
RW_IN = 3 * RW_W + 64 + 64 + 128
RW_DECAY = math.exp(-0.5)
RW_GN_EPS = 64e-5

ML_H = 4
ML_D = 128
ML_W = 512
ML_NEG = -1e30
ML_IN = 4 * ML_W + 4 * ML_H

S5_W = 512
S5_C = 16
S5_G = 32
S5_P = 64
S5_L = 16

FFN_H = 5632
GATE_IN = 4 * D

CHUNK = 64
TILE = 256
N_TILES = ROWS // TILE
SEQ_CHUNKS = (SEQ + CTX) // CHUNK
CTX_CHUNKS = CTX // CHUNK
LAT_CHUNKS = SEQ // CHUNK

VMEM_LIMIT_BYTES = 56 * 1024 * 1024


def _cp(*sem):
    return pltpu.CompilerParams(dimension_semantics=sem, vmem_limit_bytes=VMEM_LIMIT_BYTES)


def _bdot(a, b):
    return jnp.dot(a.astype(BF16), b.astype(BF16), preferred_element_type=F32)


def _bdot_nt(a, b):
    return lax.dot_general(a.astype(BF16), b.astype(BF16), (((1,), (1,)), ((), ())),
                           preferred_element_type=F32)


def _bdot_tn(a, b):
    return lax.dot_general(a.astype(BF16), b.astype(BF16), (((0,), (0,)), ((), ())),
                           preferred_element_type=F32)


def _hdot(a, b):
    return jnp.dot(a, b, precision=HP, preferred_element_type=F32)


def _sigmoid(x):
    return 1.0 / (1.0 + jnp.exp(-x))


def _silu(x):
    return x * _sigmoid(x)


def _mod_kernel(c_ref, w_ref, b_ref, o_ref):
    c_hi, c_lo = _split_bf16(_silu(c_ref[...]), 2)
    w_hi, w_lo = _split_bf16(w_ref[...], 2)
    rows = c_hi.shape[0]
    both = jnp.dot(jnp.concatenate([c_hi, c_lo], axis=0), w_hi, preferred_element_type=F32)
    o_ref[...] = both[:rows] + both[rows:] + jnp.dot(c_hi, w_lo, preferred_element_type=F32) + b_ref[...]


def _modulation(c_all, w, b, l):
    tn = 1024
    return pl.pallas_call(
        _mod_kernel,
        grid=(6 * D // tn,),
        in_specs=[pl.BlockSpec((8, D), lambda j: (0, 0)),
                  pl.BlockSpec((None, D, tn), lambda j: (l, 0, j)),
                  pl.BlockSpec((None, 1, tn), lambda j: (l, 0, j))],
        out_specs=pl.BlockSpec((8, tn), lambda j: (0, j)),
        out_shape=jax.ShapeDtypeStruct((8, 6 * D), F32),
        compiler_params=_cp("arbitrary"),
        name="modulation",
    )(c_all, w, b.reshape(DEPTH, 1, 6 * D))


def _mod_row(i, tm):
    return jnp.where(i * tm < N_LAT, (i * tm) // SEQ, NB)


def _norm_mod_kernel(x_ref, g_ref, sh_ref, sc_ref, o_ref):
    x = x_ref[...]
    y = x * lax.rsqrt(jnp.mean(x * x, axis=-1, keepdims=True) + EPS) * g_ref[...]
    o_ref[...] = (y * (1.0 + sc_ref[0]) + sh_ref[0]).astype(o_ref.dtype)


def _norm_mod(x, g, modr, shift_blk, scale_blk, n_tiles):
    return pl.pallas_call(
        _norm_mod_kernel,
        grid=(n_tiles,),
        in_specs=[pl.BlockSpec((TILE, D), lambda i: (i, 0)),
                  pl.BlockSpec((1, D), lambda i: (0, 0)),
                  pl.BlockSpec((1, 1, D), lambda i: (_mod_row(i, TILE), 0, shift_blk)),
                  pl.BlockSpec((1, 1, D), lambda i: (_mod_row(i, TILE), 0, scale_blk))],
        out_specs=pl.BlockSpec((TILE, D), lambda i: (i, 0)),
        out_shape=jax.ShapeDtypeStruct((n_tiles * TILE, D), BF16),
        compiler_params=_cp("parallel"),
        name="norm_mod",
    )(x, g.reshape(1, D), modr, modr)


W_IN_OFFSETS = tuple(int(v) for v in np.cumsum([0, ATTN_IN, RW_IN, 4 * ML_W, 4 * ML_H, S5_W, GATE_IN]))
ML_GATE_PAD = 128


def _w_in_split_kernel(w_ref, attn_ref, rwkv_ref, ml_ref, mlg_ref, s5_ref, gate_ref):
    o = W_IN_OFFSETS
    attn_ref[...] = w_ref[:, o[0]:o[1]].astype(BF16)
    rwkv_ref[...] = w_ref[:, o[1]:o[2]].astype(BF16)
    ml_ref[...] = w_ref[:, o[2]:o[3]].astype(BF16)
    tail = w_ref[:, o[3]:o[3] + ML_GATE_PAD]
    lane = lax.broadcasted_iota(jnp.int32, tail.shape, 1)
    mlg_ref[...] = jnp.where(lane < o[4] - o[3], tail, 0.0).astype(BF16)
    s5_ref[...] = w_ref[:, o[4]:o[5]].astype(BF16)
    gate_ref[...] = w_ref[:, o[5]:o[6]].astype(BF16)


def _w_in_split(w, l):
    tr = 128
    widths = (ATTN_IN, RW_IN, 4 * ML_W, ML_GATE_PAD, S5_W, GATE_IN)
    return pl.pallas_call(
        _w_in_split_kernel,
        grid=(D // tr,),
        in_specs=[pl.BlockSpec((None, tr, w.shape[2]), lambda i: (l, i, 0))],
        out_specs=[pl.BlockSpec((tr, n), lambda i: (i, 0)) for n in widths],
        out_shape=[jax.ShapeDtypeStruct((D, n), BF16) for n in widths],
        compiler_params=_cp("parallel"),
        name="w_in_split",
    )(w)


def _mm_kernel(a_ref, w_ref, o_ref):
    o_ref[...] = jnp.dot(a_ref[...], w_ref[...], preferred_element_type=F32).astype(o_ref.dtype)


def _mm(a, w, tm, tn, out_dtype=F32):
    m, k = a.shape
    n = w.shape[1]
    return pl.pallas_call(
        _mm_kernel,
        grid=(n // tn, m // tm),
        in_specs=[pl.BlockSpec((tm, k), lambda j, i: (i, 0)),
                  pl.BlockSpec((k, tn), lambda j, i: (0, j))],
        out_specs=pl.BlockSpec((tm, tn), lambda j, i: (i, j)),
        out_shape=jax.ShapeDtypeStruct((m, n), out_dtype),
        compiler_params=_cp("parallel", "parallel"),
        name="matmul",
    )(a, w)


ROW_SUB = 256


def _row_blocks(ref):
    return [slice(r, r + ROW_SUB) for r in range(0, ref.shape[0], ROW_SUB)]


def _mm_gate_kernel(a_ref, w_ref, b_ref, o_ref):
    for rows in _row_blocks(o_ref):
        z = jnp.dot(a_ref[rows, :], w_ref[...], preferred_element_type=F32)
        o_ref[rows, :] = _sigmoid(z + b_ref[...]).astype(o_ref.dtype)


def _mm_gate(a, w, b_gate, l, tm, tn, m):
    k = a.shape[1]
    n = w.shape[1]
    return pl.pallas_call(
        _mm_gate_kernel,
        grid=(n // tn, m // tm),
        in_specs=[pl.BlockSpec((tm, k), lambda j, i: (i, 0)),
                  pl.BlockSpec((k, tn), lambda j, i: (0, j)),
                  pl.BlockSpec((None, 1, tn), lambda j, i: (l, 0, j))],
        out_specs=pl.BlockSpec((tm, tn), lambda j, i: (i, j)),
        out_shape=jax.ShapeDtypeStruct((m, n), BF16),
        compiler_params=_cp("parallel", "parallel"),
        name="matmul_gate",
    )(a, w, b_gate.reshape(DEPTH, 1, GATE_IN))


def _mm_res_kernel(a_ref, w_ref, x_ref, g_ref, o_ref, wb_ref):
    @pl.when(pl.program_id(1) == 0)
    def _():
        wb_ref[...] = w_ref[...].astype(BF16)

    for rows in _row_blocks(o_ref):
        y = jnp.dot(a_ref[rows, :], wb_ref[...], preferred_element_type=F32)
        o_ref[rows, :] = x_ref[rows, :] + g_ref[0] * y


def _mm_res(a, w, l, x, mod, gate_blk, tm, tn, n_rows):
    k = a.shape[1]
    n = w.shape[2]
    gpb = D // tn
    return pl.pallas_call(
        _mm_res_kernel,
        grid=(n // tn, n_rows // tm),
        in_specs=[pl.BlockSpec((tm, k), lambda j, i: (i, 0)),
                  pl.BlockSpec((None, k, tn), lambda j, i: (l, 0, j)),
                  pl.BlockSpec((tm, tn), lambda j, i: (i, j)),
                  pl.BlockSpec((1, 1, tn), lambda j, i: (_mod_row(i, tm), 0, gate_blk * gpb + j))],
        out_specs=pl.BlockSpec((tm, tn), lambda j, i: (i, j)),
        out_shape=jax.ShapeDtypeStruct((n_rows, n), F32),
        scratch_shapes=[pltpu.VMEM((k, tn), BF16)],
        compiler_params=_cp("arbitrary", "arbitrary"),
        name="matmul_residual",
    )(a, w, x, mod)


def _ffn_in_kernel(a_ref, wa_ref, wb_ref, o_ref, wab_ref, wbb_ref):
    @pl.when(pl.program_id(1) == 0)
    def _():
        wab_ref[...] = wa_ref[...].astype(BF16)
        wbb_ref[...] = wb_ref[...].astype(BF16)

    for rows in _row_blocks(o_ref):
        a = a_ref[rows, :]
        u = jnp.dot(a, wab_ref[...], preferred_element_type=F32)
        v = jnp.dot(a, wbb_ref[...], preferred_element_type=F32)
        o_ref[rows, :] = (_silu(u) * v).astype(o_ref.dtype)


def _ffn_in(h, w, l, tm, tn, n_rows):
    nb = FFN_H // tn
    return pl.pallas_call(
        _ffn_in_kernel,
        grid=(nb, n_rows // tm),
        in_specs=[pl.BlockSpec((tm, D), lambda j, i: (i, 0)),
                  pl.BlockSpec((None, D, tn), lambda j, i: (l, 0, j)),
                  pl.BlockSpec((None, D, tn), lambda j, i: (l, 0, nb + j))],
        out_specs=pl.BlockSpec((tm, tn), lambda j, i: (i, j)),
        out_shape=jax.ShapeDtypeStruct((n_rows, FFN_H), BF16),
        scratch_shapes=[pltpu.VMEM((D, tn), BF16), pltpu.VMEM((D, tn), BF16)],
        compiler_params=_cp("arbitrary", "arbitrary"),
        name="ffn_in",
    )(h, w, w)


def _merge_kernel(ya_ref, yr_ref, ym_ref, ys_ref, ga_ref, gr_ref, gm_ref, gs_ref,
                  wa_ref, wr_ref, wm_ref, ws_ref, o_ref, wab_ref, wrb_ref, wmb_ref, wsb_ref):
    @pl.when(pl.program_id(1) == 0)
    def _():
        wab_ref[...] = wa_ref[...].astype(BF16)
        wrb_ref[...] = wr_ref[...].astype(BF16)
        wmb_ref[...] = wm_ref[...].astype(BF16)
        wsb_ref[...] = ws_ref[...].astype(BF16)

    for rows in _row_blocks(o_ref):
        acc = None
        for y_ref, g_ref, w_ref in ((ya_ref, ga_ref, wab_ref), (yr_ref, gr_ref, wrb_ref),
                                    (ym_ref, gm_ref, wmb_ref), (ys_ref, gs_ref, wsb_ref)):
            term = g_ref[rows, :].astype(F32) * jnp.dot(y_ref[rows, :], w_ref[...], preferred_element_type=F32)
            acc = term if acc is None else acc + term
        o_ref[rows, :] = acc.astype(o_ref.dtype)


def _merge(ya, yr, ym, ys, gates, wa, wr, wm, ws, l, tm, tn, n_rows):
    nb = D // tn

    def act(width):
        return pl.BlockSpec((tm, width), lambda j, i: (i, 0))

    def gate(br):
        return pl.BlockSpec((tm, tn), lambda j, i: (i, br * nb + j))

    def wgt(width):
        return pl.BlockSpec((None, width, tn), lambda j, i: (l, 0, j))

    return pl.pallas_call(
        _merge_kernel,
        grid=(nb, n_rows // tm),
        in_specs=[act(ATTN_Q), act(RW_W), act(ML_W), act(S5_W),
                  gate(0), gate(1), gate(2), gate(3),
                  wgt(ATTN_Q), wgt(RW_W), wgt(ML_W), wgt(S5_W)],
        out_specs=pl.BlockSpec((tm, tn), lambda j, i: (i, j)),
        out_shape=jax.ShapeDtypeStruct((n_rows, D), BF16),
        scratch_shapes=[pltpu.VMEM((ATTN_Q, tn), BF16), pltpu.VMEM((RW_W, tn), BF16),
                        pltpu.VMEM((ML_W, tn), BF16), pltpu.VMEM((S5_W, tn), BF16)],
        compiler_params=_cp("arbitrary", "arbitrary"),
        name="gated_merge",
    )(ya, yr, ym, ys, gates, gates, gates, gates, wa, wr, wm, ws)


def _final_norm_kernel(x_ref, g_ref, o_ref):
    x = x_ref[...]
    o_ref[...] = x * lax.rsqrt(jnp.mean(x * x, axis=-1, keepdims=True) + EPS) * g_ref[...]


def _final_norm(x, g):
    n_tiles = N_LAT // TILE
    return pl.pallas_call(
        _final_norm_kernel,
        grid=(n_tiles,),
        in_specs=[pl.BlockSpec((TILE, D), lambda i: (i, 0)),
                  pl.BlockSpec((1, D), lambda i: (0, 0))],
        out_specs=pl.BlockSpec((TILE, D), lambda i: (i, 0)),
        out_shape=jax.ShapeDtypeStruct((N_LAT, D), F32),
        compiler_params=_cp("parallel"),
        name="final_norm",
    )(x, g.reshape(1, D))


def _tile_halo_specs(width, col_blk=0):
    last = ROWS // 8 - 1
    per = TILE // 8
    return [pl.BlockSpec((TILE, width), lambda t: (t, col_blk)),
            pl.BlockSpec((8, width), lambda t: (jnp.maximum(t * per - 1, 0), col_blk)),
            pl.BlockSpec((8, width), lambda t: (jnp.minimum((t + 1) * per, last), col_blk))]


def _neighbours(z, prev_blk, next_blk):
    t = pl.program_id(0)
    pos = t % (SEQ // TILE)
    is_lat = t < N_LAT // TILE
    has_prev = jnp.logical_and(is_lat, pos > 0).astype(F32)
    has_next = jnp.logical_and(is_lat, pos < SEQ // TILE - 1).astype(F32)
    row = lax.broadcasted_iota(jnp.int32, z.shape, 0)
    zp = jnp.where(row == 0, prev_blk[7:8, :] * has_prev, pltpu.roll(z, 1, 0))
    zn = jnp.where(row == TILE - 1, next_blk[0:1, :] * has_next, pltpu.roll(z, TILE - 1, 0))
    return zp, zn


def _seq_row_block(b, d, n):
    ctx_c = jnp.where(d == 0, n, CTX_CHUNKS - 1 - n)
    lat_c = jnp.where(d == 0, n - CTX_CHUNKS, SEQ_CHUNKS - 1 - n)
    return jnp.where(n < CTX_CHUNKS, N_LAT // CHUNK + CTX_CHUNKS * b + ctx_c, LAT_CHUNKS * b + lat_c)


def _rope(x, cos, sin):
    lane = lax.broadcasted_iota(jnp.int32, x.shape, 1)
    first = (lane % 64) < 32
    partner = jnp.where(first, pltpu.roll(x, 96, 1), pltpu.roll(x, 32, 1))
    return x * cos + partner * sin


def _rms(x, g):
    return x * lax.rsqrt(jnp.mean(x * x, axis=-1, keepdims=True) + EPS) * g


def _attn_kernel(q_ref, kl_ref, kc_ref, vl_ref, vc_ref, cos_ref, sin_ref, cos_t_ref, sin_t_ref,
                 qg_ref, kg_ref, o_ref, klb_ref, kcb_ref, vlb_ref, vcb_ref):
    qi = pl.program_id(2)
    n_lat_tiles = SEQ // TILE

    @pl.when(qi == 0)
    def _():
        kg = kg_ref[...]
        klb_ref[...] = _rope(_rms(kl_ref[...], kg), cos_ref[...], sin_ref[...]).astype(BF16)
        kcb_ref[...] = _rms(kc_ref[...], kg).astype(BF16)
        vlb_ref[...] = vl_ref[...].astype(BF16)
        vcb_ref[...] = vc_ref[...].astype(BF16)

    scale = HEAD_DIM ** -0.5 * math.log2(math.e)
    nt = (((1,), (1,)), ((), ()))

    def heads(latent):
        sls = [slice(h * HEAD_DIM, (h + 1) * HEAD_DIM) for h in range(ATTN_HEADS // ATTN_KV)]
        q = [_rms(q_ref[:, sl], qg_ref[...]) for sl in sls]
        if latent:
            q = [_rope(x, cos_t_ref[...], sin_t_ref[...]) for x in q]
        q = [(x * scale).astype(BF16) for x in q]
        s_c = [lax.dot_general(x, kcb_ref[...], nt, preferred_element_type=F32) for x in q]
        m = [jnp.max(x, axis=-1, keepdims=True) for x in s_c]
        if latent:
            s_l = [lax.dot_general(x, klb_ref[...], nt, preferred_element_type=F32) for x in q]
            m = [jnp.maximum(a, jnp.max(x, axis=-1, keepdims=True)) for a, x in zip(m, s_l)]
        p_c = [jnp.exp2(x - a) for x, a in zip(s_c, m)]
        den = [jnp.sum(x, axis=-1, keepdims=True) for x in p_c]
        acc = [jnp.dot(x.astype(BF16), vcb_ref[...], preferred_element_type=F32) for x in p_c]
        if latent:
            p_l = [jnp.exp2(x - a) for x, a in zip(s_l, m)]
            den = [a + jnp.sum(x, axis=-1, keepdims=True) for a, x in zip(den, p_l)]
            acc = [a + jnp.dot(x.astype(BF16), vlb_ref[...], preferred_element_type=F32) for a, x in zip(acc, p_l)]
        for sl, a, dn in zip(sls, acc, den):
            o_ref[:, sl] = (a / dn).astype(o_ref.dtype)

    @pl.when(qi < n_lat_tiles)
    def _():
        heads(True)

    @pl.when(qi == n_lat_tiles)
    def _():
        heads(False)


def _rope_tables():
    rows = SEQ // GRID_W
    row = jnp.repeat(jnp.arange(rows, dtype=F32), GRID_W)
    col = jnp.tile(jnp.arange(GRID_W, dtype=F32), rows)
    axis_dim = HEAD_DIM // 2
    inv_freq = ROPE_THETA ** (-jnp.arange(0, axis_dim, 2, dtype=F32) / axis_dim)
    ang_r = row[:, None] * inv_freq[None]
    ang_c = col[:, None] * inv_freq[None]
    cos = jnp.concatenate([jnp.cos(ang_r), jnp.cos(ang_r), jnp.cos(ang_c), jnp.cos(ang_c)], axis=-1)
    sin = jnp.concatenate([-jnp.sin(ang_r), jnp.sin(ang_r), -jnp.sin(ang_c), jnp.sin(ang_c)], axis=-1)
    return cos, sin


def _attention(z, q_g, k_g, cos, sin):
    n_lat_tiles = SEQ // TILE
    qw = ATTN_Q // ATTN_KV
    kcol = ATTN_Q // HEAD_DIM
    vcol = kcol + ATTN_KV

    def q_row(b, g, qi):
        return jnp.where(qi < n_lat_tiles, n_lat_tiles * b + qi, N_LAT // TILE + b)

    return pl.pallas_call(
        _attn_kernel,
        grid=(NB, ATTN_KV, n_lat_tiles + 1),
        in_specs=[pl.BlockSpec((TILE, qw), lambda b, g, qi: (q_row(b, g, qi), g)),
                  pl.BlockSpec((SEQ, HEAD_DIM), lambda b, g, qi: (b, kcol + g)),
                  pl.BlockSpec((CTX, HEAD_DIM), lambda b, g, qi: (N_LAT // CTX + b, kcol + g)),
                  pl.BlockSpec((SEQ, HEAD_DIM), lambda b, g, qi: (b, vcol + g)),
                  pl.BlockSpec((CTX, HEAD_DIM), lambda b, g, qi: (N_LAT // CTX + b, vcol + g)),
                  pl.BlockSpec((SEQ, HEAD_DIM), lambda b, g, qi: (0, 0)),
                  pl.BlockSpec((SEQ, HEAD_DIM), lambda b, g, qi: (0, 0)),
                  pl.BlockSpec((TILE, HEAD_DIM), lambda b, g, qi: (jnp.minimum(qi, n_lat_tiles - 1), 0)),
                  pl.BlockSpec((TILE, HEAD_DIM), lambda b, g, qi: (jnp.minimum(qi, n_lat_tiles - 1), 0)),
                  pl.BlockSpec((1, HEAD_DIM), lambda b, g, qi: (0, 0)),
                  pl.BlockSpec((1, HEAD_DIM), lambda b, g, qi: (0, 0))],
        out_specs=pl.BlockSpec((TILE, qw), lambda b, g, qi: (q_row(b, g, qi), g)),
        out_shape=jax.ShapeDtypeStruct((ROWS, ATTN_Q), BF16),
        scratch_shapes=[pltpu.VMEM((SEQ, HEAD_DIM), BF16), pltpu.VMEM((CTX, HEAD_DIM), BF16),
                        pltpu.VMEM((SEQ, HEAD_DIM), BF16), pltpu.VMEM((CTX, HEAD_DIM), BF16)],
        compiler_params=_cp("arbitrary", "arbitrary", "arbitrary"),
        name="attention",
    )(z, z, z, z, z, cos, sin, cos, sin, q_g.reshape(1, HEAD_DIM), k_g.reshape(1, HEAD_DIM))


def _rwkv_prep_kernel(z_ref, zp_ref, zn_ref, mu_ref, w0_ref, wup_ref, a0_ref, aup_ref, gup_ref,
                      kk_ref, ka_ref, rk_ref, bd_ref,
                      r_out, v_out, kkn_out, g_out, bonus_out, lw_out, b_out, km_out):
    z = z_ref[...]
    zp, zn = _neighbours(z, zp_ref[...], zn_ref[...])
    zs = z + mu_ref[...] * (0.5 * (zp + zn) - z)
    r = zs[:, 0:RW_W]
    k = zs[:, RW_W:2 * RW_W]
    v = zs[:, 2 * RW_W:3 * RW_W]
    w_lo = zs[:, 3 * RW_W:3 * RW_W + 64]
    a_lo = zs[:, 3 * RW_W + 64:3 * RW_W + 128]
    g_lo = zs[:, 3 * RW_W + 128:3 * RW_W + 256]
    bd = bd_ref[...]
    kk = k * kk_ref[...]
    kk = kk * lax.rsqrt(_dot_rhs_exact(kk * kk, bd) + 1e-12)
    r_out[...] = r
    v_out[...] = v
    kkn_out[...] = kk
    g_out[...] = _bdot(_sigmoid(g_lo), gup_ref[...])
    tw = jnp.tanh(w_lo)
    km_sum = None
    for d in range(2):
        lw = -RW_DECAY * _sigmoid(w0_ref[d] + _bdot(tw, wup_ref[d]))
        a = _sigmoid(a0_ref[d] + _bdot(a_lo, aup_ref[d]))
        km = k * (1.0 + (a - 1.0) * ka_ref[...])
        lw_out[d] = lw
        b_out[d] = a * kk
        km_out[d] = km
        km_sum = km if km_sum is None else km_sum + km
    bonus_out[...] = _dot_rhs_exact(r * km_sum * rk_ref[...], bd) * v


def _rwkv_prep(z, p):
    row = lambda a: a.reshape(1, -1)
    full = lambda shape: pl.BlockSpec(shape, lambda t: (0,) * len(shape))
    out_tok = pl.BlockSpec((TILE, RW_W), lambda t: (t, 0))
    out_dir = pl.BlockSpec((2, TILE, RW_W), lambda t: (0, t, 0))
    tok = jax.ShapeDtypeStruct((ROWS, RW_W), F32)
    drn = jax.ShapeDtypeStruct((2, ROWS, RW_W), F32)
    return pl.pallas_call(
        _rwkv_prep_kernel,
        grid=(N_TILES,),
        in_specs=_tile_halo_specs(RW_IN) + [
            full((1, RW_IN)), full((2, 1, RW_W)), full((2, 64, RW_W)), full((2, 1, RW_W)),
            full((2, 64, RW_W)), full((128, RW_W)), full((1, RW_W)), full((1, RW_W)), full((1, RW_W)),
            full((RW_W, RW_W))],
        out_specs=[out_tok] * 5 + [out_dir] * 3,
        out_shape=[tok] * 5 + [drn] * 3,
        compiler_params=_cp("parallel"),
        name="rwkv_prep",
    )(z, z, z, row(p["mu"]), p["w0"].reshape(2, 1, RW_W), p["w_up"], p["a0"].reshape(2, 1, RW_W),
      p["a_up"], p["g_up"], row(p["k_k"]), row(p["k_a"]), row(p["r_k"]),
      _head_block_ones(RW_W, RW_D).astype(BF16))


def _head_block_ones(width, head):
    idx = np.arange(width) // head
    return jnp.asarray((idx[:, None] == idx[None, :]).astype(np.float32))


def _split_bf16(x, pieces):
    out = []
    for _ in range(pieces):
        p = x.astype(BF16)
        out.append(p)
        x = x - p.astype(F32)
    return out


def _dot_rhs_exact(x, m, pieces=2):
    return sum(jnp.dot(p, m, preferred_element_type=F32) for p in _split_bf16(x, pieces))


def _dot_lhs_exact(m, x, pieces=3):
    return sum(jnp.dot(m, p, preferred_element_type=F32) for p in _split_bf16(x, pieces))


RW_SUB = 2


def _rwkv_chunk_kernel(r_ref, v_ref, kk_ref, lw_ref, b_ref, km_ref, p_ref, sl_ref, re_ref, ol_ref):
    d = pl.program_id(0)
    c = CHUNK
    hd = RW_D
    ti = lax.broadcasted_iota(jnp.int32, (c, c), 0)
    si = lax.broadcasted_iota(jnp.int32, (c, c), 1)
    delta = (ti - si) * (1 - 2 * d)
    incl = delta >= 0
    strict = delta > 0
    eye = jnp.where(ti == si, 1.0, 0.0)
    tri = jnp.where(incl, 1.0, 0.0).astype(BF16)
    gr = lax.broadcasted_iota(jnp.int32, (2 * c, c), 0)
    gc = lax.broadcasted_iota(jnp.int32, (2 * c, c), 1)
    gmask = (gr % c - gc) * (1 - 2 * d) >= jnp.where(gr < c, 1, 0)

    items = []
    for sub in range(RW_SUB):
        rows = slice(sub * c, (sub + 1) * c)
        lw = lw_ref[0, rows, :]
        cs = _dot_lhs_exact(tri, lw)
        tot = jnp.sum(lw, axis=0, keepdims=True)
        r = r_ref[rows, :]
        v = v_ref[rows, :]
        kk = kk_ref[rows, :]
        bb = b_ref[0, rows, :]
        km = km_ref[0, rows, :]
        e_neg = jnp.exp(-cs)
        e_rem = jnp.exp(tot - cs)
        kkt = kk * jnp.exp(cs - lw)
        rt = r * jnp.exp(cs)
        bt = bb * e_neg
        kt = km * e_neg
        bh = bb * e_rem
        kh = km * e_rem
        e_tot = jnp.exp(tot)
        for h in range(RW_H):
            s = slice(h * hd, (h + 1) * hd)
            items.append(dict(sub=sub, s=s, kkt=kkt[:, s], rt=rt[:, s], bt=bt[:, s], kt=kt[:, s], v=v[:, s],
                              bh=bh[:, s], kh=kh[:, s], e_tot=e_tot[:, s]))

    kr = [jnp.concatenate([it["kkt"], it["rt"]], axis=0) for it in items]
    gb = [jnp.where(gmask, _bdot_nt(x, it["bt"]), 0.0) for x, it in zip(kr, items)]
    gk = [jnp.where(gmask, _bdot_nt(x, it["kt"]), 0.0) for x, it in zip(kr, items)]
    l_b = [x[:c] for x in gb]
    a_b = [x[c:] for x in gb]
    lkv_akv = [_bdot(x, it["v"]) for x, it in zip(gk, items)]
    pw = [_bdot(x, x) for x in l_b]
    inv = [eye - x for x in l_b]
    for _ in range(int(math.log2(c)) - 2):
        res = [_bdot(jnp.concatenate([p, i], axis=0), p) for p, i in zip(pw, inv)]
        pw = [x[:c] for x in res]
        inv = [i + x[c:] for i, x in zip(inv, res)]
    inv = [i + _bdot(i, p) for p, i in zip(pw, inv)]
    w = [_bdot(i, it["kkt"]) for i, it in zip(inv, items)]
    y_loc = [_bdot(i, x[:c]) for i, x in zip(inv, lkv_akv)]
    ab_w = [_bdot(a, x) for a, x in zip(a_b, w)]
    ab_y = [_bdot(a, x) for a, x in zip(a_b, y_loc)]
    vk = [_bdot_tn(it["v"], it["kh"]) for it in items]
    yb = [_bdot_tn(x, it["bh"]) for x, it in zip(y_loc, items)]
    wb = [_bdot_tn(x, it["bh"]) for x, it in zip(w, items)]
    for n, it in enumerate(items):
        s = it["s"]
        j = jnp.where(d == 0, it["sub"], RW_SUB - 1 - it["sub"])
        re_ref[0, 0, j, :, s] = it["rt"] - ab_w[n]
        ol_ref[0, 0, j, :, s] = lkv_akv[n][c:] - ab_y[n]
        sl_ref[0, 0, j, :, s] = vk[n] - yb[n]
        p_ref[0, 0, j, :, s] = eye * it["e_tot"] - wb[n]


def _rwkv_carry_kernel(p_ref, sl_ref, s_out_ref, st_ref):
    @pl.when(pl.program_id(0) == 0)
    def _():
        st_ref[...] = jnp.zeros_like(st_ref)

    for d in range(2):
        for b in range(NB):
            s = st_ref[d, b]
            s_out_ref[d, b, 0] = s
            for h in range(RW_H):
                sl = slice(h * RW_D, (h + 1) * RW_D)
                st_ref[d, b, :, sl] = _hdot(s[:, sl], p_ref[d, b, 0, :, sl]) + sl_ref[d, b, 0, :, sl]


TILE_CHUNKS = TILE // CHUNK


def _tile_of_group(b, grp):
    return jnp.where(grp == 0, N_LAT // TILE + b, (SEQ // TILE) * b + grp - 1)


def _bwd_group(grp):
    return jnp.where(grp == 0, 0, SEQ_CHUNKS // TILE_CHUNKS - grp)


def _rwkv_out_kernel(olf_ref, ref_ref, sf_ref, olb_ref, reb_ref, sb_ref, bonus_ref, g_ref, gng_ref, gnb_ref,
                     bd_ref, y_ref, o_scr):
    n = TILE_CHUNKS
    items = [(i, slice(h * RW_D, (h + 1) * RW_D)) for i in range(n) for h in range(RW_H)]
    pf = [_bdot_nt(ref_ref[0, 0, i, :, s], sf_ref[0, 0, i, :, s]) for i, s in items]
    pb = [_bdot_nt(reb_ref[0, 0, n - 1 - i, :, s], sb_ref[0, 0, n - 1 - i, :, s]) for i, s in items]
    for (i, s), a, b in zip(items, pf, pb):
        o_scr[i * CHUNK:(i + 1) * CHUNK, s] = olf_ref[0, 0, i, :, s] + olb_ref[0, 0, n - 1 - i, :, s] + a + b
    o = o_scr[...] + bonus_ref[...]
    bd = bd_ref[...]
    cen = o - _dot_rhs_exact(o, bd) * (1.0 / RW_D)
    var = _dot_rhs_exact(cen * cen, bd) * (1.0 / RW_D)
    y = cen * lax.rsqrt(var + RW_GN_EPS) * gng_ref[...] + gnb_ref[...]
    y_ref[...] = (y * g_ref[...]).astype(y_ref.dtype)


def _rwkv_scan(r, v, kk, lw, bmat, km, bonus, g, gn_g, gn_b):
    sub_rows = RW_SUB * CHUNK
    lat_blocks = N_LAT // sub_rows
    lat_per_b = SEQ // sub_rows
    ctx_per_b = CTX // sub_rows

    def step_block(d, rb):
        is_lat = rb < lat_blocks
        b = jnp.where(is_lat, rb // lat_per_b, (rb - lat_blocks) // ctx_per_b)
        i = jnp.where(is_lat, rb % lat_per_b, (rb - lat_blocks) % ctx_per_b)
        fwd = jnp.where(is_lat, ctx_per_b + i, i)
        bwd = jnp.where(is_lat, ctx_per_b + lat_per_b - 1 - i, ctx_per_b - 1 - i)
        return b, jnp.where(d == 0, fwd, bwd)

    tok = pl.BlockSpec((sub_rows, RW_W), lambda d, rb: (rb, 0))
    drn = pl.BlockSpec((1, sub_rows, RW_W), lambda d, rb: (d, rb, 0))
    step_shape = jax.ShapeDtypeStruct((2, NB, SEQ_CHUNKS, CHUNK, RW_W), F32)
    step_blk = pl.BlockSpec((1, 1, RW_SUB, CHUNK, RW_W), lambda d, rb: (d,) + step_block(d, rb) + (0, 0))
    p, s_loc, r_eff, o_loc = pl.pallas_call(
        _rwkv_chunk_kernel,
        grid=(2, ROWS // sub_rows),
        in_specs=[tok, tok, tok, drn, drn, drn],
        out_specs=[step_blk] * 4,
        out_shape=[step_shape] * 4,
        compiler_params=_cp("parallel", "parallel"),
        name="rwkv_chunk",
    )(r, v, kk, lw, bmat, km)

    all_blk = pl.BlockSpec((2, NB, 1, CHUNK, RW_W), lambda n: (0, 0, n, 0, 0))
    s_in = pl.pallas_call(
        _rwkv_carry_kernel,
        grid=(SEQ_CHUNKS,),
        in_specs=[all_blk, all_blk],
        out_specs=all_blk,
        out_shape=step_shape,
        scratch_shapes=[pltpu.VMEM((2, NB, CHUNK, RW_W), F32)],
        compiler_params=_cp("arbitrary"),
        name="rwkv_carry",
    )(p, s_loc)

    fwd = pl.BlockSpec((1, 1, TILE_CHUNKS, CHUNK, RW_W), lambda b, grp: (0, b, grp, 0, 0))
    bwd = pl.BlockSpec((1, 1, TILE_CHUNKS, CHUNK, RW_W), lambda b, grp: (1, b, _bwd_group(grp), 0, 0))
    rows = pl.BlockSpec((TILE, RW_W), lambda b, grp: (_tile_of_group(b, grp), 0))
    vec = pl.BlockSpec((1, RW_W), lambda b, grp: (0, 0))
    return pl.pallas_call(
        _rwkv_out_kernel,
        grid=(NB, SEQ_CHUNKS // TILE_CHUNKS),
        in_specs=[fwd, fwd, fwd, bwd, bwd, bwd, rows, rows, vec, vec,
                  pl.BlockSpec((RW_W, RW_W), lambda b, grp: (0, 0))],
        out_specs=rows,
        out_shape=jax.ShapeDtypeStruct((ROWS, RW_W), BF16),
        scratch_shapes=[pltpu.VMEM((TILE, RW_W), F32)],
        compiler_params=_cp("parallel", "parallel"),
        name="rwkv_out",
    )(o_loc, r_eff, s_in, o_loc, r_eff, s_in, bonus, g, gn_g.reshape(1, RW_W), gn_b.reshape(1, RW_W),
      _head_block_ones(RW_W, RW_D).astype(BF16))


def _mlstm_prep_kernel(z_ref, zp_ref, zn_ref, w_ref, o_ref):
    z = z_ref[...]
    zp, zn = _neighbours(z, zp_ref[...], zn_ref[...])
    y = _silu(zp * w_ref[0:1, :] + z * w_ref[1:2, :] + zn * w_ref[2:3, :])
    col = lax.broadcasted_iota(jnp.int32, y.shape, 1)
    o_ref[...] = jnp.where(col >= ML_W, y * (ML_D ** -0.5), y)


def _mlstm_prep(z, conv_w):
    return pl.pallas_call(
        _mlstm_prep_kernel,
        grid=(N_TILES,),
        in_specs=_tile_halo_specs(2 * ML_W) + [pl.BlockSpec((3, 2 * ML_W), lambda t: (0, 0))],
        out_specs=pl.BlockSpec((TILE, 2 * ML_W), lambda t: (t, 0)),
        out_shape=jax.ShapeDtypeStruct((ROWS, 2 * ML_W), F32),
        compiler_params=_cp("parallel"),
        name="mlstm_prep",
    )(z, z, z, conv_w)


def _log_sigmoid(x):
    return jnp.minimum(x, 0.0) - jnp.log(1.0 + jnp.exp(-jnp.abs(x)))


N_CHAINS = 2 * NB


def _mlstm_scan_kernel(*refs):
    nc = N_CHAINS
    q_refs, k_refs, v_refs = refs[0:nc], refs[nc:2 * nc], refs[2 * nc:3 * nc]
    gc_refs, gr_refs = refs[3 * nc:4 * nc], refs[4 * nc:5 * nc]
    bc_ref, br_ref, o_ref, c_ref, n_ref, m_ref = refs[5 * nc:]

    @pl.when(pl.program_id(0) == 0)
    def _():
        c_ref[...] = jnp.zeros_like(c_ref)
        n_ref[...] = jnp.zeros_like(n_ref)
        m_ref[...] = jnp.zeros_like(m_ref)

    c = CHUNK
    ti = lax.broadcasted_iota(jnp.int32, (c, c), 0)
    si = lax.broadcasted_iota(jnp.int32, (c, c), 1)
    masks = (ti >= si, ti <= si)
    items = [(ci, h) for ci in range(nc) for h in range(ML_H)]
    sls = [slice(h * ML_D, (h + 1) * ML_D) for h in range(ML_H)]
    gcol = [gc_refs[ci][0, 0] + bc_ref[ci // NB] for ci in range(nc)]
    grow = [gr_refs[ci][0, 0] + br_ref[ci // NB] for ci in range(nc)]

    q = [q_refs[ci][:, sls[h]] for ci, h in items]
    k = [k_refs[ci][:, sls[h]] for ci, h in items]
    v = [v_refs[ci][:, sls[h]] for ci, h in items]
    qk = [_bdot_nt(a, b) for a, b in zip(q, k)]
    c_mat = [c_ref[ci, h] for ci, h in items]
    n_vec = [n_ref[ci, h] for ci, h in items]
    m_prev = [m_ref[ci, h][0:1, 0:1] for ci, h in items]
    qc = [_bdot_nt(a, b) for a, b in zip(q, c_mat)]
    qn = [jnp.sum(a * b, axis=1, keepdims=True) for a, b in zip(q, n_vec)]

    log_w, m_inter, cum_col, i_col, total = [], [], [], [], []
    for ci, h in items:
        mask = masks[ci // NB]
        mask_t = masks[1 - ci // NB]
        f_col = _log_sigmoid(gcol[ci][:, ML_H + h:ML_H + h + 1])
        f_row = _log_sigmoid(grow[ci][ML_H + h:ML_H + h + 1, :])
        cc = jnp.sum(jnp.where(mask, f_row, 0.0), axis=1, keepdims=True)
        cr = jnp.sum(jnp.where(mask_t, f_col, 0.0), axis=0, keepdims=True)
        log_w.append(jnp.where(mask, cc - cr + grow[ci][h:h + 1, :], ML_NEG))
        cum_col.append(cc)
        i_col.append(gcol[ci][:, h:h + 1])
        total.append(jnp.sum(f_row, axis=1, keepdims=True))
    m_inter = [a + b for a, b in zip(cum_col, m_prev)]
    m_t = [jnp.maximum(jnp.max(a, axis=1, keepdims=True), b) for a, b in zip(log_w, m_inter)]
    s = [a * jnp.exp(b - m) for a, b, m in zip(qk, log_w, m_t)]
    w_inter = [jnp.exp(a - m) for a, m in zip(m_inter, m_t)]
    sv = [_bdot(a, b) for a, b in zip(s, v)]
    for i, (ci, h) in enumerate(items):
        num = sv[i] + w_inter[i] * qc[i]
        den = jnp.sum(s[i], axis=1, keepdims=True) + w_inter[i] * qn[i]
        o_ref[ci // NB, ci % NB, 0, :, sls[h]] = num / jnp.maximum(jnp.abs(den), jnp.exp(-m_t[i]))
    log_src = [t - a + b for t, a, b in zip(total, cum_col, i_col)]
    m_new = [jnp.maximum(t + mp, jnp.max(ls, axis=0, keepdims=True)) for t, mp, ls in zip(total, m_prev, log_src)]
    src = [jnp.exp(ls - mn) for ls, mn in zip(log_src, m_new)]
    decay = [jnp.exp(t + mp - mn) for t, mp, mn in zip(total, m_prev, m_new)]
    vk = [_bdot_tn(a * sr, b) for a, sr, b in zip(v, src, k)]
    for i, (ci, h) in enumerate(items):
        c_ref[ci, h] = decay[i] * c_mat[i] + vk[i]
        n_ref[ci, h] = decay[i] * n_vec[i] + jnp.sum(src[i] * k[i], axis=0, keepdims=True)
        m_ref[ci, h] = jnp.broadcast_to(m_new[i], m_ref.shape[2:])


def _bwd_step(pos):
    return jnp.where(pos < CTX_CHUNKS, CTX_CHUNKS - 1 - pos, SEQ_CHUNKS - 1 + CTX_CHUNKS - pos)


def _mlstm_scan(qk, z, gcol, grow, bcol, brow):
    chains = [(d, b) for d in range(2) for b in range(NB)]

    def tok(col_blk):
        return [pl.BlockSpec((CHUNK, ML_W), lambda n, d=d, b=b: (_seq_row_block(b, d, n), col_blk))
                for d, b in chains]

    gc_specs = [pl.BlockSpec((1, 1, CHUNK, 2 * ML_H), lambda n, d=d, b=b: (d, _seq_row_block(b, d, n), 0, 0))
                for d, b in chains]
    gr_specs = [pl.BlockSpec((1, 1, 2 * ML_H, CHUNK), lambda n, d=d, b=b: (d, _seq_row_block(b, d, n), 0, 0))
                for d, b in chains]
    nc = N_CHAINS
    return pl.pallas_call(
        _mlstm_scan_kernel,
        grid=(SEQ_CHUNKS,),
        in_specs=tok(0) + tok(1) + tok(2) + gc_specs + gr_specs + [
            pl.BlockSpec((2, 1, 2 * ML_H), lambda n: (0, 0, 0)),
            pl.BlockSpec((2, 2 * ML_H, 1), lambda n: (0, 0, 0))],
        out_specs=pl.BlockSpec((2, NB, 1, CHUNK, ML_W), lambda n: (0, 0, n, 0, 0)),
        out_shape=jax.ShapeDtypeStruct((2, NB, SEQ_CHUNKS, CHUNK, ML_W), F32),
        scratch_shapes=[pltpu.VMEM((nc, ML_H, ML_D, ML_D), F32), pltpu.VMEM((nc, ML_H, 1, ML_D), F32),
                        pltpu.VMEM((nc, ML_H, 8, 128), F32)],
        compiler_params=_cp("arbitrary"),
        name="mlstm_scan",
    )(*([qk] * (2 * nc) + [z] * nc + [gcol] * nc + [grow] * nc + [bcol, brow]))


def _mlstm_out_kernel(hf_ref, hb_ref, og_ref, gng_ref, y_ref):
    n = TILE_CHUNKS
    for i in range(n):
        rows = slice(i * CHUNK, (i + 1) * CHUNK)
        hsum = _sigmoid(og_ref[rows, :]) * (hf_ref[0, 0, i] + hb_ref[0, 0, n - 1 - i])
        for h in range(ML_H):
            sl = slice(h * ML_D, (h + 1) * ML_D)
            x = hsum[:, sl]
            cen = x - jnp.mean(x, axis=-1, keepdims=True)
            var = jnp.mean(cen * cen, axis=-1, keepdims=True)
            y_ref[rows, sl] = (cen * lax.rsqrt(var + EPS) * gng_ref[:, sl]).astype(y_ref.dtype)


def _mlstm_out(hs, z, gn_g):
    blk = (1, 1, TILE_CHUNKS, CHUNK, ML_W)
    return pl.pallas_call(
        _mlstm_out_kernel,
        grid=(NB, SEQ_CHUNKS // TILE_CHUNKS),
        in_specs=[pl.BlockSpec(blk, lambda b, grp: (0, b, grp, 0, 0)),
                  pl.BlockSpec(blk, lambda b, grp: (1, b, _bwd_group(grp), 0, 0)),
                  pl.BlockSpec((TILE, ML_W), lambda b, grp: (_tile_of_group(b, grp), 3)),
                  pl.BlockSpec((1, ML_W), lambda b, grp: (0, 0))],
        out_specs=pl.BlockSpec((TILE, ML_W), lambda b, grp: (_tile_of_group(b, grp), 0)),
        out_shape=jax.ShapeDtypeStruct((ROWS, ML_W), BF16),
        compiler_params=_cp("parallel", "parallel"),
        name="mlstm_out",
    )(hs, hs, z, gn_g.reshape(1, ML_W))


def _mlstm(z_main, z_gates, conv_w, i_b, f_b, gn_g):
    qk = _mlstm_prep(z_main, conv_w)
    n_chunks = ROWS // CHUNK
    gates = z_gates[:, :4 * ML_H].reshape(n_chunks, CHUNK, 2, 2, ML_H)
    gcol = jnp.transpose(gates, (3, 0, 1, 2, 4)).reshape(2, n_chunks, CHUNK, 2 * ML_H)
    grow = jnp.swapaxes(gcol, 2, 3)
    bias = jnp.concatenate([i_b, f_b], axis=-1)
    hs = _mlstm_scan(qk, z_main, gcol, grow, bias.reshape(2, 1, 2 * ML_H), bias.reshape(2, 2 * ML_H, 1))
    return _mlstm_out(hs, z_main, gn_g)


def _s5_toeplitz_kernel(kf_ref, kb_ref, o_ref):
    kf = kf_ref[0]
    kb = kb_ref[0]
    lane = lax.broadcasted_iota(jnp.int32, kf.shape, 1)
    width = S5_L * S5_C
    for j in range(S5_L):
        f = kf if j == 0 else jnp.where(lane >= S5_C * j, pltpu.roll(kf, S5_C * j, 1), 0.0)
        back = S5_L - 1 - j
        b = kb if back == 0 else jnp.where(lane < S5_C * (j + 1), pltpu.roll(kb, width - S5_C * back, 1), 0.0)
        o_ref[0, 0, :, j] = f.reshape(S5_G, S5_C, width).astype(o_ref.dtype)
        o_ref[0, 1, :, j] = b.reshape(S5_G, S5_C, width).astype(o_ref.dtype)


def _s5_operators(lam_re, lam_im, log_step, b_re, b_im, c_re, c_im):
    nl = lam_re.shape[0]
    dt = jnp.exp(log_step)[..., None]
    mag = jnp.exp(lam_re * dt)
    a_re = mag * jnp.cos(lam_im * dt)
    a_im = mag * jnp.sin(lam_im * dt)
    den = lam_re * lam_re + lam_im * lam_im
    f_re = ((a_re - 1) * lam_re + a_im * lam_im) / den
    f_im = (a_im * lam_re - (a_re - 1) * lam_im) / den
    bb_re = f_re[..., None] * b_re - f_im[..., None] * b_im
    bb_im = f_re[..., None] * b_im + f_im[..., None] * b_re
    bt_re = jnp.swapaxes(bb_re, -1, -2)
    bt_im = jnp.swapaxes(bb_im, -1, -2)
    pr = [jnp.ones_like(a_re)]
    pi = [jnp.zeros_like(a_im)]
    for _ in range(S5_L):
        pr.append(pr[-1] * a_re - pi[-1] * a_im)
        pi.append(pr[-2] * a_im + pi[-1] * a_re)
    pr = jnp.stack(pr, axis=3)
    pi = jnp.stack(pi, axis=3)

    def times_b(qr, qi):
        qr, qi = qr[..., None, :], qi[..., None, :]
        br, bi = bt_re[:, :, :, None], bt_im[:, :, :, None]
        return qr * br - qi * bi, qr * bi + qi * br

    wr, wi = times_b(pr[:, :, :, :S5_L], pi[:, :, :, :S5_L])
    wr = jnp.swapaxes(wr, 3, 4)
    wi = jnp.swapaxes(wi, 3, 4)
    kern = (jnp.einsum("ldgktp,ldgcp->ldgktc", wr, c_re, precision=HP)
            - jnp.einsum("ldgktp,ldgcp->ldgktc", wi, c_im, precision=HP))
    kf = kern[:, 0].reshape(nl, S5_G * S5_C, S5_L * S5_C)
    kb = jnp.flip(kern[:, 1], axis=3).reshape(nl, S5_G * S5_C, S5_L * S5_C)
    rows_blk = pl.BlockSpec((1, S5_G * S5_C, S5_L * S5_C), lambda l: (l, 0, 0))
    m = pl.pallas_call(
        _s5_toeplitz_kernel,
        grid=(nl,),
        in_specs=[rows_blk, rows_blk],
        out_specs=pl.BlockSpec((1, 2, S5_G, S5_L, S5_C, S5_L * S5_C), lambda l: (l, 0, 0, 0, 0, 0)),
        out_shape=jax.ShapeDtypeStruct((nl, 2, S5_G, S5_L, S5_C, S5_L * S5_C), BF16),
        compiler_params=_cp("parallel"),
        name="s5_toeplitz",
    )(kf, kb).reshape(nl, 2, S5_G, S5_L * S5_C, S5_L * S5_C)

    def stack_dirs(fwd, bwd):
        return jnp.stack([fwd[:, 0], bwd[:, 1]], axis=1)

    er, ei = times_b(stack_dirs(jnp.flip(pr[:, :, :, :S5_L], axis=3), pr[:, :, :, :S5_L]),
                     stack_dirs(jnp.flip(pi[:, :, :, :S5_L], axis=3), pi[:, :, :, :S5_L]))
    e = jnp.concatenate([er, ei], axis=-1).reshape(nl, 2, S5_G, S5_L * S5_C, 2 * S5_P).astype(BF16)
    qr = stack_dirs(pr[:, :, :, 1:], jnp.flip(pr[:, :, :, 1:], axis=3))[..., None, :]
    qi = stack_dirs(pi[:, :, :, 1:], jnp.flip(pi[:, :, :, 1:], axis=3))[..., None, :]
    cr, ci = c_re[:, :, :, None], c_im[:, :, :, None]
    ft = jnp.concatenate([cr * qr - ci * qi, -(cr * qi + ci * qr)], axis=-1)
    ft = ft.reshape(nl, 2, S5_G, S5_L * S5_C, 2 * S5_P).astype(BF16)
    return m, e, ft, pr[:, :, :, S5_L], pi[:, :, :, S5_L]


def _s5_local_kernel(u_ref, m_ref, e_ref, y_ref, x_ref):
    u = u_ref[0]
    for d in range(2):
        y_ref[d, 0] = jnp.dot(u, m_ref[d, 0], preferred_element_type=F32)
        x_ref[d, 0] = jnp.dot(u, e_ref[d, 0], preferred_element_type=F32)


def _s5_carry_kernel(xc_ref, ar_ref, ai_ref, x0_ref, st_ref):
    @pl.when(pl.program_id(1) == 0)
    def _():
        st_ref[...] = jnp.zeros_like(st_ref)

    ar = ar_ref[0]
    ai = ai_ref[0]

    def run(order):
        x = st_ref[...]
        for i in order:
            x0_ref[0, :, :, i, :] = x
            x = x * ar + pltpu.roll(x, S5_P, 2) * ai + xc_ref[0, :, :, i, :]
        st_ref[...] = x

    n = xc_ref.shape[3]

    @pl.when(pl.program_id(0) == 0)
    def _():
        run(range(n))

    @pl.when(pl.program_id(0) == 1)
    def _():
        run(reversed(range(n)))


def _s5_state_kernel(y_ref, x0_ref, ft_ref, o_ref):
    acc = y_ref[0, 0] + y_ref[1, 0]
    for d in range(2):
        acc = acc + _bdot_nt(x0_ref[d, 0], ft_ref[d, 0])
    o_ref[0] = acc.astype(o_ref.dtype)


def _s5_glu_kernel(y_ref, u_ref, d_ref, w_ref, o_ref):
    x = y_ref[...] + u_ref[...] * d_ref[...]
    ge = 0.5 * x * (1.0 + jnp.tanh(math.sqrt(2.0 / math.pi) * (x + 0.044715 * (x * x * x))))
    p = _bdot(ge, w_ref[...])
    o_ref[...] = (p[:, :S5_W] * _sigmoid(p[:, S5_W:])).astype(o_ref.dtype)


def _s5(z, m, e, ft, a_re, a_im, d_skip, w_glu):
    n_ch = (SEQ + CTX) // S5_L
    n_ctx_ch = CTX // S5_L
    rows = NB * n_ch
    width = S5_L * S5_C

    def to_groups(x):
        x = x.reshape(NB, -1, S5_L, S5_G, S5_C)
        return jnp.transpose(x, (3, 0, 1, 2, 4)).reshape(S5_G, NB, -1, width)

    zb = z.astype(BF16)
    u = jnp.concatenate([to_groups(zb[N_LAT:]), to_groups(zb[:N_LAT])], axis=2).reshape(S5_G, rows, width)
    y_loc, x_in = pl.pallas_call(
        _s5_local_kernel,
        grid=(S5_G,),
        in_specs=[pl.BlockSpec((1, rows, width), lambda g: (g, 0, 0)),
                  pl.BlockSpec((2, 1, width, width), lambda g: (0, g, 0, 0)),
                  pl.BlockSpec((2, 1, width, 2 * S5_P), lambda g: (0, g, 0, 0))],
        out_specs=[pl.BlockSpec((2, 1, rows, width), lambda g: (0, g, 0, 0)),
                   pl.BlockSpec((2, 1, rows, 2 * S5_P), lambda g: (0, g, 0, 0))],
        out_shape=[jax.ShapeDtypeStruct((2, S5_G, rows, width), F32),
                   jax.ShapeDtypeStruct((2, S5_G, rows, 2 * S5_P), F32)],
        compiler_params=_cp("parallel"),
        name="s5_local",
    )(u, m, e)

    step = n_ctx_ch
    n_blk = n_ch // step
    coef_r = jnp.concatenate([a_re, a_re], axis=-1)[:, :, None, :]
    coef_i = jnp.concatenate([-a_im, a_im], axis=-1)[:, :, None, :]

    def chunk_blk(d, i):
        return jnp.where(d == 0, i, jnp.where(i == 0, 0, n_blk - i))

    st_blk = pl.BlockSpec((1, S5_G, NB, step, 2 * S5_P), lambda d, i: (d, 0, 0, chunk_blk(d, i), 0))
    coef_blk = pl.BlockSpec((1, S5_G, 1, 2 * S5_P), lambda d, i: (d, 0, 0, 0))
    x0 = pl.pallas_call(
        _s5_carry_kernel,
        grid=(2, n_blk),
        in_specs=[st_blk, coef_blk, coef_blk],
        out_specs=st_blk,
        out_shape=jax.ShapeDtypeStruct((2, S5_G, NB, n_ch, 2 * S5_P), F32),
        scratch_shapes=[pltpu.VMEM((S5_G, NB, 2 * S5_P), F32)],
        compiler_params=_cp("arbitrary", "arbitrary"),
        name="s5_carry",
    )(x_in.reshape(2, S5_G, NB, n_ch, 2 * S5_P), coef_r, coef_i).reshape(2, S5_G, rows, 2 * S5_P)

    y = pl.pallas_call(
        _s5_state_kernel,
        grid=(S5_G,),
        in_specs=[pl.BlockSpec((2, 1, rows, width), lambda g: (0, g, 0, 0)),
                  pl.BlockSpec((2, 1, rows, 2 * S5_P), lambda g: (0, g, 0, 0)),
                  pl.BlockSpec((2, 1, width, 2 * S5_P), lambda g: (0, g, 0, 0))],
        out_specs=pl.BlockSpec((1, rows, width), lambda g: (g, 0, 0)),
        out_shape=jax.ShapeDtypeStruct((S5_G, rows, width), BF16),
        compiler_params=_cp("parallel"),
        name="s5_state",
    )(y_loc, x0, ft)

    y = jnp.transpose(y.reshape(S5_G, NB, n_ch, S5_L, S5_C), (1, 2, 3, 0, 4)).reshape(NB, n_ch * S5_L, S5_W)
    y = jnp.concatenate([y[:, CTX:].reshape(N_LAT, S5_W), y[:, :CTX].reshape(N_CTX, S5_W)], axis=0)
    tok = pl.BlockSpec((TILE, S5_W), lambda t: (t, 0))
    return pl.pallas_call(
        _s5_glu_kernel,
        grid=(N_TILES,),
        in_specs=[tok, tok, pl.BlockSpec((1, S5_W), lambda t: (0, 0)),
                  pl.BlockSpec((S5_W, 2 * S5_W), lambda t: (0, 0))],
        out_specs=tok,
        out_shape=jax.ShapeDtypeStruct((ROWS, S5_W), BF16),
        compiler_params=_cp("parallel"),
        name="s5_glu",
    )(y, z, d_skip.reshape(1, S5_W), w_glu)


def kernel(x, c, ctx, c_ctx, w_mod, b_mod, norm1_g, norm2_g, w_in, b_gate, q_norm_g, k_norm_g, rwkv_mu, rwkv_w0, rwkv_w_up, rwkv_a0, rwkv_a_up, rwkv_g_up, rwkv_k_k, rwkv_k_a, rwkv_r_k, rwkv_gn_g, rwkv_gn_b, mlstm_conv_w, mlstm_i_b, mlstm_f_b, mlstm_gn_g, s5_lam_re, s5_lam_im, s5_log_step, s5_b_re, s5_b_im, s5_c_re, s5_c_im, s5_d, s5_w_glu, w_br_attn, w_br_rwkv, w_br_mlstm, w_br_s5, w_out, w_ffn_in, w_ffn_out, final_norm_g):
    cos, sin = _rope_tables()
    xs = jnp.concatenate([x.reshape(N_LAT, D), ctx.reshape(N_CTX, D)], axis=0)
    c_all = jnp.concatenate([c, c_ctx[None], jnp.zeros((3, D), F32)], axis=0)
    s5_m, s5_e, s5_ft, s5_ar, s5_ai = _s5_operators(s5_lam_re, s5_lam_im, s5_log_step, s5_b_re, s5_b_im,
                                                    s5_c_re, s5_c_im)
    tm = 1024
    for l in range(DEPTH):
        last = l == DEPTH - 1
        n_rows = N_LAT if last else ROWS
        mod = _modulation(c_all, w_mod, b_mod, l).reshape(8, 1, 6 * D)
        h = _norm_mod(xs, norm1_g[l], mod, 0, 1, N_TILES)
        w_attn, w_rwkv, w_ml, w_mlg, w_s5, w_gate = _w_in_split(w_in, l)
        z_attn = _mm(h, w_attn, tm, 512)
        z_rwkv = _mm(h, w_rwkv, tm, 896)
        z_ml = _mm(h, w_ml, tm, 1024)
        z_mlg = _mm(h, w_mlg, tm, 128)
        z_s5 = _mm(h, w_s5, tm, 512)
        gates = _mm_gate(h, w_gate, b_gate, l, tm, 1024, n_rows)

        ya = _attention(z_attn, q_norm_g[l], k_norm_g[l], cos, sin)
        rp = dict(mu=rwkv_mu[l], w0=rwkv_w0[l], w_up=rwkv_w_up[l], a0=rwkv_a0[l], a_up=rwkv_a_up[l],
                  g_up=rwkv_g_up[l], k_k=rwkv_k_k[l], k_a=rwkv_k_a[l], r_k=rwkv_r_k[l].reshape(RW_W))
        r, v, kk, g, bonus, lw, bmat, km = _rwkv_prep(z_rwkv, rp)
        yr = _rwkv_scan(r, v, kk, lw, bmat, km, bonus, g, rwkv_gn_g[l], rwkv_gn_b[l])
        ym = _mlstm(z_ml, z_mlg, mlstm_conv_w[l], mlstm_i_b[l], mlstm_f_b[l], mlstm_gn_g[l])
        ys = _s5(z_s5, s5_m[l], s5_e[l], s5_ft[l], s5_ar[l], s5_ai[l], s5_d[l], s5_w_glu[l])

        y = _merge(ya, yr, ym, ys, gates, w_br_attn, w_br_rwkv, w_br_mlstm, w_br_s5, l, tm, 512, n_rows)
        xs = _mm_res(y, w_out, l, xs, mod, 2, tm, 1024, n_rows)
        h2 = _norm_mod(xs, norm2_g[l], mod, 3, 4, n_rows // TILE)
        u = _ffn_in(h2, w_ffn_in, l, tm, 512, n_rows)
        xs = _mm_res(u, w_ffn_out, l, xs, mod, 5, 512, 512, n_rows)
    return _final_norm(xs, final_norm_g).reshape(NB, SEQ, D)
```

```python
import functools
import math

import numpy as np
import jax
import jax.numpy as jnp
from jax import lax
from jax.experimental import pallas as pl
from jax.experimental.pallas import tpu as pltpu

F32 = jnp.float32
BF16 = jnp.bfloat16
HP = lax.Precision.HIGHEST

D = 2048
NB = 4
SEQ = 2048
CTX = 256
DEPTH = 2
N_LAT = NB * SEQ
N_CTX = NB * CTX
ROWS = N_LAT + N_CTX
EPS = 1e-6
GRID_W = 64

HEAD_DIM = 128
ATTN_HEADS = 8
ATTN_KV = 2
ROPE_THETA = 10000.0
ATTN_Q = ATTN_HEADS * HEAD_DIM
ATTN_IN = (ATTN_HEADS + 2 * ATTN_KV) * HEAD_DIM

RW_H = 8
RW_D = 64
RW_W = 512
RW_IN = 3 * RW_W + 64 + 64 + 128
RW_DECAY = math.exp(-0.5)
RW_GN_EPS = 64e-5

ML_H = 4
ML_D = 128
ML_W = 512
ML_NEG = -1e30
ML_IN = 4 * ML_W + 4 * ML_H

S5_W = 512
S5_C = 16
S5_G = 32
S5_P = 64
S5_L = 16

FFN_H = 5632
GATE_IN = 4 * D

CHUNK = 64
TILE = 256
N_TILES = ROWS // TILE
SEQ_CHUNKS = (SEQ + CTX) // CHUNK
CTX_CHUNKS = CTX // CHUNK
LAT_CHUNKS = SEQ // CHUNK

VMEM_LIMIT_BYTES = 56 * 1024 * 1024


def _cp(*sem):
    return pltpu.CompilerParams(dimension_semantics=sem, vmem_limit_bytes=VMEM_LIMIT_BYTES)


def _bdot(a, b):
    return jnp.dot(a.astype(BF16), b.astype(BF16), preferred_element_type=F32)


def _bdot_nt(a, b):
    return lax.dot_general(a.astype(BF16), b.astype(BF16), (((1,), (1,)), ((), ())),
                           preferred_element_type=F32)


def _bdot_tn(a, b):
    return lax.dot_general(a.astype(BF16), b.astype(BF16), (((0,), (0,)), ((), ())),
                           preferred_element_type=F32)


def _hdot(a, b):
    return jnp.dot(a, b, precision=HP, preferred_element_type=F32)


def _sigmoid(x):
    return 1.0 / (1.0 + jnp.exp(-x))


def _silu(x):
    return x * _sigmoid(x)


def _mod_kernel(c_ref, w_ref, b_ref, o_ref):
    c_hi, c_lo = _split_bf16(_silu(c_ref[...]), 2)
    w_hi, w_lo = _split_bf16(w_ref[...], 2)
    rows = c_hi.shape[0]
    both = jnp.dot(jnp.concatenate([c_hi, c_lo], axis=0), w_hi, preferred_element_type=F32)
    o_ref[...] = both[:rows] + both[rows:] + jnp.dot(c_hi, w_lo, preferred_element_type=F32) + b_ref[...]


def _modulation(c_all, w, b, l):
    tn = 1024
    return pl.pallas_call(
        _mod_kernel,
        grid=(6 * D // tn,),
        in_specs=[pl.BlockSpec((8, D), lambda j: (0, 0)),
                  pl.BlockSpec((None, D, tn), lambda j: (l, 0, j)),
                  pl.BlockSpec((None, 1, tn), lambda j: (l, 0, j))],
        out_specs=pl.BlockSpec((8, tn), lambda j: (0, j)),
        out_shape=jax.ShapeDtypeStruct((8, 6 * D), F32),
        compiler_params=_cp("arbitrary"),
        name="modulation",
    )(c_all, w, b.reshape(DEPTH, 1, 6 * D))


def _mod_row(i, tm):
    return jnp.where(i * tm < N_LAT, (i * tm) // SEQ, NB)


def _norm_mod_kernel(x_ref, g_ref, sh_ref, sc_ref, o_ref):
    x = x_ref[...]
    y = x * lax.rsqrt(jnp.mean(x * x, axis=-1, keepdims=True) + EPS) * g_ref[...]
    o_ref[...] = (y * (1.0 + sc_ref[0]) + sh_ref[0]).astype(o_ref.dtype)


def _norm_mod(x, g, modr, shift_blk, scale_blk, n_tiles):
    return pl.pallas_call(
        _norm_mod_kernel,
        grid=(n_tiles,),
        in_specs=[pl.BlockSpec((TILE, D), lambda i: (i, 0)),
                  pl.BlockSpec((1, D), lambda i: (0, 0)),
                  pl.BlockSpec((1, 1, D), lambda i: (_mod_row(i, TILE), 0, shift_blk)),
                  pl.BlockSpec((1, 1, D), lambda i: (_mod_row(i, TILE), 0, scale_blk))],
        out_specs=pl.BlockSpec((TILE, D), lambda i: (i, 0)),
        out_shape=jax.ShapeDtypeStruct((n_tiles * TILE, D), BF16),
        compiler_params=_cp("parallel"),
        name="norm_mod",
    )(x, g.reshape(1, D), modr, modr)


W_IN_OFFSETS = tuple(int(v) for v in np.cumsum([0, ATTN_IN, RW_IN, 4 * ML_W, 4 * ML_H, S5_W, GATE_IN]))
ML_GATE_PAD = 128


W_TILE = 128


def _w_in_group_kernel(*refs, shift, valid):
    a_ref, o_ref = refs[0], refs[-1]
    x = a_ref[...]
    if shift:
        x = jnp.concatenate([x[shift:], refs[1][:shift, :]], axis=0)
    if valid < W_TILE:
        row = lax.broadcasted_iota(jnp.int32, x.shape, 0)
        x = jnp.where(row < valid, x, 0.0)
    o_ref[...] = x.T.astype(BF16)


def _w_in_group(wt, l, start, n_cols):
    a0, shift = divmod(start, W_TILE)
    n_tiles = -(-n_cols // W_TILE)
    last_blk = (wt.shape[1] - 1) // W_TILE
    in_specs = [pl.BlockSpec((None, W_TILE, D), lambda t: (l, a0 + t, 0))]
    if shift:
        in_specs.append(pl.BlockSpec((None, W_TILE, D), lambda t: (l, jnp.minimum(a0 + t + 1, last_blk), 0)))
    return pl.pallas_call(
        functools.partial(_w_in_group_kernel, shift=shift, valid=min(n_cols, W_TILE)),
        grid=(n_tiles,),
        in_specs=in_specs,
        out_specs=pl.BlockSpec((D, W_TILE), lambda t: (0, t)),
        out_shape=jax.ShapeDtypeStruct((D, n_tiles * W_TILE), BF16),
        compiler_params=_cp("parallel"),
        name="w_in_group",
    )(*([wt] * len(in_specs)))


def _w_in_split(w, l):
    wt = jnp.swapaxes(w, 1, 2)
    o = W_IN_OFFSETS
    return tuple(_w_in_group(wt, l, o[i], o[i + 1] - o[i]) for i in range(6))


def _mm_kernel(a_ref, w_ref, o_ref):
    o_ref[...] = jnp.dot(a_ref[...], w_ref[...], preferred_element_type=F32).astype(o_ref.dtype)


def _mm(a, w, tm, tn, out_dtype=F32):
    m, k = a.shape
    n = w.shape[1]
    return pl.pallas_call(
        _mm_kernel,
        grid=(n // tn, m // tm),
        in_specs=[pl.BlockSpec((tm, k), lambda j, i: (i, 0)),
                  pl.BlockSpec((k, tn), lambda j, i: (0, j))],
        out_specs=pl.BlockSpec((tm, tn), lambda j, i: (i, j)),
        out_shape=jax.ShapeDtypeStruct((m, n), out_dtype),
        compiler_params=_cp("parallel", "parallel"),
        name="matmul",
    )(a, w)


ROW_SUB = 256


def _row_blocks(ref):
    return [slice(r, r + ROW_SUB) for r in range(0, ref.shape[0], ROW_SUB)]


def _mm_gate_kernel(a_ref, w_ref, b_ref, o_ref):
    for rows in _row_blocks(o_ref):
        z = jnp.dot(a_ref[rows, :], w_ref[...], preferred_element_type=F32)
        o_ref[rows, :] = _sigmoid(z + b_ref[...]).astype(o_ref.dtype)


def _mm_gate(a, w, b_gate, l, tm, tn, m):
    k = a.shape[1]
    n = w.shape[1]
    return pl.pallas_call(
        _mm_gate_kernel,
        grid=(n // tn, m // tm),
        in_specs=[pl.BlockSpec((tm, k), lambda j, i: (i, 0)),
                  pl.BlockSpec((k, tn), lambda j, i: (0, j)),
                  pl.BlockSpec((None, 1, tn), lambda j, i: (l, 0, j))],
        out_specs=pl.BlockSpec((tm, tn), lambda j, i: (i, j)),
        out_shape=jax.ShapeDtypeStruct((m, n), BF16),
        compiler_params=_cp("parallel", "parallel"),
        name="matmul_gate",
    )(a, w, b_gate.reshape(DEPTH, 1, GATE_IN))


def _mm_res_kernel(a_ref, w_ref, x_ref, g_ref, o_ref, wb_ref):
    @pl.when(pl.program_id(1) == 0)
    def _():
        wb_ref[...] = w_ref[...].astype(BF16)

    for rows in _row_blocks(o_ref):
        y = jnp.dot(a_ref[rows, :], wb_ref[...], preferred_element_type=F32)
        o_ref[rows, :] = x_ref[rows, :] + g_ref[0] * y


def _mm_res(a, w, l, x, mod, gate_blk, tm, tn, n_rows):
    k = a.shape[1]
    n = w.shape[2]
    gpb = D // tn
    return pl.pallas_call(
        _mm_res_kernel,
        grid=(n // tn, n_rows // tm),
        in_specs=[pl.BlockSpec((tm, k), lambda j, i: (i, 0)),
                  pl.BlockSpec((None, k, tn), lambda j, i: (l, 0, j)),
                  pl.BlockSpec((tm, tn), lambda j, i: (i, j)),
                  pl.BlockSpec((1, 1, tn), lambda j, i: (_mod_row(i, tm), 0, gate_blk * gpb + j))],
        out_specs=pl.BlockSpec((tm, tn), lambda j, i: (i, j)),
        out_shape=jax.ShapeDtypeStruct((n_rows, n), F32),
        scratch_shapes=[pltpu.VMEM((k, tn), BF16)],
        compiler_params=_cp("arbitrary", "arbitrary"),
        name="matmul_residual",
    )(a, w, x, mod)


def _ffn_in_kernel(a_ref, wa_ref, wb_ref, o_ref, wab_ref, wbb_ref):
    @pl.when(pl.program_id(1) == 0)
    def _():
        wab_ref[...] = wa_ref[...].astype(BF16)
        wbb_ref[...] = wb_ref[...].astype(BF16)

    for rows in _row_blocks(o_ref):
        a = a_ref[rows, :]
        u = jnp.dot(a, wab_ref[...], preferred_element_type=F32)
        v = jnp.dot(a, wbb_ref[...], preferred_element_type=F32)
        o_ref[rows, :] = (_silu(u) * v).astype(o_ref.dtype)


def _ffn_in(h, w, l, tm, tn, n_rows):
    nb = FFN_H // tn
    return pl.pallas_call(
        _ffn_in_kernel,
        grid=(nb, n_rows // tm),
        in_specs=[pl.BlockSpec((tm, D), lambda j, i: (i, 0)),
                  pl.BlockSpec((None, D, tn), lambda j, i: (l, 0, j)),
                  pl.BlockSpec((None, D, tn), lambda j, i: (l, 0, nb + j))],
        out_specs=pl.BlockSpec((tm, tn), lambda j, i: (i, j)),
        out_shape=jax.ShapeDtypeStruct((n_rows, FFN_H), BF16),
        scratch_shapes=[pltpu.VMEM((D, tn), BF16), pltpu.VMEM((D, tn), BF16)],
        compiler_params=_cp("arbitrary", "arbitrary"),
        name="ffn_in",
    )(h, w, w)


def _merge_kernel(ya_ref, yr_ref, ym_ref, ys_ref, ga_ref, gr_ref, gm_ref, gs_ref,
                  wa_ref, wr_ref, wm_ref, ws_ref, o_ref, wab_ref, wrb_ref, wmb_ref, wsb_ref):
    @pl.when(pl.program_id(1) == 0)
    def _():
        wab_ref[...] = wa_ref[...].astype(BF16)
        wrb_ref[...] = wr_ref[...].astype(BF16)
        wmb_ref[...] = wm_ref[...].astype(BF16)
        wsb_ref[...] = ws_ref[...].astype(BF16)

    for rows in _row_blocks(o_ref):
        acc = None
        for y_ref, g_ref, w_ref in ((ya_ref, ga_ref, wab_ref), (yr_ref, gr_ref, wrb_ref),
                                    (ym_ref, gm_ref, wmb_ref), (ys_ref, gs_ref, wsb_ref)):
            term = g_ref[rows, :].astype(F32) * jnp.dot(y_ref[rows, :], w_ref[...], preferred_element_type=F32)
            acc = term if acc is None else acc + term
        o_ref[rows, :] = acc.astype(o_ref.dtype)


def _merge(ya, yr, ym, ys, gates, wa, wr, wm, ws, l, tm, tn, n_rows):
    nb = D // tn

    def act(width):
        return pl.BlockSpec((tm, width), lambda j, i: (i, 0))

    def gate(br):
        return pl.BlockSpec((tm, tn), lambda j, i: (i, br * nb + j))

    def wgt(width):
        return pl.BlockSpec((None, width, tn), lambda j, i: (l, 0, j))

    return pl.pallas_call(
        _merge_kernel,
        grid=(nb, n_rows // tm),
        in_specs=[act(ATTN_Q), act(RW_W), act(ML_W), act(S5_W),
                  gate(0), gate(1), gate(2), gate(3),
                  wgt(ATTN_Q), wgt(RW_W), wgt(ML_W), wgt(S5_W)],
        out_specs=pl.BlockSpec((tm, tn), lambda j, i: (i, j)),
        out_shape=jax.ShapeDtypeStruct((n_rows, D), BF16),
        scratch_shapes=[pltpu.VMEM((ATTN_Q, tn), BF16), pltpu.VMEM((RW_W, tn), BF16),
                        pltpu.VMEM((ML_W, tn), BF16), pltpu.VMEM((S5_W, tn), BF16)],
        compiler_params=_cp("arbitrary", "arbitrary"),
        name="gated_merge",
    )(ya, yr, ym, ys, gates, gates, gates, gates, wa, wr, wm, ws)


def _final_norm_kernel(x_ref, g_ref, o_ref):
    x = x_ref[...]
    o_ref[...] = x * lax.rsqrt(jnp.mean(x * x, axis=-1, keepdims=True) + EPS) * g_ref[...]


def _final_norm(x, g):
    n_tiles = N_LAT // TILE
    return pl.pallas_call(
        _final_norm_kernel,
        grid=(n_tiles,),
        in_specs=[pl.BlockSpec((TILE, D), lambda i: (i, 0)),
                  pl.BlockSpec((1, D), lambda i: (0, 0))],
        out_specs=pl.BlockSpec((TILE, D), lambda i: (i, 0)),
        out_shape=jax.ShapeDtypeStruct((N_LAT, D), F32),
        compiler_params=_cp("parallel"),
        name="final_norm",
    )(x, g.reshape(1, D))


def _tile_halo_specs(width, col_blk=0):
    last = ROWS // 8 - 1
    per = TILE // 8
    return [pl.BlockSpec((TILE, width), lambda t: (t, col_blk)),
            pl.BlockSpec((8, width), lambda t: (jnp.maximum(t * per - 1, 0), col_blk)),
            pl.BlockSpec((8, width), lambda t: (jnp.minimum((t + 1) * per, last), col_blk))]


def _neighbours(z, prev_blk, next_blk):
    t = pl.program_id(0)
    pos = t % (SEQ // TILE)
    is_lat = t < N_LAT // TILE
    has_prev = jnp.logical_and(is_lat, pos > 0).astype(F32)
    has_next = jnp.logical_and(is_lat, pos < SEQ // TILE - 1).astype(F32)
    row = lax.broadcasted_iota(jnp.int32, z.shape, 0)
    zp = jnp.where(row == 0, prev_blk[7:8, :] * has_prev, pltpu.roll(z, 1, 0))
    zn = jnp.where(row == TILE - 1, next_blk[0:1, :] * has_next, pltpu.roll(z, TILE - 1, 0))
    return zp, zn


def _seq_row_block(b, d, n):
    ctx_c = jnp.where(d == 0, n, CTX_CHUNKS - 1 - n)
    lat_c = jnp.where(d == 0, n - CTX_CHUNKS, SEQ_CHUNKS - 1 - n)
    return jnp.where(n < CTX_CHUNKS, N_LAT // CHUNK + CTX_CHUNKS * b + ctx_c, LAT_CHUNKS * b + lat_c)


def _rope(x, cos, sin):
    lane = lax.broadcasted_iota(jnp.int32, x.shape, 1)
    first = (lane % 64) < 32
    partner = jnp.where(first, pltpu.roll(x, 96, 1), pltpu.roll(x, 32, 1))
    return x * cos + partner * sin


def _rms(x, g):
    return x * lax.rsqrt(jnp.mean(x * x, axis=-1, keepdims=True) + EPS) * g


def _attn_kernel(q_ref, kl_ref, kc_ref, vl_ref, vc_ref, cos_ref, sin_ref, cos_t_ref, sin_t_ref,
                 qg_ref, kg_ref, o_ref, klb_ref, kcb_ref, vlb_ref, vcb_ref):
    qi = pl.program_id(2)
    n_lat_tiles = SEQ // TILE

    @pl.when(qi == 0)
    def _():
        kg = kg_ref[...]
        klb_ref[...] = _rope(_rms(kl_ref[...], kg), cos_ref[...], sin_ref[...]).astype(BF16)
        kcb_ref[...] = _rms(kc_ref[...], kg).astype(BF16)
        vlb_ref[...] = vl_ref[...].astype(BF16)
        vcb_ref[...] = vc_ref[...].astype(BF16)

    scale = HEAD_DIM ** -0.5 * math.log2(math.e)
    nt = (((1,), (1,)), ((), ()))

    def heads(latent):
        sls = [slice(h * HEAD_DIM, (h + 1) * HEAD_DIM) for h in range(ATTN_HEADS // ATTN_KV)]
        q = [_rms(q_ref[:, sl], qg_ref[...]) for sl in sls]
        if latent:
            q = [_rope(x, cos_t_ref[...], sin_t_ref[...]) for x in q]
        q = [(x * scale).astype(BF16) for x in q]
        s_c = [lax.dot_general(x, kcb_ref[...], nt, preferred_element_type=F32) for x in q]
        m = [jnp.max(x, axis=-1, keepdims=True) for x in s_c]
        if latent:
            s_l = [lax.dot_general(x, klb_ref[...], nt, preferred_element_type=F32) for x in q]
            m = [jnp.maximum(a, jnp.max(x, axis=-1, keepdims=True)) for a, x in zip(m, s_l)]
        p_c = [jnp.exp2(x - a) for x, a in zip(s_c, m)]
        den = [jnp.sum(x, axis=-1, keepdims=True) for x in p_c]
        acc = [jnp.dot(x.astype(BF16), vcb_ref[...], preferred_element_type=F32) for x in p_c]
        if latent:
            p_l = [jnp.exp2(x - a) for x, a in zip(s_l, m)]
            den = [a + jnp.sum(x, axis=-1, keepdims=True) for a, x in zip(den, p_l)]
            acc = [a + jnp.dot(x.astype(BF16), vlb_ref[...], preferred_element_type=F32) for a, x in zip(acc, p_l)]
        for sl, a, dn in zip(sls, acc, den):
            o_ref[:, sl] = (a / dn).astype(o_ref.dtype)

    @pl.when(qi < n_lat_tiles)
    def _():
        heads(True)

    @pl.when(qi == n_lat_tiles)
    def _():
        heads(False)


def _rope_tables():
    rows = SEQ // GRID_W
    row = jnp.repeat(jnp.arange(rows, dtype=F32), GRID_W)
    col = jnp.tile(jnp.arange(GRID_W, dtype=F32), rows)
    axis_dim = HEAD_DIM // 2
    inv_freq = ROPE_THETA ** (-jnp.arange(0, axis_dim, 2, dtype=F32) / axis_dim)
    ang_r = row[:, None] * inv_freq[None]
    ang_c = col[:, None] * inv_freq[None]
    cos = jnp.concatenate([jnp.cos(ang_r), jnp.cos(ang_r), jnp.cos(ang_c), jnp.cos(ang_c)], axis=-1)
    sin = jnp.concatenate([-jnp.sin(ang_r), jnp.sin(ang_r), -jnp.sin(ang_c), jnp.sin(ang_c)], axis=-1)
    return cos, sin


def _attention(z, q_g, k_g, cos, sin):
    n_lat_tiles = SEQ // TILE
    qw = ATTN_Q // ATTN_KV
    kcol = ATTN_Q // HEAD_DIM
    vcol = kcol + ATTN_KV

    def q_row(b, g, qi):
        return jnp.where(qi < n_lat_tiles, n_lat_tiles * b + qi, N_LAT // TILE + b)

    return pl.pallas_call(
        _attn_kernel,
        grid=(NB, ATTN_KV, n_lat_tiles + 1),
        in_specs=[pl.BlockSpec((TILE, qw), lambda b, g, qi: (q_row(b, g, qi), g)),
                  pl.BlockSpec((SEQ, HEAD_DIM), lambda b, g, qi: (b, kcol + g)),
                  pl.BlockSpec((CTX, HEAD_DIM), lambda b, g, qi: (N_LAT // CTX + b, kcol + g)),
                  pl.BlockSpec((SEQ, HEAD_DIM), lambda b, g, qi: (b, vcol + g)),
                  pl.BlockSpec((CTX, HEAD_DIM), lambda b, g, qi: (N_LAT // CTX + b, vcol + g)),
                  pl.BlockSpec((SEQ, HEAD_DIM), lambda b, g, qi: (0, 0)),
                  pl.BlockSpec((SEQ, HEAD_DIM), lambda b, g, qi: (0, 0)),
                  pl.BlockSpec((TILE, HEAD_DIM), lambda b, g, qi: (jnp.minimum(qi, n_lat_tiles - 1), 0)),
                  pl.BlockSpec((TILE, HEAD_DIM), lambda b, g, qi: (jnp.minimum(qi, n_lat_tiles - 1), 0)),
                  pl.BlockSpec((1, HEAD_DIM), lambda b, g, qi: (0, 0)),
                  pl.BlockSpec((1, HEAD_DIM), lambda b, g, qi: (0, 0))],
        out_specs=pl.BlockSpec((TILE, qw), lambda b, g, qi: (q_row(b, g, qi), g)),
        out_shape=jax.ShapeDtypeStruct((ROWS, ATTN_Q), BF16),
        scratch_shapes=[pltpu.VMEM((SEQ, HEAD_DIM), BF16), pltpu.VMEM((CTX, HEAD_DIM), BF16),
                        pltpu.VMEM((SEQ, HEAD_DIM), BF16), pltpu.VMEM((CTX, HEAD_DIM), BF16)],
        compiler_params=_cp("arbitrary", "arbitrary", "arbitrary"),
        name="attention",
    )(z, z, z, z, z, cos, sin, cos, sin, q_g.reshape(1, HEAD_DIM), k_g.reshape(1, HEAD_DIM))


def _rwkv_prep_kernel(z_ref, zp_ref, zn_ref, mu_ref, w0_ref, wup_ref, a0_ref, aup_ref, gup_ref,
                      kk_ref, ka_ref, rk_ref, bd_ref,
                      r_out, v_out, kkn_out, g_out, bonus_out, lw_out, b_out, km_out):
    z = z_ref[...]
    zp, zn = _neighbours(z, zp_ref[...], zn_ref[...])
    zs = z + mu_ref[...] * (0.5 * (zp + zn) - z)
    r = zs[:, 0:RW_W]
    k = zs[:, RW_W:2 * RW_W]
    v = zs[:, 2 * RW_W:3 * RW_W]
    w_lo = zs[:, 3 * RW_W:3 * RW_W + 64]
    a_lo = zs[:, 3 * RW_W + 64:3 * RW_W + 128]
    g_lo = zs[:, 3 * RW_W + 128:3 * RW_W + 256]
    bd = bd_ref[...]
    kk = k * kk_ref[...]
    kk = kk * lax.rsqrt(_dot_rhs_exact(kk * kk, bd) + 1e-12)
    r_out[...] = r
    v_out[...] = v
    kkn_out[...] = kk
    g_out[...] = _bdot(_sigmoid(g_lo), gup_ref[...])
    tw = jnp.tanh(w_lo)
    km_sum = None
    for d in range(2):
        lw = -RW_DECAY * _sigmoid(w0_ref[d] + _bdot(tw, wup_ref[d]))
        a = _sigmoid(a0_ref[d] + _bdot(a_lo, aup_ref[d]))
        km = k * (1.0 + (a - 1.0) * ka_ref[...])
        lw_out[d] = lw
        b_out[d] = a * kk
        km_out[d] = km
        km_sum = km if km_sum is None else km_sum + km
    bonus_out[...] = _dot_rhs_exact(r * km_sum * rk_ref[...], bd) * v


def _rwkv_prep(z, p):
    row = lambda a: a.reshape(1, -1)
    full = lambda shape: pl.BlockSpec(shape, lambda t: (0,) * len(shape))
    out_tok = pl.BlockSpec((TILE, RW_W), lambda t: (t, 0))
    out_dir = pl.BlockSpec((2, TILE, RW_W), lambda t: (0, t, 0))
    tok = jax.ShapeDtypeStruct((ROWS, RW_W), F32)
    drn = jax.ShapeDtypeStruct((2, ROWS, RW_W), F32)
    return pl.pallas_call(
        _rwkv_prep_kernel,
        grid=(N_TILES,),
        in_specs=_tile_halo_specs(RW_IN) + [
            full((1, RW_IN)), full((2, 1, RW_W)), full((2, 64, RW_W)), full((2, 1, RW_W)),
            full((2, 64, RW_W)), full((128, RW_W)), full((1, RW_W)), full((1, RW_W)), full((1, RW_W)),
            full((RW_W, RW_W))],
        out_specs=[out_tok] * 5 + [out_dir] * 3,
        out_shape=[tok] * 5 + [drn] * 3,
        compiler_params=_cp("parallel"),
        name="rwkv_prep",
    )(z, z, z, row(p["mu"]), p["w0"].reshape(2, 1, RW_W), p["w_up"], p["a0"].reshape(2, 1, RW_W),
      p["a_up"], p["g_up"], row(p["k_k"]), row(p["k_a"]), row(p["r_k"]),
      _head_block_ones(RW_W, RW_D).astype(BF16))


def _head_block_ones(width, head):
    idx = np.arange(width) // head
    return jnp.asarray((idx[:, None] == idx[None, :]).astype(np.float32))


def _split_bf16(x, pieces):
    out = []
    for _ in range(pieces):
        p = x.astype(BF16)
        out.append(p)
        x = x - p.astype(F32)
    return out


def _dot_rhs_exact(x, m, pieces=2):
    return sum(jnp.dot(p, m, preferred_element_type=F32) for p in _split_bf16(x, pieces))


def _dot_lhs_exact(m, x, pieces=3):
    return sum(jnp.dot(m, p, preferred_element_type=F32) for p in _split_bf16(x, pieces))


RW_SUB = 2


def _rwkv_chunk_kernel(r_ref, v_ref, kk_ref, lw_ref, b_ref, km_ref, p_ref, sl_ref, re_ref, ol_ref):
    d = pl.program_id(0)
    c = CHUNK
    hd = RW_D
    ti = lax.broadcasted_iota(jnp.int32, (c, c), 0)
    si = lax.broadcasted_iota(jnp.int32, (c, c), 1)
    delta = (ti - si) * (1 - 2 * d)
    incl = delta >= 0
    strict = delta > 0
    eye = jnp.where(ti == si, 1.0, 0.0)
    tri = jnp.where(incl, 1.0, 0.0).astype(BF16)
    gr = lax.broadcasted_iota(jnp.int32, (2 * c, c), 0)
    gc = lax.broadcasted_iota(jnp.int32, (2 * c, c), 1)
    gmask = (gr % c - gc) * (1 - 2 * d) >= jnp.where(gr < c, 1, 0)

    items = []
    for sub in range(RW_SUB):
        rows = slice(sub * c, (sub + 1) * c)
        lw = lw_ref[0, rows, :]
        cs = _dot_lhs_exact(tri, lw)
        tot = jnp.sum(lw, axis=0, keepdims=True)
        r = r_ref[rows, :]
        v = v_ref[rows, :]
        kk = kk_ref[rows, :]
        bb = b_ref[0, rows, :]
        km = km_ref[0, rows, :]
        e_neg = jnp.exp(-cs)
        e_rem = jnp.exp(tot - cs)
        kkt = kk * jnp.exp(cs - lw)
        rt = r * jnp.exp(cs)
        bt = bb * e_neg
        kt = km * e_neg
        bh = bb * e_rem
        kh = km * e_rem
        e_tot = jnp.exp(tot)
        for h in range(RW_H):
            s = slice(h * hd, (h + 1) * hd)
            items.append(dict(sub=sub, s=s, kkt=kkt[:, s], rt=rt[:, s], bt=bt[:, s], kt=kt[:, s], v=v[:, s],
                              bh=bh[:, s], kh=kh[:, s], e_tot=e_tot[:, s]))

    kr = [jnp.concatenate([it["kkt"], it["rt"]], axis=0) for it in items]
    gb = [jnp.where(gmask, _bdot_nt(x, it["bt"]), 0.0) for x, it in zip(kr, items)]
    gk = [jnp.where(gmask, _bdot_nt(x, it["kt"]), 0.0) for x, it in zip(kr, items)]
    l_b = [x[:c] for x in gb]
    a_b = [x[c:] for x in gb]
    lkv_akv = [_bdot(x, it["v"]) for x, it in zip(gk, items)]
    pw = [_bdot(x, x) for x in l_b]
    inv = [eye - x for x in l_b]
    for _ in range(int(math.log2(c)) - 2):
        res = [_bdot(jnp.concatenate([p, i], axis=0), p) for p, i in zip(pw, inv)]
        pw = [x[:c] for x in res]
        inv = [i + x[c:] for i, x in zip(inv, res)]
    inv = [i + _bdot(i, p) for p, i in zip(pw, inv)]
    w = [_bdot(i, it["kkt"]) for i, it in zip(inv, items)]
    y_loc = [_bdot(i, x[:c]) for i, x in zip(inv, lkv_akv)]
    ab_w = [_bdot(a, x) for a, x in zip(a_b, w)]
    ab_y = [_bdot(a, x) for a, x in zip(a_b, y_loc)]
    vk = [_bdot_tn(it["v"], it["kh"]) for it in items]
    yb = [_bdot_tn(x, it["bh"]) for x, it in zip(y_loc, items)]
    wb = [_bdot_tn(x, it["bh"]) for x, it in zip(w, items)]
    for n, it in enumerate(items):
        s = it["s"]
        j = jnp.where(d == 0, it["sub"], RW_SUB - 1 - it["sub"])
        re_ref[0, 0, j, :, s] = it["rt"] - ab_w[n]
        ol_ref[0, 0, j, :, s] = lkv_akv[n][c:] - ab_y[n]
        sl_ref[0, 0, j, :, s] = vk[n] - yb[n]
        p_ref[0, 0, j, :, s] = eye * it["e_tot"] - wb[n]


def _rwkv_carry_kernel(p_ref, sl_ref, s_out_ref, st_ref):
    @pl.when(pl.program_id(0) == 0)
    def _():
        st_ref[...] = jnp.zeros_like(st_ref)

    for d in range(2):
        for b in range(NB):
            s = st_ref[d, b]
            s_out_ref[d, b, 0] = s
            for h in range(RW_H):
                sl = slice(h * RW_D, (h + 1) * RW_D)
                st_ref[d, b, :, sl] = _hdot(s[:, sl], p_ref[d, b, 0, :, sl]) + sl_ref[d, b, 0, :, sl]


TILE_CHUNKS = TILE // CHUNK


def _tile_of_group(b, grp):
    return jnp.where(grp == 0, N_LAT // TILE + b, (SEQ // TILE) * b + grp - 1)


def _bwd_group(grp):
    return jnp.where(grp == 0, 0, SEQ_CHUNKS // TILE_CHUNKS - grp)


def _rwkv_out_kernel(olf_ref, ref_ref, sf_ref, olb_ref, reb_ref, sb_ref, bonus_ref, g_ref, gng_ref, gnb_ref,
                     bd_ref, y_ref, o_scr):
    n = TILE_CHUNKS
    items = [(i, slice(h * RW_D, (h + 1) * RW_D)) for i in range(n) for h in range(RW_H)]
    pf = [_bdot_nt(ref_ref[0, 0, i, :, s], sf_ref[0, 0, i, :, s]) for i, s in items]
    pb = [_bdot_nt(reb_ref[0, 0, n - 1 - i, :, s], sb_ref[0, 0, n - 1 - i, :, s]) for i, s in items]
    for (i, s), a, b in zip(items, pf, pb):
        o_scr[i * CHUNK:(i + 1) * CHUNK, s] = olf_ref[0, 0, i, :, s] + olb_ref[0, 0, n - 1 - i, :, s] + a + b
    o = o_scr[...] + bonus_ref[...]
    bd = bd_ref[...]
    cen = o - _dot_rhs_exact(o, bd) * (1.0 / RW_D)
    var = _dot_rhs_exact(cen * cen, bd) * (1.0 / RW_D)
    y = cen * lax.rsqrt(var + RW_GN_EPS) * gng_ref[...] + gnb_ref[...]
    y_ref[...] = (y * g_ref[...]).astype(y_ref.dtype)


def _rwkv_scan(r, v, kk, lw, bmat, km, bonus, g, gn_g, gn_b):
    sub_rows = RW_SUB * CHUNK
    lat_blocks = N_LAT // sub_rows
    lat_per_b = SEQ // sub_rows
    ctx_per_b = CTX // sub_rows

    def step_block(d, rb):
        is_lat = rb < lat_blocks
        b = jnp.where(is_lat, rb // lat_per_b, (rb - lat_blocks) // ctx_per_b)
        i = jnp.where(is_lat, rb % lat_per_b, (rb - lat_blocks) % ctx_per_b)
        fwd = jnp.where(is_lat, ctx_per_b + i, i)
        bwd = jnp.where(is_lat, ctx_per_b + lat_per_b - 1 - i, ctx_per_b - 1 - i)
        return b, jnp.where(d == 0, fwd, bwd)

    tok = pl.BlockSpec((sub_rows, RW_W), lambda d, rb: (rb, 0))
    drn = pl.BlockSpec((1, sub_rows, RW_W), lambda d, rb: (d, rb, 0))
    step_shape = jax.ShapeDtypeStruct((2, NB, SEQ_CHUNKS, CHUNK, RW_W), F32)
    step_blk = pl.BlockSpec((1, 1, RW_SUB, CHUNK, RW_W), lambda d, rb: (d,) + step_block(d, rb) + (0, 0))
    p, s_loc, r_eff, o_loc = pl.pallas_call(
        _rwkv_chunk_kernel,
        grid=(2, ROWS // sub_rows),
        in_specs=[tok, tok, tok, drn, drn, drn],
        out_specs=[step_blk] * 4,
        out_shape=[step_shape] * 4,
        compiler_params=_cp("parallel", "parallel"),
        name="rwkv_chunk",
    )(r, v, kk, lw, bmat, km)

    all_blk = pl.BlockSpec((2, NB, 1, CHUNK, RW_W), lambda n: (0, 0, n, 0, 0))
    s_in = pl.pallas_call(
        _rwkv_carry_kernel,
        grid=(SEQ_CHUNKS,),
        in_specs=[all_blk, all_blk],
        out_specs=all_blk,
        out_shape=step_shape,
        scratch_shapes=[pltpu.VMEM((2, NB, CHUNK, RW_W), F32)],
        compiler_params=_cp("arbitrary"),
        name="rwkv_carry",
    )(p, s_loc)

    fwd = pl.BlockSpec((1, 1, TILE_CHUNKS, CHUNK, RW_W), lambda b, grp: (0, b, grp, 0, 0))
    bwd = pl.BlockSpec((1, 1, TILE_CHUNKS, CHUNK, RW_W), lambda b, grp: (1, b, _bwd_group(grp), 0, 0))
    rows = pl.BlockSpec((TILE, RW_W), lambda b, grp: (_tile_of_group(b, grp), 0))
    vec = pl.BlockSpec((1, RW_W), lambda b, grp: (0, 0))
    return pl.pallas_call(
        _rwkv_out_kernel,
        grid=(NB, SEQ_CHUNKS // TILE_CHUNKS),
        in_specs=[fwd, fwd, fwd, bwd, bwd, bwd, rows, rows, vec, vec,
                  pl.BlockSpec((RW_W, RW_W), lambda b, grp: (0, 0))],
        out_specs=rows,
        out_shape=jax.ShapeDtypeStruct((ROWS, RW_W), BF16),
        scratch_shapes=[pltpu.VMEM((TILE, RW_W), F32)],
        compiler_params=_cp("parallel", "parallel"),
        name="rwkv_out",
    )(o_loc, r_eff, s_in, o_loc, r_eff, s_in, bonus, g, gn_g.reshape(1, RW_W), gn_b.reshape(1, RW_W),
      _head_block_ones(RW_W, RW_D).astype(BF16))


def _mlstm_prep_kernel(z_ref, zp_ref, zn_ref, w_ref, o_ref):
    z = z_ref[...]
    zp, zn = _neighbours(z, zp_ref[...], zn_ref[...])
    y = _silu(zp * w_ref[0:1, :] + z * w_ref[1:2, :] + zn * w_ref[2:3, :])
    col = lax.broadcasted_iota(jnp.int32, y.shape, 1)
    o_ref[...] = jnp.where(col >= ML_W, y * (ML_D ** -0.5), y)


def _mlstm_prep(z, conv_w):
    return pl.pallas_call(
        _mlstm_prep_kernel,
        grid=(N_TILES,),
        in_specs=_tile_halo_specs(2 * ML_W) + [pl.BlockSpec((3, 2 * ML_W), lambda t: (0, 0))],
        out_specs=pl.BlockSpec((TILE, 2 * ML_W), lambda t: (t, 0)),
        out_shape=jax.ShapeDtypeStruct((ROWS, 2 * ML_W), F32),
        compiler_params=_cp("parallel"),
        name="mlstm_prep",
    )(z, z, z, conv_w)


def _log_sigmoid(x):
    return jnp.minimum(x, 0.0) - jnp.log(1.0 + jnp.exp(-jnp.abs(x)))


N_CHAINS = 2 * NB


def _mlstm_scan_kernel(*refs):
    nc = N_CHAINS
    q_refs, k_refs, v_refs = refs[0:nc], refs[nc:2 * nc], refs[2 * nc:3 * nc]
    gc_refs, gr_refs = refs[3 * nc:4 * nc], refs[4 * nc:5 * nc]
    bc_ref, br_ref, o_ref, c_ref, n_ref, m_ref = refs[5 * nc:]

    @pl.when(pl.program_id(0) == 0)
    def _():
        c_ref[...] = jnp.zeros_like(c_ref)
        n_ref[...] = jnp.zeros_like(n_ref)
        m_ref[...] = jnp.zeros_like(m_ref)

    c = CHUNK
    ti = lax.broadcasted_iota(jnp.int32, (c, c), 0)
    si = lax.broadcasted_iota(jnp.int32, (c, c), 1)
    masks = (ti >= si, ti <= si)
    items = [(ci, h) for ci in range(nc) for h in range(ML_H)]
    sls = [slice(h * ML_D, (h + 1) * ML_D) for h in range(ML_H)]
    gcol = [gc_refs[ci][0, 0] + bc_ref[ci // NB] for ci in range(nc)]
    grow = [gr_refs[ci][0, 0] + br_ref[ci // NB] for ci in range(nc)]

    q = [q_refs[ci][:, sls[h]] for ci, h in items]
    k = [k_refs[ci][:, sls[h]] for ci, h in items]
    v = [v_refs[ci][:, sls[h]] for ci, h in items]
    qk = [_bdot_nt(a, b) for a, b in zip(q, k)]
    c_mat = [c_ref[ci, h] for ci, h in items]
    n_vec = [n_ref[ci, h] for ci, h in items]
    m_prev = [m_ref[ci, h][0:1, 0:1] for ci, h in items]
    qc = [_bdot_nt(a, b) for a, b in zip(q, c_mat)]
    qn = [jnp.sum(a * b, axis=1, keepdims=True) for a, b in zip(q, n_vec)]

    log_w, m_inter, cum_col, i_col, total = [], [], [], [], []
    for ci, h in items:
        mask = masks[ci // NB]
        mask_t = masks[1 - ci // NB]
        f_col = _log_sigmoid(gcol[ci][:, ML_H + h:ML_H + h + 1])
        f_row = _log_sigmoid(grow[ci][ML_H + h:ML_H + h + 1, :])
        cc = jnp.sum(jnp.where(mask, f_row, 0.0), axis=1, keepdims=True)
        cr = jnp.sum(jnp.where(mask_t, f_col, 0.0), axis=0, keepdims=True)
        log_w.append(jnp.where(mask, cc - cr + grow[ci][h:h + 1, :], ML_NEG))
        cum_col.append(cc)
        i_col.append(gcol[ci][:, h:h + 1])
        total.append(jnp.sum(f_row, axis=1, keepdims=True))
    m_inter = [a + b for a, b in zip(cum_col, m_prev)]
    m_t = [jnp.maximum(jnp.max(a, axis=1, keepdims=True), b) for a, b in zip(log_w, m_inter)]
    s = [a * jnp.exp(b - m) for a, b, m in zip(qk, log_w, m_t)]
    w_inter = [jnp.exp(a - m) for a, m in zip(m_inter, m_t)]
    sv = [_bdot(a, b) for a, b in zip(s, v)]
    for i, (ci, h) in enumerate(items):
        num = sv[i] + w_inter[i] * qc[i]
        den = jnp.sum(s[i], axis=1, keepdims=True) + w_inter[i] * qn[i]
        o_ref[ci // NB, ci % NB, 0, :, sls[h]] = num / jnp.maximum(jnp.abs(den), jnp.exp(-m_t[i]))
    log_src = [t - a + b for t, a, b in zip(total, cum_col, i_col)]
    m_new = [jnp.maximum(t + mp, jnp.max(ls, axis=0, keepdims=True)) for t, mp, ls in zip(total, m_prev, log_src)]
    src = [jnp.exp(ls - mn) for ls, mn in zip(log_src, m_new)]
    decay = [jnp.exp(t + mp - mn) for t, mp, mn in zip(total, m_prev, m_new)]
    vk = [_bdot_tn(a * sr, b) for a, sr, b in zip(v, src, k)]
    for i, (ci, h) in enumerate(items):
        c_ref[ci, h] = decay[i] * c_mat[i] + vk[i]
        n_ref[ci, h] = decay[i] * n_vec[i] + jnp.sum(src[i] * k[i], axis=0, keepdims=True)
        m_ref[ci, h] = jnp.broadcast_to(m_new[i], m_ref.shape[2:])


def _bwd_step(pos):
    return jnp.where(pos < CTX_CHUNKS, CTX_CHUNKS - 1 - pos, SEQ_CHUNKS - 1 + CTX_CHUNKS - pos)


def _mlstm_scan(qk, z, gcol, grow, bcol, brow):
    chains = [(d, b) for d in range(2) for b in range(NB)]

    def tok(col_blk):
        return [pl.BlockSpec((CHUNK, ML_W), lambda n, d=d, b=b: (_seq_row_block(b, d, n), col_blk))
                for d, b in chains]

    gc_specs = [pl.BlockSpec((1, 1, CHUNK, 2 * ML_H), lambda n, d=d, b=b: (d, _seq_row_block(b, d, n), 0, 0))
                for d, b in chains]
    gr_specs = [pl.BlockSpec((1, 1, 2 * ML_H, CHUNK), lambda n, d=d, b=b: (d, _seq_row_block(b, d, n), 0, 0))
                for d, b in chains]
    nc = N_CHAINS
    return pl.pallas_call(
        _mlstm_scan_kernel,
        grid=(SEQ_CHUNKS,),
        in_specs=tok(0) + tok(1) + tok(2) + gc_specs + gr_specs + [
            pl.BlockSpec((2, 1, 2 * ML_H), lambda n: (0, 0, 0)),
            pl.BlockSpec((2, 2 * ML_H, 1), lambda n: (0, 0, 0))],
        out_specs=pl.BlockSpec((2, NB, 1, CHUNK, ML_W), lambda n: (0, 0, n, 0, 0)),
        out_shape=jax.ShapeDtypeStruct((2, NB, SEQ_CHUNKS, CHUNK, ML_W), F32),
        scratch_shapes=[pltpu.VMEM((nc, ML_H, ML_D, ML_D), F32), pltpu.VMEM((nc, ML_H, 1, ML_D), F32),
                        pltpu.VMEM((nc, ML_H, 8, 128), F32)],
        compiler_params=_cp("arbitrary"),
        name="mlstm_scan",
    )(*([qk] * (2 * nc) + [z] * nc + [gcol] * nc + [grow] * nc + [bcol, brow]))


def _mlstm_out_kernel(hf_ref, hb_ref, og_ref, gng_ref, y_ref):
    n = TILE_CHUNKS
    for i in range(n):
        rows = slice(i * CHUNK, (i + 1) * CHUNK)
        hsum = _sigmoid(og_ref[rows, :]) * (hf_ref[0, 0, i] + hb_ref[0, 0, n - 1 - i])
        for h in range(ML_H):
            sl = slice(h * ML_D, (h + 1) * ML_D)
            x = hsum[:, sl]
            cen = x - jnp.mean(x, axis=-1, keepdims=True)
            var = jnp.mean(cen * cen, axis=-1, keepdims=True)
            y_ref[rows, sl] = (cen * lax.rsqrt(var + EPS) * gng_ref[:, sl]).astype(y_ref.dtype)


def _mlstm_out(hs, z, gn_g):
    blk = (1, 1, TILE_CHUNKS, CHUNK, ML_W)
    return pl.pallas_call(
        _mlstm_out_kernel,
        grid=(NB, SEQ_CHUNKS // TILE_CHUNKS),
        in_specs=[pl.BlockSpec(blk, lambda b, grp: (0, b, grp, 0, 0)),
                  pl.BlockSpec(blk, lambda b, grp: (1, b, _bwd_group(grp), 0, 0)),
                  pl.BlockSpec((TILE, ML_W), lambda b, grp: (_tile_of_group(b, grp), 3)),
                  pl.BlockSpec((1, ML_W), lambda b, grp: (0, 0))],
        out_specs=pl.BlockSpec((TILE, ML_W), lambda b, grp: (_tile_of_group(b, grp), 0)),
        out_shape=jax.ShapeDtypeStruct((ROWS, ML_W), BF16),
        compiler_params=_cp("parallel", "parallel"),
        name="mlstm_out",
    )(hs, hs, z, gn_g.reshape(1, ML_W))


def _mlstm(z_main, z_gates, conv_w, i_b, f_b, gn_g):
    qk = _mlstm_prep(z_main, conv_w)
    n_chunks = ROWS // CHUNK
    gates = z_gates[:, :4 * ML_H].reshape(n_chunks, CHUNK, 2, 2, ML_H)
    gcol = jnp.transpose(gates, (3, 0, 1, 2, 4)).reshape(2, n_chunks, CHUNK, 2 * ML_H)
    grow = jnp.swapaxes(gcol, 2, 3)
    bias = jnp.concatenate([i_b, f_b], axis=-1)
    hs = _mlstm_scan(qk, z_main, gcol, grow, bias.reshape(2, 1, 2 * ML_H), bias.reshape(2, 2 * ML_H, 1))
    return _mlstm_out(hs, z_main, gn_g)


def _s5_toeplitz_kernel(kf_ref, kb_ref, o_ref):
    kf = kf_ref[0]
    kb = kb_ref[0]
    lane = lax.broadcasted_iota(jnp.int32, kf.shape, 1)
    width = S5_L * S5_C
    for j in range(S5_L):
        f = kf if j == 0 else jnp.where(lane >= S5_C * j, pltpu.roll(kf, S5_C * j, 1), 0.0)
        back = S5_L - 1 - j
        b = kb if back == 0 else jnp.where(lane < S5_C * (j + 1), pltpu.roll(kb, width - S5_C * back, 1), 0.0)
        o_ref[0, 0, :, j] = f.reshape(S5_G, S5_C, width).astype(o_ref.dtype)
        o_ref[0, 1, :, j] = b.reshape(S5_G, S5_C, width).astype(o_ref.dtype)


def _s5_operators(lam_re, lam_im, log_step, b_re, b_im, c_re, c_im):
    nl = lam_re.shape[0]
    dt = jnp.exp(log_step)[..., None]
    mag = jnp.exp(lam_re * dt)
    a_re = mag * jnp.cos(lam_im * dt)
    a_im = mag * jnp.sin(lam_im * dt)
    den = lam_re * lam_re + lam_im * lam_im
    f_re = ((a_re - 1) * lam_re + a_im * lam_im) / den
    f_im = (a_im * lam_re - (a_re - 1) * lam_im) / den
    bb_re = f_re[..., None] * b_re - f_im[..., None] * b_im
    bb_im = f_re[..., None] * b_im + f_im[..., None] * b_re
    bt_re = jnp.swapaxes(bb_re, -1, -2)
    bt_im = jnp.swapaxes(bb_im, -1, -2)
    pr = [jnp.ones_like(a_re)]
    pi = [jnp.zeros_like(a_im)]
    for _ in range(S5_L):
        pr.append(pr[-1] * a_re - pi[-1] * a_im)
        pi.append(pr[-2] * a_im + pi[-1] * a_re)
    pr = jnp.stack(pr, axis=3)
    pi = jnp.stack(pi, axis=3)

    def times_b(qr, qi):
        qr, qi = qr[..., None, :], qi[..., None, :]
        br, bi = bt_re[:, :, :, None], bt_im[:, :, :, None]
        return qr * br - qi * bi, qr * bi + qi * br

    wr, wi = times_b(pr[:, :, :, :S5_L], pi[:, :, :, :S5_L])
    wr = jnp.swapaxes(wr, 3, 4)
    wi = jnp.swapaxes(wi, 3, 4)
    kern = (jnp.einsum("ldgktp,ldgcp->ldgktc", wr, c_re, precision=HP)
            - jnp.einsum("ldgktp,ldgcp->ldgktc", wi, c_im, precision=HP))
    kf = kern[:, 0].reshape(nl, S5_G * S5_C, S5_L * S5_C)
    kb = jnp.flip(kern[:, 1], axis=3).reshape(nl, S5_G * S5_C, S5_L * S5_C)
    rows_blk = pl.BlockSpec((1, S5_G * S5_C, S5_L * S5_C), lambda l: (l, 0, 0))
    m = pl.pallas_call(
        _s5_toeplitz_kernel,
        grid=(nl,),
        in_specs=[rows_blk, rows_blk],
        out_specs=pl.BlockSpec((1, 2, S5_G, S5_L, S5_C, S5_L * S5_C), lambda l: (l, 0, 0, 0, 0, 0)),
        out_shape=jax.ShapeDtypeStruct((nl, 2, S5_G, S5_L, S5_C, S5_L * S5_C), BF16),
        compiler_params=_cp("parallel"),
        name="s5_toeplitz",
    )(kf, kb).reshape(nl, 2, S5_G, S5_L * S5_C, S5_L * S5_C)

    def stack_dirs(fwd, bwd):
        return jnp.stack([fwd[:, 0], bwd[:, 1]], axis=1)

    er, ei = times_b(stack_dirs(jnp.flip(pr[:, :, :, :S5_L], axis=3), pr[:, :, :, :S5_L]),
                     stack_dirs(jnp.flip(pi[:, :, :, :S5_L], axis=3), pi[:, :, :, :S5_L]))
    e = jnp.concatenate([er, ei], axis=-1).reshape(nl, 2, S5_G, S5_L * S5_C, 2 * S5_P).astype(BF16)
    qr = stack_dirs(pr[:, :, :, 1:], jnp.flip(pr[:, :, :, 1:], axis=3))[..., None, :]
    qi = stack_dirs(pi[:, :, :, 1:], jnp.flip(pi[:, :, :, 1:], axis=3))[..., None, :]
    cr, ci = c_re[:, :, :, None], c_im[:, :, :, None]
    ft = jnp.concatenate([cr * qr - ci * qi, -(cr * qi + ci * qr)], axis=-1)
    ft = ft.reshape(nl, 2, S5_G, S5_L * S5_C, 2 * S5_P).astype(BF16)
    return m, e, ft, pr[:, :, :, S5_L], pi[:, :, :, S5_L]


def _s5_local_kernel(u_ref, m_ref, e_ref, y_ref, x_ref):
    u = u_ref[0]
    for d in range(2):
        y_ref[d, 0] = jnp.dot(u, m_ref[d, 0], preferred_element_type=F32)
        x_ref[d, 0] = jnp.dot(u, e_ref[d, 0], preferred_element_type=F32)


def _s5_carry_kernel(xc_ref, ar_ref, ai_ref, x0_ref, st_ref):
    @pl.when(pl.program_id(1) == 0)
    def _():
        st_ref[...] = jnp.zeros_like(st_ref)

    ar = ar_ref[0]
    ai = ai_ref[0]

    def run(order):
        x = st_ref[...]
        for i in order:
            x0_ref[0, :, :, i, :] = x
            x = x * ar + pltpu.roll(x, S5_P, 2) * ai + xc_ref[0, :, :, i, :]
        st_ref[...] = x

    n = xc_ref.shape[3]

    @pl.when(pl.program_id(0) == 0)
    def _():
        run(range(n))

    @pl.when(pl.program_id(0) == 1)
    def _():
        run(reversed(range(n)))


def _s5_state_kernel(y_ref, x0_ref, ft_ref, o_ref):
    acc = y_ref[0, 0] + y_ref[1, 0]
    for d in range(2):
        acc = acc + _bdot_nt(x0_ref[d, 0], ft_ref[d, 0])
    o_ref[0] = acc.astype(o_ref.dtype)


def _s5_glu_kernel(y_ref, u_ref, d_ref, w_ref, o_ref):
    x = y_ref[...] + u_ref[...] * d_ref[...]
    ge = 0.5 * x * (1.0 + jnp.tanh(math.sqrt(2.0 / math.pi) * (x + 0.044715 * (x * x * x))))
    p = _bdot(ge, w_ref[...])
    o_ref[...] = (p[:, :S5_W] * _sigmoid(p[:, S5_W:])).astype(o_ref.dtype)


def _s5(z, m, e, ft, a_re, a_im, d_skip, w_glu):
    n_ch = (SEQ + CTX) // S5_L
    n_ctx_ch = CTX // S5_L
    rows = NB * n_ch
    width = S5_L * S5_C

    def to_groups(x):
        x = x.reshape(NB, -1, S5_L, S5_G, S5_C)
        return jnp.transpose(x, (3, 0, 1, 2, 4)).reshape(S5_G, NB, -1, width)

    zb = z.astype(BF16)
    u = jnp.concatenate([to_groups(zb[N_LAT:]), to_groups(zb[:N_LAT])], axis=2).reshape(S5_G, rows, width)
    y_loc, x_in = pl.pallas_call(
        _s5_local_kernel,
        grid=(S5_G,),
        in_specs=[pl.BlockSpec((1, rows, width), lambda g: (g, 0, 0)),
                  pl.BlockSpec((2, 1, width, width), lambda g: (0, g, 0, 0)),
                  pl.BlockSpec((2, 1, width, 2 * S5_P), lambda g: (0, g, 0, 0))],
        out_specs=[pl.BlockSpec((2, 1, rows, width), lambda g: (0, g, 0, 0)),
                   pl.BlockSpec((2, 1, rows, 2 * S5_P), lambda g: (0, g, 0, 0))],
        out_shape=[jax.ShapeDtypeStruct((2, S5_G, rows, width), F32),
                   jax.ShapeDtypeStruct((2, S5_G, rows, 2 * S5_P), F32)],
        compiler_params=_cp("parallel"),
        name="s5_local",
    )(u, m, e)

    step = n_ctx_ch
    n_blk = n_ch // step
    coef_r = jnp.concatenate([a_re, a_re], axis=-1)[:, :, None, :]
    coef_i = jnp.concatenate([-a_im, a_im], axis=-1)[:, :, None, :]

    def chunk_blk(d, i):
        return jnp.where(d == 0, i, jnp.where(i == 0, 0, n_blk - i))

    st_blk = pl.BlockSpec((1, S5_G, NB, step, 2 * S5_P), lambda d, i: (d, 0, 0, chunk_blk(d, i), 0))
    coef_blk = pl.BlockSpec((1, S5_G, 1, 2 * S5_P), lambda d, i: (d, 0, 0, 0))
    x0 = pl.pallas_call(
        _s5_carry_kernel,
        grid=(2, n_blk),
        in_specs=[st_blk, coef_blk, coef_blk],
        out_specs=st_blk,
        out_shape=jax.ShapeDtypeStruct((2, S5_G, NB, n_ch, 2 * S5_P), F32),
        scratch_shapes=[pltpu.VMEM((S5_G, NB, 2 * S5_P), F32)],
        compiler_params=_cp("arbitrary", "arbitrary"),
        name="s5_carry",
    )(x_in.reshape(2, S5_G, NB, n_ch, 2 * S5_P), coef_r, coef_i).reshape(2, S5_G, rows, 2 * S5_P)

    y = pl.pallas_call(
        _s5_state_kernel,
        grid=(S5_G,),
        in_specs=[pl.BlockSpec((2, 1, rows, width), lambda g: (0, g, 0, 0)),
                  pl.BlockSpec((2, 1, rows, 2 * S5_P), lambda g: (0, g, 0, 0)),
                  pl.BlockSpec((2, 1, width, 2 * S5_P), lambda g: (0, g, 0, 0))],
        out_specs=pl.BlockSpec((1, rows, width), lambda g: (g, 0, 0)),
        out_shape=jax.ShapeDtypeStruct((S5_G, rows, width), BF16),
        compiler_params=_cp("parallel"),
        name="s5_state",
    )(y_loc, x0, ft)

    y = jnp.transpose(y.reshape(S5_G, NB, n_ch, S5_L, S5_C), (1, 2, 3, 0, 4)).reshape(NB, n_ch * S5_L, S5_W)
    y = jnp.concatenate([y[:, CTX:].reshape(N_LAT, S5_W), y[:, :CTX].reshape(N_CTX, S5_W)], axis=0)
    tok = pl.BlockSpec((TILE, S5_W), lambda t: (t, 0))
    return pl.pallas_call(
        _s5_glu_kernel,
        grid=(N_TILES,),
        in_specs=[tok, tok, pl.BlockSpec((1, S5_W), lambda t: (0, 0)),
                  pl.BlockSpec((S5_W, 2 * S5_W), lambda t: (0, 0))],
        out_specs=tok,
        out_shape=jax.ShapeDtypeStruct((ROWS, S5_W), BF16),
        compiler_params=_cp("parallel"),
        name="s5_glu",
    )(y, z, d_skip.reshape(1, S5_W), w_glu)


def kernel(x, c, ctx, c_ctx, w_mod, b_mod, norm1_g, norm2_g, w_in, b_gate, q_norm_g, k_norm_g, rwkv_mu, rwkv_w0, rwkv_w_up, rwkv_a0, rwkv_a_up, rwkv_g_up, rwkv_k_k, rwkv_k_a, rwkv_r_k, rwkv_gn_g, rwkv_gn_b, mlstm_conv_w, mlstm_i_b, mlstm_f_b, mlstm_gn_g, s5_lam_re, s5_lam_im, s5_log_step, s5_b_re, s5_b_im, s5_c_re, s5_c_im, s5_d, s5_w_glu, w_br_attn, w_br_rwkv, w_br_mlstm, w_br_s5, w_out, w_ffn_in, w_ffn_out, final_norm_g):
    cos, sin = _rope_tables()
    xs = jnp.concatenate([x.reshape(N_LAT, D), ctx.reshape(N_CTX, D)], axis=0)
    c_all = jnp.concatenate([c, c_ctx[None], jnp.zeros((3, D), F32)], axis=0)
    s5_m, s5_e, s5_ft, s5_ar, s5_ai = _s5_operators(s5_lam_re, s5_lam_im, s5_log_step, s5_b_re, s5_b_im,
                                                    s5_c_re, s5_c_im)
    tm = 1024
    for l in range(DEPTH):
        last = l == DEPTH - 1
        n_rows = N_LAT if last else ROWS
        mod = _modulation(c_all, w_mod, b_mod, l).reshape(8, 1, 6 * D)
        h = _norm_mod(xs, norm1_g[l], mod, 0, 1, N_TILES)
        w_attn, w_rwkv, w_ml, w_mlg, w_s5, w_gate = _w_in_split(w_in, l)
        z_attn = _mm(h, w_attn, tm, 512)
        z_rwkv = _mm(h, w_rwkv, tm, 896)
        z_ml = _mm(h, w_ml, tm, 1024)
        z_mlg = _mm(h, w_mlg, tm, 128)
        z_s5 = _mm(h, w_s5, tm, 512)
        gates = _mm_gate(h, w_gate, b_gate, l, tm, 1024, n_rows)

        ya = _attention(z_attn, q_norm_g[l], k_norm_g[l], cos, sin)
        rp = dict(mu=rwkv_mu[l], w0=rwkv_w0[l], w_up=rwkv_w_up[l], a0=rwkv_a0[l], a_up=rwkv_a_up[l],
                  g_up=rwkv_g_up[l], k_k=rwkv_k_k[l], k_a=rwkv_k_a[l], r_k=rwkv_r_k[l].reshape(RW_W))
        r, v, kk, g, bonus, lw, bmat, km = _rwkv_prep(z_rwkv, rp)
        yr = _rwkv_scan(r, v, kk, lw, bmat, km, bonus, g, rwkv_gn_g[l], rwkv_gn_b[l])
        ym = _mlstm(z_ml, z_mlg, mlstm_conv_w[l], mlstm_i_b[l], mlstm_f_b[l], mlstm_gn_g[l])
        ys = _s5(z_s5, s5_m[l], s5_e[l], s5_ft[l], s5_ar[l], s5_ai[l], s5_d[l], s5_w_glu[l])

        y = _merge(ya, yr, ym, ys, gates, w_br_attn, w_br_rwkv, w_br_mlstm, w_br_s5, l, tm, 512, n_rows)
        xs = _mm_res(y, w_out, l, xs, mod, 2, tm, 1024, n_rows)
        h2 = _norm_mod(xs, norm2_g[l], mod, 3, 4, n_rows // TILE)
        u = _ffn_in(h2, w_ffn_in, l, tm, 512, n_rows)
        xs = _mm_res(u, w_ffn_out, l, xs, mod, 5, 512, 512, n_rows)
    return _final_norm(xs, final_norm_g).reshape(NB, SEQ, D)
```

```python
import functools
import math

import numpy as np
import jax
import jax.numpy as jnp
from jax import lax
from jax.experimental import pallas as pl
from jax.experimental.pallas import tpu as pltpu

F32 = jnp.float32
BF16 = jnp.bfloat16
HP = lax.Precision.HIGHEST

D = 2048
NB = 4
SEQ = 2048
CTX = 256
DEPTH = 2
N_LAT = NB * SEQ
N_CTX = NB * CTX
ROWS = N_LAT + N_CTX
EPS = 1e-6
GRID_W = 64

HEAD_DIM = 128
ATTN_HEADS = 8
ATTN_KV = 2
ROPE_THETA = 10000.0
ATTN_Q = ATTN_HEADS * HEAD_DIM
ATTN_IN = (ATTN_HEADS + 2 * ATTN_KV) * HEAD_DIM

RW_H = 8
RW_D = 64
RW_W = 512
RW_IN = 3 * RW_W + 64 + 64 + 128
RW_DECAY = math.exp(-0.5)
RW_GN_EPS = 64e-5

ML_H = 4
ML_D = 128
ML_W = 512
ML_NEG = -1e30
ML_IN = 4 * ML_W + 4 * ML_H

S5_W = 512
S5_C = 16
S5_G = 32
S5_P = 64
S5_L = 16

FFN_H = 5632
GATE_IN = 4 * D

CHUNK = 64
TILE = 256
N_TILES = ROWS // TILE
SEQ_CHUNKS = (SEQ + CTX) // CHUNK
CTX_CHUNKS = CTX // CHUNK
LAT_CHUNKS = SEQ // CHUNK

VMEM_LIMIT_BYTES = 56 * 1024 * 1024


def _cp(*sem):
    return pltpu.CompilerParams(dimension_semantics=sem, vmem_limit_bytes=VMEM_LIMIT_BYTES)


def _bdot(a, b):
    return jnp.dot(a.astype(BF16), b.astype(BF16), preferred_element_type=F32)


def _bdot_nt(a, b):
    return lax.dot_general(a.astype(BF16), b.astype(BF16), (((1,), (1,)), ((), ())),
                           preferred_element_type=F32)


def _bdot_tn(a, b):
    return lax.dot_general(a.astype(BF16), b.astype(BF16), (((0,), (0,)), ((), ())),
                           preferred_element_type=F32)


def _hdot(a, b):
    return jnp.dot(a, b, precision=HP, preferred_element_type=F32)


def _sigmoid(x):
    return 1.0 / (1.0 + jnp.exp(-x))


def _silu(x):
    return x * _sigmoid(x)


def _mod_kernel(c_ref, w_ref, b_ref, o_ref):
    c_hi, c_lo = _split_bf16(_silu(c_ref[...]), 2)
    w_hi, w_lo = _split_bf16(w_ref[...], 2)
    rows = c_hi.shape[0]
    both = jnp.dot(jnp.concatenate([c_hi, c_lo], axis=0), w_hi, preferred_element_type=F32)
    o_ref[...] = both[:rows] + both[rows:] + jnp.dot(c_hi, w_lo, preferred_element_type=F32) + b_ref[...]


def _modulation(c_all, w, b, l):
    tn = 1024
    return pl.pallas_call(
        _mod_kernel,
        grid=(6 * D // tn,),
        in_specs=[pl.BlockSpec((8, D), lambda j: (0, 0)),
                  pl.BlockSpec((None, D, tn), lambda j: (l, 0, j)),
                  pl.BlockSpec((None, 1, tn), lambda j: (l, 0, j))],
        out_specs=pl.BlockSpec((8, tn), lambda j: (0, j)),
        out_shape=jax.ShapeDtypeStruct((8, 6 * D), F32),
        compiler_params=_cp("arbitrary"),
        name="modulation",
    )(c_all, w, b.reshape(DEPTH, 1, 6 * D))


def _mod_row(i, tm):
    return jnp.where(i * tm < N_LAT, (i * tm) // SEQ, NB)


def _norm_mod_kernel(x_ref, g_ref, sh_ref, sc_ref, o_ref):
    x = x_ref[...]
    y = x * lax.rsqrt(jnp.mean(x * x, axis=-1, keepdims=True) + EPS) * g_ref[...]
    o_ref[...] = (y * (1.0 + sc_ref[0]) + sh_ref[0]).astype(o_ref.dtype)


def _norm_mod(x, g, modr, shift_blk, scale_blk, n_tiles):
    return pl.pallas_call(
        _norm_mod_kernel,
        grid=(n_tiles,),
        in_specs=[pl.BlockSpec((TILE, D), lambda i: (i, 0)),
                  pl.BlockSpec((1, D), lambda i: (0, 0)),
                  pl.BlockSpec((1, 1, D), lambda i: (_mod_row(i, TILE), 0, shift_blk)),
                  pl.BlockSpec((1, 1, D), lambda i: (_mod_row(i, TILE), 0, scale_blk))],
        out_specs=pl.BlockSpec((TILE, D), lambda i: (i, 0)),
        out_shape=jax.ShapeDtypeStruct((n_tiles * TILE, D), BF16),
        compiler_params=_cp("parallel"),
        name="norm_mod",
    )(x, g.reshape(1, D), modr, modr)


W_IN_OFFSETS = tuple(int(v) for v in np.cumsum([0, ATTN_IN, RW_IN, 4 * ML_W, 4 * ML_H, S5_W, GATE_IN]))
ML_GATE_PAD = 128


W_TILE = 256
W_ALIGN = 16


def _w_in_group_kernel(*refs, shift, valid):
    a_ref, o_ref = refs[0], refs[-1]
    x = a_ref[...]
    if shift:
        x = jnp.concatenate([x[shift:], refs[1][...]], axis=0)
    if valid < x.shape[0]:
        row = lax.broadcasted_iota(jnp.int32, x.shape, 0)
        x = jnp.where(row < valid, x, 0.0)
    o_ref[...] = x.T.astype(BF16)


def _w_in_group(wt, l, start, n_cols):
    tile = W_TILE if n_cols >= W_TILE else ML_GATE_PAD
    a0, shift = divmod(start, tile)
    assert shift in (0, W_ALIGN)
    n_tiles = -(-n_cols // tile)
    in_specs = [pl.BlockSpec((None, tile, D), lambda t: (l, a0 + t, 0))]
    if shift:
        per = tile // W_ALIGN
        in_specs.append(pl.BlockSpec((None, W_ALIGN, D), lambda t: (l, (a0 + t + 1) * per, 0)))
    return pl.pallas_call(
        functools.partial(_w_in_group_kernel, shift=shift, valid=min(n_cols, tile)),
        grid=(n_tiles,),
        in_specs=in_specs,
        out_specs=pl.BlockSpec((D, tile), lambda t: (0, t)),
        out_shape=jax.ShapeDtypeStruct((D, n_tiles * tile), BF16),
        compiler_params=_cp("parallel"),
        name="w_in_group",
    )(*([wt] * len(in_specs)))


def _w_in_split(w, l):
    wt = jnp.swapaxes(w, 1, 2)
    o = W_IN_OFFSETS
    return tuple(_w_in_group(wt, l, o[i], o[i + 1] - o[i]) for i in range(6))


def _mm_kernel(a_ref, w_ref, o_ref):
    o_ref[...] = jnp.dot(a_ref[...], w_ref[...], preferred_element_type=F32).astype(o_ref.dtype)


def _mm(a, w, tm, tn, out_dtype=F32):
    m, k = a.shape
    n = w.shape[1]
    return pl.pallas_call(
        _mm_kernel,
        grid=(n // tn, m // tm),
        in_specs=[pl.BlockSpec((tm, k), lambda j, i: (i, 0)),
                  pl.BlockSpec((k, tn), lambda j, i: (0, j))],
        out_specs=pl.BlockSpec((tm, tn), lambda j, i: (i, j)),
        out_shape=jax.ShapeDtypeStruct((m, n), out_dtype),
        compiler_params=_cp("parallel", "parallel"),
        name="matmul",
    )(a, w)


ROW_SUB = 256


def _row_blocks(ref):
    return [slice(r, r + ROW_SUB) for r in range(0, ref.shape[0], ROW_SUB)]


def _mm_gate_kernel(a_ref, w_ref, b_ref, o_ref):
    for rows in _row_blocks(o_ref):
        z = jnp.dot(a_ref[rows, :], w_ref[...], preferred_element_type=F32)
        o_ref[rows, :] = _sigmoid(z + b_ref[...]).astype(o_ref.dtype)


def _mm_gate(a, w, b_gate, l, tm, tn, m):
    k = a.shape[1]
    n = w.shape[1]
    return pl.pallas_call(
        _mm_gate_kernel,
        grid=(n // tn, m // tm),
        in_specs=[pl.BlockSpec((tm, k), lambda j, i: (i, 0)),
                  pl.BlockSpec((k, tn), lambda j, i: (0, j)),
                  pl.BlockSpec((None, 1, tn), lambda j, i: (l, 0, j))],
        out_specs=pl.BlockSpec((tm, tn), lambda j, i: (i, j)),
        out_shape=jax.ShapeDtypeStruct((m, n), BF16),
        compiler_params=_cp("parallel", "parallel"),
        name="matmul_gate",
    )(a, w, b_gate.reshape(DEPTH, 1, GATE_IN))


def _mm_res_kernel(a_ref, w_ref, x_ref, g_ref, o_ref, wb_ref):
    @pl.when(pl.program_id(1) == 0)
    def _():
        wb_ref[...] = w_ref[...].astype(BF16)

    for rows in _row_blocks(o_ref):
        y = jnp.dot(a_ref[rows, :], wb_ref[...], preferred_element_type=F32)
        o_ref[rows, :] = x_ref[rows, :] + g_ref[0] * y


def _mm_res(a, w, l, x, mod, gate_blk, tm, tn, n_rows):
    k = a.shape[1]
    n = w.shape[2]
    gpb = D // tn
    return pl.pallas_call(
        _mm_res_kernel,
        grid=(n // tn, n_rows // tm),
        in_specs=[pl.BlockSpec((tm, k), lambda j, i: (i, 0)),
                  pl.BlockSpec((None, k, tn), lambda j, i: (l, 0, j)),
                  pl.BlockSpec((tm, tn), lambda j, i: (i, j)),
                  pl.BlockSpec((1, 1, tn), lambda j, i: (_mod_row(i, tm), 0, gate_blk * gpb + j))],
        out_specs=pl.BlockSpec((tm, tn), lambda j, i: (i, j)),
        out_shape=jax.ShapeDtypeStruct((n_rows, n), F32),
        scratch_shapes=[pltpu.VMEM((k, tn), BF16)],
        compiler_params=_cp("arbitrary", "arbitrary"),
        name="matmul_residual",
    )(a, w, x, mod)


def _ffn_in_kernel(a_ref, wa_ref, wb_ref, o_ref, wab_ref, wbb_ref):
    @pl.when(pl.program_id(1) == 0)
    def _():
        wab_ref[...] = wa_ref[...].astype(BF16)
        wbb_ref[...] = wb_ref[...].astype(BF16)

    for rows in _row_blocks(o_ref):
        a = a_ref[rows, :]
        u = jnp.dot(a, wab_ref[...], preferred_element_type=F32)
        v = jnp.dot(a, wbb_ref[...], preferred_element_type=F32)
        o_ref[rows, :] = (_silu(u) * v).astype(o_ref.dtype)


def _ffn_in(h, w, l, tm, tn, n_rows):
    nb = FFN_H // tn
    return pl.pallas_call(
        _ffn_in_kernel,
        grid=(nb, n_rows // tm),
        in_specs=[pl.BlockSpec((tm, D), lambda j, i: (i, 0)),
                  pl.BlockSpec((None, D, tn), lambda j, i: (l, 0, j)),
                  pl.BlockSpec((None, D, tn), lambda j, i: (l, 0, nb + j))],
        out_specs=pl.BlockSpec((tm, tn), lambda j, i: (i, j)),
        out_shape=jax.ShapeDtypeStruct((n_rows, FFN_H), BF16),
        scratch_shapes=[pltpu.VMEM((D, tn), BF16), pltpu.VMEM((D, tn), BF16)],
        compiler_params=_cp("arbitrary", "arbitrary"),
        name="ffn_in",
    )(h, w, w)


def _merge_kernel(ya_ref, yr_ref, ym_ref, ys_ref, ga_ref, gr_ref, gm_ref, gs_ref,
                  wa_ref, wr_ref, wm_ref, ws_ref, o_ref, wab_ref, wrb_ref, wmb_ref, wsb_ref):
    @pl.when(pl.program_id(1) == 0)
    def _():
        wab_ref[...] = wa_ref[...].astype(BF16)
        wrb_ref[...] = wr_ref[...].astype(BF16)
        wmb_ref[...] = wm_ref[...].astype(BF16)
        wsb_ref[...] = ws_ref[...].astype(BF16)

    for rows in _row_blocks(o_ref):
        acc = None
        for y_ref, g_ref, w_ref in ((ya_ref, ga_ref, wab_ref), (yr_ref, gr_ref, wrb_ref),
                                    (ym_ref, gm_ref, wmb_ref), (ys_ref, gs_ref, wsb_ref)):
            term = g_ref[rows, :].astype(F32) * jnp.dot(y_ref[rows, :], w_ref[...], preferred_element_type=F32)
            acc = term if acc is None else acc + term
        o_ref[rows, :] = acc.astype(o_ref.dtype)


def _merge(ya, yr, ym, ys, gates, wa, wr, wm, ws, l, tm, tn, n_rows):
    nb = D // tn

    def act(width):
        return pl.BlockSpec((tm, width), lambda j, i: (i, 0))

    def gate(br):
        return pl.BlockSpec((tm, tn), lambda j, i: (i, br * nb + j))

    def wgt(width):
        return pl.BlockSpec((None, width, tn), lambda j, i: (l, 0, j))

    return pl.pallas_call(
        _merge_kernel,
        grid=(nb, n_rows // tm),
        in_specs=[act(ATTN_Q), act(RW_W), act(ML_W), act(S5_W),
                  gate(0), gate(1), gate(2), gate(3),
                  wgt(ATTN_Q), wgt(RW_W), wgt(ML_W), wgt(S5_W)],
        out_specs=pl.BlockSpec((tm, tn), lambda j, i: (i, j)),
        out_shape=jax.ShapeDtypeStruct((n_rows, D), BF16),
        scratch_shapes=[pltpu.VMEM((ATTN_Q, tn), BF16), pltpu.VMEM((RW_W, tn), BF16),
                        pltpu.VMEM((ML_W, tn), BF16), pltpu.VMEM((S5_W, tn), BF16)],
        compiler_params=_cp("arbitrary", "arbitrary"),
        name="gated_merge",
    )(ya, yr, ym, ys, gates, gates, gates, gates, wa, wr, wm, ws)


def _final_norm_kernel(x_ref, g_ref, o_ref):
    x = x_ref[...]
    o_ref[...] = x * lax.rsqrt(jnp.mean(x * x, axis=-1, keepdims=True) + EPS) * g_ref[...]


def _final_norm(x, g):
    n_tiles = N_LAT // TILE
    return pl.pallas_call(
        _final_norm_kernel,
        grid=(n_tiles,),
        in_specs=[pl.BlockSpec((TILE, D), lambda i: (i, 0)),
                  pl.BlockSpec((1, D), lambda i: (0, 0))],
        out_specs=pl.BlockSpec((TILE, D), lambda i: (i, 0)),
        out_shape=jax.ShapeDtypeStruct((N_LAT, D), F32),
        compiler_params=_cp("parallel"),
        name="final_norm",
    )(x, g.reshape(1, D))


def _tile_halo_specs(width, col_blk=0):
    last = ROWS // 8 - 1
    per = TILE // 8
    return [pl.BlockSpec((TILE, width), lambda t: (t, col_blk)),
            pl.BlockSpec((8, width), lambda t: (jnp.maximum(t * per - 1, 0), col_blk)),
            pl.BlockSpec((8, width), lambda t: (jnp.minimum((t + 1) * per, last), col_blk))]


def _neighbours(z, prev_blk, next_blk):
    t = pl.program_id(0)
    pos = t % (SEQ // TILE)
    is_lat = t < N_LAT // TILE
    has_prev = jnp.logical_and(is_lat, pos > 0).astype(F32)
    has_next = jnp.logical_and(is_lat, pos < SEQ // TILE - 1).astype(F32)
    row = lax.broadcasted_iota(jnp.int32, z.shape, 0)
    zp = jnp.where(row == 0, prev_blk[7:8, :] * has_prev, pltpu.roll(z, 1, 0))
    zn = jnp.where(row == TILE - 1, next_blk[0:1, :] * has_next, pltpu.roll(z, TILE - 1, 0))
    return zp, zn


def _seq_row_block(b, d, n):
    ctx_c = jnp.where(d == 0, n, CTX_CHUNKS - 1 - n)
    lat_c = jnp.where(d == 0, n - CTX_CHUNKS, SEQ_CHUNKS - 1 - n)
    return jnp.where(n < CTX_CHUNKS, N_LAT // CHUNK + CTX_CHUNKS * b + ctx_c, LAT_CHUNKS * b + lat_c)


def _rope(x, cos, sin):
    lane = lax.broadcasted_iota(jnp.int32, x.shape, 1)
    first = (lane % 64) < 32
    partner = jnp.where(first, pltpu.roll(x, 96, 1), pltpu.roll(x, 32, 1))
    return x * cos + partner * sin


def _rms(x, g):
    return x * lax.rsqrt(jnp.mean(x * x, axis=-1, keepdims=True) + EPS) * g


def _attn_kernel(q_ref, kl_ref, kc_ref, vl_ref, vc_ref, cos_ref, sin_ref, cos_t_ref, sin_t_ref,
                 qg_ref, kg_ref, o_ref, klb_ref, kcb_ref, vlb_ref, vcb_ref):
    qi = pl.program_id(2)
    n_lat_tiles = SEQ // TILE

    @pl.when(qi == 0)
    def _():
        kg = kg_ref[...]
        klb_ref[...] = _rope(_rms(kl_ref[...], kg), cos_ref[...], sin_ref[...]).astype(BF16)
        kcb_ref[...] = _rms(kc_ref[...], kg).astype(BF16)
        vlb_ref[...] = vl_ref[...].astype(BF16)
        vcb_ref[...] = vc_ref[...].astype(BF16)

    scale = HEAD_DIM ** -0.5 * math.log2(math.e)
    nt = (((1,), (1,)), ((), ()))

    def heads(latent):
        sls = [slice(h * HEAD_DIM, (h + 1) * HEAD_DIM) for h in range(ATTN_HEADS // ATTN_KV)]
        q = [_rms(q_ref[:, sl], qg_ref[...]) for sl in sls]
        if latent:
            q = [_rope(x, cos_t_ref[...], sin_t_ref[...]) for x in q]
        q = [(x * scale).astype(BF16) for x in q]
        s_c = [lax.dot_general(x, kcb_ref[...], nt, preferred_element_type=F32) for x in q]
        m = [jnp.max(x, axis=-1, keepdims=True) for x in s_c]
        if latent:
            s_l = [lax.dot_general(x, klb_ref[...], nt, preferred_element_type=F32) for x in q]
            m = [jnp.maximum(a, jnp.max(x, axis=-1, keepdims=True)) for a, x in zip(m, s_l)]
        p_c = [jnp.exp2(x - a) for x, a in zip(s_c, m)]
        den = [jnp.sum(x, axis=-1, keepdims=True) for x in p_c]
        acc = [jnp.dot(x.astype(BF16), vcb_ref[...], preferred_element_type=F32) for x in p_c]
        if latent:
            p_l = [jnp.exp2(x - a) for x, a in zip(s_l, m)]
            den = [a + jnp.sum(x, axis=-1, keepdims=True) for a, x in zip(den, p_l)]
            acc = [a + jnp.dot(x.astype(BF16), vlb_ref[...], preferred_element_type=F32) for a, x in zip(acc, p_l)]
        for sl, a, dn in zip(sls, acc, den):
            o_ref[:, sl] = (a / dn).astype(o_ref.dtype)

    @pl.when(qi < n_lat_tiles)
    def _():
        heads(True)

    @pl.when(qi == n_lat_tiles)
    def _():
        heads(False)


def _rope_tables():
    rows = SEQ // GRID_W
    row = jnp.repeat(jnp.arange(rows, dtype=F32), GRID_W)
    col = jnp.tile(jnp.arange(GRID_W, dtype=F32), rows)
    axis_dim = HEAD_DIM // 2
    inv_freq = ROPE_THETA ** (-jnp.arange(0, axis_dim, 2, dtype=F32) / axis_dim)
    ang_r = row[:, None] * inv_freq[None]
    ang_c = col[:, None] * inv_freq[None]
    cos = jnp.concatenate([jnp.cos(ang_r), jnp.cos(ang_r), jnp.cos(ang_c), jnp.cos(ang_c)], axis=-1)
    sin = jnp.concatenate([-jnp.sin(ang_r), jnp.sin(ang_r), -jnp.sin(ang_c), jnp.sin(ang_c)], axis=-1)
    return cos, sin


def _attention(z, q_g, k_g, cos, sin):
    n_lat_tiles = SEQ // TILE
    qw = ATTN_Q // ATTN_KV
    kcol = ATTN_Q // HEAD_DIM
    vcol = kcol + ATTN_KV

    def q_row(b, g, qi):
        return jnp.where(qi < n_lat_tiles, n_lat_tiles * b + qi, N_LAT // TILE + b)

    return pl.pallas_call(
        _attn_kernel,
        grid=(NB, ATTN_KV, n_lat_tiles + 1),
        in_specs=[pl.BlockSpec((TILE, qw), lambda b, g, qi: (q_row(b, g, qi), g)),
                  pl.BlockSpec((SEQ, HEAD_DIM), lambda b, g, qi: (b, kcol + g)),
                  pl.BlockSpec((CTX, HEAD_DIM), lambda b, g, qi: (N_LAT // CTX + b, kcol + g)),
                  pl.BlockSpec((SEQ, HEAD_DIM), lambda b, g, qi: (b, vcol + g)),
                  pl.BlockSpec((CTX, HEAD_DIM), lambda b, g, qi: (N_LAT // CTX + b, vcol + g)),
                  pl.BlockSpec((SEQ, HEAD_DIM), lambda b, g, qi: (0, 0)),
                  pl.BlockSpec((SEQ, HEAD_DIM), lambda b, g, qi: (0, 0)),
                  pl.BlockSpec((TILE, HEAD_DIM), lambda b, g, qi: (jnp.minimum(qi, n_lat_tiles - 1), 0)),
                  pl.BlockSpec((TILE, HEAD_DIM), lambda b, g, qi: (jnp.minimum(qi, n_lat_tiles - 1), 0)),
                  pl.BlockSpec((1, HEAD_DIM), lambda b, g, qi: (0, 0)),
                  pl.BlockSpec((1, HEAD_DIM), lambda b, g, qi: (0, 0))],
        out_specs=pl.BlockSpec((TILE, qw), lambda b, g, qi: (q_row(b, g, qi), g)),
        out_shape=jax.ShapeDtypeStruct((ROWS, ATTN_Q), BF16),
        scratch_shapes=[pltpu.VMEM((SEQ, HEAD_DIM), BF16), pltpu.VMEM((CTX, HEAD_DIM), BF16),
                        pltpu.VMEM((SEQ, HEAD_DIM), BF16), pltpu.VMEM((CTX, HEAD_DIM), BF16)],
        compiler_params=_cp("arbitrary", "arbitrary", "arbitrary"),
        name="attention",
    )(z, z, z, z, z, cos, sin, cos, sin, q_g.reshape(1, HEAD_DIM), k_g.reshape(1, HEAD_DIM))


def _rwkv_prep_kernel(z_ref, zp_ref, zn_ref, mu_ref, w0_ref, wup_ref, a0_ref, aup_ref, gup_ref,
                      kk_ref, ka_ref, rk_ref, bd_ref,
                      r_out, v_out, kkn_out, g_out, bonus_out, lw_out, b_out, km_out):
    z = z_ref[...]
    zp, zn = _neighbours(z, zp_ref[...], zn_ref[...])
    zs = z + mu_ref[...] * (0.5 * (zp + zn) - z)
    r = zs[:, 0:RW_W]
    k = zs[:, RW_W:2 * RW_W]
    v = zs[:, 2 * RW_W:3 * RW_W]
    w_lo = zs[:, 3 * RW_W:3 * RW_W + 64]
    a_lo = zs[:, 3 * RW_W + 64:3 * RW_W + 128]
    g_lo = zs[:, 3 * RW_W + 128:3 * RW_W + 256]
    bd = bd_ref[...]
    kk = k * kk_ref[...]
    kk = kk * lax.rsqrt(_dot_rhs_exact(kk * kk, bd) + 1e-12)
    r_out[...] = r
    v_out[...] = v
    kkn_out[...] = kk
    g_out[...] = _bdot(_sigmoid(g_lo), gup_ref[...])
    tw = jnp.tanh(w_lo)
    km_sum = None
    for d in range(2):
        lw = -RW_DECAY * _sigmoid(w0_ref[d] + _bdot(tw, wup_ref[d]))
        a = _sigmoid(a0_ref[d] + _bdot(a_lo, aup_ref[d]))
        km = k * (1.0 + (a - 1.0) * ka_ref[...])
        lw_out[d] = lw
        b_out[d] = a * kk
        km_out[d] = km
        km_sum = km if km_sum is None else km_sum + km
    bonus_out[...] = _dot_rhs_exact(r * km_sum * rk_ref[...], bd) * v


def _rwkv_prep(z, p):
    row = lambda a: a.reshape(1, -1)
    full = lambda shape: pl.BlockSpec(shape, lambda t: (0,) * len(shape))
    out_tok = pl.BlockSpec((TILE, RW_W), lambda t: (t, 0))
    out_dir = pl.BlockSpec((2, TILE, RW_W), lambda t: (0, t, 0))
    tok = jax.ShapeDtypeStruct((ROWS, RW_W), F32)
    drn = jax.ShapeDtypeStruct((2, ROWS, RW_W), F32)
    return pl.pallas_call(
        _rwkv_prep_kernel,
        grid=(N_TILES,),
        in_specs=_tile_halo_specs(RW_IN) + [
            full((1, RW_IN)), full((2, 1, RW_W)), full((2, 64, RW_W)), full((2, 1, RW_W)),
            full((2, 64, RW_W)), full((128, RW_W)), full((1, RW_W)), full((1, RW_W)), full((1, RW_W)),
            full((RW_W, RW_W))],
        out_specs=[out_tok] * 5 + [out_dir] * 3,
        out_shape=[tok] * 5 + [drn] * 3,
        compiler_params=_cp("parallel"),
        name="rwkv_prep",
    )(z, z, z, row(p["mu"]), p["w0"].reshape(2, 1, RW_W), p["w_up"], p["a0"].reshape(2, 1, RW_W),
      p["a_up"], p["g_up"], row(p["k_k"]), row(p["k_a"]), row(p["r_k"]),
      _head_block_ones(RW_W, RW_D).astype(BF16))


def _head_block_ones(width, head):
    idx = np.arange(width) // head
    return jnp.asarray((idx[:, None] == idx[None, :]).astype(np.float32))


def _split_bf16(x, pieces):
    out = []
    for _ in range(pieces):
        p = x.astype(BF16)
        out.append(p)
        x = x - p.astype(F32)
    return out


def _dot_rhs_exact(x, m, pieces=2):
    return sum(jnp.dot(p, m, preferred_element_type=F32) for p in _split_bf16(x, pieces))


def _dot_lhs_exact(m, x, pieces=3):
    return sum(jnp.dot(m, p, preferred_element_type=F32) for p in _split_bf16(x, pieces))


RW_SUB = 2


def _rwkv_chunk_kernel(r_ref, v_ref, kk_ref, lw_ref, b_ref, km_ref, p_ref, sl_ref, re_ref, ol_ref):
    d = pl.program_id(0)
    c = CHUNK
    hd = RW_D
    ti = lax.broadcasted_iota(jnp.int32, (c, c), 0)
    si = lax.broadcasted_iota(jnp.int32, (c, c), 1)
    delta = (ti - si) * (1 - 2 * d)
    incl = delta >= 0
    strict = delta > 0
    eye = jnp.where(ti == si, 1.0, 0.0)
    tri = jnp.where(incl, 1.0, 0.0).astype(BF16)
    gr = lax.broadcasted_iota(jnp.int32, (2 * c, c), 0)
    gc = lax.broadcasted_iota(jnp.int32, (2 * c, c), 1)
    gmask = (gr % c - gc) * (1 - 2 * d) >= jnp.where(gr < c, 1, 0)

    items = []
    for sub in range(RW_SUB):
        rows = slice(sub * c, (sub + 1) * c)
        lw = lw_ref[0, rows, :]
        cs = _dot_lhs_exact(tri, lw)
        tot = jnp.sum(lw, axis=0, keepdims=True)
        r = r_ref[rows, :]
        v = v_ref[rows, :]
        kk = kk_ref[rows, :]
        bb = b_ref[0, rows, :]
        km = km_ref[0, rows, :]
        e_neg = jnp.exp(-cs)
        e_rem = jnp.exp(tot - cs)
        kkt = kk * jnp.exp(cs - lw)
        rt = r * jnp.exp(cs)
        bt = bb * e_neg
        kt = km * e_neg
        bh = bb * e_rem
        kh = km * e_rem
        e_tot = jnp.exp(tot)
        for h in range(RW_H):
            s = slice(h * hd, (h + 1) * hd)
            items.append(dict(sub=sub, s=s, kkt=kkt[:, s], rt=rt[:, s], bt=bt[:, s], kt=kt[:, s], v=v[:, s],
                              bh=bh[:, s], kh=kh[:, s], e_tot=e_tot[:, s]))

    kr = [jnp.concatenate([it["kkt"], it["rt"]], axis=0) for it in items]
    gb = [jnp.where(gmask, _bdot_nt(x, it["bt"]), 0.0) for x, it in zip(kr, items)]
    gk = [jnp.where(gmask, _bdot_nt(x, it["kt"]), 0.0) for x, it in zip(kr, items)]
    l_b = [x[:c] for x in gb]
    a_b = [x[c:] for x in gb]
    lkv_akv = [_bdot(x, it["v"]) for x, it in zip(gk, items)]
    pw = [_bdot(x, x) for x in l_b]
    inv = [eye - x for x in l_b]
    for _ in range(int(math.log2(c)) - 2):
        res = [_bdot(jnp.concatenate([p, i], axis=0), p) for p, i in zip(pw, inv)]
        pw = [x[:c] for x in res]
        inv = [i + x[c:] for i, x in zip(inv, res)]
    inv = [i + _bdot(i, p) for p, i in zip(pw, inv)]
    w = [_bdot(i, it["kkt"]) for i, it in zip(inv, items)]
    y_loc = [_bdot(i, x[:c]) for i, x in zip(inv, lkv_akv)]
    ab_w = [_bdot(a, x) for a, x in zip(a_b, w)]
    ab_y = [_bdot(a, x) for a, x in zip(a_b, y_loc)]
    vk = [_bdot_tn(it["v"], it["kh"]) for it in items]
    yb = [_bdot_tn(x, it["bh"]) for x, it in zip(y_loc, items)]
    wb = [_bdot_tn(x, it["bh"]) for x, it in zip(w, items)]
    for n, it in enumerate(items):
        s = it["s"]
        j = jnp.where(d == 0, it["sub"], RW_SUB - 1 - it["sub"])
        re_ref[0, 0, j, :, s] = it["rt"] - ab_w[n]
        ol_ref[0, 0, j, :, s] = lkv_akv[n][c:] - ab_y[n]
        sl_ref[0, 0, j, :, s] = vk[n] - yb[n]
        p_ref[0, 0, j, :, s] = eye * it["e_tot"] - wb[n]


def _rwkv_carry_kernel(p_ref, sl_ref, s_out_ref, st_ref):
    @pl.when(pl.program_id(0) == 0)
    def _():
        st_ref[...] = jnp.zeros_like(st_ref)

    for d in range(2):
        for b in range(NB):
            s = st_ref[d, b]
            s_out_ref[d, b, 0] = s
            for h in range(RW_H):
                sl = slice(h * RW_D, (h + 1) * RW_D)
                st_ref[d, b, :, sl] = _hdot(s[:, sl], p_ref[d, b, 0, :, sl]) + sl_ref[d, b, 0, :, sl]


TILE_CHUNKS = TILE // CHUNK


def _tile_of_group(b, grp):
    return jnp.where(grp == 0, N_LAT // TILE + b, (SEQ // TILE) * b + grp - 1)


def _bwd_group(grp):
    return jnp.where(grp == 0, 0, SEQ_CHUNKS // TILE_CHUNKS - grp)


def _rwkv_out_kernel(olf_ref, ref_ref, sf_ref, olb_ref, reb_ref, sb_ref, bonus_ref, g_ref, gng_ref, gnb_ref,
                     bd_ref, y_ref, o_scr):
    n = TILE_CHUNKS
    items = [(i, slice(h * RW_D, (h + 1) * RW_D)) for i in range(n) for h in range(RW_H)]
    pf = [_bdot_nt(ref_ref[0, 0, i, :, s], sf_ref[0, 0, i, :, s]) for i, s in items]
    pb = [_bdot_nt(reb_ref[0, 0, n - 1 - i, :, s], sb_ref[0, 0, n - 1 - i, :, s]) for i, s in items]
    for (i, s), a, b in zip(items, pf, pb):
        o_scr[i * CHUNK:(i + 1) * CHUNK, s] = olf_ref[0, 0, i, :, s] + olb_ref[0, 0, n - 1 - i, :, s] + a + b
    o = o_scr[...] + bonus_ref[...]
    bd = bd_ref[...]
    cen = o - _dot_rhs_exact(o, bd) * (1.0 / RW_D)
    var = _dot_rhs_exact(cen * cen, bd) * (1.0 / RW_D)
    y = cen * lax.rsqrt(var + RW_GN_EPS) * gng_ref[...] + gnb_ref[...]
    y_ref[...] = (y * g_ref[...]).astype(y_ref.dtype)


def _rwkv_scan(r, v, kk, lw, bmat, km, bonus, g, gn_g, gn_b):
    sub_rows = RW_SUB * CHUNK
    lat_blocks = N_LAT // sub_rows
    lat_per_b = SEQ // sub_rows
    ctx_per_b = CTX // sub_rows

    def step_block(d, rb):
        is_lat = rb < lat_blocks
        b = jnp.where(is_lat, rb // lat_per_b, (rb - lat_blocks) // ctx_per_b)
        i = jnp.where(is_lat, rb % lat_per_b, (rb - lat_blocks) % ctx_per_b)
        fwd = jnp.where(is_lat, ctx_per_b + i, i)
        bwd = jnp.where(is_lat, ctx_per_b + lat_per_b - 1 - i, ctx_per_b - 1 - i)
        return b, jnp.where(d == 0, fwd, bwd)

    tok = pl.BlockSpec((sub_rows, RW_W), lambda d, rb: (rb, 0))
    drn = pl.BlockSpec((1, sub_rows, RW_W), lambda d, rb: (d, rb, 0))
    step_shape = jax.ShapeDtypeStruct((2, NB, SEQ_CHUNKS, CHUNK, RW_W), F32)
    step_blk = pl.BlockSpec((1, 1, RW_SUB, CHUNK, RW_W), lambda d, rb: (d,) + step_block(d, rb) + (0, 0))
    p, s_loc, r_eff, o_loc = pl.pallas_call(
        _rwkv_chunk_kernel,
        grid=(2, ROWS // sub_rows),
        in_specs=[tok, tok, tok, drn, drn, drn],
        out_specs=[step_blk] * 4,
        out_shape=[step_shape] * 4,
        compiler_params=_cp("parallel", "parallel"),
        name="rwkv_chunk",
    )(r, v, kk, lw, bmat, km)

    all_blk = pl.BlockSpec((2, NB, 1, CHUNK, RW_W), lambda n: (0, 0, n, 0, 0))
    s_in = pl.pallas_call(
        _rwkv_carry_kernel,
        grid=(SEQ_CHUNKS,),
        in_specs=[all_blk, all_blk],
        out_specs=all_blk,
        out_shape=step_shape,
        scratch_shapes=[pltpu.VMEM((2, NB, CHUNK, RW_W), F32)],
        compiler_params=_cp("arbitrary"),
        name="rwkv_carry",
    )(p, s_loc)

    fwd = pl.BlockSpec((1, 1, TILE_CHUNKS, CHUNK, RW_W), lambda b, grp: (0, b, grp, 0, 0))
    bwd = pl.BlockSpec((1, 1, TILE_CHUNKS, CHUNK, RW_W), lambda b, grp: (1, b, _bwd_group(grp), 0, 0))
    rows = pl.BlockSpec((TILE, RW_W), lambda b, grp: (_tile_of_group(b, grp), 0))
    vec = pl.BlockSpec((1, RW_W), lambda b, grp: (0, 0))
    return pl.pallas_call(
        _rwkv_out_kernel,
        grid=(NB, SEQ_CHUNKS // TILE_CHUNKS),
        in_specs=[fwd, fwd, fwd, bwd, bwd, bwd, rows, rows, vec, vec,
                  pl.BlockSpec((RW_W, RW_W), lambda b, grp: (0, 0))],
        out_specs=rows,
        out_shape=jax.ShapeDtypeStruct((ROWS, RW_W), BF16),
        scratch_shapes=[pltpu.VMEM((TILE, RW_W), F32)],
        compiler_params=_cp("parallel", "parallel"),
        name="rwkv_out",
    )(o_loc, r_eff, s_in, o_loc, r_eff, s_in, bonus, g, gn_g.reshape(1, RW_W), gn_b.reshape(1, RW_W),
      _head_block_ones(RW_W, RW_D).astype(BF16))


def _mlstm_prep_kernel(z_ref, zp_ref, zn_ref, w_ref, o_ref):
    z = z_ref[...]
    zp, zn = _neighbours(z, zp_ref[...], zn_ref[...])
    y = _silu(zp * w_ref[0:1, :] + z * w_ref[1:2, :] + zn * w_ref[2:3, :])
    col = lax.broadcasted_iota(jnp.int32, y.shape, 1)
    o_ref[...] = jnp.where(col >= ML_W, y * (ML_D ** -0.5), y)


def _mlstm_prep(z, conv_w):
    return pl.pallas_call(
        _mlstm_prep_kernel,
        grid=(N_TILES,),
        in_specs=_tile_halo_specs(2 * ML_W) + [pl.BlockSpec((3, 2 * ML_W), lambda t: (0, 0))],
        out_specs=pl.BlockSpec((TILE, 2 * ML_W), lambda t: (t, 0)),
        out_shape=jax.ShapeDtypeStruct((ROWS, 2 * ML_W), F32),
        compiler_params=_cp("parallel"),
        name="mlstm_prep",
    )(z, z, z, conv_w)


def _log_sigmoid(x):
    return jnp.minimum(x, 0.0) - jnp.log(1.0 + jnp.exp(-jnp.abs(x)))


N_CHAINS = 2 * NB


def _mlstm_scan_kernel(*refs):
    nc = N_CHAINS
    q_refs, k_refs, v_refs = refs[0:nc], refs[nc:2 * nc], refs[2 * nc:3 * nc]
    gc_refs, gr_refs = refs[3 * nc:4 * nc], refs[4 * nc:5 * nc]
    bc_ref, br_ref, o_ref, c_ref, n_ref, m_ref = refs[5 * nc:]

    @pl.when(pl.program_id(0) == 0)
    def _():
        c_ref[...] = jnp.zeros_like(c_ref)
        n_ref[...] = jnp.zeros_like(n_ref)
        m_ref[...] = jnp.zeros_like(m_ref)

    c = CHUNK
    ti = lax.broadcasted_iota(jnp.int32, (c, c), 0)
    si = lax.broadcasted_iota(jnp.int32, (c, c), 1)
    masks = (ti >= si, ti <= si)
    items = [(ci, h) for ci in range(nc) for h in range(ML_H)]
    sls = [slice(h * ML_D, (h + 1) * ML_D) for h in range(ML_H)]
    gcol = [gc_refs[ci][0, 0] + bc_ref[ci // NB] for ci in range(nc)]
    grow = [gr_refs[ci][0, 0] + br_ref[ci // NB] for ci in range(nc)]

    ones = jnp.ones((c, ML_D), BF16)
    q = [q_refs[ci][:, sls[h]] for ci, h in items]
    k = [k_refs[ci][:, sls[h]] for ci, h in items]
    v = [v_refs[ci][:, sls[h]] for ci, h in items]
    qk = [_bdot_nt(a, b) for a, b in zip(q, k)]
    c_mat = [c_ref[ci, h] for ci, h in items]
    n_vec = [n_ref[ci, h] for ci, h in items]
    m_prev = [m_ref[ci, h][0:1, :] for ci, h in items]
    qcn = [_bdot_nt(a, jnp.concatenate([cm, jnp.broadcast_to(nv, (ML_D, ML_D))], axis=0))
           for a, cm, nv in zip(q, c_mat, n_vec)]

    log_w, cum, i_col, total = [], [], [], []
    for ci, h in items:
        mask = masks[ci // NB]
        mask_t = masks[1 - ci // NB]
        f_col = _log_sigmoid(gcol[ci][:, ML_H + h:ML_H + h + 1])
        f_row = _log_sigmoid(grow[ci][ML_H + h:ML_H + h + 1, :])
        cc = _dot_rhs_exact(jnp.where(mask, f_row, 0.0), ones, pieces=3)
        cr = jnp.sum(jnp.where(mask_t, f_col, 0.0), axis=0, keepdims=True)
        log_w.append(jnp.where(mask, cc[:, :c] - cr + grow[ci][h:h + 1, :], ML_NEG))
        cum.append(cc)
        i_col.append(jnp.broadcast_to(gcol[ci][:, h:h + 1], (c, ML_D)))
        total.append(jnp.sum(f_row, axis=1, keepdims=True))
    m_inter = [a + b for a, b in zip(cum, m_prev)]
    m_t = [jnp.maximum(jnp.broadcast_to(jnp.max(a, axis=1, keepdims=True), (c, ML_D)), b)
           for a, b in zip(log_w, m_inter)]
    s = [a * jnp.exp(b - m[:, :c]) for a, b, m in zip(qk, log_w, m_t)]
    w_inter = [jnp.exp(a - m) for a, m in zip(m_inter, m_t)]
    sv = [_bdot(a, jnp.concatenate([b.astype(BF16), ones], axis=1)) for a, b in zip(s, v)]
    for i, (ci, h) in enumerate(items):
        num = sv[i][:, :ML_D] + w_inter[i] * qcn[i][:, :ML_D]
        den = sv[i][:, ML_D:] + w_inter[i] * qcn[i][:, ML_D:]
        o_ref[ci // NB, ci % NB, 0, :, sls[h]] = num / jnp.maximum(jnp.abs(den), jnp.exp(-m_t[i]))
    log_src = [t - a + b for t, a, b in zip(total, cum, i_col)]
    m_new = [jnp.maximum(t + mp, jnp.max(ls, axis=0, keepdims=True)) for t, mp, ls in zip(total, m_prev, log_src)]
    src = [jnp.exp(ls - mn) for ls, mn in zip(log_src, m_new)]
    decay = [jnp.exp(t + mp - mn) for t, mp, mn in zip(total, m_prev, m_new)]
    vk = [_bdot_tn(a * sr, b) for a, sr, b in zip(v, src, k)]
    for i, (ci, h) in enumerate(items):
        c_ref[ci, h] = decay[i] * c_mat[i] + vk[i]
        n_ref[ci, h] = decay[i] * n_vec[i] + jnp.sum(src[i] * k[i], axis=0, keepdims=True)
        m_ref[ci, h] = jnp.broadcast_to(m_new[i], m_ref.shape[2:])


def _bwd_step(pos):
    return jnp.where(pos < CTX_CHUNKS, CTX_CHUNKS - 1 - pos, SEQ_CHUNKS - 1 + CTX_CHUNKS - pos)


def _mlstm_scan(qk, z, gcol, grow, bcol, brow):
    chains = [(d, b) for d in range(2) for b in range(NB)]

    def tok(col_blk):
        return [pl.BlockSpec((CHUNK, ML_W), lambda n, d=d, b=b: (_seq_row_block(b, d, n), col_blk))
                for d, b in chains]

    gc_specs = [pl.BlockSpec((1, 1, CHUNK, 2 * ML_H), lambda n, d=d, b=b: (d, _seq_row_block(b, d, n), 0, 0))
                for d, b in chains]
    gr_specs = [pl.BlockSpec((1, 1, 2 * ML_H, CHUNK), lambda n, d=d, b=b: (d, _seq_row_block(b, d, n), 0, 0))
                for d, b in chains]
    nc = N_CHAINS
    return pl.pallas_call(
        _mlstm_scan_kernel,
        grid=(SEQ_CHUNKS,),
        in_specs=tok(0) + tok(1) + tok(2) + gc_specs + gr_specs + [
            pl.BlockSpec((2, 1, 2 * ML_H), lambda n: (0, 0, 0)),
            pl.BlockSpec((2, 2 * ML_H, 1), lambda n: (0, 0, 0))],
        out_specs=pl.BlockSpec((2, NB, 1, CHUNK, ML_W), lambda n: (0, 0, n, 0, 0)),
        out_shape=jax.ShapeDtypeStruct((2, NB, SEQ_CHUNKS, CHUNK, ML_W), F32),
        scratch_shapes=[pltpu.VMEM((nc, ML_H, ML_D, ML_D), F32), pltpu.VMEM((nc, ML_H, 1, ML_D), F32),
                        pltpu.VMEM((nc, ML_H, 8, 128), F32)],
        compiler_params=_cp("arbitrary"),
        name="mlstm_scan",
    )(*([qk] * (2 * nc) + [z] * nc + [gcol] * nc + [grow] * nc + [bcol, brow]))


def _mlstm_out_kernel(hf_ref, hb_ref, og_ref, gng_ref, y_ref):
    n = TILE_CHUNKS
    for i in range(n):
        rows = slice(i * CHUNK, (i + 1) * CHUNK)
        hsum = _sigmoid(og_ref[rows, :]) * (hf_ref[0, 0, i] + hb_ref[0, 0, n - 1 - i])
        for h in range(ML_H):
            sl = slice(h * ML_D, (h + 1) * ML_D)
            x = hsum[:, sl]
            cen = x - jnp.mean(x, axis=-1, keepdims=True)
            var = jnp.mean(cen * cen, axis=-1, keepdims=True)
            y_ref[rows, sl] = (cen * lax.rsqrt(var + EPS) * gng_ref[:, sl]).astype(y_ref.dtype)


def _mlstm_out(hs, z, gn_g):
    blk = (1, 1, TILE_CHUNKS, CHUNK, ML_W)
    return pl.pallas_call(
        _mlstm_out_kernel,
        grid=(NB, SEQ_CHUNKS // TILE_CHUNKS),
        in_specs=[pl.BlockSpec(blk, lambda b, grp: (0, b, grp, 0, 0)),
                  pl.BlockSpec(blk, lambda b, grp: (1, b, _bwd_group(grp), 0, 0)),
                  pl.BlockSpec((TILE, ML_W), lambda b, grp: (_tile_of_group(b, grp), 3)),
                  pl.BlockSpec((1, ML_W), lambda b, grp: (0, 0))],
        out_specs=pl.BlockSpec((TILE, ML_W), lambda b, grp: (_tile_of_group(b, grp), 0)),
        out_shape=jax.ShapeDtypeStruct((ROWS, ML_W), BF16),
        compiler_params=_cp("parallel", "parallel"),
        name="mlstm_out",
    )(hs, hs, z, gn_g.reshape(1, ML_W))


def _mlstm(z_main, z_gates, conv_w, i_b, f_b, gn_g):
    qk = _mlstm_prep(z_main, conv_w)
    n_chunks = ROWS // CHUNK
    gates = z_gates[:, :4 * ML_H].reshape(n_chunks, CHUNK, 2, 2, ML_H)
    gcol = jnp.transpose(gates, (3, 0, 1, 2, 4)).reshape(2, n_chunks, CHUNK, 2 * ML_H)
    grow = jnp.swapaxes(gcol, 2, 3)
    bias = jnp.concatenate([i_b, f_b], axis=-1)
    hs = _mlstm_scan(qk, z_main, gcol, grow, bias.reshape(2, 1, 2 * ML_H), bias.reshape(2, 2 * ML_H, 1))
    return _mlstm_out(hs, z_main, gn_g)


def _s5_toeplitz_kernel(kf_ref, kb_ref, o_ref):
    kf = kf_ref[0]
    kb = kb_ref[0]
    lane = lax.broadcasted_iota(jnp.int32, kf.shape, 1)
    width = S5_L * S5_C
    for j in range(S5_L):
        f = kf if j == 0 else jnp.where(lane >= S5_C * j, pltpu.roll(kf, S5_C * j, 1), 0.0)
        back = S5_L - 1 - j
        b = kb if back == 0 else jnp.where(lane < S5_C * (j + 1), pltpu.roll(kb, width - S5_C * back, 1), 0.0)
        o_ref[0, 0, :, j] = f.reshape(S5_G, S5_C, width).astype(o_ref.dtype)
        o_ref[0, 1, :, j] = b.reshape(S5_G, S5_C, width).astype(o_ref.dtype)


def _s5_operators(lam_re, lam_im, log_step, b_re, b_im, c_re, c_im):
    nl = lam_re.shape[0]
    dt = jnp.exp(log_step)[..., None]
    mag = jnp.exp(lam_re * dt)
    a_re = mag * jnp.cos(lam_im * dt)
    a_im = mag * jnp.sin(lam_im * dt)
    den = lam_re * lam_re + lam_im * lam_im
    f_re = ((a_re - 1) * lam_re + a_im * lam_im) / den
    f_im = (a_im * lam_re - (a_re - 1) * lam_im) / den
    bb_re = f_re[..., None] * b_re - f_im[..., None] * b_im
    bb_im = f_re[..., None] * b_im + f_im[..., None] * b_re
    bt_re = jnp.swapaxes(bb_re, -1, -2)
    bt_im = jnp.swapaxes(bb_im, -1, -2)
    pr = [jnp.ones_like(a_re)]
    pi = [jnp.zeros_like(a_im)]
    for _ in range(S5_L):
        pr.append(pr[-1] * a_re - pi[-1] * a_im)
        pi.append(pr[-2] * a_im + pi[-1] * a_re)
    pr = jnp.stack(pr, axis=3)
    pi = jnp.stack(pi, axis=3)

    def times_b(qr, qi):
        qr, qi = qr[..., None, :], qi[..., None, :]
        br, bi = bt_re[:, :, :, None], bt_im[:, :, :, None]
        return qr * br - qi * bi, qr * bi + qi * br

    wr, wi = times_b(pr[:, :, :, :S5_L], pi[:, :, :, :S5_L])
    wr = jnp.swapaxes(wr, 3, 4)
    wi = jnp.swapaxes(wi, 3, 4)
    kern = (jnp.einsum("ldgktp,ldgcp->ldgktc", wr, c_re, precision=HP)
            - jnp.einsum("ldgktp,ldgcp->ldgktc", wi, c_im, precision=HP))
    kf = kern[:, 0].reshape(nl, S5_G * S5_C, S5_L * S5_C)
    kb = jnp.flip(kern[:, 1], axis=3).reshape(nl, S5_G * S5_C, S5_L * S5_C)
    rows_blk = pl.BlockSpec((1, S5_G * S5_C, S5_L * S5_C), lambda l: (l, 0, 0))
    m = pl.pallas_call(
        _s5_toeplitz_kernel,
        grid=(nl,),
        in_specs=[rows_blk, rows_blk],
        out_specs=pl.BlockSpec((1, 2, S5_G, S5_L, S5_C, S5_L * S5_C), lambda l: (l, 0, 0, 0, 0, 0)),
        out_shape=jax.ShapeDtypeStruct((nl, 2, S5_G, S5_L, S5_C, S5_L * S5_C), BF16),
        compiler_params=_cp("parallel"),
        name="s5_toeplitz",
    )(kf, kb).reshape(nl, 2, S5_G, S5_L * S5_C, S5_L * S5_C)

    def stack_dirs(fwd, bwd):
        return jnp.stack([fwd[:, 0], bwd[:, 1]], axis=1)

    er, ei = times_b(stack_dirs(jnp.flip(pr[:, :, :, :S5_L], axis=3), pr[:, :, :, :S5_L]),
                     stack_dirs(jnp.flip(pi[:, :, :, :S5_L], axis=3), pi[:, :, :, :S5_L]))
    e = jnp.concatenate([er, ei], axis=-1).reshape(nl, 2, S5_G, S5_L * S5_C, 2 * S5_P).astype(BF16)
    qr = stack_dirs(pr[:, :, :, 1:], jnp.flip(pr[:, :, :, 1:], axis=3))[..., None, :]
    qi = stack_dirs(pi[:, :, :, 1:], jnp.flip(pi[:, :, :, 1:], axis=3))[..., None, :]
    cr, ci = c_re[:, :, :, None], c_im[:, :, :, None]
    ft = jnp.concatenate([cr * qr - ci * qi, -(cr * qi + ci * qr)], axis=-1)
    ft = ft.reshape(nl, 2, S5_G, S5_L * S5_C, 2 * S5_P).astype(BF16)
    return m, e, ft, pr[:, :, :, S5_L], pi[:, :, :, S5_L]


def _s5_local_kernel(u_ref, m_ref, e_ref, y_ref, x_ref):
    u = u_ref[0]
    for d in range(2):
        y_ref[d, 0] = jnp.dot(u, m_ref[d, 0], preferred_element_type=F32)
        x_ref[d, 0] = jnp.dot(u, e_ref[d, 0], preferred_element_type=F32)


def _s5_carry_kernel(xc_ref, ar_ref, ai_ref, x0_ref, st_ref):
    @pl.when(pl.program_id(1) == 0)
    def _():
        st_ref[...] = jnp.zeros_like(st_ref)

    ar = ar_ref[0]
    ai = ai_ref[0]

    def run(order):
        x = st_ref[...]
        for i in order:
            x0_ref[0, :, :, i, :] = x
            x = x * ar + pltpu.roll(x, S5_P, 2) * ai + xc_ref[0, :, :, i, :]
        st_ref[...] = x

    n = xc_ref.shape[3]

    @pl.when(pl.program_id(0) == 0)
    def _():
        run(range(n))

    @pl.when(pl.program_id(0) == 1)
    def _():
        run(reversed(range(n)))


def _s5_state_kernel(y_ref, x0_ref, ft_ref, o_ref):
    acc = y_ref[0, 0] + y_ref[1, 0]
    for d in range(2):
        acc = acc + _bdot_nt(x0_ref[d, 0], ft_ref[d, 0])
    o_ref[0] = acc.astype(o_ref.dtype)


def _s5_glu_kernel(y_ref, u_ref, d_ref, w_ref, o_ref):
    x = y_ref[...] + u_ref[...] * d_ref[...]
    ge = 0.5 * x * (1.0 + jnp.tanh(math.sqrt(2.0 / math.pi) * (x + 0.044715 * (x * x * x))))
    p = _bdot(ge, w_ref[...])
    o_ref[...] = (p[:, :S5_W] * _sigmoid(p[:, S5_W:])).astype(o_ref.dtype)


def _s5(z, m, e, ft, a_re, a_im, d_skip, w_glu):
    n_ch = (SEQ + CTX) // S5_L
    n_ctx_ch = CTX // S5_L
    rows = NB * n_ch
    width = S5_L * S5_C

    def to_groups(x):
        x = x.reshape(NB, -1, S5_L, S5_G, S5_C)
        return jnp.transpose(x, (3, 0, 1, 2, 4)).reshape(S5_G, NB, -1, width)

    zb = z.astype(BF16)
    u = jnp.concatenate([to_groups(zb[N_LAT:]), to_groups(zb[:N_LAT])], axis=2).reshape(S5_G, rows, width)
    y_loc, x_in = pl.pallas_call(
        _s5_local_kernel,
        grid=(S5_G,),
        in_specs=[pl.BlockSpec((1, rows, width), lambda g: (g, 0, 0)),
                  pl.BlockSpec((2, 1, width, width), lambda g: (0, g, 0, 0)),
                  pl.BlockSpec((2, 1, width, 2 * S5_P), lambda g: (0, g, 0, 0))],
        out_specs=[pl.BlockSpec((2, 1, rows, width), lambda g: (0, g, 0, 0)),
                   pl.BlockSpec((2, 1, rows, 2 * S5_P), lambda g: (0, g, 0, 0))],
        out_shape=[jax.ShapeDtypeStruct((2, S5_G, rows, width), F32),
                   jax.ShapeDtypeStruct((2, S5_G, rows, 2 * S5_P), F32)],
        compiler_params=_cp("parallel"),
        name="s5_local",
    )(u, m, e)

    step = n_ctx_ch
    n_blk = n_ch // step
    coef_r = jnp.concatenate([a_re, a_re], axis=-1)[:, :, None, :]
    coef_i = jnp.concatenate([-a_im, a_im], axis=-1)[:, :, None, :]

    def chunk_blk(d, i):
        return jnp.where(d == 0, i, jnp.where(i == 0, 0, n_blk - i))

    st_blk = pl.BlockSpec((1, S5_G, NB, step, 2 * S5_P), lambda d, i: (d, 0, 0, chunk_blk(d, i), 0))
    coef_blk = pl.BlockSpec((1, S5_G, 1, 2 * S5_P), lambda d, i: (d, 0, 0, 0))
    x0 = pl.pallas_call(
        _s5_carry_kernel,
        grid=(2, n_blk),
        in_specs=[st_blk, coef_blk, coef_blk],
        out_specs=st_blk,
        out_shape=jax.ShapeDtypeStruct((2, S5_G, NB, n_ch, 2 * S5_P), F32),
        scratch_shapes=[pltpu.VMEM((S5_G, NB, 2 * S5_P), F32)],
        compiler_params=_cp("arbitrary", "arbitrary"),
        name="s5_carry",
    )(x_in.reshape(2, S5_G, NB, n_ch, 2 * S5_P), coef_r, coef_i).reshape(2, S5_G, rows, 2 * S5_P)

    y = pl.pallas_call(
        _s5_state_kernel,
        grid=(S5_G,),
        in_specs=[pl.BlockSpec((2, 1, rows, width), lambda g: (0, g, 0, 0)),
                  pl.BlockSpec((2, 1, rows, 2 * S5_P), lambda g: (0, g, 0, 0)),
                  pl.BlockSpec((2, 1, width, 2 * S5_P), lambda g: (0, g, 0, 0))],
        out_specs=pl.BlockSpec((1, rows, width), lambda g: (g, 0, 0)),
        out_shape=jax.ShapeDtypeStruct((S5_G, rows, width), BF16),
        compiler_params=_cp("parallel"),
        name="s5_state",
    )(y_loc, x0, ft)

    y = jnp.transpose(y.reshape(S5_G, NB, n_ch, S5_L, S5_C), (1, 2, 3, 0, 4)).reshape(NB, n_ch * S5_L, S5_W)
    y = jnp.concatenate([y[:, CTX:].reshape(N_LAT, S5_W), y[:, :CTX].reshape(N_CTX, S5_W)], axis=0)
    tok = pl.BlockSpec((TILE, S5_W), lambda t: (t, 0))
    return pl.pallas_call(
        _s5_glu_kernel,
        grid=(N_TILES,),
        in_specs=[tok, tok, pl.BlockSpec((1, S5_W), lambda t: (0, 0)),
                  pl.BlockSpec((S5_W, 2 * S5_W), lambda t: (0, 0))],
        out_specs=tok,
        out_shape=jax.ShapeDtypeStruct((ROWS, S5_W), BF16),
        compiler_params=_cp("parallel"),
        name="s5_glu",
    )(y, z, d_skip.reshape(1, S5_W), w_glu)


def kernel(x, c, ctx, c_ctx, w_mod, b_mod, norm1_g, norm2_g, w_in, b_gate, q_norm_g, k_norm_g, rwkv_mu, rwkv_w0, rwkv_w_up, rwkv_a0, rwkv_a_up, rwkv_g_up, rwkv_k_k, rwkv_k_a, rwkv_r_k, rwkv_gn_g, rwkv_gn_b, mlstm_conv_w, mlstm_i_b, mlstm_f_b, mlstm_gn_g, s5_lam_re, s5_lam_im, s5_log_step, s5_b_re, s5_b_im, s5_c_re, s5_c_im, s5_d, s5_w_glu, w_br_attn, w_br_rwkv, w_br_mlstm, w_br_s5, w_out, w_ffn_in, w_ffn_out, final_norm_g):
    cos, sin = _rope_tables()
    xs = jnp.concatenate([x.reshape(N_LAT, D), ctx.reshape(N_CTX, D)], axis=0)
    c_all = jnp.concatenate([c, c_ctx[None], jnp.zeros((3, D), F32)], axis=0)
    s5_m, s5_e, s5_ft, s5_ar, s5_ai = _s5_operators(s5_lam_re, s5_lam_im, s5_log_step, s5_b_re, s5_b_im,
                                                    s5_c_re, s5_c_im)
    tm = 1024
    for l in range(DEPTH):
        last = l == DEPTH - 1
        n_rows = N_LAT if last else ROWS
        mod = _modulation(c_all, w_mod, b_mod, l).reshape(8, 1, 6 * D)
        h = _norm_mod(xs, norm1_g[l], mod, 0, 1, N_TILES)
        w_attn, w_rwkv, w_ml, w_mlg, w_s5, w_gate = _w_in_split(w_in, l)
        z_attn = _mm(h, w_attn, tm, 512)
        z_rwkv = _mm(h, w_rwkv, tm, 896)
        z_ml = _mm(h, w_ml, tm, 1024)
        z_mlg = _mm(h, w_mlg, tm, 128)
        z_s5 = _mm(h, w_s5, tm, 512)
        gates = _mm_gate(h, w_gate, b_gate, l, tm, 1024, n_rows)

        ya = _attention(z_attn, q_norm_g[l], k_norm_g[l], cos, sin)
        rp = dict(mu=rwkv_mu[l], w0=rwkv_w0[l], w_up=rwkv_w_up[l], a0=rwkv_a0[l], a_up=rwkv_a_up[l],
                  g_up=rwkv_g_up[l], k_k=rwkv_k_k[l], k_a=rwkv_k_a[l], r_k=rwkv_r_k[l].reshape(RW_W))
        r, v, kk, g, bonus, lw, bmat, km = _rwkv_prep(z_rwkv, rp)
        yr = _rwkv_scan(r, v, kk, lw, bmat, km, bonus, g, rwkv_gn_g[l], rwkv_gn_b[l])
        ym = _mlstm(z_ml, z_mlg, mlstm_conv_w[l], mlstm_i_b[l], mlstm_f_b[l], mlstm_gn_g[l])
        ys = _s5(z_s5, s5_m[l], s5_e[l], s5_ft[l], s5_ar[l], s5_ai[l], s5_d[l], s5_w_glu[l])

        y = _merge(ya, yr, ym, ys, gates, w_br_attn, w_br_rwkv, w_br_mlstm, w_br_s5, l, tm, 512, n_rows)
        xs = _mm_res(y, w_out, l, xs, mod, 2, tm, 1024, n_rows)
        h2 = _norm_mod(xs, norm2_g[l], mod, 3, 4, n_rows // TILE)
        u = _ffn_in(h2, w_ffn_in, l, tm, 512, n_rows)
        xs = _mm_res(u, w_ffn_out, l, xs, mod, 5, 512, 512, n_rows)
    return _final_norm(xs, final_norm_g).reshape(NB, SEQ, D)
```

```python
import functools
import math

import numpy as np
import jax
import jax.numpy as jnp
from jax import lax
from jax.experimental import pallas as pl
from jax.experimental.pallas import tpu as pltpu

F32 = jnp.float32
BF16 = jnp.bfloat16
HP = lax.Precision.HIGHEST

D = 2048
NB = 4
SEQ = 2048
CTX = 256
DEPTH = 2
N_LAT = NB * SEQ
N_CTX = NB * CTX
ROWS = N_LAT + N_CTX
EPS = 1e-6
GRID_W = 64

HEAD_DIM = 128
ATTN_HEADS = 8
ATTN_KV = 2
ROPE_THETA = 10000.0
ATTN_Q = ATTN_HEADS * HEAD_DIM
ATTN_IN = (ATTN_HEADS + 2 * ATTN_KV) * HEAD_DIM

RW_H = 8
RW_D = 64
RW_W = 512
RW_IN = 3 * RW_W + 64 + 64 + 128
RW_DECAY = math.exp(-0.5)
RW_GN_EPS = 64e-5

ML_H = 4
ML_D = 128
ML_W = 512
ML_NEG = -1e30
ML_IN = 4 * ML_W + 4 * ML_H

S5_W = 512
S5_C = 16
S5_G = 32
S5_P = 64
S5_L = 16

FFN_H = 5632
GATE_IN = 4 * D

CHUNK = 64
TILE = 256
N_TILES = ROWS // TILE
SEQ_CHUNKS = (SEQ + CTX) // CHUNK
CTX_CHUNKS = CTX // CHUNK
LAT_CHUNKS = SEQ // CHUNK

VMEM_LIMIT_BYTES = 56 * 1024 * 1024


def _cp(*sem):
    return pltpu.CompilerParams(dimension_semantics=sem, vmem_limit_bytes=VMEM_LIMIT_BYTES)


def _bdot(a, b):
    return jnp.dot(a.astype(BF16), b.astype(BF16), preferred_element_type=F32)


def _bdot_nt(a, b):
    return lax.dot_general(a.astype(BF16), b.astype(BF16), (((1,), (1,)), ((), ())),
                           preferred_element_type=F32)


def _bdot_tn(a, b):
    return lax.dot_general(a.astype(BF16), b.astype(BF16), (((0,), (0,)), ((), ())),
                           preferred_element_type=F32)


def _hdot(a, b):
    return jnp.dot(a, b, precision=HP, preferred_element_type=F32)


def _sigmoid(x):
    return 1.0 / (1.0 + jnp.exp(-x))


def _silu(x):
    return x * _sigmoid(x)


def _mod_kernel(c_ref, w_ref, b_ref, o_ref):
    c_hi, c_lo = _split_bf16(_silu(c_ref[...]), 2)
    w_hi, w_lo = _split_bf16(w_ref[...], 2)
    rows = c_hi.shape[0]
    both = jnp.dot(jnp.concatenate([c_hi, c_lo], axis=0), w_hi, preferred_element_type=F32)
    o_ref[...] = both[:rows] + both[rows:] + jnp.dot(c_hi, w_lo, preferred_element_type=F32) + b_ref[...]


def _modulation(c_all, w, b, l):
    tn = 1024
    return pl.pallas_call(
        _mod_kernel,
        grid=(6 * D // tn,),
        in_specs=[pl.BlockSpec((8, D), lambda j: (0, 0)),
                  pl.BlockSpec((None, D, tn), lambda j: (l, 0, j)),
                  pl.BlockSpec((None, 1, tn), lambda j: (l, 0, j))],
        out_specs=pl.BlockSpec((8, tn), lambda j: (0, j)),
        out_shape=jax.ShapeDtypeStruct((8, 6 * D), F32),
        compiler_params=_cp("arbitrary"),
        name="modulation",
    )(c_all, w, b.reshape(DEPTH, 1, 6 * D))


def _mod_row(i, tm):
    return jnp.where(i * tm < N_LAT, (i * tm) // SEQ, NB)


def _norm_mod_kernel(x_ref, g_ref, sh_ref, sc_ref, o_ref):
    x = x_ref[...]
    y = x * lax.rsqrt(jnp.mean(x * x, axis=-1, keepdims=True) + EPS) * g_ref[...]
    o_ref[...] = (y * (1.0 + sc_ref[0]) + sh_ref[0]).astype(o_ref.dtype)


NORM_TILE = 512


def _norm_mod(x, g, modr, shift_blk, scale_blk, n_rows):
    tr = NORM_TILE
    return pl.pallas_call(
        _norm_mod_kernel,
        grid=(n_rows // tr,),
        in_specs=[pl.BlockSpec((tr, D), lambda i: (i, 0)),
                  pl.BlockSpec((1, D), lambda i: (0, 0)),
                  pl.BlockSpec((1, 1, D), lambda i: (_mod_row(i, tr), 0, shift_blk)),
                  pl.BlockSpec((1, 1, D), lambda i: (_mod_row(i, tr), 0, scale_blk))],
        out_specs=pl.BlockSpec((tr, D), lambda i: (i, 0)),
        out_shape=jax.ShapeDtypeStruct((n_rows, D), BF16),
        compiler_params=_cp("parallel"),
        name="norm_mod",
    )(x, g.reshape(1, D), modr, modr)


W_IN_OFFSETS = tuple(int(v) for v in np.cumsum([0, ATTN_IN, RW_IN, 4 * ML_W, 4 * ML_H, S5_W, GATE_IN]))
ML_GATE_PAD = 128


W_TILE = 256
W_ALIGN = 16


def _w_in_group_kernel(*refs, shift, valid):
    a_ref, o_ref = refs[0], refs[-1]
    x = a_ref[...]
    if shift:
        x = jnp.concatenate([x[shift:], refs[1][...]], axis=0)
    if valid < x.shape[0]:
        row = lax.broadcasted_iota(jnp.int32, x.shape, 0)
        x = jnp.where(row < valid, x, 0.0)
    o_ref[...] = x.T.astype(BF16)


def _w_in_group(wt, l, start, n_cols):
    tile = W_TILE if n_cols >= W_TILE else ML_GATE_PAD
    a0, shift = divmod(start, tile)
    assert shift in (0, W_ALIGN)
    n_tiles = -(-n_cols // tile)
    in_specs = [pl.BlockSpec((None, tile, D), lambda t: (l, a0 + t, 0))]
    if shift:
        per = tile // W_ALIGN
        in_specs.append(pl.BlockSpec((None, W_ALIGN, D), lambda t: (l, (a0 + t + 1) * per, 0)))
    return pl.pallas_call(
        functools.partial(_w_in_group_kernel, shift=shift, valid=min(n_cols, tile)),
        grid=(n_tiles,),
        in_specs=in_specs,
        out_specs=pl.BlockSpec((D, tile), lambda t: (0, t)),
        out_shape=jax.ShapeDtypeStruct((D, n_tiles * tile), BF16),
        compiler_params=_cp("parallel"),
        name="w_in_group",
    )(*([wt] * len(in_specs)))


def _w_in_split(w, l):
    wt = jnp.swapaxes(w, 1, 2)
    o = W_IN_OFFSETS
    return tuple(_w_in_group(wt, l, o[i], o[i + 1] - o[i]) for i in range(6))


def _mm_kernel(a_ref, w_ref, o_ref):
    o_ref[...] = jnp.dot(a_ref[...], w_ref[...], preferred_element_type=F32).astype(o_ref.dtype)


def _mm(a, w, tm, tn, out_dtype=F32):
    m, k = a.shape
    n = w.shape[1]
    return pl.pallas_call(
        _mm_kernel,
        grid=(n // tn, m // tm),
        in_specs=[pl.BlockSpec((tm, k), lambda j, i: (i, 0)),
                  pl.BlockSpec((k, tn), lambda j, i: (0, j))],
        out_specs=pl.BlockSpec((tm, tn), lambda j, i: (i, j)),
        out_shape=jax.ShapeDtypeStruct((m, n), out_dtype),
        compiler_params=_cp("parallel", "parallel"),
        name="matmul",
    )(a, w)


ROW_SUB = 256


def _row_blocks(ref):
    return [slice(r, r + ROW_SUB) for r in range(0, ref.shape[0], ROW_SUB)]


def _mm_gate_kernel(a_ref, w_ref, b_ref, o_ref):
    for rows in _row_blocks(o_ref):
        z = jnp.dot(a_ref[rows, :], w_ref[...], preferred_element_type=F32)
        o_ref[rows, :] = _sigmoid(z + b_ref[...]).astype(o_ref.dtype)


def _mm_gate(a, w, b_gate, l, tm, tn, m):
    k = a.shape[1]
    n = w.shape[1]
    return pl.pallas_call(
        _mm_gate_kernel,
        grid=(n // tn, m // tm),
        in_specs=[pl.BlockSpec((tm, k), lambda j, i: (i, 0)),
                  pl.BlockSpec((k, tn), lambda j, i: (0, j)),
                  pl.BlockSpec((None, 1, tn), lambda j, i: (l, 0, j))],
        out_specs=pl.BlockSpec((tm, tn), lambda j, i: (i, j)),
        out_shape=jax.ShapeDtypeStruct((m, n), BF16),
        compiler_params=_cp("parallel", "parallel"),
        name="matmul_gate",
    )(a, w, b_gate.reshape(DEPTH, 1, GATE_IN))


def _mm_res_kernel(a_ref, w_ref, x_ref, g_ref, o_ref, wb_ref):
    @pl.when(pl.program_id(1) == 0)
    def _():
        wb_ref[...] = w_ref[...].astype(BF16)

    for rows in _row_blocks(o_ref):
        y = jnp.dot(a_ref[rows, :], wb_ref[...], preferred_element_type=F32)
        o_ref[rows, :] = x_ref[rows, :] + g_ref[0] * y


def _mm_res(a, w, l, x, mod, gate_blk, tm, tn, n_rows):
    k = a.shape[1]
    n = w.shape[2]
    gpb = D // tn
    return pl.pallas_call(
        _mm_res_kernel,
        grid=(n // tn, n_rows // tm),
        in_specs=[pl.BlockSpec((tm, k), lambda j, i: (i, 0)),
                  pl.BlockSpec((None, k, tn), lambda j, i: (l, 0, j)),
                  pl.BlockSpec((tm, tn), lambda j, i: (i, j)),
                  pl.BlockSpec((1, 1, tn), lambda j, i: (_mod_row(i, tm), 0, gate_blk * gpb + j))],
        out_specs=pl.BlockSpec((tm, tn), lambda j, i: (i, j)),
        out_shape=jax.ShapeDtypeStruct((n_rows, n), F32),
        scratch_shapes=[pltpu.VMEM((k, tn), BF16)],
        compiler_params=_cp("arbitrary", "arbitrary"),
        name="matmul_residual",
    )(a, w, x, mod)


def _ffn_in_kernel(a_ref, wa_ref, wb_ref, o_ref, wab_ref, wbb_ref):
    @pl.when(pl.program_id(1) == 0)
    def _():
        wab_ref[...] = wa_ref[...].astype(BF16)
        wbb_ref[...] = wb_ref[...].astype(BF16)

    for rows in _row_blocks(o_ref):
        a = a_ref[rows, :]
        u = jnp.dot(a, wab_ref[...], preferred_element_type=F32)
        v = jnp.dot(a, wbb_ref[...], preferred_element_type=F32)
        o_ref[rows, :] = (_silu(u) * v).astype(o_ref.dtype)


def _ffn_in(h, w, l, tm, tn, n_rows):
    nb = FFN_H // tn
    return pl.pallas_call(
        _ffn_in_kernel,
        grid=(nb, n_rows // tm),
        in_specs=[pl.BlockSpec((tm, D), lambda j, i: (i, 0)),
                  pl.BlockSpec((None, D, tn), lambda j, i: (l, 0, j)),
                  pl.BlockSpec((None, D, tn), lambda j, i: (l, 0, nb + j))],
        out_specs=pl.BlockSpec((tm, tn), lambda j, i: (i, j)),
        out_shape=jax.ShapeDtypeStruct((n_rows, FFN_H), BF16),
        scratch_shapes=[pltpu.VMEM((D, tn), BF16), pltpu.VMEM((D, tn), BF16)],
        compiler_params=_cp("arbitrary", "arbitrary"),
        name="ffn_in",
    )(h, w, w)


def _merge_kernel(ya_ref, yr_ref, ym_ref, ys_ref, ga_ref, gr_ref, gm_ref, gs_ref,
                  wa_ref, wr_ref, wm_ref, ws_ref, o_ref, wab_ref, wrb_ref, wmb_ref, wsb_ref):
    @pl.when(pl.program_id(1) == 0)
    def _():
        wab_ref[...] = wa_ref[...].astype(BF16)
        wrb_ref[...] = wr_ref[...].astype(BF16)
        wmb_ref[...] = wm_ref[...].astype(BF16)
        wsb_ref[...] = ws_ref[...].astype(BF16)

    for rows in _row_blocks(o_ref):
        acc = None
        for y_ref, g_ref, w_ref in ((ya_ref, ga_ref, wab_ref), (yr_ref, gr_ref, wrb_ref),
                                    (ym_ref, gm_ref, wmb_ref), (ys_ref, gs_ref, wsb_ref)):
            term = g_ref[rows, :].astype(F32) * jnp.dot(y_ref[rows, :], w_ref[...], preferred_element_type=F32)
            acc = term if acc is None else acc + term
        o_ref[rows, :] = acc.astype(o_ref.dtype)


def _merge(ya, yr, ym, ys, gates, wa, wr, wm, ws, l, tm, tn, n_rows):
    nb = D // tn

    def act(width):
        return pl.BlockSpec((tm, width), lambda j, i: (i, 0))

    def gate(br):
        return pl.BlockSpec((tm, tn), lambda j, i: (i, br * nb + j))

    def wgt(width):
        return pl.BlockSpec((None, width, tn), lambda j, i: (l, 0, j))

    return pl.pallas_call(
        _merge_kernel,
        grid=(nb, n_rows // tm),
        in_specs=[act(ATTN_Q), act(RW_W), act(ML_W), act(S5_W),
                  gate(0), gate(1), gate(2), gate(3),
                  wgt(ATTN_Q), wgt(RW_W), wgt(ML_W), wgt(S5_W)],
        out_specs=pl.BlockSpec((tm, tn), lambda j, i: (i, j)),
        out_shape=jax.ShapeDtypeStruct((n_rows, D), BF16),
        scratch_shapes=[pltpu.VMEM((ATTN_Q, tn), BF16), pltpu.VMEM((RW_W, tn), BF16),
                        pltpu.VMEM((ML_W, tn), BF16), pltpu.VMEM((S5_W, tn), BF16)],
        compiler_params=_cp("arbitrary", "arbitrary"),
        name="gated_merge",
    )(ya, yr, ym, ys, gates, gates, gates, gates, wa, wr, wm, ws)


def _final_norm_kernel(x_ref, g_ref, o_ref):
    x = x_ref[...]
    o_ref[...] = x * lax.rsqrt(jnp.mean(x * x, axis=-1, keepdims=True) + EPS) * g_ref[...]


def _final_norm(x, g):
    tr = NORM_TILE
    return pl.pallas_call(
        _final_norm_kernel,
        grid=(N_LAT // tr,),
        in_specs=[pl.BlockSpec((tr, D), lambda i: (i, 0)),
                  pl.BlockSpec((1, D), lambda i: (0, 0))],
        out_specs=pl.BlockSpec((tr, D), lambda i: (i, 0)),
        out_shape=jax.ShapeDtypeStruct((N_LAT, D), F32),
        compiler_params=_cp("parallel"),
        name="final_norm",
    )(x, g.reshape(1, D))


def _tile_halo_specs(width, col_blk=0):
    last = ROWS // 8 - 1
    per = TILE // 8
    return [pl.BlockSpec((TILE, width), lambda t: (t, col_blk)),
            pl.BlockSpec((8, width), lambda t: (jnp.maximum(t * per - 1, 0), col_blk)),
            pl.BlockSpec((8, width), lambda t: (jnp.minimum((t + 1) * per, last), col_blk))]


def _neighbours(z, prev_blk, next_blk):
    t = pl.program_id(0)
    pos = t % (SEQ // TILE)
    is_lat = t < N_LAT // TILE
    has_prev = jnp.logical_and(is_lat, pos > 0).astype(F32)
    has_next = jnp.logical_and(is_lat, pos < SEQ // TILE - 1).astype(F32)
    row = lax.broadcasted_iota(jnp.int32, z.shape, 0)
    zp = jnp.where(row == 0, prev_blk[7:8, :] * has_prev, pltpu.roll(z, 1, 0))
    zn = jnp.where(row == TILE - 1, next_blk[0:1, :] * has_next, pltpu.roll(z, TILE - 1, 0))
    return zp, zn


def _seq_row_block(b, d, n):
    ctx_c = jnp.where(d == 0, n, CTX_CHUNKS - 1 - n)
    lat_c = jnp.where(d == 0, n - CTX_CHUNKS, SEQ_CHUNKS - 1 - n)
    return jnp.where(n < CTX_CHUNKS, N_LAT // CHUNK + CTX_CHUNKS * b + ctx_c, LAT_CHUNKS * b + lat_c)


def _rope(x, cos, sin):
    lane = lax.broadcasted_iota(jnp.int32, x.shape, 1)
    first = (lane % 64) < 32
    partner = jnp.where(first, pltpu.roll(x, 96, 1), pltpu.roll(x, 32, 1))
    return x * cos + partner * sin


def _rms(x, g):
    return x * lax.rsqrt(jnp.mean(x * x, axis=-1, keepdims=True) + EPS) * g


def _attn_kernel(q_ref, kl_ref, kc_ref, vl_ref, vc_ref, cos_ref, sin_ref, cos_t_ref, sin_t_ref,
                 qg_ref, kg_ref, o_ref, klb_ref, kcb_ref, vlb_ref, vcb_ref):
    qi = pl.program_id(2)
    n_lat_tiles = SEQ // TILE

    @pl.when(qi == 0)
    def _():
        kg = kg_ref[...]
        klb_ref[...] = _rope(_rms(kl_ref[...], kg), cos_ref[...], sin_ref[...]).astype(BF16)
        kcb_ref[...] = _rms(kc_ref[...], kg).astype(BF16)
        vlb_ref[...] = vl_ref[...].astype(BF16)
        vcb_ref[...] = vc_ref[...].astype(BF16)

    scale = HEAD_DIM ** -0.5 * math.log2(math.e)
    nt = (((1,), (1,)), ((), ()))

    def heads(latent):
        sls = [slice(h * HEAD_DIM, (h + 1) * HEAD_DIM) for h in range(ATTN_HEADS // ATTN_KV)]
        q = [_rms(q_ref[:, sl], qg_ref[...]) for sl in sls]
        if latent:
            q = [_rope(x, cos_t_ref[...], sin_t_ref[...]) for x in q]
        q = [(x * scale).astype(BF16) for x in q]
        s_c = [lax.dot_general(x, kcb_ref[...], nt, preferred_element_type=F32) for x in q]
        m = [jnp.max(x, axis=-1, keepdims=True) for x in s_c]
        if latent:
            s_l = [lax.dot_general(x, klb_ref[...], nt, preferred_element_type=F32) for x in q]
            m = [jnp.maximum(a, jnp.max(x, axis=-1, keepdims=True)) for a, x in zip(m, s_l)]
        p_c = [jnp.exp2(x - a) for x, a in zip(s_c, m)]
        den = [jnp.sum(x, axis=-1, keepdims=True) for x in p_c]
        acc = [jnp.dot(x.astype(BF16), vcb_ref[...], preferred_element_type=F32) for x in p_c]
        if latent:
            p_l = [jnp.exp2(x - a) for x, a in zip(s_l, m)]
            den = [a + jnp.sum(x, axis=-1, keepdims=True) for a, x in zip(den, p_l)]
            acc = [a + jnp.dot(x.astype(BF16), vlb_ref[...], preferred_element_type=F32) for a, x in zip(acc, p_l)]
        for sl, a, dn in zip(sls, acc, den):
            o_ref[:, sl] = (a / dn).astype(o_ref.dtype)

    @pl.when(qi < n_lat_tiles)
    def _():
        heads(True)

    @pl.when(qi == n_lat_tiles)
    def _():
        heads(False)


def _rope_tables():
    rows = SEQ // GRID_W
    row = jnp.repeat(jnp.arange(rows, dtype=F32), GRID_W)
    col = jnp.tile(jnp.arange(GRID_W, dtype=F32), rows)
    axis_dim = HEAD_DIM // 2
    inv_freq = ROPE_THETA ** (-jnp.arange(0, axis_dim, 2, dtype=F32) / axis_dim)
    ang_r = row[:, None] * inv_freq[None]
    ang_c = col[:, None] * inv_freq[None]
    cos = jnp.concatenate([jnp.cos(ang_r), jnp.cos(ang_r), jnp.cos(ang_c), jnp.cos(ang_c)], axis=-1)
    sin = jnp.concatenate([-jnp.sin(ang_r), jnp.sin(ang_r), -jnp.sin(ang_c), jnp.sin(ang_c)], axis=-1)
    return cos, sin


def _attention(z, q_g, k_g, cos, sin):
    n_lat_tiles = SEQ // TILE
    qw = ATTN_Q // ATTN_KV
    kcol = ATTN_Q // HEAD_DIM
    vcol = kcol + ATTN_KV

    def q_row(b, g, qi):
        return jnp.where(qi < n_lat_tiles, n_lat_tiles * b + qi, N_LAT // TILE + b)

    return pl.pallas_call(
        _attn_kernel,
        grid=(NB, ATTN_KV, n_lat_tiles + 1),
        in_specs=[pl.BlockSpec((TILE, qw), lambda b, g, qi: (q_row(b, g, qi), g)),
                  pl.BlockSpec((SEQ, HEAD_DIM), lambda b, g, qi: (b, kcol + g)),
                  pl.BlockSpec((CTX, HEAD_DIM), lambda b, g, qi: (N_LAT // CTX + b, kcol + g)),
                  pl.BlockSpec((SEQ, HEAD_DIM), lambda b, g, qi: (b, vcol + g)),
                  pl.BlockSpec((CTX, HEAD_DIM), lambda b, g, qi: (N_LAT // CTX + b, vcol + g)),
                  pl.BlockSpec((SEQ, HEAD_DIM), lambda b, g, qi: (0, 0)),
                  pl.BlockSpec((SEQ, HEAD_DIM), lambda b, g, qi: (0, 0)),
                  pl.BlockSpec((TILE, HEAD_DIM), lambda b, g, qi: (jnp.minimum(qi, n_lat_tiles - 1), 0)),
                  pl.BlockSpec((TILE, HEAD_DIM), lambda b, g, qi: (jnp.minimum(qi, n_lat_tiles - 1), 0)),
                  pl.BlockSpec((1, HEAD_DIM), lambda b, g, qi: (0, 0)),
                  pl.BlockSpec((1, HEAD_DIM), lambda b, g, qi: (0, 0))],
        out_specs=pl.BlockSpec((TILE, qw), lambda b, g, qi: (q_row(b, g, qi), g)),
        out_shape=jax.ShapeDtypeStruct((ROWS, ATTN_Q), BF16),
        scratch_shapes=[pltpu.VMEM((SEQ, HEAD_DIM), BF16), pltpu.VMEM((CTX, HEAD_DIM), BF16),
                        pltpu.VMEM((SEQ, HEAD_DIM), BF16), pltpu.VMEM((CTX, HEAD_DIM), BF16)],
        compiler_params=_cp("arbitrary", "arbitrary", "arbitrary"),
        name="attention",
    )(z, z, z, z, z, cos, sin, cos, sin, q_g.reshape(1, HEAD_DIM), k_g.reshape(1, HEAD_DIM))


def _rwkv_prep_kernel(z_ref, zp_ref, zn_ref, mu_ref, w0_ref, wup_ref, a0_ref, aup_ref, gup_ref,
                      kk_ref, ka_ref, rk_ref, bd_ref,
                      r_out, v_out, kkn_out, g_out, bonus_out, lw_out, b_out, km_out):
    z = z_ref[...]
    zp, zn = _neighbours(z, zp_ref[...], zn_ref[...])
    zs = z + mu_ref[...] * (0.5 * (zp + zn) - z)
    r = zs[:, 0:RW_W]
    k = zs[:, RW_W:2 * RW_W]
    v = zs[:, 2 * RW_W:3 * RW_W]
    w_lo = zs[:, 3 * RW_W:3 * RW_W + 64]
    a_lo = zs[:, 3 * RW_W + 64:3 * RW_W + 128]
    g_lo = zs[:, 3 * RW_W + 128:3 * RW_W + 256]
    bd = bd_ref[...]
    kk = k * kk_ref[...]
    kk = kk * lax.rsqrt(_dot_rhs_exact(kk * kk, bd) + 1e-12)
    r_out[...] = r
    v_out[...] = v
    kkn_out[...] = kk
    g_out[...] = _bdot(_sigmoid(g_lo), gup_ref[...])
    tw = jnp.tanh(w_lo)
    km_sum = None
    for d in range(2):
        lw = -RW_DECAY * _sigmoid(w0_ref[d] + _bdot(tw, wup_ref[d]))
        a = _sigmoid(a0_ref[d] + _bdot(a_lo, aup_ref[d]))
        km = k * (1.0 + (a - 1.0) * ka_ref[...])
        lw_out[d] = lw
        b_out[d] = a * kk
        km_out[d] = km
        km_sum = km if km_sum is None else km_sum + km
    bonus_out[...] = _dot_rhs_exact(r * km_sum * rk_ref[...], bd) * v


def _rwkv_prep(z, p):
    row = lambda a: a.reshape(1, -1)
    full = lambda shape: pl.BlockSpec(shape, lambda t: (0,) * len(shape))
    out_tok = pl.BlockSpec((TILE, RW_W), lambda t: (t, 0))
    out_dir = pl.BlockSpec((2, TILE, RW_W), lambda t: (0, t, 0))
    tok = jax.ShapeDtypeStruct((ROWS, RW_W), F32)
    drn = jax.ShapeDtypeStruct((2, ROWS, RW_W), F32)
    return pl.pallas_call(
        _rwkv_prep_kernel,
        grid=(N_TILES,),
        in_specs=_tile_halo_specs(RW_IN) + [
            full((1, RW_IN)), full((2, 1, RW_W)), full((2, 64, RW_W)), full((2, 1, RW_W)),
            full((2, 64, RW_W)), full((128, RW_W)), full((1, RW_W)), full((1, RW_W)), full((1, RW_W)),
            full((RW_W, RW_W))],
        out_specs=[out_tok] * 5 + [out_dir] * 3,
        out_shape=[tok] * 5 + [drn] * 3,
        compiler_params=_cp("parallel"),
        name="rwkv_prep",
    )(z, z, z, row(p["mu"]), p["w0"].reshape(2, 1, RW_W), p["w_up"], p["a0"].reshape(2, 1, RW_W),
      p["a_up"], p["g_up"], row(p["k_k"]), row(p["k_a"]), row(p["r_k"]),
      _head_block_ones(RW_W, RW_D).astype(BF16))


def _head_block_ones(width, head):
    idx = np.arange(width) // head
    return jnp.asarray((idx[:, None] == idx[None, :]).astype(np.float32))


def _split_bf16(x, pieces):
    out = []
    for _ in range(pieces):
        p = x.astype(BF16)
        out.append(p)
        x = x - p.astype(F32)
    return out


def _dot_rhs_exact(x, m, pieces=2):
    return sum(jnp.dot(p, m, preferred_element_type=F32) for p in _split_bf16(x, pieces))


def _dot_lhs_exact(m, x, pieces=3):
    return sum(jnp.dot(m, p, preferred_element_type=F32) for p in _split_bf16(x, pieces))


RW_SUB = 2


def _rwkv_chunk_kernel(r_ref, v_ref, kk_ref, lw_ref, b_ref, km_ref, p_ref, sl_ref, re_ref, ol_ref):
    d = pl.program_id(0)
    c = CHUNK
    hd = RW_D
    ti = lax.broadcasted_iota(jnp.int32, (c, c), 0)
    si = lax.broadcasted_iota(jnp.int32, (c, c), 1)
    delta = (ti - si) * (1 - 2 * d)
    incl = delta >= 0
    strict = delta > 0
    eye = jnp.where(ti == si, 1.0, 0.0)
    tri = jnp.where(incl, 1.0, 0.0).astype(BF16)
    gr = lax.broadcasted_iota(jnp.int32, (2 * c, c), 0)
    gc = lax.broadcasted_iota(jnp.int32, (2 * c, c), 1)
    gmask = (gr % c - gc) * (1 - 2 * d) >= jnp.where(gr < c, 1, 0)

    items = []
    for sub in range(RW_SUB):
        rows = slice(sub * c, (sub + 1) * c)
        lw = lw_ref[0, rows, :]
        cs = _dot_lhs_exact(tri, lw)
        tot = jnp.sum(lw, axis=0, keepdims=True)
        r = r_ref[rows, :]
        v = v_ref[rows, :]
        kk = kk_ref[rows, :]
        bb = b_ref[0, rows, :]
        km = km_ref[0, rows, :]
        e_neg = jnp.exp(-cs)
        e_rem = jnp.exp(tot - cs)
        kkt = kk * jnp.exp(cs - lw)
        rt = r * jnp.exp(cs)
        bt = bb * e_neg
        kt = km * e_neg
        bh = bb * e_rem
        kh = km * e_rem
        e_tot = jnp.exp(tot)
        for h in range(RW_H):
            s = slice(h * hd, (h + 1) * hd)
            items.append(dict(sub=sub, s=s, kkt=kkt[:, s], rt=rt[:, s], bt=bt[:, s], kt=kt[:, s], v=v[:, s],
                              bh=bh[:, s], kh=kh[:, s], e_tot=e_tot[:, s]))

    kr = [jnp.concatenate([it["kkt"], it["rt"]], axis=0) for it in items]
    gb = [jnp.where(gmask, _bdot_nt(x, it["bt"]), 0.0) for x, it in zip(kr, items)]
    gk = [jnp.where(gmask, _bdot_nt(x, it["kt"]), 0.0) for x, it in zip(kr, items)]
    l_b = [x[:c] for x in gb]
    a_b = [x[c:] for x in gb]
    lkv_akv = [_bdot(x, it["v"]) for x, it in zip(gk, items)]
    pw = [_bdot(x, x) for x in l_b]
    inv = [eye - x for x in l_b]
    for _ in range(int(math.log2(c)) - 2):
        res = [_bdot(jnp.concatenate([p, i], axis=0), p) for p, i in zip(pw, inv)]
        pw = [x[:c] for x in res]
        inv = [i + x[c:] for i, x in zip(inv, res)]
    inv = [i + _bdot(i, p) for p, i in zip(pw, inv)]
    w = [_bdot(i, it["kkt"]) for i, it in zip(inv, items)]
    y_loc = [_bdot(i, x[:c]) for i, x in zip(inv, lkv_akv)]
    ab_w = [_bdot(a, x) for a, x in zip(a_b, w)]
    ab_y = [_bdot(a, x) for a, x in zip(a_b, y_loc)]
    vk = [_bdot_tn(it["v"], it["kh"]) for it in items]
    yb = [_bdot_tn(x, it["bh"]) for x, it in zip(y_loc, items)]
    wb = [_bdot_tn(x, it["bh"]) for x, it in zip(w, items)]
    for n, it in enumerate(items):
        s = it["s"]
        j = jnp.where(d == 0, it["sub"], RW_SUB - 1 - it["sub"])
        re_ref[0, 0, j, :, s] = it["rt"] - ab_w[n]
        ol_ref[0, 0, j, :, s] = lkv_akv[n][c:] - ab_y[n]
        sl_ref[0, 0, j, :, s] = vk[n] - yb[n]
        p_ref[0, 0, j, :, s] = eye * it["e_tot"] - wb[n]


def _rwkv_carry_kernel(p_ref, sl_ref, s_out_ref, st_ref):
    @pl.when(pl.program_id(0) == 0)
    def _():
        st_ref[...] = jnp.zeros_like(st_ref)

    for d in range(2):
        for b in range(NB):
            s = st_ref[d, b]
            s_out_ref[d, b, 0] = s
            for h in range(RW_H):
                sl = slice(h * RW_D, (h + 1) * RW_D)
                st_ref[d, b, :, sl] = _hdot(s[:, sl], p_ref[d, b, 0, :, sl]) + sl_ref[d, b, 0, :, sl]


TILE_CHUNKS = TILE // CHUNK


def _tile_of_group(b, grp):
    return jnp.where(grp == 0, N_LAT // TILE + b, (SEQ // TILE) * b + grp - 1)


def _bwd_group(grp):
    return jnp.where(grp == 0, 0, SEQ_CHUNKS // TILE_CHUNKS - grp)


def _rwkv_out_kernel(olf_ref, ref_ref, sf_ref, olb_ref, reb_ref, sb_ref, bonus_ref, g_ref, gng_ref, gnb_ref,
                     bd_ref, y_ref, o_scr):
    n = TILE_CHUNKS
    items = [(i, slice(h * RW_D, (h + 1) * RW_D)) for i in range(n) for h in range(RW_H)]
    pf = [_bdot_nt(ref_ref[0, 0, i, :, s], sf_ref[0, 0, i, :, s]) for i, s in items]
    pb = [_bdot_nt(reb_ref[0, 0, n - 1 - i, :, s], sb_ref[0, 0, n - 1 - i, :, s]) for i, s in items]
    for (i, s), a, b in zip(items, pf, pb):
        o_scr[i * CHUNK:(i + 1) * CHUNK, s] = olf_ref[0, 0, i, :, s] + olb_ref[0, 0, n - 1 - i, :, s] + a + b
    o = o_scr[...] + bonus_ref[...]
    bd = bd_ref[...]
    cen = o - _dot_rhs_exact(o, bd) * (1.0 / RW_D)
    var = _dot_rhs_exact(cen * cen, bd) * (1.0 / RW_D)
    y = cen * lax.rsqrt(var + RW_GN_EPS) * gng_ref[...] + gnb_ref[...]
    y_ref[...] = (y * g_ref[...]).astype(y_ref.dtype)


def _rwkv_scan(r, v, kk, lw, bmat, km, bonus, g, gn_g, gn_b):
    sub_rows = RW_SUB * CHUNK
    lat_blocks = N_LAT // sub_rows
    lat_per_b = SEQ // sub_rows
    ctx_per_b = CTX // sub_rows

    def step_block(d, rb):
        is_lat = rb < lat_blocks
        b = jnp.where(is_lat, rb // lat_per_b, (rb - lat_blocks) // ctx_per_b)
        i = jnp.where(is_lat, rb % lat_per_b, (rb - lat_blocks) % ctx_per_b)
        fwd = jnp.where(is_lat, ctx_per_b + i, i)
        bwd = jnp.where(is_lat, ctx_per_b + lat_per_b - 1 - i, ctx_per_b - 1 - i)
        return b, jnp.where(d == 0, fwd, bwd)

    tok = pl.BlockSpec((sub_rows, RW_W), lambda d, rb: (rb, 0))
    drn = pl.BlockSpec((1, sub_rows, RW_W), lambda d, rb: (d, rb, 0))
    step_shape = jax.ShapeDtypeStruct((2, NB, SEQ_CHUNKS, CHUNK, RW_W), F32)
    step_blk = pl.BlockSpec((1, 1, RW_SUB, CHUNK, RW_W), lambda d, rb: (d,) + step_block(d, rb) + (0, 0))
    p, s_loc, r_eff, o_loc = pl.pallas_call(
        _rwkv_chunk_kernel,
        grid=(2, ROWS // sub_rows),
        in_specs=[tok, tok, tok, drn, drn, drn],
        out_specs=[step_blk] * 4,
        out_shape=[step_shape] * 4,
        compiler_params=_cp("parallel", "parallel"),
        name="rwkv_chunk",
    )(r, v, kk, lw, bmat, km)

    all_blk = pl.BlockSpec((2, NB, 1, CHUNK, RW_W), lambda n: (0, 0, n, 0, 0))
    s_in = pl.pallas_call(
        _rwkv_carry_kernel,
        grid=(SEQ_CHUNKS,),
        in_specs=[all_blk, all_blk],
        out_specs=all_blk,
        out_shape=step_shape,
        scratch_shapes=[pltpu.VMEM((2, NB, CHUNK, RW_W), F32)],
        compiler_params=_cp("arbitrary"),
        name="rwkv_carry",
    )(p, s_loc)

    fwd = pl.BlockSpec((1, 1, TILE_CHUNKS, CHUNK, RW_W), lambda b, grp: (0, b, grp, 0, 0))
    bwd = pl.BlockSpec((1, 1, TILE_CHUNKS, CHUNK, RW_W), lambda b, grp: (1, b, _bwd_group(grp), 0, 0))
    rows = pl.BlockSpec((TILE, RW_W), lambda b, grp: (_tile_of_group(b, grp), 0))
    vec = pl.BlockSpec((1, RW_W), lambda b, grp: (0, 0))
    return pl.pallas_call(
        _rwkv_out_kernel,
        grid=(NB, SEQ_CHUNKS // TILE_CHUNKS),
        in_specs=[fwd, fwd, fwd, bwd, bwd, bwd, rows, rows, vec, vec,
                  pl.BlockSpec((RW_W, RW_W), lambda b, grp: (0, 0))],
        out_specs=rows,
        out_shape=jax.ShapeDtypeStruct((ROWS, RW_W), BF16),
        scratch_shapes=[pltpu.VMEM((TILE, RW_W), F32)],
        compiler_params=_cp("parallel", "parallel"),
        name="rwkv_out",
    )(o_loc, r_eff, s_in, o_loc, r_eff, s_in, bonus, g, gn_g.reshape(1, RW_W), gn_b.reshape(1, RW_W),
      _head_block_ones(RW_W, RW_D).astype(BF16))


def _mlstm_prep_kernel(z_ref, zp_ref, zn_ref, w_ref, o_ref):
    z = z_ref[...]
    zp, zn = _neighbours(z, zp_ref[...], zn_ref[...])
    y = _silu(zp * w_ref[0:1, :] + z * w_ref[1:2, :] + zn * w_ref[2:3, :])
    col = lax.broadcasted_iota(jnp.int32, y.shape, 1)
    o_ref[...] = jnp.where(col >= ML_W, y * (ML_D ** -0.5), y)


def _mlstm_prep(z, conv_w):
    return pl.pallas_call(
        _mlstm_prep_kernel,
        grid=(N_TILES,),
        in_specs=_tile_halo_specs(2 * ML_W) + [pl.BlockSpec((3, 2 * ML_W), lambda t: (0, 0))],
        out_specs=pl.BlockSpec((TILE, 2 * ML_W), lambda t: (t, 0)),
        out_shape=jax.ShapeDtypeStruct((ROWS, 2 * ML_W), F32),
        compiler_params=_cp("parallel"),
        name="mlstm_prep",
    )(z, z, z, conv_w)


def _log_sigmoid(x):
    return jnp.minimum(x, 0.0) - jnp.log(1.0 + jnp.exp(-jnp.abs(x)))


N_CHAINS = 2 * NB


def _mlstm_scan_kernel(*refs):
    nc = N_CHAINS
    q_refs, k_refs, v_refs = refs[0:nc], refs[nc:2 * nc], refs[2 * nc:3 * nc]
    gc_refs, gr_refs = refs[3 * nc:4 * nc], refs[4 * nc:5 * nc]
    bc_ref, br_ref, o_ref, c_ref, n_ref, m_ref = refs[5 * nc:]

    @pl.when(pl.program_id(0) == 0)
    def _():
        c_ref[...] = jnp.zeros_like(c_ref)
        n_ref[...] = jnp.zeros_like(n_ref)
        m_ref[...] = jnp.zeros_like(m_ref)

    c = CHUNK
    ti = lax.broadcasted_iota(jnp.int32, (c, c), 0)
    si = lax.broadcasted_iota(jnp.int32, (c, c), 1)
    masks = (ti >= si, ti <= si)
    items = [(ci, h) for ci in range(nc) for h in range(ML_H)]
    sls = [slice(h * ML_D, (h + 1) * ML_D) for h in range(ML_H)]
    gcol = [gc_refs[ci][0, 0] + bc_ref[ci // NB] for ci in range(nc)]
    grow = [gr_refs[ci][0, 0] + br_ref[ci // NB] for ci in range(nc)]

    ones = jnp.ones((c, ML_D), BF16)
    q = [q_refs[ci][:, sls[h]] for ci, h in items]
    k = [k_refs[ci][:, sls[h]] for ci, h in items]
    v = [v_refs[ci][:, sls[h]] for ci, h in items]
    qk = [_bdot_nt(a, b) for a, b in zip(q, k)]
    c_mat = [c_ref[ci, h] for ci, h in items]
    n_vec = [n_ref[ci, h] for ci, h in items]
    m_prev = [m_ref[ci, h][0:1, :] for ci, h in items]
    qcn = [_bdot_nt(a, jnp.concatenate([cm, jnp.broadcast_to(nv, (ML_D, ML_D))], axis=0))
           for a, cm, nv in zip(q, c_mat, n_vec)]

    log_w, cum, i_col, total = [], [], [], []
    for ci, h in items:
        mask = masks[ci // NB]
        mask_t = masks[1 - ci // NB]
        f_col = _log_sigmoid(gcol[ci][:, ML_H + h:ML_H + h + 1])
        f_row = _log_sigmoid(grow[ci][ML_H + h:ML_H + h + 1, :])
        cc = _dot_rhs_exact(jnp.where(mask, f_row, 0.0), ones, pieces=3)
        cr = jnp.sum(jnp.where(mask_t, f_col, 0.0), axis=0, keepdims=True)
        log_w.append(jnp.where(mask, cc[:, :c] - cr + grow[ci][h:h + 1, :], ML_NEG))
        cum.append(cc)
        i_col.append(jnp.broadcast_to(gcol[ci][:, h:h + 1], (c, ML_D)))
        total.append(jnp.sum(f_row, axis=1, keepdims=True))
    m_inter = [a + b for a, b in zip(cum, m_prev)]
    m_t = [jnp.maximum(jnp.broadcast_to(jnp.max(a, axis=1, keepdims=True), (c, ML_D)), b)
           for a, b in zip(log_w, m_inter)]
    s = [a * jnp.exp(b - m[:, :c]) for a, b, m in zip(qk, log_w, m_t)]
    w_inter = [jnp.exp(a - m) for a, m in zip(m_inter, m_t)]
    sv = [_bdot(a, jnp.concatenate([b.astype(BF16), ones], axis=1)) for a, b in zip(s, v)]
    for i, (ci, h) in enumerate(items):
        num = sv[i][:, :ML_D] + w_inter[i] * qcn[i][:, :ML_D]
        den = sv[i][:, ML_D:] + w_inter[i] * qcn[i][:, ML_D:]
        o_ref[ci // NB, ci % NB, 0, :, sls[h]] = num / jnp.maximum(jnp.abs(den), jnp.exp(-m_t[i]))
    log_src = [t - a + b for t, a, b in zip(total, cum, i_col)]
    m_new = [jnp.maximum(t + mp, jnp.max(ls, axis=0, keepdims=True)) for t, mp, ls in zip(total, m_prev, log_src)]
    src = [jnp.exp(ls - mn) for ls, mn in zip(log_src, m_new)]
    decay = [jnp.exp(t + mp - mn) for t, mp, mn in zip(total, m_prev, m_new)]
    vk = [_bdot_tn(a * sr, b) for a, sr, b in zip(v, src, k)]
    for i, (ci, h) in enumerate(items):
        c_ref[ci, h] = decay[i] * c_mat[i] + vk[i]
        n_ref[ci, h] = decay[i] * n_vec[i] + jnp.sum(src[i] * k[i], axis=0, keepdims=True)
        m_ref[ci, h] = jnp.broadcast_to(m_new[i], m_ref.shape[2:])


def _mlstm_scan(qk, z, gcol, grow, bcol, brow):
    chains = [(d, b) for d in range(2) for b in range(NB)]

    def tok(col_blk):
        return [pl.BlockSpec((CHUNK, ML_W), lambda n, d=d, b=b: (_seq_row_block(b, d, n), col_blk))
                for d, b in chains]

    gc_specs = [pl.BlockSpec((1, 1, CHUNK, 2 * ML_H), lambda n, d=d, b=b: (d, _seq_row_block(b, d, n), 0, 0))
                for d, b in chains]
    gr_specs = [pl.BlockSpec((1, 1, 2 * ML_H, CHUNK), lambda n, d=d, b=b: (d, _seq_row_block(b, d, n), 0, 0))
                for d, b in chains]
    nc = N_CHAINS
    return pl.pallas_call(
        _mlstm_scan_kernel,
        grid=(SEQ_CHUNKS,),
        in_specs=tok(0) + tok(1) + tok(2) + gc_specs + gr_specs + [
            pl.BlockSpec((2, 1, 2 * ML_H), lambda n: (0, 0, 0)),
            pl.BlockSpec((2, 2 * ML_H, 1), lambda n: (0, 0, 0))],
        out_specs=pl.BlockSpec((2, NB, 1, CHUNK, ML_W), lambda n: (0, 0, n, 0, 0)),
        out_shape=jax.ShapeDtypeStruct((2, NB, SEQ_CHUNKS, CHUNK, ML_W), F32),
        scratch_shapes=[pltpu.VMEM((nc, ML_H, ML_D, ML_D), F32), pltpu.VMEM((nc, ML_H, 1, ML_D), F32),
                        pltpu.VMEM((nc, ML_H, 8, 128), F32)],
        compiler_params=_cp("arbitrary"),
        name="mlstm_scan",
    )(*([qk] * (2 * nc) + [z] * nc + [gcol] * nc + [grow] * nc + [bcol, brow]))


def _mlstm_out_kernel(hf_ref, hb_ref, og_ref, gng_ref, y_ref):
    n = TILE_CHUNKS
    for i in range(n):
        rows = slice(i * CHUNK, (i + 1) * CHUNK)
        hsum = _sigmoid(og_ref[rows, :]) * (hf_ref[0, 0, i] + hb_ref[0, 0, n - 1 - i])
        for h in range(ML_H):
            sl = slice(h * ML_D, (h + 1) * ML_D)
            x = hsum[:, sl]
            cen = x - jnp.mean(x, axis=-1, keepdims=True)
            var = jnp.mean(cen * cen, axis=-1, keepdims=True)
            y_ref[rows, sl] = (cen * lax.rsqrt(var + EPS) * gng_ref[:, sl]).astype(y_ref.dtype)


def _mlstm_out(hs, z, gn_g):
    blk = (1, 1, TILE_CHUNKS, CHUNK, ML_W)
    return pl.pallas_call(
        _mlstm_out_kernel,
        grid=(NB, SEQ_CHUNKS // TILE_CHUNKS),
        in_specs=[pl.BlockSpec(blk, lambda b, grp: (0, b, grp, 0, 0)),
                  pl.BlockSpec(blk, lambda b, grp: (1, b, _bwd_group(grp), 0, 0)),
                  pl.BlockSpec((TILE, ML_W), lambda b, grp: (_tile_of_group(b, grp), 3)),
                  pl.BlockSpec((1, ML_W), lambda b, grp: (0, 0))],
        out_specs=pl.BlockSpec((TILE, ML_W), lambda b, grp: (_tile_of_group(b, grp), 0)),
        out_shape=jax.ShapeDtypeStruct((ROWS, ML_W), BF16),
        compiler_params=_cp("parallel", "parallel"),
        name="mlstm_out",
    )(hs, hs, z, gn_g.reshape(1, ML_W))


def _mlstm(z_main, z_gates, conv_w, i_b, f_b, gn_g):
    qk = _mlstm_prep(z_main, conv_w)
    n_chunks = ROWS // CHUNK
    gates = z_gates[:, :4 * ML_H].reshape(n_chunks, CHUNK, 2, 2, ML_H)
    gcol = jnp.transpose(gates, (3, 0, 1, 2, 4)).reshape(2, n_chunks, CHUNK, 2 * ML_H)
    grow = jnp.swapaxes(gcol, 2, 3)
    bias = jnp.concatenate([i_b, f_b], axis=-1)
    hs = _mlstm_scan(qk, z_main, gcol, grow, bias.reshape(2, 1, 2 * ML_H), bias.reshape(2, 2 * ML_H, 1))
    return _mlstm_out(hs, z_main, gn_g)


def _s5_toeplitz_kernel(kf_ref, kb_ref, o_ref):
    kf = kf_ref[0]
    kb = kb_ref[0]
    lane = lax.broadcasted_iota(jnp.int32, kf.shape, 1)
    width = S5_L * S5_C
    for j in range(S5_L):
        f = kf if j == 0 else jnp.where(lane >= S5_C * j, pltpu.roll(kf, S5_C * j, 1), 0.0)
        back = S5_L - 1 - j
        b = kb if back == 0 else jnp.where(lane < S5_C * (j + 1), pltpu.roll(kb, width - S5_C * back, 1), 0.0)
        o_ref[0, 0, :, j] = f.reshape(S5_G, S5_C, width).astype(o_ref.dtype)
        o_ref[0, 1, :, j] = b.reshape(S5_G, S5_C, width).astype(o_ref.dtype)


def _s5_operators(lam_re, lam_im, log_step, b_re, b_im, c_re, c_im):
    nl = lam_re.shape[0]
    dt = jnp.exp(log_step)[..., None]
    mag = jnp.exp(lam_re * dt)
    a_re = mag * jnp.cos(lam_im * dt)
    a_im = mag * jnp.sin(lam_im * dt)
    den = lam_re * lam_re + lam_im * lam_im
    f_re = ((a_re - 1) * lam_re + a_im * lam_im) / den
    f_im = (a_im * lam_re - (a_re - 1) * lam_im) / den
    bb_re = f_re[..., None] * b_re - f_im[..., None] * b_im
    bb_im = f_re[..., None] * b_im + f_im[..., None] * b_re
    bt_re = jnp.swapaxes(bb_re, -1, -2)
    bt_im = jnp.swapaxes(bb_im, -1, -2)
    pr = [jnp.ones_like(a_re)]
    pi = [jnp.zeros_like(a_im)]
    for _ in range(S5_L):
        pr.append(pr[-1] * a_re - pi[-1] * a_im)
        pi.append(pr[-2] * a_im + pi[-1] * a_re)
    pr = jnp.stack(pr, axis=3)
    pi = jnp.stack(pi, axis=3)

    def times_b(qr, qi):
        qr, qi = qr[..., None, :], qi[..., None, :]
        br, bi = bt_re[:, :, :, None], bt_im[:, :, :, None]
        return qr * br - qi * bi, qr * bi + qi * br

    wr, wi = times_b(pr[:, :, :, :S5_L], pi[:, :, :, :S5_L])
    wr = jnp.swapaxes(wr, 3, 4)
    wi = jnp.swapaxes(wi, 3, 4)
    kern = (jnp.einsum("ldgktp,ldgcp->ldgktc", wr, c_re, precision=HP)
            - jnp.einsum("ldgktp,ldgcp->ldgktc", wi, c_im, precision=HP))
    kf = kern[:, 0].reshape(nl, S5_G * S5_C, S5_L * S5_C)
    kb = jnp.flip(kern[:, 1], axis=3).reshape(nl, S5_G * S5_C, S5_L * S5_C)
    rows_blk = pl.BlockSpec((1, S5_G * S5_C, S5_L * S5_C), lambda l: (l, 0, 0))
    m = pl.pallas_call(
        _s5_toeplitz_kernel,
        grid=(nl,),
        in_specs=[rows_blk, rows_blk],
        out_specs=pl.BlockSpec((1, 2, S5_G, S5_L, S5_C, S5_L * S5_C), lambda l: (l, 0, 0, 0, 0, 0)),
        out_shape=jax.ShapeDtypeStruct((nl, 2, S5_G, S5_L, S5_C, S5_L * S5_C), BF16),
        compiler_params=_cp("parallel"),
        name="s5_toeplitz",
    )(kf, kb).reshape(nl, 2, S5_G, S5_L * S5_C, S5_L * S5_C)

    def stack_dirs(fwd, bwd):
        return jnp.stack([fwd[:, 0], bwd[:, 1]], axis=1)

    er, ei = times_b(stack_dirs(jnp.flip(pr[:, :, :, :S5_L], axis=3), pr[:, :, :, :S5_L]),
                     stack_dirs(jnp.flip(pi[:, :, :, :S5_L], axis=3), pi[:, :, :, :S5_L]))
    e = jnp.concatenate([er, ei], axis=-1).reshape(nl, 2, S5_G, S5_L * S5_C, 2 * S5_P).astype(BF16)
    qr = stack_dirs(pr[:, :, :, 1:], jnp.flip(pr[:, :, :, 1:], axis=3))[..., None, :]
    qi = stack_dirs(pi[:, :, :, 1:], jnp.flip(pi[:, :, :, 1:], axis=3))[..., None, :]
    cr, ci = c_re[:, :, :, None], c_im[:, :, :, None]
    ft = jnp.concatenate([cr * qr - ci * qi, -(cr * qi + ci * qr)], axis=-1)
    ft = ft.reshape(nl, 2, S5_G, S5_L * S5_C, 2 * S5_P).astype(BF16)
    return m, e, ft, pr[:, :, :, S5_L], pi[:, :, :, S5_L]


def _s5_local_kernel(u_ref, m_ref, e_ref, y_ref, x_ref):
    u = u_ref[0]
    for d in range(2):
        y_ref[d, 0] = jnp.dot(u, m_ref[d, 0], preferred_element_type=F32)
        x_ref[d, 0] = jnp.dot(u, e_ref[d, 0], preferred_element_type=F32)


def _s5_carry_kernel(xc_ref, ar_ref, ai_ref, x0_ref, st_ref):
    @pl.when(pl.program_id(1) == 0)
    def _():
        st_ref[...] = jnp.zeros_like(st_ref)

    ar = ar_ref[0]
    ai = ai_ref[0]

    def run(order):
        x = st_ref[...]
        for i in order:
            x0_ref[0, :, :, i, :] = x
            x = x * ar + pltpu.roll(x, S5_P, 2) * ai + xc_ref[0, :, :, i, :]
        st_ref[...] = x

    n = xc_ref.shape[3]

    @pl.when(pl.program_id(0) == 0)
    def _():
        run(range(n))

    @pl.when(pl.program_id(0) == 1)
    def _():
        run(reversed(range(n)))


def _s5_state_kernel(y_ref, x0_ref, ft_ref, o_ref):
    acc = y_ref[0, 0] + y_ref[1, 0]
    for d in range(2):
        acc = acc + _bdot_nt(x0_ref[d, 0], ft_ref[d, 0])
    o_ref[0] = acc.astype(o_ref.dtype)


def _s5_glu_kernel(y_ref, u_ref, d_ref, w_ref, o_ref):
    x = y_ref[...] + u_ref[...] * d_ref[...]
    ge = 0.5 * x * (1.0 + jnp.tanh(math.sqrt(2.0 / math.pi) * (x + 0.044715 * (x * x * x))))
    p = _bdot(ge, w_ref[...])
    o_ref[...] = (p[:, :S5_W] * _sigmoid(p[:, S5_W:])).astype(o_ref.dtype)


def _s5(z, m, e, ft, a_re, a_im, d_skip, w_glu):
    n_ch = (SEQ + CTX) // S5_L
    n_ctx_ch = CTX // S5_L
    rows = NB * n_ch
    width = S5_L * S5_C

    def to_groups(x):
        x = x.reshape(NB, -1, S5_L, S5_G, S5_C)
        return jnp.transpose(x, (3, 0, 1, 2, 4)).reshape(S5_G, NB, -1, width)

    zb = z.astype(BF16)
    u = jnp.concatenate([to_groups(zb[N_LAT:]), to_groups(zb[:N_LAT])], axis=2).reshape(S5_G, rows, width)
    y_loc, x_in = pl.pallas_call(
        _s5_local_kernel,
        grid=(S5_G,),
        in_specs=[pl.BlockSpec((1, rows, width), lambda g: (g, 0, 0)),
                  pl.BlockSpec((2, 1, width, width), lambda g: (0, g, 0, 0)),
                  pl.BlockSpec((2, 1, width, 2 * S5_P), lambda g: (0, g, 0, 0))],
        out_specs=[pl.BlockSpec((2, 1, rows, width), lambda g: (0, g, 0, 0)),
                   pl.BlockSpec((2, 1, rows, 2 * S5_P), lambda g: (0, g, 0, 0))],
        out_shape=[jax.ShapeDtypeStruct((2, S5_G, rows, width), F32),
                   jax.ShapeDtypeStruct((2, S5_G, rows, 2 * S5_P), F32)],
        compiler_params=_cp("parallel"),
        name="s5_local",
    )(u, m, e)

    step = n_ctx_ch
    n_blk = n_ch // step
    coef_r = jnp.concatenate([a_re, a_re], axis=-1)[:, :, None, :]
    coef_i = jnp.concatenate([-a_im, a_im], axis=-1)[:, :, None, :]

    def chunk_blk(d, i):
        return jnp.where(d == 0, i, jnp.where(i == 0, 0, n_blk - i))

    st_blk = pl.BlockSpec((1, S5_G, NB, step, 2 * S5_P), lambda d, i: (d, 0, 0, chunk_blk(d, i), 0))
    coef_blk = pl.BlockSpec((1, S5_G, 1, 2 * S5_P), lambda d, i: (d, 0, 0, 0))
    x0 = pl.pallas_call(
        _s5_carry_kernel,
        grid=(2, n_blk),
        in_specs=[st_blk, coef_blk, coef_blk],
        out_specs=st_blk,
        out_shape=jax.ShapeDtypeStruct((2, S5_G, NB, n_ch, 2 * S5_P), F32),
        scratch_shapes=[pltpu.VMEM((S5_G, NB, 2 * S5_P), F32)],
        compiler_params=_cp("arbitrary", "arbitrary"),
        name="s5_carry",
    )(x_in.reshape(2, S5_G, NB, n_ch, 2 * S5_P), coef_r, coef_i).reshape(2, S5_G, rows, 2 * S5_P)

    y = pl.pallas_call(
        _s5_state_kernel,
        grid=(S5_G,),
        in_specs=[pl.BlockSpec((2, 1, rows, width), lambda g: (0, g, 0, 0)),
                  pl.BlockSpec((2, 1, rows, 2 * S5_P), lambda g: (0, g, 0, 0)),
                  pl.BlockSpec((2, 1, width, 2 * S5_P), lambda g: (0, g, 0, 0))],
        out_specs=pl.BlockSpec((1, rows, width), lambda g: (g, 0, 0)),
        out_shape=jax.ShapeDtypeStruct((S5_G, rows, width), BF16),
        compiler_params=_cp("parallel"),
        name="s5_state",
    )(y_loc, x0, ft)

    y = jnp.transpose(y.reshape(S5_G, NB, n_ch, S5_L, S5_C), (1, 2, 3, 0, 4)).reshape(NB, n_ch * S5_L, S5_W)
    y = jnp.concatenate([y[:, CTX:].reshape(N_LAT, S5_W), y[:, :CTX].reshape(N_CTX, S5_W)], axis=0)
    tok = pl.BlockSpec((TILE, S5_W), lambda t: (t, 0))
    return pl.pallas_call(
        _s5_glu_kernel,
        grid=(N_TILES,),
        in_specs=[tok, tok, pl.BlockSpec((1, S5_W), lambda t: (0, 0)),
                  pl.BlockSpec((S5_W, 2 * S5_W), lambda t: (0, 0))],
        out_specs=tok,
        out_shape=jax.ShapeDtypeStruct((ROWS, S5_W), BF16),
        compiler_params=_cp("parallel"),
        name="s5_glu",
    )(y, z, d_skip.reshape(1, S5_W), w_glu)


def kernel(x, c, ctx, c_ctx, w_mod, b_mod, norm1_g, norm2_g, w_in, b_gate, q_norm_g, k_norm_g, rwkv_mu, rwkv_w0, rwkv_w_up, rwkv_a0, rwkv_a_up, rwkv_g_up, rwkv_k_k, rwkv_k_a, rwkv_r_k, rwkv_gn_g, rwkv_gn_b, mlstm_conv_w, mlstm_i_b, mlstm_f_b, mlstm_gn_g, s5_lam_re, s5_lam_im, s5_log_step, s5_b_re, s5_b_im, s5_c_re, s5_c_im, s5_d, s5_w_glu, w_br_attn, w_br_rwkv, w_br_mlstm, w_br_s5, w_out, w_ffn_in, w_ffn_out, final_norm_g):
    cos, sin = _rope_tables()
    xs = jnp.concatenate([x.reshape(N_LAT, D), ctx.reshape(N_CTX, D)], axis=0)
    c_all = jnp.concatenate([c, c_ctx[None], jnp.zeros((3, D), F32)], axis=0)
    s5_m, s5_e, s5_ft, s5_ar, s5_ai = _s5_operators(s5_lam_re, s5_lam_im, s5_log_step, s5_b_re, s5_b_im,
                                                    s5_c_re, s5_c_im)
    tm = 1024
    for l in range(DEPTH):
        last = l == DEPTH - 1
        n_rows = N_LAT if last else ROWS
        mod = _modulation(c_all, w_mod, b_mod, l).reshape(8, 1, 6 * D)
        h = _norm_mod(xs, norm1_g[l], mod, 0, 1, ROWS)
        w_attn, w_rwkv, w_ml, w_mlg, w_s5, w_gate = _w_in_split(w_in, l)
        z_attn = _mm(h, w_attn, tm, 512)
        z_rwkv = _mm(h, w_rwkv, tm, 896)
        z_ml = _mm(h, w_ml, tm, 1024)
        z_mlg = _mm(h, w_mlg, tm, 128)
        z_s5 = _mm(h, w_s5, tm, 512)
        gates = _mm_gate(h, w_gate, b_gate, l, tm, 1024, n_rows)

        ya = _attention(z_attn, q_norm_g[l], k_norm_g[l], cos, sin)
        rp = dict(mu=rwkv_mu[l], w0=rwkv_w0[l], w_up=rwkv_w_up[l], a0=rwkv_a0[l], a_up=rwkv_a_up[l],
                  g_up=rwkv_g_up[l], k_k=rwkv_k_k[l], k_a=rwkv_k_a[l], r_k=rwkv_r_k[l].reshape(RW_W))
        r, v, kk, g, bonus, lw, bmat, km = _rwkv_prep(z_rwkv, rp)
        yr = _rwkv_scan(r, v, kk, lw, bmat, km, bonus, g, rwkv_gn_g[l], rwkv_gn_b[l])
        ym = _mlstm(z_ml, z_mlg, mlstm_conv_w[l], mlstm_i_b[l], mlstm_f_b[l], mlstm_gn_g[l])
        ys = _s5(z_s5, s5_m[l], s5_e[l], s5_ft[l], s5_ar[l], s5_ai[l], s5_d[l], s5_w_glu[l])

        y = _merge(ya, yr, ym, ys, gates, w_br_attn, w_br_rwkv, w_br_mlstm, w_br_s5, l, tm, 512, n_rows)
        xs = _mm_res(y, w_out, l, xs, mod, 2, tm, 1024, n_rows)
        h2 = _norm_mod(xs, norm2_g[l], mod, 3, 4, n_rows)
        u = _ffn_in(h2, w_ffn_in, l, tm, 512, n_rows)
        xs = _mm_res(u, w_ffn_out, l, xs, mod, 5, 512, 512, n_rows)
    return _final_norm(xs, final_norm_g).reshape(NB, SEQ, D)
```

```python
import functools
import math

import numpy as np
import jax
import jax.numpy as jnp
from jax import lax
from jax.experimental import pallas as pl
from jax.experimental.pallas import tpu as pltpu

F32 = jnp.float32
BF16 = jnp.bfloat16
HP = lax.Precision.HIGHEST

D = 2048
NB = 4
SEQ = 2048
CTX = 256
DEPTH = 2
N_LAT = NB * SEQ
N_CTX = NB * CTX
ROWS = N_LAT + N_CTX
EPS = 1e-6
GRID_W = 64

HEAD_DIM = 128
ATTN_HEADS = 8
ATTN_KV = 2
ROPE_THETA = 10000.0
ATTN_Q = ATTN_HEADS * HEAD_DIM
ATTN_IN = (ATTN_HEADS + 2 * ATTN_KV) * HEAD_DIM

RW_H = 8
RW_D = 64
RW_W = 512
RW_IN = 3 * RW_W + 64 + 64 + 128
RW_DECAY = math.exp(-0.5)
RW_GN_EPS = 64e-5

ML_H = 4
ML_D = 128
ML_W = 512
ML_NEG = -1e30
ML_IN = 4 * ML_W + 4 * ML_H

S5_W = 512
S5_C = 16
S5_G = 32
S5_P = 64
S5_L = 16

FFN_H = 5632
GATE_IN = 4 * D

CHUNK = 64
TILE = 256
N_TILES = ROWS // TILE
SEQ_CHUNKS = (SEQ + CTX) // CHUNK
CTX_CHUNKS = CTX // CHUNK
LAT_CHUNKS = SEQ // CHUNK

VMEM_LIMIT_BYTES = 56 * 1024 * 1024


def _cp(*sem):
    return pltpu.CompilerParams(dimension_semantics=sem, vmem_limit_bytes=VMEM_LIMIT_BYTES)


def _bdot(a, b):
    return jnp.dot(a.astype(BF16), b.astype(BF16), preferred_element_type=F32)


def _bdot_nt(a, b):
    return lax.dot_general(a.astype(BF16), b.astype(BF16), (((1,), (1,)), ((), ())),
                           preferred_element_type=F32)


def _bdot_tn(a, b):
    return lax.dot_general(a.astype(BF16), b.astype(BF16), (((0,), (0,)), ((), ())),
                           preferred_element_type=F32)


def _hdot(a, b):
    return jnp.dot(a, b, precision=HP, preferred_element_type=F32)


def _sigmoid(x):
    return 1.0 / (1.0 + jnp.exp(-x))


def _silu(x):
    return x * _sigmoid(x)


def _mod_kernel(c_ref, w_ref, b_ref, o_ref):
    c_hi, c_lo = _split_bf16(_silu(c_ref[...]), 2)
    w_hi, w_lo = _split_bf16(w_ref[...], 2)
    rows = c_hi.shape[0]
    both = jnp.dot(jnp.concatenate([c_hi, c_lo], axis=0), w_hi, preferred_element_type=F32)
    o_ref[...] = both[:rows] + both[rows:] + jnp.dot(c_hi, w_lo, preferred_element_type=F32) + b_ref[...]


def _modulation(c_all, w, b, l):
    tn = 1024
    return pl.pallas_call(
        _mod_kernel,
        grid=(6 * D // tn,),
        in_specs=[pl.BlockSpec((8, D), lambda j: (0, 0)),
                  pl.BlockSpec((None, D, tn), lambda j: (l, 0, j)),
                  pl.BlockSpec((None, 1, tn), lambda j: (l, 0, j))],
        out_specs=pl.BlockSpec((8, tn), lambda j: (0, j)),
        out_shape=jax.ShapeDtypeStruct((8, 6 * D), F32),
        compiler_params=_cp("arbitrary"),
        name="modulation",
    )(c_all, w, b.reshape(DEPTH, 1, 6 * D))


def _mod_row(i, tm):
    return jnp.where(i * tm < N_LAT, (i * tm) // SEQ, NB)


def _stream_specs(x, tr, width, row_axis, col_of):
    def spec(row_of):
        return pl.BlockSpec((tr, width), lambda *ids: (row_of(ids[row_axis]), col_of(ids)))

    if not isinstance(x, tuple):
        return [spec(lambda i: i)], (x,)
    n_lat = N_LAT // tr
    return [spec(lambda i: jnp.minimum(i, n_lat - 1)), spec(lambda i: jnp.maximum(i - n_lat, 0))], x


def _stream_rows(x_refs, tile_id, tr, rows=slice(None)):
    x = x_refs[0][rows, :]
    if len(x_refs) == 2:
        x = jnp.where(tile_id < N_LAT // tr, x, x_refs[1][rows, :])
    return x


def _norm_mod_kernel(*refs):
    x_refs, (g_ref, sh_ref, sc_ref, o_ref) = refs[:-4], refs[-4:]
    x = _stream_rows(x_refs, pl.program_id(0), o_ref.shape[0])
    y = x * lax.rsqrt(jnp.mean(x * x, axis=-1, keepdims=True) + EPS) * g_ref[...]
    o_ref[...] = (y * (1.0 + sc_ref[0]) + sh_ref[0]).astype(o_ref.dtype)


NORM_TILE = 512


def _norm_mod(x, g, modr, shift_blk, scale_blk, n_rows):
    tr = NORM_TILE
    x_specs, x_args = _stream_specs(x, tr, D, 0, lambda ids: 0)
    return pl.pallas_call(
        _norm_mod_kernel,
        grid=(n_rows // tr,),
        in_specs=x_specs + [
            pl.BlockSpec((1, D), lambda i: (0, 0)),
            pl.BlockSpec((1, 1, D), lambda i: (_mod_row(i, tr), 0, shift_blk)),
            pl.BlockSpec((1, 1, D), lambda i: (_mod_row(i, tr), 0, scale_blk))],
        out_specs=pl.BlockSpec((tr, D), lambda i: (i, 0)),
        out_shape=jax.ShapeDtypeStruct((n_rows, D), BF16),
        compiler_params=_cp("parallel"),
        name="norm_mod",
    )(*x_args, g.reshape(1, D), modr, modr)


W_IN_OFFSETS = tuple(int(v) for v in np.cumsum([0, ATTN_IN, RW_IN, 4 * ML_W, 4 * ML_H, S5_W, GATE_IN]))
ML_GATE_PAD = 128


W_TILE = 256
W_ALIGN = 16


def _w_in_group_kernel(*refs, shift, valid):
    a_ref, o_ref = refs[0], refs[-1]
    x = a_ref[...]
    if shift:
        x = jnp.concatenate([x[shift:], refs[1][...]], axis=0)
    if valid < x.shape[0]:
        row = lax.broadcasted_iota(jnp.int32, x.shape, 0)
        x = jnp.where(row < valid, x, 0.0)
    o_ref[...] = x.T.astype(BF16)


def _w_in_group(wt, l, start, n_cols):
    tile = W_TILE if n_cols >= W_TILE else ML_GATE_PAD
    a0, shift = divmod(start, tile)
    assert shift in (0, W_ALIGN)
    n_tiles = -(-n_cols // tile)
    in_specs = [pl.BlockSpec((None, tile, D), lambda t: (l, a0 + t, 0))]
    if shift:
        per = tile // W_ALIGN
        in_specs.append(pl.BlockSpec((None, W_ALIGN, D), lambda t: (l, (a0 + t + 1) * per, 0)))
    return pl.pallas_call(
        functools.partial(_w_in_group_kernel, shift=shift, valid=min(n_cols, tile)),
        grid=(n_tiles,),
        in_specs=in_specs,
        out_specs=pl.BlockSpec((D, tile), lambda t: (0, t)),
        out_shape=jax.ShapeDtypeStruct((D, n_tiles * tile), BF16),
        compiler_params=_cp("parallel"),
        name="w_in_group",
    )(*([wt] * len(in_specs)))


def _w_in_split(w, l):
    wt = jnp.swapaxes(w, 1, 2)
    o = W_IN_OFFSETS
    return tuple(_w_in_group(wt, l, o[i], o[i + 1] - o[i]) for i in range(6))


def _mm_kernel(a_ref, w_ref, o_ref):
    o_ref[...] = jnp.dot(a_ref[...], w_ref[...], preferred_element_type=F32).astype(o_ref.dtype)


def _mm(a, w, tm, tn, out_dtype=F32):
    m, k = a.shape
    n = w.shape[1]
    return pl.pallas_call(
        _mm_kernel,
        grid=(n // tn, m // tm),
        in_specs=[pl.BlockSpec((tm, k), lambda j, i: (i, 0)),
                  pl.BlockSpec((k, tn), lambda j, i: (0, j))],
        out_specs=pl.BlockSpec((tm, tn), lambda j, i: (i, j)),
        out_shape=jax.ShapeDtypeStruct((m, n), out_dtype),
        compiler_params=_cp("parallel", "parallel"),
        name="matmul",
    )(a, w)


ROW_SUB = 256


def _row_blocks(ref):
    return [slice(r, r + ROW_SUB) for r in range(0, ref.shape[0], ROW_SUB)]


def _mm_gate_kernel(a_ref, w_ref, b_ref, o_ref):
    for rows in _row_blocks(o_ref):
        z = jnp.dot(a_ref[rows, :], w_ref[...], preferred_element_type=F32)
        o_ref[rows, :] = _sigmoid(z + b_ref[...]).astype(o_ref.dtype)


def _mm_gate(a, w, b_gate, l, tm, tn, m):
    k = a.shape[1]
    n = w.shape[1]
    return pl.pallas_call(
        _mm_gate_kernel,
        grid=(n // tn, m // tm),
        in_specs=[pl.BlockSpec((tm, k), lambda j, i: (i, 0)),
                  pl.BlockSpec((k, tn), lambda j, i: (0, j)),
                  pl.BlockSpec((None, 1, tn), lambda j, i: (l, 0, j))],
        out_specs=pl.BlockSpec((tm, tn), lambda j, i: (i, j)),
        out_shape=jax.ShapeDtypeStruct((m, n), BF16),
        compiler_params=_cp("parallel", "parallel"),
        name="matmul_gate",
    )(a, w, b_gate.reshape(DEPTH, 1, GATE_IN))


def _mm_res_kernel(a_ref, w_ref, *rest):
    x_refs, (g_ref, o_ref, wb_ref) = rest[:-3], rest[-3:]

    @pl.when(pl.program_id(1) == 0)
    def _():
        wb_ref[...] = w_ref[...].astype(BF16)

    for rows in _row_blocks(o_ref):
        y = jnp.dot(a_ref[rows, :], wb_ref[...], preferred_element_type=F32)
        o_ref[rows, :] = _stream_rows(x_refs, pl.program_id(1), o_ref.shape[0], rows) + g_ref[0] * y


def _mm_res(a, w, l, x, mod, gate_blk, tm, tn, n_rows):
    k = a.shape[1]
    n = w.shape[2]
    gpb = D // tn
    x_specs, x_args = _stream_specs(x, tm, tn, 1, lambda ids: ids[0])
    return pl.pallas_call(
        _mm_res_kernel,
        grid=(n // tn, n_rows // tm),
        in_specs=[pl.BlockSpec((tm, k), lambda j, i: (i, 0)),
                  pl.BlockSpec((None, k, tn), lambda j, i: (l, 0, j))] + x_specs + [
                  pl.BlockSpec((1, 1, tn), lambda j, i: (_mod_row(i, tm), 0, gate_blk * gpb + j))],
        out_specs=pl.BlockSpec((tm, tn), lambda j, i: (i, j)),
        out_shape=jax.ShapeDtypeStruct((n_rows, n), F32),
        scratch_shapes=[pltpu.VMEM((k, tn), BF16)],
        compiler_params=_cp("arbitrary", "arbitrary"),
        name="matmul_residual",
    )(a, w, *x_args, mod)


def _ffn_in_kernel(a_ref, wa_ref, wb_ref, o_ref, wab_ref, wbb_ref):
    @pl.when(pl.program_id(1) == 0)
    def _():
        wab_ref[...] = wa_ref[...].astype(BF16)
        wbb_ref[...] = wb_ref[...].astype(BF16)

    for rows in _row_blocks(o_ref):
        a = a_ref[rows, :]
        u = jnp.dot(a, wab_ref[...], preferred_element_type=F32)
        v = jnp.dot(a, wbb_ref[...], preferred_element_type=F32)
        o_ref[rows, :] = (_silu(u) * v).astype(o_ref.dtype)


def _ffn_in(h, w, l, tm, tn, n_rows):
    nb = FFN_H // tn
    return pl.pallas_call(
        _ffn_in_kernel,
        grid=(nb, n_rows // tm),
        in_specs=[pl.BlockSpec((tm, D), lambda j, i: (i, 0)),
                  pl.BlockSpec((None, D, tn), lambda j, i: (l, 0, j)),
                  pl.BlockSpec((None, D, tn), lambda j, i: (l, 0, nb + j))],
        out_specs=pl.BlockSpec((tm, tn), lambda j, i: (i, j)),
        out_shape=jax.ShapeDtypeStruct((n_rows, FFN_H), BF16),
        scratch_shapes=[pltpu.VMEM((D, tn), BF16), pltpu.VMEM((D, tn), BF16)],
        compiler_params=_cp("arbitrary", "arbitrary"),
        name="ffn_in",
    )(h, w, w)


def _merge_kernel(ya_ref, yr_ref, ym_ref, ys_ref, ga_ref, gr_ref, gm_ref, gs_ref,
                  wa_ref, wr_ref, wm_ref, ws_ref, o_ref, wab_ref, wrb_ref, wmb_ref, wsb_ref):
    @pl.when(pl.program_id(1) == 0)
    def _():
        wab_ref[...] = wa_ref[...].astype(BF16)
        wrb_ref[...] = wr_ref[...].astype(BF16)
        wmb_ref[...] = wm_ref[...].astype(BF16)
        wsb_ref[...] = ws_ref[...].astype(BF16)

    for rows in _row_blocks(o_ref):
        acc = None
        for y_ref, g_ref, w_ref in ((ya_ref, ga_ref, wab_ref), (yr_ref, gr_ref, wrb_ref),
                                    (ym_ref, gm_ref, wmb_ref), (ys_ref, gs_ref, wsb_ref)):
            term = g_ref[rows, :].astype(F32) * jnp.dot(y_ref[rows, :], w_ref[...], preferred_element_type=F32)
            acc = term if acc is None else acc + term
        o_ref[rows, :] = acc.astype(o_ref.dtype)


def _merge(ya, yr, ym, ys, gates, wa, wr, wm, ws, l, tm, tn, n_rows):
    nb = D // tn

    def act(width):
        return pl.BlockSpec((tm, width), lambda j, i: (i, 0))

    def gate(br):
        return pl.BlockSpec((tm, tn), lambda j, i: (i, br * nb + j))

    def wgt(width):
        return pl.BlockSpec((None, width, tn), lambda j, i: (l, 0, j))

    return pl.pallas_call(
        _merge_kernel,
        grid=(nb, n_rows // tm),
        in_specs=[act(ATTN_Q), act(RW_W), act(ML_W), act(S5_W),
                  gate(0), gate(1), gate(2), gate(3),
                  wgt(ATTN_Q), wgt(RW_W), wgt(ML_W), wgt(S5_W)],
        out_specs=pl.BlockSpec((tm, tn), lambda j, i: (i, j)),
        out_shape=jax.ShapeDtypeStruct((n_rows, D), BF16),
        scratch_shapes=[pltpu.VMEM((ATTN_Q, tn), BF16), pltpu.VMEM((RW_W, tn), BF16),
                        pltpu.VMEM((ML_W, tn), BF16), pltpu.VMEM((S5_W, tn), BF16)],
        compiler_params=_cp("arbitrary", "arbitrary"),
        name="gated_merge",
    )(ya, yr, ym, ys, gates, gates, gates, gates, wa, wr, wm, ws)


def _final_norm_kernel(x_ref, g_ref, o_ref):
    x = x_ref[...]
    o_ref[...] = x * lax.rsqrt(jnp.mean(x * x, axis=-1, keepdims=True) + EPS) * g_ref[...]


def _final_norm(x, g):
    tr = NORM_TILE
    return pl.pallas_call(
        _final_norm_kernel,
        grid=(N_LAT // tr,),
        in_specs=[pl.BlockSpec((tr, D), lambda i: (i, 0)),
                  pl.BlockSpec((1, D), lambda i: (0, 0))],
        out_specs=pl.BlockSpec((tr, D), lambda i: (i, 0)),
        out_shape=jax.ShapeDtypeStruct((N_LAT, D), F32),
        compiler_params=_cp("parallel"),
        name="final_norm",
    )(x, g.reshape(1, D))


def _tile_halo_specs(width, col_blk=0):
    last = ROWS // 8 - 1
    per = TILE // 8
    return [pl.BlockSpec((TILE, width), lambda t: (t, col_blk)),
            pl.BlockSpec((8, width), lambda t: (jnp.maximum(t * per - 1, 0), col_blk)),
            pl.BlockSpec((8, width), lambda t: (jnp.minimum((t + 1) * per, last), col_blk))]


def _neighbours(z, prev_blk, next_blk):
    t = pl.program_id(0)
    pos = t % (SEQ // TILE)
    is_lat = t < N_LAT // TILE
    has_prev = jnp.logical_and(is_lat, pos > 0).astype(F32)
    has_next = jnp.logical_and(is_lat, pos < SEQ // TILE - 1).astype(F32)
    row = lax.broadcasted_iota(jnp.int32, z.shape, 0)
    zp = jnp.where(row == 0, prev_blk[7:8, :] * has_prev, pltpu.roll(z, 1, 0))
    zn = jnp.where(row == TILE - 1, next_blk[0:1, :] * has_next, pltpu.roll(z, TILE - 1, 0))
    return zp, zn


def _seq_row_block(b, d, n):
    ctx_c = jnp.where(d == 0, n, CTX_CHUNKS - 1 - n)
    lat_c = jnp.where(d == 0, n - CTX_CHUNKS, SEQ_CHUNKS - 1 - n)
    return jnp.where(n < CTX_CHUNKS, N_LAT // CHUNK + CTX_CHUNKS * b + ctx_c, LAT_CHUNKS * b + lat_c)


def _rope(x, cos, sin):
    lane = lax.broadcasted_iota(jnp.int32, x.shape, 1)
    first = (lane % 64) < 32
    partner = jnp.where(first, pltpu.roll(x, 96, 1), pltpu.roll(x, 32, 1))
    return x * cos + partner * sin


def _rms(x, g):
    return x * lax.rsqrt(jnp.mean(x * x, axis=-1, keepdims=True) + EPS) * g


def _attn_kernel(q_ref, kl_ref, kc_ref, vl_ref, vc_ref, cos_ref, sin_ref, cos_t_ref, sin_t_ref,
                 qg_ref, kg_ref, o_ref, klb_ref, kcb_ref, vlb_ref, vcb_ref):
    qi = pl.program_id(2)
    n_lat_tiles = SEQ // TILE

    @pl.when(qi == 0)
    def _():
        kg = kg_ref[...]
        klb_ref[...] = _rope(_rms(kl_ref[...], kg), cos_ref[...], sin_ref[...]).astype(BF16)
        kcb_ref[...] = _rms(kc_ref[...], kg).astype(BF16)
        vlb_ref[...] = vl_ref[...].astype(BF16)
        vcb_ref[...] = vc_ref[...].astype(BF16)

    scale = HEAD_DIM ** -0.5 * math.log2(math.e)
    nt = (((1,), (1,)), ((), ()))

    def heads(latent):
        sls = [slice(h * HEAD_DIM, (h + 1) * HEAD_DIM) for h in range(ATTN_HEADS // ATTN_KV)]
        q = [_rms(q_ref[:, sl], qg_ref[...]) for sl in sls]
        if latent:
            q = [_rope(x, cos_t_ref[...], sin_t_ref[...]) for x in q]
        q = [(x * scale).astype(BF16) for x in q]
        s_c = [lax.dot_general(x, kcb_ref[...], nt, preferred_element_type=F32) for x in q]
        m = [jnp.max(x, axis=-1, keepdims=True) for x in s_c]
        if latent:
            s_l = [lax.dot_general(x, klb_ref[...], nt, preferred_element_type=F32) for x in q]
            m = [jnp.maximum(a, jnp.max(x, axis=-1, keepdims=True)) for a, x in zip(m, s_l)]
        p_c = [jnp.exp2(x - a) for x, a in zip(s_c, m)]
        den = [jnp.sum(x, axis=-1, keepdims=True) for x in p_c]
        acc = [jnp.dot(x.astype(BF16), vcb_ref[...], preferred_element_type=F32) for x in p_c]
        if latent:
            p_l = [jnp.exp2(x - a) for x, a in zip(s_l, m)]
            den = [a + jnp.sum(x, axis=-1, keepdims=True) for a, x in zip(den, p_l)]
            acc = [a + jnp.dot(x.astype(BF16), vlb_ref[...], preferred_element_type=F32) for a, x in zip(acc, p_l)]
        for sl, a, dn in zip(sls, acc, den):
            o_ref[:, sl] = (a / dn).astype(o_ref.dtype)

    @pl.when(qi < n_lat_tiles)
    def _():
        heads(True)

    @pl.when(qi == n_lat_tiles)
    def _():
        heads(False)


def _rope_tables():
    rows = SEQ // GRID_W
    row = jnp.repeat(jnp.arange(rows, dtype=F32), GRID_W)
    col = jnp.tile(jnp.arange(GRID_W, dtype=F32), rows)
    axis_dim = HEAD_DIM // 2
    inv_freq = ROPE_THETA ** (-jnp.arange(0, axis_dim, 2, dtype=F32) / axis_dim)
    ang_r = row[:, None] * inv_freq[None]
    ang_c = col[:, None] * inv_freq[None]
    cos = jnp.concatenate([jnp.cos(ang_r), jnp.cos(ang_r), jnp.cos(ang_c), jnp.cos(ang_c)], axis=-1)
    sin = jnp.concatenate([-jnp.sin(ang_r), jnp.sin(ang_r), -jnp.sin(ang_c), jnp.sin(ang_c)], axis=-1)
    return cos, sin


def _attention(z, q_g, k_g, cos, sin):
    n_lat_tiles = SEQ // TILE
    qw = ATTN_Q // ATTN_KV
    kcol = ATTN_Q // HEAD_DIM
    vcol = kcol + ATTN_KV

    def q_row(b, g, qi):
        return jnp.where(qi < n_lat_tiles, n_lat_tiles * b + qi, N_LAT // TILE + b)

    return pl.pallas_call(
        _attn_kernel,
        grid=(NB, ATTN_KV, n_lat_tiles + 1),
        in_specs=[pl.BlockSpec((TILE, qw), lambda b, g, qi: (q_row(b, g, qi), g)),
                  pl.BlockSpec((SEQ, HEAD_DIM), lambda b, g, qi: (b, kcol + g)),
                  pl.BlockSpec((CTX, HEAD_DIM), lambda b, g, qi: (N_LAT // CTX + b, kcol + g)),
                  pl.BlockSpec((SEQ, HEAD_DIM), lambda b, g, qi: (b, vcol + g)),
                  pl.BlockSpec((CTX, HEAD_DIM), lambda b, g, qi: (N_LAT // CTX + b, vcol + g)),
                  pl.BlockSpec((SEQ, HEAD_DIM), lambda b, g, qi: (0, 0)),
                  pl.BlockSpec((SEQ, HEAD_DIM), lambda b, g, qi: (0, 0)),
                  pl.BlockSpec((TILE, HEAD_DIM), lambda b, g, qi: (jnp.minimum(qi, n_lat_tiles - 1), 0)),
                  pl.BlockSpec((TILE, HEAD_DIM), lambda b, g, qi: (jnp.minimum(qi, n_lat_tiles - 1), 0)),
                  pl.BlockSpec((1, HEAD_DIM), lambda b, g, qi: (0, 0)),
                  pl.BlockSpec((1, HEAD_DIM), lambda b, g, qi: (0, 0))],
        out_specs=pl.BlockSpec((TILE, qw), lambda b, g, qi: (q_row(b, g, qi), g)),
        out_shape=jax.ShapeDtypeStruct((ROWS, ATTN_Q), BF16),
        scratch_shapes=[pltpu.VMEM((SEQ, HEAD_DIM), BF16), pltpu.VMEM((CTX, HEAD_DIM), BF16),
                        pltpu.VMEM((SEQ, HEAD_DIM), BF16), pltpu.VMEM((CTX, HEAD_DIM), BF16)],
        compiler_params=_cp("arbitrary", "arbitrary", "arbitrary"),
        name="attention",
    )(z, z, z, z, z, cos, sin, cos, sin, q_g.reshape(1, HEAD_DIM), k_g.reshape(1, HEAD_DIM))


def _rwkv_prep_kernel(z_ref, zp_ref, zn_ref, mu_ref, w0_ref, wup_ref, a0_ref, aup_ref, gup_ref,
                      kk_ref, ka_ref, rk_ref, bd_ref,
                      r_out, v_out, kkn_out, g_out, bonus_out, lw_out, b_out, km_out):
    z = z_ref[...]
    zp, zn = _neighbours(z, zp_ref[...], zn_ref[...])
    zs = z + mu_ref[...] * (0.5 * (zp + zn) - z)
    r = zs[:, 0:RW_W]
    k = zs[:, RW_W:2 * RW_W]
    v = zs[:, 2 * RW_W:3 * RW_W]
    w_lo = zs[:, 3 * RW_W:3 * RW_W + 64]
    a_lo = zs[:, 3 * RW_W + 64:3 * RW_W + 128]
    g_lo = zs[:, 3 * RW_W + 128:3 * RW_W + 256]
    bd = bd_ref[...]
    kk = k * kk_ref[...]
    kk = kk * lax.rsqrt(_dot_rhs_exact(kk * kk, bd) + 1e-12)
    r_out[...] = r
    v_out[...] = v
    kkn_out[...] = kk
    g_out[...] = _bdot(_sigmoid(g_lo), gup_ref[...])
    tw = jnp.tanh(w_lo)
    km_sum = None
    for d in range(2):
        lw = -RW_DECAY * _sigmoid(w0_ref[d] + _bdot(tw, wup_ref[d]))
        a = _sigmoid(a0_ref[d] + _bdot(a_lo, aup_ref[d]))
        km = k * (1.0 + (a - 1.0) * ka_ref[...])
        lw_out[d] = lw
        b_out[d] = a * kk
        km_out[d] = km
        km_sum = km if km_sum is None else km_sum + km
    bonus_out[...] = _dot_rhs_exact(r * km_sum * rk_ref[...], bd) * v


def _rwkv_prep(z, p):
    row = lambda a: a.reshape(1, -1)
    full = lambda shape: pl.BlockSpec(shape, lambda t: (0,) * len(shape))
    out_tok = pl.BlockSpec((TILE, RW_W), lambda t: (t, 0))
    out_dir = pl.BlockSpec((2, TILE, RW_W), lambda t: (0, t, 0))
    tok = jax.ShapeDtypeStruct((ROWS, RW_W), F32)
    drn = jax.ShapeDtypeStruct((2, ROWS, RW_W), F32)
    return pl.pallas_call(
        _rwkv_prep_kernel,
        grid=(N_TILES,),
        in_specs=_tile_halo_specs(RW_IN) + [
            full((1, RW_IN)), full((2, 1, RW_W)), full((2, 64, RW_W)), full((2, 1, RW_W)),
            full((2, 64, RW_W)), full((128, RW_W)), full((1, RW_W)), full((1, RW_W)), full((1, RW_W)),
            full((RW_W, RW_W))],
        out_specs=[out_tok] * 5 + [out_dir] * 3,
        out_shape=[tok] * 5 + [drn] * 3,
        compiler_params=_cp("parallel"),
        name="rwkv_prep",
    )(z, z, z, row(p["mu"]), p["w0"].reshape(2, 1, RW_W), p["w_up"], p["a0"].reshape(2, 1, RW_W),
      p["a_up"], p["g_up"], row(p["k_k"]), row(p["k_a"]), row(p["r_k"]),
      _head_block_ones(RW_W, RW_D).astype(BF16))


def _head_block_ones(width, head):
    idx = np.arange(width) // head
    return jnp.asarray((idx[:, None] == idx[None, :]).astype(np.float32))


def _split_bf16(x, pieces):
    out = []
    for _ in range(pieces):
        p = x.astype(BF16)
        out.append(p)
        x = x - p.astype(F32)
    return out


def _dot_rhs_exact(x, m, pieces=2):
    return sum(jnp.dot(p, m, preferred_element_type=F32) for p in _split_bf16(x, pieces))


def _dot_lhs_exact(m, x, pieces=3):
    return sum(jnp.dot(m, p, preferred_element_type=F32) for p in _split_bf16(x, pieces))


RW_SUB = 2


def _rwkv_chunk_kernel(r_ref, v_ref, kk_ref, lw_ref, b_ref, km_ref, p_ref, sl_ref, re_ref, ol_ref):
    d = pl.program_id(0)
    c = CHUNK
    hd = RW_D
    ti = lax.broadcasted_iota(jnp.int32, (c, c), 0)
    si = lax.broadcasted_iota(jnp.int32, (c, c), 1)
    delta = (ti - si) * (1 - 2 * d)
    incl = delta >= 0
    strict = delta > 0
    eye = jnp.where(ti == si, 1.0, 0.0)
    tri = jnp.where(incl, 1.0, 0.0).astype(BF16)
    gr = lax.broadcasted_iota(jnp.int32, (2 * c, c), 0)
    gc = lax.broadcasted_iota(jnp.int32, (2 * c, c), 1)
    gmask = (gr % c - gc) * (1 - 2 * d) >= jnp.where(gr < c, 1, 0)

    items = []
    for sub in range(RW_SUB):
        rows = slice(sub * c, (sub + 1) * c)
        lw = lw_ref[0, rows, :]
        cs = _dot_lhs_exact(tri, lw)
        tot = jnp.sum(lw, axis=0, keepdims=True)
        r = r_ref[rows, :]
        v = v_ref[rows, :]
        kk = kk_ref[rows, :]
        bb = b_ref[0, rows, :]
        km = km_ref[0, rows, :]
        e_neg = jnp.exp(-cs)
        e_rem = jnp.exp(tot - cs)
        kkt = kk * jnp.exp(cs - lw)
        rt = r * jnp.exp(cs)
        bt = bb * e_neg
        kt = km * e_neg
        bh = bb * e_rem
        kh = km * e_rem
        e_tot = jnp.exp(tot)
        for h in range(RW_H):
            s = slice(h * hd, (h + 1) * hd)
            items.append(dict(sub=sub, s=s, kkt=kkt[:, s], rt=rt[:, s], bt=bt[:, s], kt=kt[:, s], v=v[:, s],
                              bh=bh[:, s], kh=kh[:, s], e_tot=e_tot[:, s]))

    kr = [jnp.concatenate([it["kkt"], it["rt"]], axis=0) for it in items]
    gb = [jnp.where(gmask, _bdot_nt(x, it["bt"]), 0.0) for x, it in zip(kr, items)]
    gk = [jnp.where(gmask, _bdot_nt(x, it["kt"]), 0.0) for x, it in zip(kr, items)]
    l_b = [x[:c] for x in gb]
    a_b = [x[c:] for x in gb]
    lkv_akv = [_bdot(x, it["v"]) for x, it in zip(gk, items)]
    pw = [_bdot(x, x) for x in l_b]
    inv = [eye - x for x in l_b]
    for _ in range(int(math.log2(c)) - 2):
        res = [_bdot(jnp.concatenate([p, i], axis=0), p) for p, i in zip(pw, inv)]
        pw = [x[:c] for x in res]
        inv = [i + x[c:] for i, x in zip(inv, res)]
    inv = [i + _bdot(i, p) for p, i in zip(pw, inv)]
    w = [_bdot(i, it["kkt"]) for i, it in zip(inv, items)]
    y_loc = [_bdot(i, x[:c]) for i, x in zip(inv, lkv_akv)]
    ab_w = [_bdot(a, x) for a, x in zip(a_b, w)]
    ab_y = [_bdot(a, x) for a, x in zip(a_b, y_loc)]
    vk = [_bdot_tn(it["v"], it["kh"]) for it in items]
    yb = [_bdot_tn(x, it["bh"]) for x, it in zip(y_loc, items)]
    wb = [_bdot_tn(x, it["bh"]) for x, it in zip(w, items)]
    for n, it in enumerate(items):
        s = it["s"]
        j = jnp.where(d == 0, it["sub"], RW_SUB - 1 - it["sub"])
        re_ref[0, 0, j, :, s] = it["rt"] - ab_w[n]
        ol_ref[0, 0, j, :, s] = lkv_akv[n][c:] - ab_y[n]
        sl_ref[0, 0, j, :, s] = vk[n] - yb[n]
        p_ref[0, 0, j, :, s] = eye * it["e_tot"] - wb[n]


def _rwkv_carry_kernel(p_ref, sl_ref, s_out_ref, st_ref):
    @pl.when(pl.program_id(0) == 0)
    def _():
        st_ref[...] = jnp.zeros_like(st_ref)

    for d in range(2):
        for b in range(NB):
            s = st_ref[d, b]
            s_out_ref[d, b, 0] = s
            for h in range(RW_H):
                sl = slice(h * RW_D, (h + 1) * RW_D)
                st_ref[d, b, :, sl] = _hdot(s[:, sl], p_ref[d, b, 0, :, sl]) + sl_ref[d, b, 0, :, sl]


TILE_CHUNKS = TILE // CHUNK


def _tile_of_group(b, grp):
    return jnp.where(grp == 0, N_LAT // TILE + b, (SEQ // TILE) * b + grp - 1)


def _bwd_group(grp):
    return jnp.where(grp == 0, 0, SEQ_CHUNKS // TILE_CHUNKS - grp)


def _rwkv_out_kernel(olf_ref, ref_ref, sf_ref, olb_ref, reb_ref, sb_ref, bonus_ref, g_ref, gng_ref, gnb_ref,
                     bd_ref, y_ref, o_scr):
    n = TILE_CHUNKS
    items = [(i, slice(h * RW_D, (h + 1) * RW_D)) for i in range(n) for h in range(RW_H)]
    pf = [_bdot_nt(ref_ref[0, 0, i, :, s], sf_ref[0, 0, i, :, s]) for i, s in items]
    pb = [_bdot_nt(reb_ref[0, 0, n - 1 - i, :, s], sb_ref[0, 0, n - 1 - i, :, s]) for i, s in items]
    for (i, s), a, b in zip(items, pf, pb):
        o_scr[i * CHUNK:(i + 1) * CHUNK, s] = olf_ref[0, 0, i, :, s] + olb_ref[0, 0, n - 1 - i, :, s] + a + b
    o = o_scr[...] + bonus_ref[...]
    bd = bd_ref[...]
    cen = o - _dot_rhs_exact(o, bd) * (1.0 / RW_D)
    var = _dot_rhs_exact(cen * cen, bd) * (1.0 / RW_D)
    y = cen * lax.rsqrt(var + RW_GN_EPS) * gng_ref[...] + gnb_ref[...]
    y_ref[...] = (y * g_ref[...]).astype(y_ref.dtype)


def _rwkv_scan(r, v, kk, lw, bmat, km, bonus, g, gn_g, gn_b):
    sub_rows = RW_SUB * CHUNK
    lat_blocks = N_LAT // sub_rows
    lat_per_b = SEQ // sub_rows
    ctx_per_b = CTX // sub_rows

    def step_block(d, rb):
        is_lat = rb < lat_blocks
        b = jnp.where(is_lat, rb // lat_per_b, (rb - lat_blocks) // ctx_per_b)
        i = jnp.where(is_lat, rb % lat_per_b, (rb - lat_blocks) % ctx_per_b)
        fwd = jnp.where(is_lat, ctx_per_b + i, i)
        bwd = jnp.where(is_lat, ctx_per_b + lat_per_b - 1 - i, ctx_per_b - 1 - i)
        return b, jnp.where(d == 0, fwd, bwd)

    tok = pl.BlockSpec((sub_rows, RW_W), lambda d, rb: (rb, 0))
    drn = pl.BlockSpec((1, sub_rows, RW_W), lambda d, rb: (d, rb, 0))
    step_shape = jax.ShapeDtypeStruct((2, NB, SEQ_CHUNKS, CHUNK, RW_W), F32)
    step_blk = pl.BlockSpec((1, 1, RW_SUB, CHUNK, RW_W), lambda d, rb: (d,) + step_block(d, rb) + (0, 0))
    p, s_loc, r_eff, o_loc = pl.pallas_call(
        _rwkv_chunk_kernel,
        grid=(2, ROWS // sub_rows),
        in_specs=[tok, tok, tok, drn, drn, drn],
        out_specs=[step_blk] * 4,
        out_shape=[step_shape] * 4,
        compiler_params=_cp("parallel", "parallel"),
        name="rwkv_chunk",
    )(r, v, kk, lw, bmat, km)

    all_blk = pl.BlockSpec((2, NB, 1, CHUNK, RW_W), lambda n: (0, 0, n, 0, 0))
    s_in = pl.pallas_call(
        _rwkv_carry_kernel,
        grid=(SEQ_CHUNKS,),
        in_specs=[all_blk, all_blk],
        out_specs=all_blk,
        out_shape=step_shape,
        scratch_shapes=[pltpu.VMEM((2, NB, CHUNK, RW_W), F32)],
        compiler_params=_cp("arbitrary"),
        name="rwkv_carry",
    )(p, s_loc)

    fwd = pl.BlockSpec((1, 1, TILE_CHUNKS, CHUNK, RW_W), lambda b, grp: (0, b, grp, 0, 0))
    bwd = pl.BlockSpec((1, 1, TILE_CHUNKS, CHUNK, RW_W), lambda b, grp: (1, b, _bwd_group(grp), 0, 0))
    rows = pl.BlockSpec((TILE, RW_W), lambda b, grp: (_tile_of_group(b, grp), 0))
    vec = pl.BlockSpec((1, RW_W), lambda b, grp: (0, 0))
    return pl.pallas_call(
        _rwkv_out_kernel,
        grid=(NB, SEQ_CHUNKS // TILE_CHUNKS),
        in_specs=[fwd, fwd, fwd, bwd, bwd, bwd, rows, rows, vec, vec,
                  pl.BlockSpec((RW_W, RW_W), lambda b, grp: (0, 0))],
        out_specs=rows,
        out_shape=jax.ShapeDtypeStruct((ROWS, RW_W), BF16),
        scratch_shapes=[pltpu.VMEM((TILE, RW_W), F32)],
        compiler_params=_cp("parallel", "parallel"),
        name="rwkv_out",
    )(o_loc, r_eff, s_in, o_loc, r_eff, s_in, bonus, g, gn_g.reshape(1, RW_W), gn_b.reshape(1, RW_W),
      _head_block_ones(RW_W, RW_D).astype(BF16))


def _mlstm_prep_kernel(z_ref, zp_ref, zn_ref, w_ref, o_ref):
    z = z_ref[...]
    zp, zn = _neighbours(z, zp_ref[...], zn_ref[...])
    y = _silu(zp * w_ref[0:1, :] + z * w_ref[1:2, :] + zn * w_ref[2:3, :])
    col = lax.broadcasted_iota(jnp.int32, y.shape, 1)
    o_ref[...] = jnp.where(col >= ML_W, y * (ML_D ** -0.5), y)


def _mlstm_prep(z, conv_w):
    return pl.pallas_call(
        _mlstm_prep_kernel,
        grid=(N_TILES,),
        in_specs=_tile_halo_specs(2 * ML_W) + [pl.BlockSpec((3, 2 * ML_W), lambda t: (0, 0))],
        out_specs=pl.BlockSpec((TILE, 2 * ML_W), lambda t: (t, 0)),
        out_shape=jax.ShapeDtypeStruct((ROWS, 2 * ML_W), F32),
        compiler_params=_cp("parallel"),
        name="mlstm_prep",
    )(z, z, z, conv_w)


def _log_sigmoid(x):
    return jnp.minimum(x, 0.0) - jnp.log(1.0 + jnp.exp(-jnp.abs(x)))


N_CHAINS = 2 * NB


def _mlstm_scan_kernel(*refs):
    nc = N_CHAINS
    q_refs, k_refs, v_refs = refs[0:nc], refs[nc:2 * nc], refs[2 * nc:3 * nc]
    gc_refs, gr_refs = refs[3 * nc:4 * nc], refs[4 * nc:5 * nc]
    bc_ref, br_ref, o_ref, c_ref, n_ref, m_ref = refs[5 * nc:]

    @pl.when(pl.program_id(0) == 0)
    def _():
        c_ref[...] = jnp.zeros_like(c_ref)
        n_ref[...] = jnp.zeros_like(n_ref)
        m_ref[...] = jnp.zeros_like(m_ref)

    c = CHUNK
    ti = lax.broadcasted_iota(jnp.int32, (c, c), 0)
    si = lax.broadcasted_iota(jnp.int32, (c, c), 1)
    masks = (ti >= si, ti <= si)
    items = [(ci, h) for ci in range(nc) for h in range(ML_H)]
    sls = [slice(h * ML_D, (h + 1) * ML_D) for h in range(ML_H)]
    gcol = [gc_refs[ci][0, 0] + bc_ref[ci // NB] for ci in range(nc)]
    grow = [gr_refs[ci][0, 0] + br_ref[ci // NB] for ci in range(nc)]

    ones = jnp.ones((c, ML_D), BF16)
    q = [q_refs[ci][:, sls[h]] for ci, h in items]
    k = [k_refs[ci][:, sls[h]] for ci, h in items]
    v = [v_refs[ci][:, sls[h]] for ci, h in items]
    qk = [_bdot_nt(a, b) for a, b in zip(q, k)]
    c_mat = [c_ref[ci, h] for ci, h in items]
    n_vec = [n_ref[ci, h] for ci, h in items]
    m_prev = [m_ref[ci, h][0:1, :] for ci, h in items]
    qcn = [_bdot_nt(a, jnp.concatenate([cm, jnp.broadcast_to(nv, (ML_D, ML_D))], axis=0))
           for a, cm, nv in zip(q, c_mat, n_vec)]

    log_w, cum, i_col, total = [], [], [], []
    for ci, h in items:
        mask = masks[ci // NB]
        mask_t = masks[1 - ci // NB]
        f_col = _log_sigmoid(gcol[ci][:, ML_H + h:ML_H + h + 1])
        f_row = _log_sigmoid(grow[ci][ML_H + h:ML_H + h + 1, :])
        cc = _dot_rhs_exact(jnp.where(mask, f_row, 0.0), ones, pieces=3)
        cr = jnp.sum(jnp.where(mask_t, f_col, 0.0), axis=0, keepdims=True)
        log_w.append(jnp.where(mask, cc[:, :c] - cr + grow[ci][h:h + 1, :], ML_NEG))
        cum.append(cc)
        i_col.append(jnp.broadcast_to(gcol[ci][:, h:h + 1], (c, ML_D)))
        total.append(jnp.sum(f_row, axis=1, keepdims=True))
    m_inter = [a + b for a, b in zip(cum, m_prev)]
    m_t = [jnp.maximum(jnp.broadcast_to(jnp.max(a, axis=1, keepdims=True), (c, ML_D)), b)
           for a, b in zip(log_w, m_inter)]
    s = [a * jnp.exp(b - m[:, :c]) for a, b, m in zip(qk, log_w, m_t)]
    w_inter = [jnp.exp(a - m) for a, m in zip(m_inter, m_t)]
    sv = [_bdot(a, jnp.concatenate([b.astype(BF16), ones], axis=1)) for a, b in zip(s, v)]
    for i, (ci, h) in enumerate(items):
        num = sv[i][:, :ML_D] + w_inter[i] * qcn[i][:, :ML_D]
        den = sv[i][:, ML_D:] + w_inter[i] * qcn[i][:, ML_D:]
        o_ref[ci // NB, ci % NB, 0, :, sls[h]] = num / jnp.maximum(jnp.abs(den), jnp.exp(-m_t[i]))
    log_src = [t - a + b for t, a, b in zip(total, cum, i_col)]
    m_new = [jnp.maximum(t + mp, jnp.max(ls, axis=0, keepdims=True)) for t, mp, ls in zip(total, m_prev, log_src)]
    src = [jnp.exp(ls - mn) for ls, mn in zip(log_src, m_new)]
    decay = [jnp.exp(t + mp - mn) for t, mp, mn in zip(total, m_prev, m_new)]
    vk = [_bdot_tn(a * sr, b) for a, sr, b in zip(v, src, k)]
    for i, (ci, h) in enumerate(items):
        c_ref[ci, h] = decay[i] * c_mat[i] + vk[i]
        n_ref[ci, h] = decay[i] * n_vec[i] + jnp.sum(src[i] * k[i], axis=0, keepdims=True)
        m_ref[ci, h] = jnp.broadcast_to(m_new[i], m_ref.shape[2:])


def _mlstm_scan(qk, z, gcol, grow, bcol, brow):
    chains = [(d, b) for d in range(2) for b in range(NB)]

    def tok(col_blk):
        return [pl.BlockSpec((CHUNK, ML_W), lambda n, d=d, b=b: (_seq_row_block(b, d, n), col_blk))
                for d, b in chains]

    gc_specs = [pl.BlockSpec((1, 1, CHUNK, 2 * ML_H), lambda n, d=d, b=b: (d, _seq_row_block(b, d, n), 0, 0))
                for d, b in chains]
    gr_specs = [pl.BlockSpec((1, 1, 2 * ML_H, CHUNK), lambda n, d=d, b=b: (d, _seq_row_block(b, d, n), 0, 0))
                for d, b in chains]
    nc = N_CHAINS
    return pl.pallas_call(
        _mlstm_scan_kernel,
        grid=(SEQ_CHUNKS,),
        in_specs=tok(0) + tok(1) + tok(2) + gc_specs + gr_specs + [
            pl.BlockSpec((2, 1, 2 * ML_H), lambda n: (0, 0, 0)),
            pl.BlockSpec((2, 2 * ML_H, 1), lambda n: (0, 0, 0))],
        out_specs=pl.BlockSpec((2, NB, 1, CHUNK, ML_W), lambda n: (0, 0, n, 0, 0)),
        out_shape=jax.ShapeDtypeStruct((2, NB, SEQ_CHUNKS, CHUNK, ML_W), F32),
        scratch_shapes=[pltpu.VMEM((nc, ML_H, ML_D, ML_D), F32), pltpu.VMEM((nc, ML_H, 1, ML_D), F32),
                        pltpu.VMEM((nc, ML_H, 8, 128), F32)],
        compiler_params=_cp("arbitrary"),
        name="mlstm_scan",
    )(*([qk] * (2 * nc) + [z] * nc + [gcol] * nc + [grow] * nc + [bcol, brow]))


def _mlstm_out_kernel(hf_ref, hb_ref, og_ref, gng_ref, y_ref):
    n = TILE_CHUNKS
    for i in range(n):
        rows = slice(i * CHUNK, (i + 1) * CHUNK)
        hsum = _sigmoid(og_ref[rows, :]) * (hf_ref[0, 0, i] + hb_ref[0, 0, n - 1 - i])
        for h in range(ML_H):
            sl = slice(h * ML_D, (h + 1) * ML_D)
            x = hsum[:, sl]
            cen = x - jnp.mean(x, axis=-1, keepdims=True)
            var = jnp.mean(cen * cen, axis=-1, keepdims=True)
            y_ref[rows, sl] = (cen * lax.rsqrt(var + EPS) * gng_ref[:, sl]).astype(y_ref.dtype)


def _mlstm_out(hs, z, gn_g):
    blk = (1, 1, TILE_CHUNKS, CHUNK, ML_W)
    return pl.pallas_call(
        _mlstm_out_kernel,
        grid=(NB, SEQ_CHUNKS // TILE_CHUNKS),
        in_specs=[pl.BlockSpec(blk, lambda b, grp: (0, b, grp, 0, 0)),
                  pl.BlockSpec(blk, lambda b, grp: (1, b, _bwd_group(grp), 0, 0)),
                  pl.BlockSpec((TILE, ML_W), lambda b, grp: (_tile_of_group(b, grp), 3)),
                  pl.BlockSpec((1, ML_W), lambda b, grp: (0, 0))],
        out_specs=pl.BlockSpec((TILE, ML_W), lambda b, grp: (_tile_of_group(b, grp), 0)),
        out_shape=jax.ShapeDtypeStruct((ROWS, ML_W), BF16),
        compiler_params=_cp("parallel", "parallel"),
        name="mlstm_out",
    )(hs, hs, z, gn_g.reshape(1, ML_W))


def _mlstm(z_main, z_gates, conv_w, i_b, f_b, gn_g):
    qk = _mlstm_prep(z_main, conv_w)
    n_chunks = ROWS // CHUNK
    gates = z_gates[:, :4 * ML_H].reshape(n_chunks, CHUNK, 2, 2, ML_H)
    gcol = jnp.transpose(gates, (3, 0, 1, 2, 4)).reshape(2, n_chunks, CHUNK, 2 * ML_H)
    grow = jnp.swapaxes(gcol, 2, 3)
    bias = jnp.concatenate([i_b, f_b], axis=-1)
    hs = _mlstm_scan(qk, z_main, gcol, grow, bias.reshape(2, 1, 2 * ML_H), bias.reshape(2, 2 * ML_H, 1))
    return _mlstm_out(hs, z_main, gn_g)


def _s5_toeplitz_kernel(kf_ref, kb_ref, o_ref):
    kf = kf_ref[0]
    kb = kb_ref[0]
    lane = lax.broadcasted_iota(jnp.int32, kf.shape, 1)
    width = S5_L * S5_C
    for j in range(S5_L):
        f = kf if j == 0 else jnp.where(lane >= S5_C * j, pltpu.roll(kf, S5_C * j, 1), 0.0)
        back = S5_L - 1 - j
        b = kb if back == 0 else jnp.where(lane < S5_C * (j + 1), pltpu.roll(kb, width - S5_C * back, 1), 0.0)
        o_ref[0, 0, :, j] = f.reshape(S5_G, S5_C, width).astype(o_ref.dtype)
        o_ref[0, 1, :, j] = b.reshape(S5_G, S5_C, width).astype(o_ref.dtype)


def _s5_operators(lam_re, lam_im, log_step, b_re, b_im, c_re, c_im):
    nl = lam_re.shape[0]
    dt = jnp.exp(log_step)[..., None]
    mag = jnp.exp(lam_re * dt)
    a_re = mag * jnp.cos(lam_im * dt)
    a_im = mag * jnp.sin(lam_im * dt)
    den = lam_re * lam_re + lam_im * lam_im
    f_re = ((a_re - 1) * lam_re + a_im * lam_im) / den
    f_im = (a_im * lam_re - (a_re - 1) * lam_im) / den
    bb_re = f_re[..., None] * b_re - f_im[..., None] * b_im
    bb_im = f_re[..., None] * b_im + f_im[..., None] * b_re
    bt_re = jnp.swapaxes(bb_re, -1, -2)
    bt_im = jnp.swapaxes(bb_im, -1, -2)
    pr = [jnp.ones_like(a_re)]
    pi = [jnp.zeros_like(a_im)]
    for _ in range(S5_L):
        pr.append(pr[-1] * a_re - pi[-1] * a_im)
        pi.append(pr[-2] * a_im + pi[-1] * a_re)
    pr = jnp.stack(pr, axis=3)
    pi = jnp.stack(pi, axis=3)

    def times_b(qr, qi):
        qr, qi = qr[..., None, :], qi[..., None, :]
        br, bi = bt_re[:, :, :, None], bt_im[:, :, :, None]
        return qr * br - qi * bi, qr * bi + qi * br

    wr, wi = times_b(pr[:, :, :, :S5_L], pi[:, :, :, :S5_L])
    wr = jnp.swapaxes(wr, 3, 4)
    wi = jnp.swapaxes(wi, 3, 4)
    kern = (jnp.einsum("ldgktp,ldgcp->ldgktc", wr, c_re, precision=HP)
            - jnp.einsum("ldgktp,ldgcp->ldgktc", wi, c_im, precision=HP))
    kf = kern[:, 0].reshape(nl, S5_G * S5_C, S5_L * S5_C)
    kb = jnp.flip(kern[:, 1], axis=3).reshape(nl, S5_G * S5_C, S5_L * S5_C)
    rows_blk = pl.BlockSpec((1, S5_G * S5_C, S5_L * S5_C), lambda l: (l, 0, 0))
    m = pl.pallas_call(
        _s5_toeplitz_kernel,
        grid=(nl,),
        in_specs=[rows_blk, rows_blk],
        out_specs=pl.BlockSpec((1, 2, S5_G, S5_L, S5_C, S5_L * S5_C), lambda l: (l, 0, 0, 0, 0, 0)),
        out_shape=jax.ShapeDtypeStruct((nl, 2, S5_G, S5_L, S5_C, S5_L * S5_C), BF16),
        compiler_params=_cp("parallel"),
        name="s5_toeplitz",
    )(kf, kb).reshape(nl, 2, S5_G, S5_L * S5_C, S5_L * S5_C)

    def stack_dirs(fwd, bwd):
        return jnp.stack([fwd[:, 0], bwd[:, 1]], axis=1)

    er, ei = times_b(stack_dirs(jnp.flip(pr[:, :, :, :S5_L], axis=3), pr[:, :, :, :S5_L]),
                     stack_dirs(jnp.flip(pi[:, :, :, :S5_L], axis=3), pi[:, :, :, :S5_L]))
    e = jnp.concatenate([er, ei], axis=-1).reshape(nl, 2, S5_G, S5_L * S5_C, 2 * S5_P).astype(BF16)
    qr = stack_dirs(pr[:, :, :, 1:], jnp.flip(pr[:, :, :, 1:], axis=3))[..., None, :]
    qi = stack_dirs(pi[:, :, :, 1:], jnp.flip(pi[:, :, :, 1:], axis=3))[..., None, :]
    cr, ci = c_re[:, :, :, None], c_im[:, :, :, None]
    ft = jnp.concatenate([cr * qr - ci * qi, -(cr * qi + ci * qr)], axis=-1)
    ft = ft.reshape(nl, 2, S5_G, S5_L * S5_C, 2 * S5_P).astype(BF16)
    return m, e, ft, pr[:, :, :, S5_L], pi[:, :, :, S5_L]


def _s5_local_kernel(u_ref, m_ref, e_ref, y_ref, x_ref):
    u = u_ref[0]
    for d in range(2):
        y_ref[d, 0] = jnp.dot(u, m_ref[d, 0], preferred_element_type=F32)
        x_ref[d, 0] = jnp.dot(u, e_ref[d, 0], preferred_element_type=F32)


def _s5_carry_kernel(xc_ref, ar_ref, ai_ref, x0_ref, st_ref):
    @pl.when(pl.program_id(1) == 0)
    def _():
        st_ref[...] = jnp.zeros_like(st_ref)

    ar = ar_ref[0]
    ai = ai_ref[0]

    def run(order):
        x = st_ref[...]
        for i in order:
            x0_ref[0, :, :, i, :] = x
            x = x * ar + pltpu.roll(x, S5_P, 2) * ai + xc_ref[0, :, :, i, :]
        st_ref[...] = x

    n = xc_ref.shape[3]

    @pl.when(pl.program_id(0) == 0)
    def _():
        run(range(n))

    @pl.when(pl.program_id(0) == 1)
    def _():
        run(reversed(range(n)))


def _s5_state_kernel(y_ref, x0_ref, ft_ref, o_ref):
    acc = y_ref[0, 0] + y_ref[1, 0]
    for d in range(2):
        acc = acc + _bdot_nt(x0_ref[d, 0], ft_ref[d, 0])
    o_ref[0] = acc.astype(o_ref.dtype)


def _s5_glu_kernel(y_ref, u_ref, d_ref, w_ref, o_ref):
    x = y_ref[...] + u_ref[...] * d_ref[...]
    ge = 0.5 * x * (1.0 + jnp.tanh(math.sqrt(2.0 / math.pi) * (x + 0.044715 * (x * x * x))))
    p = _bdot(ge, w_ref[...])
    o_ref[...] = (p[:, :S5_W] * _sigmoid(p[:, S5_W:])).astype(o_ref.dtype)


def _s5(z, m, e, ft, a_re, a_im, d_skip, w_glu):
    n_ch = (SEQ + CTX) // S5_L
    n_ctx_ch = CTX // S5_L
    rows = NB * n_ch
    width = S5_L * S5_C

    def to_groups(x):
        x = x.reshape(NB, -1, S5_L, S5_G, S5_C)
        return jnp.transpose(x, (3, 0, 1, 2, 4)).reshape(S5_G, NB, -1, width)

    zb = z.astype(BF16)
    u = jnp.concatenate([to_groups(zb[N_LAT:]), to_groups(zb[:N_LAT])], axis=2).reshape(S5_G, rows, width)
    y_loc, x_in = pl.pallas_call(
        _s5_local_kernel,
        grid=(S5_G,),
        in_specs=[pl.BlockSpec((1, rows, width), lambda g: (g, 0, 0)),
                  pl.BlockSpec((2, 1, width, width), lambda g: (0, g, 0, 0)),
                  pl.BlockSpec((2, 1, width, 2 * S5_P), lambda g: (0, g, 0, 0))],
        out_specs=[pl.BlockSpec((2, 1, rows, width), lambda g: (0, g, 0, 0)),
                   pl.BlockSpec((2, 1, rows, 2 * S5_P), lambda g: (0, g, 0, 0))],
        out_shape=[jax.ShapeDtypeStruct((2, S5_G, rows, width), F32),
                   jax.ShapeDtypeStruct((2, S5_G, rows, 2 * S5_P), F32)],
        compiler_params=_cp("parallel"),
        name="s5_local",
    )(u, m, e)

    step = n_ctx_ch
    n_blk = n_ch // step
    coef_r = jnp.concatenate([a_re, a_re], axis=-1)[:, :, None, :]
    coef_i = jnp.concatenate([-a_im, a_im], axis=-1)[:, :, None, :]

    def chunk_blk(d, i):
        return jnp.where(d == 0, i, jnp.where(i == 0, 0, n_blk - i))

    st_blk = pl.BlockSpec((1, S5_G, NB, step, 2 * S5_P), lambda d, i: (d, 0, 0, chunk_blk(d, i), 0))
    coef_blk = pl.BlockSpec((1, S5_G, 1, 2 * S5_P), lambda d, i: (d, 0, 0, 0))
    x0 = pl.pallas_call(
        _s5_carry_kernel,
        grid=(2, n_blk),
        in_specs=[st_blk, coef_blk, coef_blk],
        out_specs=st_blk,
        out_shape=jax.ShapeDtypeStruct((2, S5_G, NB, n_ch, 2 * S5_P), F32),
        scratch_shapes=[pltpu.VMEM((S5_G, NB, 2 * S5_P), F32)],
        compiler_params=_cp("arbitrary", "arbitrary"),
        name="s5_carry",
    )(x_in.reshape(2, S5_G, NB, n_ch, 2 * S5_P), coef_r, coef_i).reshape(2, S5_G, rows, 2 * S5_P)

    y = pl.pallas_call(
        _s5_state_kernel,
        grid=(S5_G,),
        in_specs=[pl.BlockSpec((2, 1, rows, width), lambda g: (0, g, 0, 0)),
                  pl.BlockSpec((2, 1, rows, 2 * S5_P), lambda g: (0, g, 0, 0)),
                  pl.BlockSpec((2, 1, width, 2 * S5_P), lambda g: (0, g, 0, 0))],
        out_specs=pl.BlockSpec((1, rows, width), lambda g: (g, 0, 0)),
        out_shape=jax.ShapeDtypeStruct((S5_G, rows, width), BF16),
        compiler_params=_cp("parallel"),
        name="s5_state",
    )(y_loc, x0, ft)

    y = jnp.transpose(y.reshape(S5_G, NB, n_ch, S5_L, S5_C), (1, 2, 3, 0, 4)).reshape(NB, n_ch * S5_L, S5_W)
    y = jnp.concatenate([y[:, CTX:].reshape(N_LAT, S5_W), y[:, :CTX].reshape(N_CTX, S5_W)], axis=0)
    tok = pl.BlockSpec((TILE, S5_W), lambda t: (t, 0))
    return pl.pallas_call(
        _s5_glu_kernel,
        grid=(N_TILES,),
        in_specs=[tok, tok, pl.BlockSpec((1, S5_W), lambda t: (0, 0)),
                  pl.BlockSpec((S5_W, 2 * S5_W), lambda t: (0, 0))],
        out_specs=tok,
        out_shape=jax.ShapeDtypeStruct((ROWS, S5_W), BF16),
        compiler_params=_cp("parallel"),
        name="s5_glu",
    )(y, z, d_skip.reshape(1, S5_W), w_glu)


def kernel(x, c, ctx, c_ctx, w_mod, b_mod, norm1_g, norm2_g, w_in, b_gate, q_norm_g, k_norm_g, rwkv_mu, rwkv_w0, rwkv_w_up, rwkv_a0, rwkv_a_up, rwkv_g_up, rwkv_k_k, rwkv_k_a, rwkv_r_k, rwkv_gn_g, rwkv_gn_b, mlstm_conv_w, mlstm_i_b, mlstm_f_b, mlstm_gn_g, s5_lam_re, s5_lam_im, s5_log_step, s5_b_re, s5_b_im, s5_c_re, s5_c_im, s5_d, s5_w_glu, w_br_attn, w_br_rwkv, w_br_mlstm, w_br_s5, w_out, w_ffn_in, w_ffn_out, final_norm_g):
    cos, sin = _rope_tables()
    xs = (x.reshape(N_LAT, D), ctx.reshape(N_CTX, D))
    c_all = jnp.concatenate([c, c_ctx[None], jnp.zeros((3, D), F32)], axis=0)
    s5_m, s5_e, s5_ft, s5_ar, s5_ai = _s5_operators(s5_lam_re, s5_lam_im, s5_log_step, s5_b_re, s5_b_im,
                                                    s5_c_re, s5_c_im)
    tm = 1024
    for l in range(DEPTH):
        last = l == DEPTH - 1
        n_rows = N_LAT if last else ROWS
        mod = _modulation(c_all, w_mod, b_mod, l).reshape(8, 1, 6 * D)
        h = _norm_mod(xs, norm1_g[l], mod, 0, 1, ROWS)
        w_attn, w_rwkv, w_ml, w_mlg, w_s5, w_gate = _w_in_split(w_in, l)
        z_attn = _mm(h, w_attn, tm, 512)
        z_rwkv = _mm(h, w_rwkv, tm, 896)
        z_ml = _mm(h, w_ml, tm, 1024)
        z_mlg = _mm(h, w_mlg, tm, 128)
        z_s5 = _mm(h, w_s5, tm, 512)
        gates = _mm_gate(h, w_gate, b_gate, l, tm, 1024, n_rows)

        ya = _attention(z_attn, q_norm_g[l], k_norm_g[l], cos, sin)
        rp = dict(mu=rwkv_mu[l], w0=rwkv_w0[l], w_up=rwkv_w_up[l], a0=rwkv_a0[l], a_up=rwkv_a_up[l],
                  g_up=rwkv_g_up[l], k_k=rwkv_k_k[l], k_a=rwkv_k_a[l], r_k=rwkv_r_k[l].reshape(RW_W))
        r, v, kk, g, bonus, lw, bmat, km = _rwkv_prep(z_rwkv, rp)
        yr = _rwkv_scan(r, v, kk, lw, bmat, km, bonus, g, rwkv_gn_g[l], rwkv_gn_b[l])
        ym = _mlstm(z_ml, z_mlg, mlstm_conv_w[l], mlstm_i_b[l], mlstm_f_b[l], mlstm_gn_g[l])
        ys = _s5(z_s5, s5_m[l], s5_e[l], s5_ft[l], s5_ar[l], s5_ai[l], s5_d[l], s5_w_glu[l])

        y = _merge(ya, yr, ym, ys, gates, w_br_attn, w_br_rwkv, w_br_mlstm, w_br_s5, l, tm, 512, n_rows)
        xs = _mm_res(y, w_out, l, xs, mod, 2, tm, 512 if isinstance(xs, tuple) else 1024, n_rows)
        h2 = _norm_mod(xs, norm2_g[l], mod, 3, 4, n_rows)
        u = _ffn_in(h2, w_ffn_in, l, tm, 512, n_rows)
        xs = _mm_res(u, w_ffn_out, l, xs, mod, 5, 512, 512, n_rows)
    return _final_norm(xs, final_norm_g).reshape(NB, SEQ, D)
```

```python
import functools
import math

import numpy as np
import jax
import jax.numpy as jnp
from jax import lax
from jax.experimental import pallas as pl
from jax.experimental.pallas import tpu as pltpu

F32 = jnp.float32
BF16 = jnp.bfloat16
HP = lax.Precision.HIGHEST

D = 2048
NB = 4
SEQ = 2048
CTX = 256
DEPTH = 2
N_LAT = NB * SEQ
N_CTX = NB * CTX
ROWS = N_LAT + N_CTX
EPS = 1e-6
GRID_W = 64

HEAD_DIM = 128
ATTN_HEADS = 8
ATTN_KV = 2
ROPE_THETA = 10000.0
ATTN_Q = ATTN_HEADS * HEAD_DIM
ATTN_IN = (ATTN_HEADS + 2 * ATTN_KV) * HEAD_DIM

RW_H = 8
RW_D = 64
RW_W = 512
RW_IN = 3 * RW_W + 64 + 64 + 128
RW_DECAY = math.exp(-0.5)
RW_GN_EPS = 64e-5

ML_H = 4
ML_D = 128
ML_W = 512
ML_NEG = -1e30
ML_IN = 4 * ML_W + 4 * ML_H

S5_W = 512
S5_C = 16
S5_G = 32
S5_P = 64
S5_L = 16

FFN_H = 5632
GATE_IN = 4 * D

CHUNK = 64
TILE = 256
N_TILES = ROWS // TILE
SEQ_CHUNKS = (SEQ + CTX) // CHUNK
CTX_CHUNKS = CTX // CHUNK
LAT_CHUNKS = SEQ // CHUNK

VMEM_LIMIT_BYTES = 56 * 1024 * 1024


def _cp(*sem):
    return pltpu.CompilerParams(dimension_semantics=sem, vmem_limit_bytes=VMEM_LIMIT_BYTES)


def _bdot(a, b):
    return jnp.dot(a.astype(BF16), b.astype(BF16), preferred_element_type=F32)


def _bdot_nt(a, b):
    return lax.dot_general(a.astype(BF16), b.astype(BF16), (((1,), (1,)), ((), ())),
                           preferred_element_type=F32)


def _bdot_tn(a, b):
    return lax.dot_general(a.astype(BF16), b.astype(BF16), (((0,), (0,)), ((), ())),
                           preferred_element_type=F32)


def _hdot(a, b):
    return jnp.dot(a, b, precision=HP, preferred_element_type=F32)


def _sigmoid(x):
    return 0.5 * jnp.tanh(0.5 * x) + 0.5


def _silu(x):
    return x * _sigmoid(x)


def _mod_kernel(c_ref, w_ref, b_ref, o_ref):
    c_hi, c_lo = _split_bf16(_silu(c_ref[...]), 2)
    w_hi, w_lo = _split_bf16(w_ref[...], 2)
    rows = c_hi.shape[0]
    both = jnp.dot(jnp.concatenate([c_hi, c_lo], axis=0), w_hi, preferred_element_type=F32)
    o_ref[...] = both[:rows] + both[rows:] + jnp.dot(c_hi, w_lo, preferred_element_type=F32) + b_ref[...]


def _modulation(c_all, w, b, l):
    tn = 1024
    return pl.pallas_call(
        _mod_kernel,
        grid=(6 * D // tn,),
        in_specs=[pl.BlockSpec((8, D), lambda j: (0, 0)),
                  pl.BlockSpec((None, D, tn), lambda j: (l, 0, j)),
                  pl.BlockSpec((None, 1, tn), lambda j: (l, 0, j))],
        out_specs=pl.BlockSpec((8, tn), lambda j: (0, j)),
        out_shape=jax.ShapeDtypeStruct((8, 6 * D), F32),
        compiler_params=_cp("arbitrary"),
        name="modulation",
    )(c_all, w, b.reshape(DEPTH, 1, 6 * D))


def _mod_row(i, tm):
    return jnp.where(i * tm < N_LAT, (i * tm) // SEQ, NB)


def _stream_specs(x, tr, width, row_axis, col_of):
    def spec(row_of):
        return pl.BlockSpec((tr, width), lambda *ids: (row_of(ids[row_axis]), col_of(ids)))

    if not isinstance(x, tuple):
        return [spec(lambda i: i)], (x,)
    n_lat = N_LAT // tr
    return [spec(lambda i: jnp.minimum(i, n_lat - 1)), spec(lambda i: jnp.maximum(i - n_lat, 0))], x


def _stream_rows(x_refs, tile_id, tr, rows=slice(None)):
    x = x_refs[0][rows, :]
    if len(x_refs) == 2:
        x = jnp.where(tile_id < N_LAT // tr, x, x_refs[1][rows, :])
    return x


def _norm_mod_kernel(*refs):
    x_refs, (g_ref, sh_ref, sc_ref, o_ref) = refs[:-4], refs[-4:]
    x = _stream_rows(x_refs, pl.program_id(0), o_ref.shape[0])
    y = x * lax.rsqrt(jnp.mean(x * x, axis=-1, keepdims=True) + EPS) * g_ref[...]
    o_ref[...] = (y * (1.0 + sc_ref[0]) + sh_ref[0]).astype(o_ref.dtype)


NORM_TILE = 512


def _norm_mod(x, g, modr, shift_blk, scale_blk, n_rows):
    tr = NORM_TILE
    x_specs, x_args = _stream_specs(x, tr, D, 0, lambda ids: 0)
    return pl.pallas_call(
        _norm_mod_kernel,
        grid=(n_rows // tr,),
        in_specs=x_specs + [
            pl.BlockSpec((1, D), lambda i: (0, 0)),
            pl.BlockSpec((1, 1, D), lambda i: (_mod_row(i, tr), 0, shift_blk)),
            pl.BlockSpec((1, 1, D), lambda i: (_mod_row(i, tr), 0, scale_blk))],
        out_specs=pl.BlockSpec((tr, D), lambda i: (i, 0)),
        out_shape=jax.ShapeDtypeStruct((n_rows, D), BF16),
        compiler_params=_cp("parallel"),
        name="norm_mod",
    )(*x_args, g.reshape(1, D), modr, modr)


W_IN_OFFSETS = tuple(int(v) for v in np.cumsum([0, ATTN_IN, RW_IN, 4 * ML_W, 4 * ML_H, S5_W, GATE_IN]))
ML_GATE_PAD = 128


W_TILE = 256
W_ALIGN = 16


def _w_in_group_kernel(*refs, shift, valid):
    a_ref, o_ref = refs[0], refs[-1]
    x = a_ref[...]
    if shift:
        x = jnp.concatenate([x[shift:], refs[1][...]], axis=0)
    if valid < x.shape[0]:
        row = lax.broadcasted_iota(jnp.int32, x.shape, 0)
        x = jnp.where(row < valid, x, 0.0)
    o_ref[...] = x.T.astype(BF16)


def _w_in_group(wt, l, start, n_cols):
    tile = W_TILE if n_cols >= W_TILE else ML_GATE_PAD
    a0, shift = divmod(start, tile)
    assert shift in (0, W_ALIGN)
    n_tiles = -(-n_cols // tile)
    in_specs = [pl.BlockSpec((None, tile, D), lambda t: (l, a0 + t, 0))]
    if shift:
        per = tile // W_ALIGN
        in_specs.append(pl.BlockSpec((None, W_ALIGN, D), lambda t: (l, (a0 + t + 1) * per, 0)))
    return pl.pallas_call(
        functools.partial(_w_in_group_kernel, shift=shift, valid=min(n_cols, tile)),
        grid=(n_tiles,),
        in_specs=in_specs,
        out_specs=pl.BlockSpec((D, tile), lambda t: (0, t)),
        out_shape=jax.ShapeDtypeStruct((D, n_tiles * tile), BF16),
        compiler_params=_cp("parallel"),
        name="w_in_group",
    )(*([wt] * len(in_specs)))


def _w_in_split(w, l):
    wt = jnp.swapaxes(w, 1, 2)
    o = W_IN_OFFSETS
    return tuple(_w_in_group(wt, l, o[i], o[i + 1] - o[i]) for i in range(6))


def _mm_kernel(a_ref, w_ref, o_ref):
    o_ref[...] = jnp.dot(a_ref[...], w_ref[...], preferred_element_type=F32).astype(o_ref.dtype)


def _mm(a, w, tm, tn, out_dtype=F32):
    m, k = a.shape
    n = w.shape[1]
    return pl.pallas_call(
        _mm_kernel,
        grid=(n // tn, m // tm),
        in_specs=[pl.BlockSpec((tm, k), lambda j, i: (i, 0)),
                  pl.BlockSpec((k, tn), lambda j, i: (0, j))],
        out_specs=pl.BlockSpec((tm, tn), lambda j, i: (i, j)),
        out_shape=jax.ShapeDtypeStruct((m, n), out_dtype),
        compiler_params=_cp("parallel", "parallel"),
        name="matmul",
    )(a, w)


ROW_SUB = 256


def _row_blocks(ref):
    return [slice(r, r + ROW_SUB) for r in range(0, ref.shape[0], ROW_SUB)]


def _mm_gate_kernel(a_ref, w_ref, b_ref, o_ref):
    for rows in _row_blocks(o_ref):
        z = jnp.dot(a_ref[rows, :], w_ref[...], preferred_element_type=F32)
        o_ref[rows, :] = _sigmoid(z + b_ref[...]).astype(o_ref.dtype)


def _mm_gate(a, w, b_gate, l, tm, tn, m):
    k = a.shape[1]
    n = w.shape[1]
    return pl.pallas_call(
        _mm_gate_kernel,
        grid=(n // tn, m // tm),
        in_specs=[pl.BlockSpec((tm, k), lambda j, i: (i, 0)),
                  pl.BlockSpec((k, tn), lambda j, i: (0, j)),
                  pl.BlockSpec((None, 1, tn), lambda j, i: (l, 0, j))],
        out_specs=pl.BlockSpec((tm, tn), lambda j, i: (i, j)),
        out_shape=jax.ShapeDtypeStruct((m, n), BF16),
        compiler_params=_cp("parallel", "parallel"),
        name="matmul_gate",
    )(a, w, b_gate.reshape(DEPTH, 1, GATE_IN))


def _mm_res_kernel(a_ref, w_ref, *rest):
    x_refs, (g_ref, o_ref, wb_ref) = rest[:-3], rest[-3:]

    @pl.when(pl.program_id(1) == 0)
    def _():
        wb_ref[...] = w_ref[...].astype(BF16)

    for rows in _row_blocks(o_ref):
        y = jnp.dot(a_ref[rows, :], wb_ref[...], preferred_element_type=F32)
        o_ref[rows, :] = _stream_rows(x_refs, pl.program_id(1), o_ref.shape[0], rows) + g_ref[0] * y


def _mm_res(a, w, l, x, mod, gate_blk, tm, tn, n_rows):
    k = a.shape[1]
    n = w.shape[2]
    gpb = D // tn
    x_specs, x_args = _stream_specs(x, tm, tn, 1, lambda ids: ids[0])
    return pl.pallas_call(
        _mm_res_kernel,
        grid=(n // tn, n_rows // tm),
        in_specs=[pl.BlockSpec((tm, k), lambda j, i: (i, 0)),
                  pl.BlockSpec((None, k, tn), lambda j, i: (l, 0, j))] + x_specs + [
                  pl.BlockSpec((1, 1, tn), lambda j, i: (_mod_row(i, tm), 0, gate_blk * gpb + j))],
        out_specs=pl.BlockSpec((tm, tn), lambda j, i: (i, j)),
        out_shape=jax.ShapeDtypeStruct((n_rows, n), F32),
        scratch_shapes=[pltpu.VMEM((k, tn), BF16)],
        compiler_params=_cp("arbitrary", "arbitrary"),
        name="matmul_residual",
    )(a, w, *x_args, mod)


def _ffn_in_kernel(a_ref, wa_ref, wb_ref, o_ref, wab_ref, wbb_ref):
    @pl.when(pl.program_id(1) == 0)
    def _():
        wab_ref[...] = wa_ref[...].astype(BF16)
        wbb_ref[...] = wb_ref[...].astype(BF16)

    for rows in _row_blocks(o_ref):
        a = a_ref[rows, :]
        u = jnp.dot(a, wab_ref[...], preferred_element_type=F32)
        v = jnp.dot(a, wbb_ref[...], preferred_element_type=F32)
        o_ref[rows, :] = (_silu(u) * v).astype(o_ref.dtype)


def _ffn_in(h, w, l, tm, tn, n_rows):
    nb = FFN_H // tn
    return pl.pallas_call(
        _ffn_in_kernel,
        grid=(nb, n_rows // tm),
        in_specs=[pl.BlockSpec((tm, D), lambda j, i: (i, 0)),
                  pl.BlockSpec((None, D, tn), lambda j, i: (l, 0, j)),
                  pl.BlockSpec((None, D, tn), lambda j, i: (l, 0, nb + j))],
        out_specs=pl.BlockSpec((tm, tn), lambda j, i: (i, j)),
        out_shape=jax.ShapeDtypeStruct((n_rows, FFN_H), BF16),
        scratch_shapes=[pltpu.VMEM((D, tn), BF16), pltpu.VMEM((D, tn), BF16)],
        compiler_params=_cp("arbitrary", "arbitrary"),
        name="ffn_in",
    )(h, w, w)


def _merge_kernel(ya_ref, yr_ref, ym_ref, ys_ref, ga_ref, gr_ref, gm_ref, gs_ref,
                  wa_ref, wr_ref, wm_ref, ws_ref, o_ref, wab_ref, wrb_ref, wmb_ref, wsb_ref):
    @pl.when(pl.program_id(1) == 0)
    def _():
        wab_ref[...] = wa_ref[...].astype(BF16)
        wrb_ref[...] = wr_ref[...].astype(BF16)
        wmb_ref[...] = wm_ref[...].astype(BF16)
        wsb_ref[...] = ws_ref[...].astype(BF16)

    for rows in _row_blocks(o_ref):
        acc = None
        for y_ref, g_ref, w_ref in ((ya_ref, ga_ref, wab_ref), (yr_ref, gr_ref, wrb_ref),
                                    (ym_ref, gm_ref, wmb_ref), (ys_ref, gs_ref, wsb_ref)):
            term = g_ref[rows, :].astype(F32) * jnp.dot(y_ref[rows, :], w_ref[...], preferred_element_type=F32)
            acc = term if acc is None else acc + term
        o_ref[rows, :] = acc.astype(o_ref.dtype)


def _merge(ya, yr, ym, ys, gates, wa, wr, wm, ws, l, tm, tn, n_rows):
    nb = D // tn

    def act(width):
        return pl.BlockSpec((tm, width), lambda j, i: (i, 0))

    def gate(br):
        return pl.BlockSpec((tm, tn), lambda j, i: (i, br * nb + j))

    def wgt(width):
        return pl.BlockSpec((None, width, tn), lambda j, i: (l, 0, j))

    return pl.pallas_call(
        _merge_kernel,
        grid=(nb, n_rows // tm),
        in_specs=[act(ATTN_Q), act(RW_W), act(ML_W), act(S5_W),
                  gate(0), gate(1), gate(2), gate(3),
                  wgt(ATTN_Q), wgt(RW_W), wgt(ML_W), wgt(S5_W)],
        out_specs=pl.BlockSpec((tm, tn), lambda j, i: (i, j)),
        out_shape=jax.ShapeDtypeStruct((n_rows, D), BF16),
        scratch_shapes=[pltpu.VMEM((ATTN_Q, tn), BF16), pltpu.VMEM((RW_W, tn), BF16),
                        pltpu.VMEM((ML_W, tn), BF16), pltpu.VMEM((S5_W, tn), BF16)],
        compiler_params=_cp("arbitrary", "arbitrary"),
        name="gated_merge",
    )(ya, yr, ym, ys, gates, gates, gates, gates, wa, wr, wm, ws)


def _final_norm_kernel(x_ref, g_ref, o_ref):
    x = x_ref[...]
    o_ref[...] = x * lax.rsqrt(jnp.mean(x * x, axis=-1, keepdims=True) + EPS) * g_ref[...]


def _final_norm(x, g):
    tr = NORM_TILE
    return pl.pallas_call(
        _final_norm_kernel,
        grid=(N_LAT // tr,),
        in_specs=[pl.BlockSpec((tr, D), lambda i: (i, 0)),
                  pl.BlockSpec((1, D), lambda i: (0, 0))],
        out_specs=pl.BlockSpec((tr, D), lambda i: (i, 0)),
        out_shape=jax.ShapeDtypeStruct((N_LAT, D), F32),
        compiler_params=_cp("parallel"),
        name="final_norm",
    )(x, g.reshape(1, D))


def _tile_halo_specs(width, col_blk=0):
    last = ROWS // 8 - 1
    per = TILE // 8
    return [pl.BlockSpec((TILE, width), lambda t: (t, col_blk)),
            pl.BlockSpec((8, width), lambda t: (jnp.maximum(t * per - 1, 0), col_blk)),
            pl.BlockSpec((8, width), lambda t: (jnp.minimum((t + 1) * per, last), col_blk))]


def _neighbours(z, prev_blk, next_blk):
    t = pl.program_id(0)
    pos = t % (SEQ // TILE)
    is_lat = t < N_LAT // TILE
    has_prev = jnp.logical_and(is_lat, pos > 0).astype(F32)
    has_next = jnp.logical_and(is_lat, pos < SEQ // TILE - 1).astype(F32)
    row = lax.broadcasted_iota(jnp.int32, z.shape, 0)
    zp = jnp.where(row == 0, prev_blk[7:8, :] * has_prev, pltpu.roll(z, 1, 0))
    zn = jnp.where(row == TILE - 1, next_blk[0:1, :] * has_next, pltpu.roll(z, TILE - 1, 0))
    return zp, zn


def _seq_row_block(b, d, n):
    ctx_c = jnp.where(d == 0, n, CTX_CHUNKS - 1 - n)
    lat_c = jnp.where(d == 0, n - CTX_CHUNKS, SEQ_CHUNKS - 1 - n)
    return jnp.where(n < CTX_CHUNKS, N_LAT // CHUNK + CTX_CHUNKS * b + ctx_c, LAT_CHUNKS * b + lat_c)


def _rope(x, cos, sin):
    lane = lax.broadcasted_iota(jnp.int32, x.shape, 1)
    first = (lane % 64) < 32
    partner = jnp.where(first, pltpu.roll(x, 96, 1), pltpu.roll(x, 32, 1))
    return x * cos + partner * sin


def _rms(x, g):
    return x * lax.rsqrt(jnp.mean(x * x, axis=-1, keepdims=True) + EPS) * g


def _attn_kernel(q_ref, kl_ref, kc_ref, vl_ref, vc_ref, cos_ref, sin_ref, cos_t_ref, sin_t_ref,
                 qg_ref, kg_ref, o_ref, klb_ref, kcb_ref, vlb_ref, vcb_ref):
    qi = pl.program_id(2)
    n_lat_tiles = SEQ // TILE

    @pl.when(qi == 0)
    def _():
        kg = kg_ref[...]
        klb_ref[...] = _rope(_rms(kl_ref[...], kg), cos_ref[...], sin_ref[...]).astype(BF16)
        kcb_ref[...] = _rms(kc_ref[...], kg).astype(BF16)
        vlb_ref[...] = vl_ref[...].astype(BF16)
        vcb_ref[...] = vc_ref[...].astype(BF16)

    scale = HEAD_DIM ** -0.5 * math.log2(math.e)
    nt = (((1,), (1,)), ((), ()))

    def heads(latent):
        sls = [slice(h * HEAD_DIM, (h + 1) * HEAD_DIM) for h in range(ATTN_HEADS // ATTN_KV)]
        q = [_rms(q_ref[:, sl], qg_ref[...]) for sl in sls]
        if latent:
            q = [_rope(x, cos_t_ref[...], sin_t_ref[...]) for x in q]
        q = [(x * scale).astype(BF16) for x in q]
        s_c = [lax.dot_general(x, kcb_ref[...], nt, preferred_element_type=F32) for x in q]
        m = [jnp.max(x, axis=-1, keepdims=True) for x in s_c]
        if latent:
            s_l = [lax.dot_general(x, klb_ref[...], nt, preferred_element_type=F32) for x in q]
            m = [jnp.maximum(a, jnp.max(x, axis=-1, keepdims=True)) for a, x in zip(m, s_l)]
        p_c = [jnp.exp2(x - a) for x, a in zip(s_c, m)]
        den = [jnp.sum(x, axis=-1, keepdims=True) for x in p_c]
        acc = [jnp.dot(x.astype(BF16), vcb_ref[...], preferred_element_type=F32) for x in p_c]
        if latent:
            p_l = [jnp.exp2(x - a) for x, a in zip(s_l, m)]
            den = [a + jnp.sum(x, axis=-1, keepdims=True) for a, x in zip(den, p_l)]
            acc = [a + jnp.dot(x.astype(BF16), vlb_ref[...], preferred_element_type=F32) for a, x in zip(acc, p_l)]
        for sl, a, dn in zip(sls, acc, den):
            o_ref[:, sl] = (a / dn).astype(o_ref.dtype)

    @pl.when(qi < n_lat_tiles)
    def _():
        heads(True)

    @pl.when(qi == n_lat_tiles)
    def _():
        heads(False)


def _rope_tables():
    rows = SEQ // GRID_W
    row = jnp.repeat(jnp.arange(rows, dtype=F32), GRID_W)
    col = jnp.tile(jnp.arange(GRID_W, dtype=F32), rows)
    axis_dim = HEAD_DIM // 2
    inv_freq = ROPE_THETA ** (-jnp.arange(0, axis_dim, 2, dtype=F32) / axis_dim)
    ang_r = row[:, None] * inv_freq[None]
    ang_c = col[:, None] * inv_freq[None]
    cos = jnp.concatenate([jnp.cos(ang_r), jnp.cos(ang_r), jnp.cos(ang_c), jnp.cos(ang_c)], axis=-1)
    sin = jnp.concatenate([-jnp.sin(ang_r), jnp.sin(ang_r), -jnp.sin(ang_c), jnp.sin(ang_c)], axis=-1)
    return cos, sin


def _attention(z, q_g, k_g, cos, sin):
    n_lat_tiles = SEQ // TILE
    qw = ATTN_Q // ATTN_KV
    kcol = ATTN_Q // HEAD_DIM
    vcol = kcol + ATTN_KV

    def q_row(b, g, qi):
        return jnp.where(qi < n_lat_tiles, n_lat_tiles * b + qi, N_LAT // TILE + b)

    return pl.pallas_call(
        _attn_kernel,
        grid=(NB, ATTN_KV, n_lat_tiles + 1),
        in_specs=[pl.BlockSpec((TILE, qw), lambda b, g, qi: (q_row(b, g, qi), g)),
                  pl.BlockSpec((SEQ, HEAD_DIM), lambda b, g, qi: (b, kcol + g)),
                  pl.BlockSpec((CTX, HEAD_DIM), lambda b, g, qi: (N_LAT // CTX + b, kcol + g)),
                  pl.BlockSpec((SEQ, HEAD_DIM), lambda b, g, qi: (b, vcol + g)),
                  pl.BlockSpec((CTX, HEAD_DIM), lambda b, g, qi: (N_LAT // CTX + b, vcol + g)),
                  pl.BlockSpec((SEQ, HEAD_DIM), lambda b, g, qi: (0, 0)),
                  pl.BlockSpec((SEQ, HEAD_DIM), lambda b, g, qi: (0, 0)),
                  pl.BlockSpec((TILE, HEAD_DIM), lambda b, g, qi: (jnp.minimum(qi, n_lat_tiles - 1), 0)),
                  pl.BlockSpec((TILE, HEAD_DIM), lambda b, g, qi: (jnp.minimum(qi, n_lat_tiles - 1), 0)),
                  pl.BlockSpec((1, HEAD_DIM), lambda b, g, qi: (0, 0)),
                  pl.BlockSpec((1, HEAD_DIM), lambda b, g, qi: (0, 0))],
        out_specs=pl.BlockSpec((TILE, qw), lambda b, g, qi: (q_row(b, g, qi), g)),
        out_shape=jax.ShapeDtypeStruct((ROWS, ATTN_Q), BF16),
        scratch_shapes=[pltpu.VMEM((SEQ, HEAD_DIM), BF16), pltpu.VMEM((CTX, HEAD_DIM), BF16),
                        pltpu.VMEM((SEQ, HEAD_DIM), BF16), pltpu.VMEM((CTX, HEAD_DIM), BF16)],
        compiler_params=_cp("arbitrary", "arbitrary", "arbitrary"),
        name="attention",
    )(z, z, z, z, z, cos, sin, cos, sin, q_g.reshape(1, HEAD_DIM), k_g.reshape(1, HEAD_DIM))


def _rwkv_prep_kernel(z_ref, zp_ref, zn_ref, mu_ref, w0_ref, wup_ref, a0_ref, aup_ref, gup_ref,
                      kk_ref, ka_ref, rk_ref, bd_ref,
                      r_out, v_out, kkn_out, g_out, bonus_out, lw_out, b_out, km_out):
    z = z_ref[...]
    zp, zn = _neighbours(z, zp_ref[...], zn_ref[...])
    zs = z + mu_ref[...] * (0.5 * (zp + zn) - z)
    r = zs[:, 0:RW_W]
    k = zs[:, RW_W:2 * RW_W]
    v = zs[:, 2 * RW_W:3 * RW_W]
    w_lo = zs[:, 3 * RW_W:3 * RW_W + 64]
    a_lo = zs[:, 3 * RW_W + 64:3 * RW_W + 128]
    g_lo = zs[:, 3 * RW_W + 128:3 * RW_W + 256]
    bd = bd_ref[...]
    kk = k * kk_ref[...]
    kk = kk * lax.rsqrt(_dot_rhs_exact(kk * kk, bd) + 1e-12)
    r_out[...] = r
    v_out[...] = v
    kkn_out[...] = kk
    g_out[...] = _bdot(_sigmoid(g_lo), gup_ref[...])
    tw = jnp.tanh(w_lo)
    km_sum = None
    for d in range(2):
        lw = -RW_DECAY * _sigmoid(w0_ref[d] + _bdot(tw, wup_ref[d]))
        a = _sigmoid(a0_ref[d] + _bdot(a_lo, aup_ref[d]))
        km = k * (1.0 + (a - 1.0) * ka_ref[...])
        lw_out[d] = lw
        b_out[d] = a * kk
        km_out[d] = km
        km_sum = km if km_sum is None else km_sum + km
    bonus_out[...] = _dot_rhs_exact(r * km_sum * rk_ref[...], bd) * v


def _rwkv_prep(z, p):
    row = lambda a: a.reshape(1, -1)
    full = lambda shape: pl.BlockSpec(shape, lambda t: (0,) * len(shape))
    out_tok = pl.BlockSpec((TILE, RW_W), lambda t: (t, 0))
    out_dir = pl.BlockSpec((2, TILE, RW_W), lambda t: (0, t, 0))
    tok = jax.ShapeDtypeStruct((ROWS, RW_W), F32)
    drn = jax.ShapeDtypeStruct((2, ROWS, RW_W), F32)
    return pl.pallas_call(
        _rwkv_prep_kernel,
        grid=(N_TILES,),
        in_specs=_tile_halo_specs(RW_IN) + [
            full((1, RW_IN)), full((2, 1, RW_W)), full((2, 64, RW_W)), full((2, 1, RW_W)),
            full((2, 64, RW_W)), full((128, RW_W)), full((1, RW_W)), full((1, RW_W)), full((1, RW_W)),
            full((RW_W, RW_W))],
        out_specs=[out_tok] * 5 + [out_dir] * 3,
        out_shape=[tok] * 5 + [drn] * 3,
        compiler_params=_cp("parallel"),
        name="rwkv_prep",
    )(z, z, z, row(p["mu"]), p["w0"].reshape(2, 1, RW_W), p["w_up"], p["a0"].reshape(2, 1, RW_W),
      p["a_up"], p["g_up"], row(p["k_k"]), row(p["k_a"]), row(p["r_k"]),
      _head_block_ones(RW_W, RW_D).astype(BF16))


def _head_block_ones(width, head):
    idx = np.arange(width) // head
    return jnp.asarray((idx[:, None] == idx[None, :]).astype(np.float32))


def _split_bf16(x, pieces):
    out = []
    for _ in range(pieces):
        p = x.astype(BF16)
        out.append(p)
        x = x - p.astype(F32)
    return out


def _dot_rhs_exact(x, m, pieces=2):
    return sum(jnp.dot(p, m, preferred_element_type=F32) for p in _split_bf16(x, pieces))


def _dot_lhs_exact(m, x, pieces=3):
    return sum(jnp.dot(m, p, preferred_element_type=F32) for p in _split_bf16(x, pieces))


RW_SUB = 2


def _rwkv_chunk_kernel(r_ref, v_ref, kk_ref, lw_ref, b_ref, km_ref, p_ref, sl_ref, re_ref, ol_ref):
    d = pl.program_id(0)
    c = CHUNK
    hd = RW_D
    ti = lax.broadcasted_iota(jnp.int32, (c, c), 0)
    si = lax.broadcasted_iota(jnp.int32, (c, c), 1)
    delta = (ti - si) * (1 - 2 * d)
    incl = delta >= 0
    strict = delta > 0
    eye = jnp.where(ti == si, 1.0, 0.0)
    tri = jnp.where(incl, 1.0, 0.0).astype(BF16)
    gr = lax.broadcasted_iota(jnp.int32, (2 * c, c), 0)
    gc = lax.broadcasted_iota(jnp.int32, (2 * c, c), 1)
    gmask = (gr % c - gc) * (1 - 2 * d) >= jnp.where(gr < c, 1, 0)

    items = []
    for sub in range(RW_SUB):
        rows = slice(sub * c, (sub + 1) * c)
        lw = lw_ref[0, rows, :]
        cs = _dot_lhs_exact(tri, lw)
        tot = jnp.sum(lw, axis=0, keepdims=True)
        r = r_ref[rows, :]
        v = v_ref[rows, :]
        kk = kk_ref[rows, :]
        bb = b_ref[0, rows, :]
        km = km_ref[0, rows, :]
        e_neg = jnp.exp(-cs)
        e_rem = jnp.exp(tot - cs)
        kkt = kk * jnp.exp(cs - lw)
        rt = r * jnp.exp(cs)
        bt = bb * e_neg
        kt = km * e_neg
        bh = bb * e_rem
        kh = km * e_rem
        e_tot = jnp.exp(tot)
        for h in range(RW_H):
            s = slice(h * hd, (h + 1) * hd)
            items.append(dict(sub=sub, s=s, kkt=kkt[:, s], rt=rt[:, s], bt=bt[:, s], kt=kt[:, s], v=v[:, s],
                              bh=bh[:, s], kh=kh[:, s], e_tot=e_tot[:, s]))

    kr = [jnp.concatenate([it["kkt"], it["rt"]], axis=0) for it in items]
    gb = [jnp.where(gmask, _bdot_nt(x, it["bt"]), 0.0) for x, it in zip(kr, items)]
    gk = [jnp.where(gmask, _bdot_nt(x, it["kt"]), 0.0) for x, it in zip(kr, items)]
    l_b = [x[:c] for x in gb]
    a_b = [x[c:] for x in gb]
    lkv_akv = [_bdot(x, it["v"]) for x, it in zip(gk, items)]
    pw = [_bdot(x, x) for x in l_b]
    inv = [eye - x for x in l_b]
    for _ in range(int(math.log2(c)) - 2):
        res = [_bdot(jnp.concatenate([p, i], axis=0), p) for p, i in zip(pw, inv)]
        pw = [x[:c] for x in res]
        inv = [i + x[c:] for i, x in zip(inv, res)]
    inv = [i + _bdot(i, p) for p, i in zip(pw, inv)]
    w = [_bdot(i, it["kkt"]) for i, it in zip(inv, items)]
    y_loc = [_bdot(i, x[:c]) for i, x in zip(inv, lkv_akv)]
    ab_w = [_bdot(a, x) for a, x in zip(a_b, w)]
    ab_y = [_bdot(a, x) for a, x in zip(a_b, y_loc)]
    vk = [_bdot_tn(it["v"], it["kh"]) for it in items]
    yb = [_bdot_tn(x, it["bh"]) for x, it in zip(y_loc, items)]
    wb = [_bdot_tn(x, it["bh"]) for x, it in zip(w, items)]
    for n, it in enumerate(items):
        s = it["s"]
        j = jnp.where(d == 0, it["sub"], RW_SUB - 1 - it["sub"])
        re_ref[0, 0, j, :, s] = it["rt"] - ab_w[n]
        ol_ref[0, 0, j, :, s] = lkv_akv[n][c:] - ab_y[n]
        sl_ref[0, 0, j, :, s] = vk[n] - yb[n]
        p_ref[0, 0, j, :, s] = eye * it["e_tot"] - wb[n]


def _rwkv_carry_kernel(p_ref, sl_ref, s_out_ref, st_ref):
    @pl.when(pl.program_id(0) == 0)
    def _():
        st_ref[...] = jnp.zeros_like(st_ref)

    for d in range(2):
        for b in range(NB):
            s = st_ref[d, b]
            s_out_ref[d, b, 0] = s
            for h in range(RW_H):
                sl = slice(h * RW_D, (h + 1) * RW_D)
                st_ref[d, b, :, sl] = _hdot(s[:, sl], p_ref[d, b, 0, :, sl]) + sl_ref[d, b, 0, :, sl]


TILE_CHUNKS = TILE // CHUNK


def _tile_of_group(b, grp):
    return jnp.where(grp == 0, N_LAT // TILE + b, (SEQ // TILE) * b + grp - 1)


def _bwd_group(grp):
    return jnp.where(grp == 0, 0, SEQ_CHUNKS // TILE_CHUNKS - grp)


def _rwkv_out_kernel(olf_ref, ref_ref, sf_ref, olb_ref, reb_ref, sb_ref, bonus_ref, g_ref, gng_ref, gnb_ref,
                     bd_ref, y_ref, o_scr):
    n = TILE_CHUNKS
    items = [(i, slice(h * RW_D, (h + 1) * RW_D)) for i in range(n) for h in range(RW_H)]
    pf = [_bdot_nt(ref_ref[0, 0, i, :, s], sf_ref[0, 0, i, :, s]) for i, s in items]
    pb = [_bdot_nt(reb_ref[0, 0, n - 1 - i, :, s], sb_ref[0, 0, n - 1 - i, :, s]) for i, s in items]
    for (i, s), a, b in zip(items, pf, pb):
        o_scr[i * CHUNK:(i + 1) * CHUNK, s] = olf_ref[0, 0, i, :, s] + olb_ref[0, 0, n - 1 - i, :, s] + a + b
    o = o_scr[...] + bonus_ref[...]
    bd = bd_ref[...]
    cen = o - _dot_rhs_exact(o, bd) * (1.0 / RW_D)
    var = _dot_rhs_exact(cen * cen, bd) * (1.0 / RW_D)
    y = cen * lax.rsqrt(var + RW_GN_EPS) * gng_ref[...] + gnb_ref[...]
    y_ref[...] = (y * g_ref[...]).astype(y_ref.dtype)


def _rwkv_scan(r, v, kk, lw, bmat, km, bonus, g, gn_g, gn_b):
    sub_rows = RW_SUB * CHUNK
    lat_blocks = N_LAT // sub_rows
    lat_per_b = SEQ // sub_rows
    ctx_per_b = CTX // sub_rows

    def step_block(d, rb):
        is_lat = rb < lat_blocks
        b = jnp.where(is_lat, rb // lat_per_b, (rb - lat_blocks) // ctx_per_b)
        i = jnp.where(is_lat, rb % lat_per_b, (rb - lat_blocks) % ctx_per_b)
        fwd = jnp.where(is_lat, ctx_per_b + i, i)
        bwd = jnp.where(is_lat, ctx_per_b + lat_per_b - 1 - i, ctx_per_b - 1 - i)
        return b, jnp.where(d == 0, fwd, bwd)

    tok = pl.BlockSpec((sub_rows, RW_W), lambda d, rb: (rb, 0))
    drn = pl.BlockSpec((1, sub_rows, RW_W), lambda d, rb: (d, rb, 0))
    step_shape = jax.ShapeDtypeStruct((2, NB, SEQ_CHUNKS, CHUNK, RW_W), F32)
    step_blk = pl.BlockSpec((1, 1, RW_SUB, CHUNK, RW_W), lambda d, rb: (d,) + step_block(d, rb) + (0, 0))
    p, s_loc, r_eff, o_loc = pl.pallas_call(
        _rwkv_chunk_kernel,
        grid=(2, ROWS // sub_rows),
        in_specs=[tok, tok, tok, drn, drn, drn],
        out_specs=[step_blk] * 4,
        out_shape=[step_shape] * 4,
        compiler_params=_cp("parallel", "parallel"),
        name="rwkv_chunk",
    )(r, v, kk, lw, bmat, km)

    all_blk = pl.BlockSpec((2, NB, 1, CHUNK, RW_W), lambda n: (0, 0, n, 0, 0))
    s_in = pl.pallas_call(
        _rwkv_carry_kernel,
        grid=(SEQ_CHUNKS,),
        in_specs=[all_blk, all_blk],
        out_specs=all_blk,
        out_shape=step_shape,
        scratch_shapes=[pltpu.VMEM((2, NB, CHUNK, RW_W), F32)],
        compiler_params=_cp("arbitrary"),
        name="rwkv_carry",
    )(p, s_loc)

    fwd = pl.BlockSpec((1, 1, TILE_CHUNKS, CHUNK, RW_W), lambda b, grp: (0, b, grp, 0, 0))
    bwd = pl.BlockSpec((1, 1, TILE_CHUNKS, CHUNK, RW_W), lambda b, grp: (1, b, _bwd_group(grp), 0, 0))
    rows = pl.BlockSpec((TILE, RW_W), lambda b, grp: (_tile_of_group(b, grp), 0))
    vec = pl.BlockSpec((1, RW_W), lambda b, grp: (0, 0))
    return pl.pallas_call(
        _rwkv_out_kernel,
        grid=(NB, SEQ_CHUNKS // TILE_CHUNKS),
        in_specs=[fwd, fwd, fwd, bwd, bwd, bwd, rows, rows, vec, vec,
                  pl.BlockSpec((RW_W, RW_W), lambda b, grp: (0, 0))],
        out_specs=rows,
        out_shape=jax.ShapeDtypeStruct((ROWS, RW_W), BF16),
        scratch_shapes=[pltpu.VMEM((TILE, RW_W), F32)],
        compiler_params=_cp("parallel", "parallel"),
        name="rwkv_out",
    )(o_loc, r_eff, s_in, o_loc, r_eff, s_in, bonus, g, gn_g.reshape(1, RW_W), gn_b.reshape(1, RW_W),
      _head_block_ones(RW_W, RW_D).astype(BF16))


def _mlstm_prep_kernel(z_ref, zp_ref, zn_ref, w_ref, o_ref):
    z = z_ref[...]
    zp, zn = _neighbours(z, zp_ref[...], zn_ref[...])
    y = _silu(zp * w_ref[0:1, :] + z * w_ref[1:2, :] + zn * w_ref[2:3, :])
    col = lax.broadcasted_iota(jnp.int32, y.shape, 1)
    o_ref[...] = jnp.where(col >= ML_W, y * (ML_D ** -0.5), y)


def _mlstm_prep(z, conv_w):
    return pl.pallas_call(
        _mlstm_prep_kernel,
        grid=(N_TILES,),
        in_specs=_tile_halo_specs(2 * ML_W) + [pl.BlockSpec((3, 2 * ML_W), lambda t: (0, 0))],
        out_specs=pl.BlockSpec((TILE, 2 * ML_W), lambda t: (t, 0)),
        out_shape=jax.ShapeDtypeStruct((ROWS, 2 * ML_W), F32),
        compiler_params=_cp("parallel"),
        name="mlstm_prep",
    )(z, z, z, conv_w)


def _log_sigmoid(x):
    return jnp.minimum(x, 0.0) - jnp.log(1.0 + jnp.exp(-jnp.abs(x)))


N_CHAINS = 2 * NB


def _mlstm_scan_kernel(*refs):
    nc = N_CHAINS
    q_refs, k_refs, v_refs = refs[0:nc], refs[nc:2 * nc], refs[2 * nc:3 * nc]
    gc_refs, gr_refs = refs[3 * nc:4 * nc], refs[4 * nc:5 * nc]
    bc_ref, br_ref, o_ref, c_ref, n_ref, m_ref = refs[5 * nc:]

    @pl.when(pl.program_id(0) == 0)
    def _():
        c_ref[...] = jnp.zeros_like(c_ref)
        n_ref[...] = jnp.zeros_like(n_ref)
        m_ref[...] = jnp.zeros_like(m_ref)

    c = CHUNK
    ti = lax.broadcasted_iota(jnp.int32, (c, c), 0)
    si = lax.broadcasted_iota(jnp.int32, (c, c), 1)
    masks = (ti >= si, ti <= si)
    items = [(ci, h) for ci in range(nc) for h in range(ML_H)]
    sls = [slice(h * ML_D, (h + 1) * ML_D) for h in range(ML_H)]
    gcol = [gc_refs[ci][0, 0] + bc_ref[ci // NB] for ci in range(nc)]
    grow = [gr_refs[ci][0, 0] + br_ref[ci // NB] for ci in range(nc)]

    ones = jnp.ones((c, ML_D), BF16)
    q = [q_refs[ci][:, sls[h]] for ci, h in items]
    k = [k_refs[ci][:, sls[h]] for ci, h in items]
    v = [v_refs[ci][:, sls[h]] for ci, h in items]
    qk = [_bdot_nt(a, b) for a, b in zip(q, k)]
    c_mat = [c_ref[ci, h] for ci, h in items]
    n_vec = [n_ref[ci, h] for ci, h in items]
    m_prev = [m_ref[ci, h][0:1, :] for ci, h in items]
    qcn = [_bdot_nt(a, jnp.concatenate([cm, jnp.broadcast_to(nv, (ML_D, ML_D))], axis=0))
           for a, cm, nv in zip(q, c_mat, n_vec)]

    log_w, cum, i_col, total = [], [], [], []
    for ci, h in items:
        mask = masks[ci // NB]
        mask_t = masks[1 - ci // NB]
        f_col = _log_sigmoid(gcol[ci][:, ML_H + h:ML_H + h + 1])
        f_row = _log_sigmoid(grow[ci][ML_H + h:ML_H + h + 1, :])
        cc = _dot_rhs_exact(jnp.where(mask, f_row, 0.0), ones, pieces=3)
        cr = jnp.sum(jnp.where(mask_t, f_col, 0.0), axis=0, keepdims=True)
        log_w.append(jnp.where(mask, cc[:, :c] - cr + grow[ci][h:h + 1, :], ML_NEG))
        cum.append(cc)
        i_col.append(jnp.broadcast_to(gcol[ci][:, h:h + 1], (c, ML_D)))
        total.append(jnp.sum(f_row, axis=1, keepdims=True))
    m_inter = [a + b for a, b in zip(cum, m_prev)]
    m_t = [jnp.maximum(jnp.broadcast_to(jnp.max(a, axis=1, keepdims=True), (c, ML_D)), b)
           for a, b in zip(log_w, m_inter)]
    s = [a * jnp.exp(b - m[:, :c]) for a, b, m in zip(qk, log_w, m_t)]
    w_inter = [jnp.exp(a - m) for a, m in zip(m_inter, m_t)]
    sv = [_bdot(a, jnp.concatenate([b.astype(BF16), ones], axis=1)) for a, b in zip(s, v)]
    for i, (ci, h) in enumerate(items):
        num = sv[i][:, :ML_D] + w_inter[i] * qcn[i][:, :ML_D]
        den = sv[i][:, ML_D:] + w_inter[i] * qcn[i][:, ML_D:]
        o_ref[ci // NB, ci % NB, 0, :, sls[h]] = num / jnp.maximum(jnp.abs(den), jnp.exp(-m_t[i]))
    log_src = [t - a + b for t, a, b in zip(total, cum, i_col)]
    m_new = [jnp.maximum(t + mp, jnp.max(ls, axis=0, keepdims=True)) for t, mp, ls in zip(total, m_prev, log_src)]
    src = [jnp.exp(ls - mn) for ls, mn in zip(log_src, m_new)]
    decay = [jnp.exp(t + mp - mn) for t, mp, mn in zip(total, m_prev, m_new)]
    vk = [_bdot_tn(a * sr, b) for a, sr, b in zip(v, src, k)]
    for i, (ci, h) in enumerate(items):
        c_ref[ci, h] = decay[i] * c_mat[i] + vk[i]
        n_ref[ci, h] = decay[i] * n_vec[i] + jnp.sum(src[i] * k[i], axis=0, keepdims=True)
        m_ref[ci, h] = jnp.broadcast_to(m_new[i], m_ref.shape[2:])


def _mlstm_scan(qk, z, gcol, grow, bcol, brow):
    chains = [(d, b) for d in range(2) for b in range(NB)]

    def tok(col_blk):
        return [pl.BlockSpec((CHUNK, ML_W), lambda n, d=d, b=b: (_seq_row_block(b, d, n), col_blk))
                for d, b in chains]

    gc_specs = [pl.BlockSpec((1, 1, CHUNK, 2 * ML_H), lambda n, d=d, b=b: (d, _seq_row_block(b, d, n), 0, 0))
                for d, b in chains]
    gr_specs = [pl.BlockSpec((1, 1, 2 * ML_H, CHUNK), lambda n, d=d, b=b: (d, _seq_row_block(b, d, n), 0, 0))
                for d, b in chains]
    nc = N_CHAINS
    return pl.pallas_call(
        _mlstm_scan_kernel,
        grid=(SEQ_CHUNKS,),
        in_specs=tok(0) + tok(1) + tok(2) + gc_specs + gr_specs + [
            pl.BlockSpec((2, 1, 2 * ML_H), lambda n: (0, 0, 0)),
            pl.BlockSpec((2, 2 * ML_H, 1), lambda n: (0, 0, 0))],
        out_specs=pl.BlockSpec((2, NB, 1, CHUNK, ML_W), lambda n: (0, 0, n, 0, 0)),
        out_shape=jax.ShapeDtypeStruct((2, NB, SEQ_CHUNKS, CHUNK, ML_W), F32),
        scratch_shapes=[pltpu.VMEM((nc, ML_H, ML_D, ML_D), F32), pltpu.VMEM((nc, ML_H, 1, ML_D), F32),
                        pltpu.VMEM((nc, ML_H, 8, 128), F32)],
        compiler_params=_cp("arbitrary"),
        name="mlstm_scan",
    )(*([qk] * (2 * nc) + [z] * nc + [gcol] * nc + [grow] * nc + [bcol, brow]))


def _mlstm_out_kernel(hf_ref, hb_ref, og_ref, gng_ref, y_ref):
    n = TILE_CHUNKS
    for i in range(n):
        rows = slice(i * CHUNK, (i + 1) * CHUNK)
        hsum = _sigmoid(og_ref[rows, :]) * (hf_ref[0, 0, i] + hb_ref[0, 0, n - 1 - i])
        for h in range(ML_H):
            sl = slice(h * ML_D, (h + 1) * ML_D)
            x = hsum[:, sl]
            cen = x - jnp.mean(x, axis=-1, keepdims=True)
            var = jnp.mean(cen * cen, axis=-1, keepdims=True)
            y_ref[rows, sl] = (cen * lax.rsqrt(var + EPS) * gng_ref[:, sl]).astype(y_ref.dtype)


def _mlstm_out(hs, z, gn_g):
    blk = (1, 1, TILE_CHUNKS, CHUNK, ML_W)
    return pl.pallas_call(
        _mlstm_out_kernel,
        grid=(NB, SEQ_CHUNKS // TILE_CHUNKS),
        in_specs=[pl.BlockSpec(blk, lambda b, grp: (0, b, grp, 0, 0)),
                  pl.BlockSpec(blk, lambda b, grp: (1, b, _bwd_group(grp), 0, 0)),
                  pl.BlockSpec((TILE, ML_W), lambda b, grp: (_tile_of_group(b, grp), 3)),
                  pl.BlockSpec((1, ML_W), lambda b, grp: (0, 0))],
        out_specs=pl.BlockSpec((TILE, ML_W), lambda b, grp: (_tile_of_group(b, grp), 0)),
        out_shape=jax.ShapeDtypeStruct((ROWS, ML_W), BF16),
        compiler_params=_cp("parallel", "parallel"),
        name="mlstm_out",
    )(hs, hs, z, gn_g.reshape(1, ML_W))


def _mlstm(z_main, z_gates, conv_w, i_b, f_b, gn_g):
    qk = _mlstm_prep(z_main, conv_w)
    n_chunks = ROWS // CHUNK
    gates = z_gates[:, :4 * ML_H].reshape(n_chunks, CHUNK, 2, 2, ML_H)
    gcol = jnp.transpose(gates, (3, 0, 1, 2, 4)).reshape(2, n_chunks, CHUNK, 2 * ML_H)
    grow = jnp.swapaxes(gcol, 2, 3)
    bias = jnp.concatenate([i_b, f_b], axis=-1)
    hs = _mlstm_scan(qk, z_main, gcol, grow, bias.reshape(2, 1, 2 * ML_H), bias.reshape(2, 2 * ML_H, 1))
    return _mlstm_out(hs, z_main, gn_g)


def _s5_toeplitz_kernel(kf_ref, kb_ref, o_ref):
    kf = kf_ref[0]
    kb = kb_ref[0]
    lane = lax.broadcasted_iota(jnp.int32, kf.shape, 1)
    width = S5_L * S5_C
    for j in range(S5_L):
        f = kf if j == 0 else jnp.where(lane >= S5_C * j, pltpu.roll(kf, S5_C * j, 1), 0.0)
        back = S5_L - 1 - j
        b = kb if back == 0 else jnp.where(lane < S5_C * (j + 1), pltpu.roll(kb, width - S5_C * back, 1), 0.0)
        o_ref[0, 0, :, j] = f.reshape(S5_G, S5_C, width).astype(o_ref.dtype)
        o_ref[0, 1, :, j] = b.reshape(S5_G, S5_C, width).astype(o_ref.dtype)


def _s5_operators(lam_re, lam_im, log_step, b_re, b_im, c_re, c_im):
    nl = lam_re.shape[0]
    dt = jnp.exp(log_step)[..., None]
    mag = jnp.exp(lam_re * dt)
    a_re = mag * jnp.cos(lam_im * dt)
    a_im = mag * jnp.sin(lam_im * dt)
    den = lam_re * lam_re + lam_im * lam_im
    f_re = ((a_re - 1) * lam_re + a_im * lam_im) / den
    f_im = (a_im * lam_re - (a_re - 1) * lam_im) / den
    bb_re = f_re[..., None] * b_re - f_im[..., None] * b_im
    bb_im = f_re[..., None] * b_im + f_im[..., None] * b_re
    bt_re = jnp.swapaxes(bb_re, -1, -2)
    bt_im = jnp.swapaxes(bb_im, -1, -2)
    pr = [jnp.ones_like(a_re)]
    pi = [jnp.zeros_like(a_im)]
    for _ in range(S5_L):
        pr.append(pr[-1] * a_re - pi[-1] * a_im)
        pi.append(pr[-2] * a_im + pi[-1] * a_re)
    pr = jnp.stack(pr, axis=3)
    pi = jnp.stack(pi, axis=3)

    def times_b(qr, qi):
        qr, qi = qr[..., None, :], qi[..., None, :]
        br, bi = bt_re[:, :, :, None], bt_im[:, :, :, None]
        return qr * br - qi * bi, qr * bi + qi * br

    wr, wi = times_b(pr[:, :, :, :S5_L], pi[:, :, :, :S5_L])
    wr = jnp.swapaxes(wr, 3, 4)
    wi = jnp.swapaxes(wi, 3, 4)
    kern = (jnp.einsum("ldgktp,ldgcp->ldgktc", wr, c_re, precision=HP)
            - jnp.einsum("ldgktp,ldgcp->ldgktc", wi, c_im, precision=HP))
    kf = kern[:, 0].reshape(nl, S5_G * S5_C, S5_L * S5_C)
    kb = jnp.flip(kern[:, 1], axis=3).reshape(nl, S5_G * S5_C, S5_L * S5_C)
    rows_blk = pl.BlockSpec((1, S5_G * S5_C, S5_L * S5_C), lambda l: (l, 0, 0))
    m = pl.pallas_call(
        _s5_toeplitz_kernel,
        grid=(nl,),
        in_specs=[rows_blk, rows_blk],
        out_specs=pl.BlockSpec((1, 2, S5_G, S5_L, S5_C, S5_L * S5_C), lambda l: (l, 0, 0, 0, 0, 0)),
        out_shape=jax.ShapeDtypeStruct((nl, 2, S5_G, S5_L, S5_C, S5_L * S5_C), BF16),
        compiler_params=_cp("parallel"),
        name="s5_toeplitz",
    )(kf, kb).reshape(nl, 2, S5_G, S5_L * S5_C, S5_L * S5_C)

    def stack_dirs(fwd, bwd):
        return jnp.stack([fwd[:, 0], bwd[:, 1]], axis=1)

    er, ei = times_b(stack_dirs(jnp.flip(pr[:, :, :, :S5_L], axis=3), pr[:, :, :, :S5_L]),
                     stack_dirs(jnp.flip(pi[:, :, :, :S5_L], axis=3), pi[:, :, :, :S5_L]))
    e = jnp.concatenate([er, ei], axis=-1).reshape(nl, 2, S5_G, S5_L * S5_C, 2 * S5_P).astype(BF16)
    qr = stack_dirs(pr[:, :, :, 1:], jnp.flip(pr[:, :, :, 1:], axis=3))[..., None, :]
    qi = stack_dirs(pi[:, :, :, 1:], jnp.flip(pi[:, :, :, 1:], axis=3))[..., None, :]
    cr, ci = c_re[:, :, :, None], c_im[:, :, :, None]
    ft = jnp.concatenate([cr * qr - ci * qi, -(cr * qi + ci * qr)], axis=-1)
    ft = ft.reshape(nl, 2, S5_G, S5_L * S5_C, 2 * S5_P).astype(BF16)
    return m, e, ft, pr[:, :, :, S5_L], pi[:, :, :, S5_L]


def _s5_local_kernel(u_ref, m_ref, e_ref, y_ref, x_ref):
    u = u_ref[0]
    for d in range(2):
        y_ref[d, 0] = jnp.dot(u, m_ref[d, 0], preferred_element_type=F32)
        x_ref[d, 0] = jnp.dot(u, e_ref[d, 0], preferred_element_type=F32)


def _s5_carry_kernel(xc_ref, ar_ref, ai_ref, x0_ref, st_ref):
    @pl.when(pl.program_id(1) == 0)
    def _():
        st_ref[...] = jnp.zeros_like(st_ref)

    ar = ar_ref[0]
    ai = ai_ref[0]

    def run(order):
        x = st_ref[...]
        for i in order:
            x0_ref[0, :, :, i, :] = x
            x = x * ar + pltpu.roll(x, S5_P, 2) * ai + xc_ref[0, :, :, i, :]
        st_ref[...] = x

    n = xc_ref.shape[3]

    @pl.when(pl.program_id(0) == 0)
    def _():
        run(range(n))

    @pl.when(pl.program_id(0) == 1)
    def _():
        run(reversed(range(n)))


def _s5_state_kernel(y_ref, x0_ref, ft_ref, o_ref):
    acc = y_ref[0, 0] + y_ref[1, 0]
    for d in range(2):
        acc = acc + _bdot_nt(x0_ref[d, 0], ft_ref[d, 0])
    o_ref[0] = acc.astype(o_ref.dtype)


def _s5_glu_kernel(y_ref, u_ref, d_ref, w_ref, o_ref):
    x = y_ref[...] + u_ref[...] * d_ref[...]
    ge = 0.5 * x * (1.0 + jnp.tanh(math.sqrt(2.0 / math.pi) * (x + 0.044715 * (x * x * x))))
    p = _bdot(ge, w_ref[...])
    o_ref[...] = (p[:, :S5_W] * _sigmoid(p[:, S5_W:])).astype(o_ref.dtype)


def _s5(z, m, e, ft, a_re, a_im, d_skip, w_glu):
    n_ch = (SEQ + CTX) // S5_L
    n_ctx_ch = CTX // S5_L
    rows = NB * n_ch
    width = S5_L * S5_C

    def to_groups(x):
        x = x.reshape(NB, -1, S5_L, S5_G, S5_C)
        return jnp.transpose(x, (3, 0, 1, 2, 4)).reshape(S5_G, NB, -1, width)

    zb = z.astype(BF16)
    u = jnp.concatenate([to_groups(zb[N_LAT:]), to_groups(zb[:N_LAT])], axis=2).reshape(S5_G, rows, width)
    y_loc, x_in = pl.pallas_call(
        _s5_local_kernel,
        grid=(S5_G,),
        in_specs=[pl.BlockSpec((1, rows, width), lambda g: (g, 0, 0)),
                  pl.BlockSpec((2, 1, width, width), lambda g: (0, g, 0, 0)),
                  pl.BlockSpec((2, 1, width, 2 * S5_P), lambda g: (0, g, 0, 0))],
        out_specs=[pl.BlockSpec((2, 1, rows, width), lambda g: (0, g, 0, 0)),
                   pl.BlockSpec((2, 1, rows, 2 * S5_P), lambda g: (0, g, 0, 0))],
        out_shape=[jax.ShapeDtypeStruct((2, S5_G, rows, width), F32),
                   jax.ShapeDtypeStruct((2, S5_G, rows, 2 * S5_P), F32)],
        compiler_params=_cp("parallel"),
        name="s5_local",
    )(u, m, e)

    step = n_ctx_ch
    n_blk = n_ch // step
    coef_r = jnp.concatenate([a_re, a_re], axis=-1)[:, :, None, :]
    coef_i = jnp.concatenate([-a_im, a_im], axis=-1)[:, :, None, :]

    def chunk_blk(d, i):
        return jnp.where(d == 0, i, jnp.where(i == 0, 0, n_blk - i))

    st_blk = pl.BlockSpec((1, S5_G, NB, step, 2 * S5_P), lambda d, i: (d, 0, 0, chunk_blk(d, i), 0))
    coef_blk = pl.BlockSpec((1, S5_G, 1, 2 * S5_P), lambda d, i: (d, 0, 0, 0))
    x0 = pl.pallas_call(
        _s5_carry_kernel,
        grid=(2, n_blk),
        in_specs=[st_blk, coef_blk, coef_blk],
        out_specs=st_blk,
        out_shape=jax.ShapeDtypeStruct((2, S5_G, NB, n_ch, 2 * S5_P), F32),
        scratch_shapes=[pltpu.VMEM((S5_G, NB, 2 * S5_P), F32)],
        compiler_params=_cp("arbitrary", "arbitrary"),
        name="s5_carry",
    )(x_in.reshape(2, S5_G, NB, n_ch, 2 * S5_P), coef_r, coef_i).reshape(2, S5_G, rows, 2 * S5_P)

    y = pl.pallas_call(
        _s5_state_kernel,
        grid=(S5_G,),
        in_specs=[pl.BlockSpec((2, 1, rows, width), lambda g: (0, g, 0, 0)),
                  pl.BlockSpec((2, 1, rows, 2 * S5_P), lambda g: (0, g, 0, 0)),
                  pl.BlockSpec((2, 1, width, 2 * S5_P), lambda g: (0, g, 0, 0))],
        out_specs=pl.BlockSpec((1, rows, width), lambda g: (g, 0, 0)),
        out_shape=jax.ShapeDtypeStruct((S5_G, rows, width), BF16),
        compiler_params=_cp("parallel"),
        name="s5_state",
    )(y_loc, x0, ft)

    y = jnp.transpose(y.reshape(S5_G, NB, n_ch, S5_L, S5_C), (1, 2, 3, 0, 4)).reshape(NB, n_ch * S5_L, S5_W)
    y = jnp.concatenate([y[:, CTX:].reshape(N_LAT, S5_W), y[:, :CTX].reshape(N_CTX, S5_W)], axis=0)
    tok = pl.BlockSpec((TILE, S5_W), lambda t: (t, 0))
    return pl.pallas_call(
        _s5_glu_kernel,
        grid=(N_TILES,),
        in_specs=[tok, tok, pl.BlockSpec((1, S5_W), lambda t: (0, 0)),
                  pl.BlockSpec((S5_W, 2 * S5_W), lambda t: (0, 0))],
        out_specs=tok,
        out_shape=jax.ShapeDtypeStruct((ROWS, S5_W), BF16),
        compiler_params=_cp("parallel"),
        name="s5_glu",
    )(y, z, d_skip.reshape(1, S5_W), w_glu)


def kernel(x, c, ctx, c_ctx, w_mod, b_mod, norm1_g, norm2_g, w_in, b_gate, q_norm_g, k_norm_g, rwkv_mu, rwkv_w0, rwkv_w_up, rwkv_a0, rwkv_a_up, rwkv_g_up, rwkv_k_k, rwkv_k_a, rwkv_r_k, rwkv_gn_g, rwkv_gn_b, mlstm_conv_w, mlstm_i_b, mlstm_f_b, mlstm_gn_g, s5_lam_re, s5_lam_im, s5_log_step, s5_b_re, s5_b_im, s5_c_re, s5_c_im, s5_d, s5_w_glu, w_br_attn, w_br_rwkv, w_br_mlstm, w_br_s5, w_out, w_ffn_in, w_ffn_out, final_norm_g):
    cos, sin = _rope_tables()
    xs = (x.reshape(N_LAT, D), ctx.reshape(N_CTX, D))
    c_all = jnp.concatenate([c, c_ctx[None], jnp.zeros((3, D), F32)], axis=0)
    s5_m, s5_e, s5_ft, s5_ar, s5_ai = _s5_operators(s5_lam_re, s5_lam_im, s5_log_step, s5_b_re, s5_b_im,
                                                    s5_c_re, s5_c_im)
    tm = 1024
    for l in range(DEPTH):
        last = l == DEPTH - 1
        n_rows = N_LAT if last else ROWS
        mod = _modulation(c_all, w_mod, b_mod, l).reshape(8, 1, 6 * D)
        h = _norm_mod(xs, norm1_g[l], mod, 0, 1, ROWS)
        w_attn, w_rwkv, w_ml, w_mlg, w_s5, w_gate = _w_in_split(w_in, l)
        z_attn = _mm(h, w_attn, tm, 512)
        z_rwkv = _mm(h, w_rwkv, tm, 896)
        z_ml = _mm(h, w_ml, tm, 1024)
        z_mlg = _mm(h, w_mlg, tm, 128)
        z_s5 = _mm(h, w_s5, tm, 512)
        gates = _mm_gate(h, w_gate, b_gate, l, tm, 1024, n_rows)

        ya = _attention(z_attn, q_norm_g[l], k_norm_g[l], cos, sin)
        rp = dict(mu=rwkv_mu[l], w0=rwkv_w0[l], w_up=rwkv_w_up[l], a0=rwkv_a0[l], a_up=rwkv_a_up[l],
                  g_up=rwkv_g_up[l], k_k=rwkv_k_k[l], k_a=rwkv_k_a[l], r_k=rwkv_r_k[l].reshape(RW_W))
        r, v, kk, g, bonus, lw, bmat, km = _rwkv_prep(z_rwkv, rp)
        yr = _rwkv_scan(r, v, kk, lw, bmat, km, bonus, g, rwkv_gn_g[l], rwkv_gn_b[l])
        ym = _mlstm(z_ml, z_mlg, mlstm_conv_w[l], mlstm_i_b[l], mlstm_f_b[l], mlstm_gn_g[l])
        ys = _s5(z_s5, s5_m[l], s5_e[l], s5_ft[l], s5_ar[l], s5_ai[l], s5_d[l], s5_w_glu[l])

        y = _merge(ya, yr, ym, ys, gates, w_br_attn, w_br_rwkv, w_br_mlstm, w_br_s5, l, tm, 512, n_rows)
        xs = _mm_res(y, w_out, l, xs, mod, 2, tm, 512 if isinstance(xs, tuple) else 1024, n_rows)
        h2 = _norm_mod(xs, norm2_g[l], mod, 3, 4, n_rows)
        u = _ffn_in(h2, w_ffn_in, l, tm, 512, n_rows)
        xs = _mm_res(u, w_ffn_out, l, xs, mod, 5, 512, 512, n_rows)
    return _final_norm(xs, final_norm_g).reshape(NB, SEQ, D)
```
